```python
import math
import jax, jax.numpy as jnp
from jax import lax
import numpy as np

D_MODEL = 1024
BATCH = 16
SEQ = 2048
DEPTH = 4

N_META = 16
D_FF = 2816
REC_WIDTH = 512
REC_BLOCKS = 8
REC_BLOCK_DIM = REC_WIDTH // REC_BLOCKS
CONV_WIDTH = 4
LRU_C = 8.0
N_Q_HEADS = 8
N_KV_HEADS = 2
Q_PER_KV = N_Q_HEADS // N_KV_HEADS
HEAD_DIM = 64
ATTN_WIDTH = N_Q_HEADS * HEAD_DIM
KV_WIDTH = N_KV_HEADS * HEAD_DIM
WINDOW = 128
BLOCK = 128
ROPE_DIM = HEAD_DIM // 4
ROPE_THETA = 500000.0
D_IN = 2 * REC_WIDTH + ATTN_WIDTH + 2 * KV_WIDTH
D_MIX = REC_WIDTH + ATTN_WIDTH
DEEPNORM_ALPHA = (2.0 * DEPTH) ** 0.25
DEEPNORM_BETA = (8.0 * DEPTH) ** -0.25
LN_EPS = 1e-5
RMS_EPS = 1e-6
NEG_INF = -1e30

kernel_name = "hybrid_rglru_swa_sink_macaron_deepnorm"


def layer_norm(x, g, b):
    xf = x.astype(jnp.float32)
    mu = jnp.mean(xf, axis=-1, keepdims=True)
    var = jnp.mean(jnp.square(xf - mu), axis=-1, keepdims=True)
    return ((xf - mu) * lax.rsqrt(var + LN_EPS) * g.astype(jnp.float32) + b.astype(jnp.float32)).astype(x.dtype)


def rms_norm(x, g):
    xf = x.astype(jnp.float32)
    ms = jnp.mean(jnp.square(xf), axis=-1, keepdims=True)
    return (xf * lax.rsqrt(ms + RMS_EPS) * g.astype(jnp.float32)).astype(x.dtype)


def swiglu(x, w_gate, w_up, w_down):
    return (jax.nn.silu(x @ w_gate) * (x @ w_up)) @ w_down


def causal_depthwise_conv(x, w, b):
    L = x.shape[1]
    xp = jnp.pad(x, ((0, 0), (CONV_WIDTH - 1, 0), (0, 0)))
    y = b + xp[:, 0:L] * w[0]
    for k in range(1, CONV_WIDTH):
        y = y + xp[:, k:k + L] * w[k]
    return y


def rglru(x, wa, ba, wx, bx, lam):
    B, L, _ = x.shape
    xb = x.reshape(B, L, REC_BLOCKS, REC_BLOCK_DIM)
    r = jax.nn.sigmoid(jnp.einsum('blhi,hij->blhj', xb, wa).reshape(B, L, REC_WIDTH) + ba).astype(jnp.float32)
    i = jax.nn.sigmoid(jnp.einsum('blhi,hij->blhj', xb, wx).reshape(B, L, REC_WIDTH) + bx)
    log_a = -LRU_C * jax.nn.softplus(-lam.astype(jnp.float32)) * r
    a = jnp.exp(log_a)
    gated_x = jnp.sqrt(-jnp.expm1(2.0 * log_a)) * (i * x).astype(jnp.float32)

    def combine(c1, c2):
        a1, b1 = c1
        a2, b2 = c2
        return a1 * a2, a2 * b1 + b2

    _, h = lax.associative_scan(combine, (a, gated_x), axis=1)
    return h.astype(x.dtype)


def rope_tables(L):
    pos = jnp.arange(L, dtype=jnp.float32)
    inv_freq = ROPE_THETA ** (-jnp.arange(0, ROPE_DIM, 2, dtype=jnp.float32) / ROPE_DIM)
    ang = pos[:, None] * inv_freq[None, :]
    return jnp.cos(ang), jnp.sin(ang)


def apply_partial_rope(t, cos, sin):
    tf = t.astype(jnp.float32)
    x1 = tf[..., :ROPE_DIM // 2]
    x2 = tf[..., ROPE_DIM // 2:ROPE_DIM]
    c = cos[None, :, None, :]
    s = sin[None, :, None, :]
    out = jnp.concatenate([x1 * c - x2 * s, x2 * c + x1 * s, tf[..., ROPE_DIM:]], axis=-1)
    return out.astype(t.dtype)


def swa_sink_attention(q, k, v, sinks):
    B, L = q.shape[0], q.shape[1]
    pad = (-L) % BLOCK
    nb = (L + pad) // BLOCK
    padl = ((0, 0), (pad, 0), (0, 0), (0, 0))
    qb = jnp.pad(q, padl).reshape(B, nb, BLOCK, N_KV_HEADS, Q_PER_KV, HEAD_DIM)
    kb = jnp.pad(k, padl).reshape(B, nb, BLOCK, N_KV_HEADS, HEAD_DIM)
    vb = jnp.pad(v, padl).reshape(B, nb, BLOCK, N_KV_HEADS, HEAD_DIM)
    prev = ((0, 0), (1, 0), (0, 0), (0, 0), (0, 0))
    k_band = jnp.concatenate([jnp.pad(kb, prev)[:, :-1], kb], axis=2)
    v_band = jnp.concatenate([jnp.pad(vb, prev)[:, :-1], vb], axis=2)
    k_meta = k[:, :N_META]
    v_meta = v[:, :N_META]
    scale = HEAD_DIM ** -0.5
    s_band = jnp.einsum('bnqgrd,bnkgd->bngrqk', qb, k_band).astype(jnp.float32) * scale
    s_meta = jnp.einsum('bnqgrd,bmgd->bngrqm', qb, k_meta).astype(jnp.float32) * scale
    q_pos = jnp.arange(nb)[:, None] * BLOCK + jnp.arange(BLOCK)[None, :] - pad
    k_pos = (jnp.arange(nb)[:, None] - 1) * BLOCK + jnp.arange(2 * BLOCK)[None, :] - pad
    dist = q_pos[:, :, None] - k_pos[:, None, :]
    band_mask = (dist >= 0) & (dist < WINDOW) & (k_pos[:, None, :] >= N_META)
    meta_mask = jnp.arange(N_META)[None, None, :] <= q_pos[:, :, None]
    s_band = jnp.where(band_mask[None, :, None, None], s_band, NEG_INF)
    s_meta = jnp.where(meta_mask[None, :, None, None], s_meta, NEG_INF)
    sink = jnp.broadcast_to(sinks.astype(jnp.float32).reshape(1, 1, N_KV_HEADS, Q_PER_KV, 1, 1),
                            s_meta.shape[:-1] + (1,))
    probs = jax.nn.softmax(jnp.concatenate([sink, s_meta, s_band], axis=-1), axis=-1)
    p_meta = probs[..., 1:1 + N_META].astype(v.dtype)
    p_band = probs[..., 1 + N_META:].astype(v.dtype)
    o = (jnp.einsum('bngrqm,bmgd->bnqgrd', p_meta, v_meta)
         + jnp.einsum('bngrqk,bnkgd->bnqgrd', p_band, v_band))
    return o.reshape(B, nb * BLOCK, ATTN_WIDTH)[:, pad:]


def hybrid_mixer(h, w_in, conv_w, conv_b, ga_w, ga_b, gx_w, gx_b, lam, sinks, g_rec, g_attn, w_out, cos, sin):
    B, L, _ = h.shape
    proj = h @ w_in
    x_rec, gate, q, k, v = jnp.split(
        proj, [REC_WIDTH, 2 * REC_WIDTH, 2 * REC_WIDTH + ATTN_WIDTH, 2 * REC_WIDTH + ATTN_WIDTH + KV_WIDTH], axis=-1)
    x_rec = causal_depthwise_conv(x_rec, conv_w, conv_b)
    y_rec = rglru(x_rec, ga_w, ga_b, gx_w, gx_b, lam) * jax.nn.gelu(gate, approximate=True)
    q = apply_partial_rope(q.reshape(B, L, N_Q_HEADS, HEAD_DIM), cos, sin)
    k = apply_partial_rope(k.reshape(B, L, N_KV_HEADS, HEAD_DIM), cos, sin)
    v = v.reshape(B, L, N_KV_HEADS, HEAD_DIM)
    y_attn = swa_sink_attention(q, k, v, sinks)
    y = jnp.concatenate([rms_norm(y_rec, g_rec), rms_norm(y_attn, g_attn)], axis=-1)
    return y @ w_out


def _fwd_setup_inputs(seed: int = 0) -> dict:
    key = jax.random.key(seed)
    ks = jax.random.split(key, 26)
    f32 = jnp.float32

    def nrm(k, shape, scale):
        return jax.random.normal(k, shape, f32) * scale

    u = jax.random.uniform(ks[12], (DEPTH, REC_WIDTH), f32, 0.9, 0.999)
    s = u ** (1.0 / LRU_C)
    lru_lambda = jnp.log(s) - jnp.log1p(-s)
    return {
        'x': nrm(ks[0], (BATCH, SEQ, D_MODEL), 1.0),
        'meta_tokens': nrm(ks[1], (N_META, D_MODEL), 1.0),
        'ffn1_w_gate': nrm(ks[2], (DEPTH, D_MODEL, D_FF), D_MODEL ** -0.5),
        'ffn1_w_up': nrm(ks[3], (DEPTH, D_MODEL, D_FF), D_MODEL ** -0.5),
        'ffn1_w_down': nrm(ks[4], (DEPTH, D_FF, D_MODEL), D_FF ** -0.5 * DEEPNORM_BETA),
        'ln1_g': 1.0 + nrm(ks[5], (DEPTH, D_MODEL), 0.02),
        'ln1_b': nrm(ks[6], (DEPTH, D_MODEL), 0.02),
        'w_in': nrm(ks[7], (DEPTH, D_MODEL, D_IN), D_MODEL ** -0.5),
        'conv_w': nrm(ks[8], (DEPTH, CONV_WIDTH, REC_WIDTH), CONV_WIDTH ** -0.5),
        'conv_b': nrm(ks[9], (DEPTH, REC_WIDTH), 0.02),
        'gate_a_w': nrm(ks[10], (DEPTH, REC_BLOCKS, REC_BLOCK_DIM, REC_BLOCK_DIM), REC_BLOCK_DIM ** -0.5),
        'gate_a_b': nrm(ks[11], (DEPTH, REC_WIDTH), 0.02),
        'gate_x_w': nrm(ks[13], (DEPTH, REC_BLOCKS, REC_BLOCK_DIM, REC_BLOCK_DIM), REC_BLOCK_DIM ** -0.5),
        'gate_x_b': nrm(ks[14], (DEPTH, REC_WIDTH), 0.02),
        'lru_lambda': lru_lambda,
        'attn_sinks': nrm(ks[15], (DEPTH, N_Q_HEADS), 0.5),
        'norm_rec_g': 1.0 + nrm(ks[16], (DEPTH, REC_WIDTH), 0.02),
        'norm_attn_g': 1.0 + nrm(ks[17], (DEPTH, ATTN_WIDTH), 0.02),
        'w_out': nrm(ks[18], (DEPTH, D_MIX, D_MODEL), D_MIX ** -0.5 * DEEPNORM_BETA),
        'ln2_g': 1.0 + nrm(ks[19], (DEPTH, D_MODEL), 0.02),
        'ln2_b': nrm(ks[20], (DEPTH, D_MODEL), 0.02),
        'ffn2_w_gate': nrm(ks[21], (DEPTH, D_MODEL, D_FF), D_MODEL ** -0.5),
        'ffn2_w_up': nrm(ks[22], (DEPTH, D_MODEL, D_FF), D_MODEL ** -0.5),
        'ffn2_w_down': nrm(ks[23], (DEPTH, D_FF, D_MODEL), D_FF ** -0.5 * DEEPNORM_BETA),
        'ln3_g': 1.0 + nrm(ks[24], (DEPTH, D_MODEL), 0.02),
        'ln3_b': nrm(ks[25], (DEPTH, D_MODEL), 0.02),
    }


def _fwd_reference(x, meta_tokens, ffn1_w_gate, ffn1_w_up, ffn1_w_down, ln1_g, ln1_b, w_in, conv_w, conv_b,
              gate_a_w, gate_a_b, gate_x_w, gate_x_b, lru_lambda, attn_sinks, norm_rec_g, norm_attn_g,
              w_out, ln2_g, ln2_b, ffn2_w_gate, ffn2_w_up, ffn2_w_down, ln3_g, ln3_b):
    B = x.shape[0]
    meta = jnp.broadcast_to(meta_tokens[None].astype(x.dtype), (B, N_META, D_MODEL))
    h = jnp.concatenate([meta, x], axis=1)
    cos, sin = rope_tables(h.shape[1])
    for l in range(DEPTH):
        h = layer_norm(DEEPNORM_ALPHA * h + 0.5 * swiglu(h, ffn1_w_gate[l], ffn1_w_up[l], ffn1_w_down[l]),
                       ln1_g[l], ln1_b[l])
        m = hybrid_mixer(h, w_in[l], conv_w[l], conv_b[l], gate_a_w[l], gate_a_b[l], gate_x_w[l], gate_x_b[l],
                         lru_lambda[l], attn_sinks[l], norm_rec_g[l], norm_attn_g[l], w_out[l], cos, sin)
        h = layer_norm(DEEPNORM_ALPHA * h + m, ln2_g[l], ln2_b[l])
        h = layer_norm(DEEPNORM_ALPHA * h + 0.5 * swiglu(h, ffn2_w_gate[l], ffn2_w_up[l], ffn2_w_down[l]),
                       ln3_g[l], ln3_b[l])
    return h[:, N_META:]


import jax as _jax
import jax.numpy as _jnp

TWIN_FORMAT = 'train_step'
FWD_PARAMS = ['x', 'meta_tokens', 'ffn1_w_gate', 'ffn1_w_up', 'ffn1_w_down', 'ln1_g', 'ln1_b', 'w_in', 'conv_w', 'conv_b', 'gate_a_w', 'gate_a_b', 'gate_x_w', 'gate_x_b', 'lru_lambda', 'attn_sinks', 'norm_rec_g', 'norm_attn_g', 'w_out', 'ln2_g', 'ln2_b', 'ffn2_w_gate', 'ffn2_w_up', 'ffn2_w_down', 'ln3_g', 'ln3_b']
TWIN_WEIGHTS = ['meta_tokens', 'ffn1_w_gate', 'ffn1_w_up', 'ffn1_w_down', 'ln1_g', 'ln1_b', 'w_in', 'conv_w', 'conv_b', 'gate_a_w', 'gate_a_b', 'gate_x_w', 'gate_x_b', 'lru_lambda', 'attn_sinks', 'norm_rec_g', 'norm_attn_g', 'w_out', 'ln2_g', 'ln2_b', 'ffn2_w_gate', 'ffn2_w_up', 'ffn2_w_down', 'ln3_g', 'ln3_b']
TWIN_DIFF_INPUT = 'x'
TWIN_INPUTS = ['x', 'meta_tokens', 'ffn1_w_gate', 'ffn1_w_up', 'ffn1_w_down', 'ln1_g', 'ln1_b', 'w_in', 'conv_w', 'conv_b', 'gate_a_w', 'gate_a_b', 'gate_x_w', 'gate_x_b', 'lru_lambda', 'attn_sinks', 'norm_rec_g', 'norm_attn_g', 'w_out', 'ln2_g', 'ln2_b', 'ffn2_w_gate', 'ffn2_w_up', 'ffn2_w_down', 'ln3_g', 'ln3_b', 'loss_target', 'm_meta_tokens', 'm_ffn1_w_gate', 'm_ffn1_w_up', 'm_ffn1_w_down', 'm_ln1_g', 'm_ln1_b', 'm_w_in', 'm_conv_w', 'm_conv_b', 'm_gate_a_w', 'm_gate_a_b', 'm_gate_x_w', 'm_gate_x_b', 'm_lru_lambda', 'm_attn_sinks', 'm_norm_rec_g', 'm_norm_attn_g', 'm_w_out', 'm_ln2_g', 'm_ln2_b', 'm_ffn2_w_gate', 'm_ffn2_w_up', 'm_ffn2_w_down', 'm_ln3_g', 'm_ln3_b', 'v_meta_tokens', 'v_ffn1_w_gate', 'v_ffn1_w_up', 'v_ffn1_w_down', 'v_ln1_g', 'v_ln1_b', 'v_w_in', 'v_conv_w', 'v_conv_b', 'v_gate_a_w', 'v_gate_a_b', 'v_gate_x_w', 'v_gate_x_b', 'v_lru_lambda', 'v_attn_sinks', 'v_norm_rec_g', 'v_norm_attn_g', 'v_w_out', 'v_ln2_g', 'v_ln2_b', 'v_ffn2_w_gate', 'v_ffn2_w_up', 'v_ffn2_w_down', 'v_ln3_g', 'v_ln3_b']
TWIN_OUTPUTS = ['loss', 'grad_x', 'grad_meta_tokens', 'grad_ffn1_w_gate', 'grad_ffn1_w_up', 'grad_ffn1_w_down', 'grad_ln1_g', 'grad_ln1_b', 'grad_w_in', 'grad_conv_w', 'grad_conv_b', 'grad_gate_a_w', 'grad_gate_a_b', 'grad_gate_x_w', 'grad_gate_x_b', 'grad_lru_lambda', 'grad_attn_sinks', 'grad_norm_rec_g', 'grad_norm_attn_g', 'grad_w_out', 'grad_ln2_g', 'grad_ln2_b', 'grad_ffn2_w_gate', 'grad_ffn2_w_up', 'grad_ffn2_w_down', 'grad_ln3_g', 'grad_ln3_b', 'delta_meta_tokens', 'delta_ffn1_w_gate', 'delta_ffn1_w_up', 'delta_ffn1_w_down', 'delta_ln1_g', 'delta_ln1_b', 'delta_w_in', 'delta_conv_w', 'delta_conv_b', 'delta_gate_a_w', 'delta_gate_a_b', 'delta_gate_x_w', 'delta_gate_x_b', 'delta_lru_lambda', 'delta_attn_sinks', 'delta_norm_rec_g', 'delta_norm_attn_g', 'delta_w_out', 'delta_ln2_g', 'delta_ln2_b', 'delta_ffn2_w_gate', 'delta_ffn2_w_up', 'delta_ffn2_w_down', 'delta_ln3_g', 'delta_ln3_b', 'new_m_meta_tokens', 'new_m_ffn1_w_gate', 'new_m_ffn1_w_up', 'new_m_ffn1_w_down', 'new_m_ln1_g', 'new_m_ln1_b', 'new_m_w_in', 'new_m_conv_w', 'new_m_conv_b', 'new_m_gate_a_w', 'new_m_gate_a_b', 'new_m_gate_x_w', 'new_m_gate_x_b', 'new_m_lru_lambda', 'new_m_attn_sinks', 'new_m_norm_rec_g', 'new_m_norm_attn_g', 'new_m_w_out', 'new_m_ln2_g', 'new_m_ln2_b', 'new_m_ffn2_w_gate', 'new_m_ffn2_w_up', 'new_m_ffn2_w_down', 'new_m_ln3_g', 'new_m_ln3_b', 'new_v_meta_tokens', 'new_v_ffn1_w_gate', 'new_v_ffn1_w_up', 'new_v_ffn1_w_down', 'new_v_ln1_g', 'new_v_ln1_b', 'new_v_w_in', 'new_v_conv_w', 'new_v_conv_b', 'new_v_gate_a_w', 'new_v_gate_a_b', 'new_v_gate_x_w', 'new_v_gate_x_b', 'new_v_lru_lambda', 'new_v_attn_sinks', 'new_v_norm_rec_g', 'new_v_norm_attn_g', 'new_v_w_out', 'new_v_ln2_g', 'new_v_ln2_b', 'new_v_ffn2_w_gate', 'new_v_ffn2_w_up', 'new_v_ffn2_w_down', 'new_v_ln3_g', 'new_v_ln3_b']
TWIN_LEAF_KINDS = {'loss': 'loss', 'grad_x': 'grad_x', 'grad_meta_tokens': 'grad_w', 'grad_ffn1_w_gate': 'grad_w', 'grad_ffn1_w_up': 'grad_w', 'grad_ffn1_w_down': 'grad_w', 'grad_ln1_g': 'grad_w', 'grad_ln1_b': 'grad_w', 'grad_w_in': 'grad_w', 'grad_conv_w': 'grad_w', 'grad_conv_b': 'grad_w', 'grad_gate_a_w': 'grad_w', 'grad_gate_a_b': 'grad_w', 'grad_gate_x_w': 'grad_w', 'grad_gate_x_b': 'grad_w', 'grad_lru_lambda': 'grad_w', 'grad_attn_sinks': 'grad_w', 'grad_norm_rec_g': 'grad_w', 'grad_norm_attn_g': 'grad_w', 'grad_w_out': 'grad_w', 'grad_ln2_g': 'grad_w', 'grad_ln2_b': 'grad_w', 'grad_ffn2_w_gate': 'grad_w', 'grad_ffn2_w_up': 'grad_w', 'grad_ffn2_w_down': 'grad_w', 'grad_ln3_g': 'grad_w', 'grad_ln3_b': 'grad_w', 'delta_meta_tokens': 'delta_w', 'delta_ffn1_w_gate': 'delta_w', 'delta_ffn1_w_up': 'delta_w', 'delta_ffn1_w_down': 'delta_w', 'delta_ln1_g': 'delta_w', 'delta_ln1_b': 'delta_w', 'delta_w_in': 'delta_w', 'delta_conv_w': 'delta_w', 'delta_conv_b': 'delta_w', 'delta_gate_a_w': 'delta_w', 'delta_gate_a_b': 'delta_w', 'delta_gate_x_w': 'delta_w', 'delta_gate_x_b': 'delta_w', 'delta_lru_lambda': 'delta_w', 'delta_attn_sinks': 'delta_w', 'delta_norm_rec_g': 'delta_w', 'delta_norm_attn_g': 'delta_w', 'delta_w_out': 'delta_w', 'delta_ln2_g': 'delta_w', 'delta_ln2_b': 'delta_w', 'delta_ffn2_w_gate': 'delta_w', 'delta_ffn2_w_up': 'delta_w', 'delta_ffn2_w_down': 'delta_w', 'delta_ln3_g': 'delta_w', 'delta_ln3_b': 'delta_w', 'new_m_meta_tokens': 'new_m', 'new_m_ffn1_w_gate': 'new_m', 'new_m_ffn1_w_up': 'new_m', 'new_m_ffn1_w_down': 'new_m', 'new_m_ln1_g': 'new_m', 'new_m_ln1_b': 'new_m', 'new_m_w_in': 'new_m', 'new_m_conv_w': 'new_m', 'new_m_conv_b': 'new_m', 'new_m_gate_a_w': 'new_m', 'new_m_gate_a_b': 'new_m', 'new_m_gate_x_w': 'new_m', 'new_m_gate_x_b': 'new_m', 'new_m_lru_lambda': 'new_m', 'new_m_attn_sinks': 'new_m', 'new_m_norm_rec_g': 'new_m', 'new_m_norm_attn_g': 'new_m', 'new_m_w_out': 'new_m', 'new_m_ln2_g': 'new_m', 'new_m_ln2_b': 'new_m', 'new_m_ffn2_w_gate': 'new_m', 'new_m_ffn2_w_up': 'new_m', 'new_m_ffn2_w_down': 'new_m', 'new_m_ln3_g': 'new_m', 'new_m_ln3_b': 'new_m', 'new_v_meta_tokens': 'new_v', 'new_v_ffn1_w_gate': 'new_v', 'new_v_ffn1_w_up': 'new_v', 'new_v_ffn1_w_down': 'new_v', 'new_v_ln1_g': 'new_v', 'new_v_ln1_b': 'new_v', 'new_v_w_in': 'new_v', 'new_v_conv_w': 'new_v', 'new_v_conv_b': 'new_v', 'new_v_gate_a_w': 'new_v', 'new_v_gate_a_b': 'new_v', 'new_v_gate_x_w': 'new_v', 'new_v_gate_x_b': 'new_v', 'new_v_lru_lambda': 'new_v', 'new_v_attn_sinks': 'new_v', 'new_v_norm_rec_g': 'new_v', 'new_v_norm_attn_g': 'new_v', 'new_v_w_out': 'new_v', 'new_v_ln2_g': 'new_v', 'new_v_ln2_b': 'new_v', 'new_v_ffn2_w_gate': 'new_v', 'new_v_ffn2_w_up': 'new_v', 'new_v_ffn2_w_down': 'new_v', 'new_v_ln3_g': 'new_v', 'new_v_ln3_b': 'new_v'}


def _forward(args):
    return _fwd_reference(*[args[k] for k in FWD_PARAMS])


def _output_shape():
    out = _jax.eval_shape(lambda: _forward(_fwd_setup_inputs(0)))
    return out.shape, out.dtype

N_MICROBATCH = 1
ADAM_LR = 0.001
ADAM_B1 = 0.9
ADAM_B2 = 0.999
ADAM_EPS = 1e-08
ADAM_WD = 0.01
ADAM_STEP = 10
PER_EXAMPLE_BATCH_AXIS = {'x': 0, 'loss_target': 0}
SHARED_INPUTS = []
_WEIGHT_DTYPES = {'meta_tokens': _jnp.float32, 'ffn1_w_gate': _jnp.float32, 'ffn1_w_up': _jnp.float32, 'ffn1_w_down': _jnp.float32, 'ln1_g': _jnp.float32, 'ln1_b': _jnp.float32, 'w_in': _jnp.float32, 'conv_w': _jnp.float32, 'conv_b': _jnp.float32, 'gate_a_w': _jnp.float32, 'gate_a_b': _jnp.float32, 'gate_x_w': _jnp.float32, 'gate_x_b': _jnp.float32, 'lru_lambda': _jnp.float32, 'attn_sinks': _jnp.float32, 'norm_rec_g': _jnp.float32, 'norm_attn_g': _jnp.float32, 'w_out': _jnp.float32, 'ln2_g': _jnp.float32, 'ln2_b': _jnp.float32, 'ffn2_w_gate': _jnp.float32, 'ffn2_w_up': _jnp.float32, 'ffn2_w_down': _jnp.float32, 'ln3_g': _jnp.float32, 'ln3_b': _jnp.float32}
MOMENT_SCALE = {'meta_tokens': 1.383372e-02, 'ffn1_w_gate': 8.788163e-03, 'ffn1_w_up': 8.531305e-03, 'ffn1_w_down': 3.364292e-02, 'ln1_g': 9.549113e-01, 'ln1_b': 1.356065e+00, 'w_in': 6.014154e-02, 'conv_w': 6.033507e-02, 'conv_b': 6.527092e-01, 'gate_a_w': 1.846332e-02, 'gate_a_b': 1.508341e-02, 'gate_x_w': 3.469916e-02, 'gate_x_b': 1.972970e-02, 'lru_lambda': 2.936244e-02, 'attn_sinks': 1.022111e-02, 'norm_rec_g': 6.270476e-02, 'norm_attn_g': 8.532671e-02, 'w_out': 1.735614e-01, 'ln2_g': 1.041715e+00, 'ln2_b': 7.864986e-01, 'ffn2_w_gate': 8.453661e-03, 'ffn2_w_up': 8.217897e-03, 'ffn2_w_down': 3.238926e-02, 'ln3_g': 1.613111e+01, 'ln3_b': 2.944054e+00}


def _to_microbatches(a, axis):
    t = _jnp.moveaxis(a, axis, 0)
    t = t.reshape((N_MICROBATCH, t.shape[0] // N_MICROBATCH) + t.shape[1:])
    return _jnp.moveaxis(t, 1, axis + 1)


def setup_inputs(seed: int = 0) -> dict:
    inp = _fwd_setup_inputs(seed)
    key = _jax.random.fold_in(_jax.random.key(seed), 7919)
    shape, _ = _output_shape()
    out = dict(inp)
    out["loss_target"] = _jax.random.normal(_jax.random.fold_in(key, 0), shape, _jnp.float32)
    for i, name in enumerate(TWIN_WEIGHTS):
        w = inp[name].astype(_jnp.float32)
        if MOMENT_SCALE is None:
            s = _jnp.sqrt(_jnp.mean(_jnp.square(w)) + 1e-30)
        else:
            s = MOMENT_SCALE[name]
        km, kv = _jax.random.split(_jax.random.fold_in(key, i + 1))
        out[name] = w
        out["m_" + name] = s * _jax.random.normal(km, w.shape, _jnp.float32)
        out["v_" + name] = (s * s) * _jax.random.uniform(kv, w.shape, _jnp.float32, 0.5, 1.5)
    if N_MICROBATCH > 1:
        for name, axis in PER_EXAMPLE_BATCH_AXIS.items():
            out[name] = _to_microbatches(out[name], axis)
    return {'x': out['x'], 'meta_tokens': out['meta_tokens'], 'ffn1_w_gate': out['ffn1_w_gate'], 'ffn1_w_up': out['ffn1_w_up'], 'ffn1_w_down': out['ffn1_w_down'], 'ln1_g': out['ln1_g'], 'ln1_b': out['ln1_b'], 'w_in': out['w_in'], 'conv_w': out['conv_w'], 'conv_b': out['conv_b'], 'gate_a_w': out['gate_a_w'], 'gate_a_b': out['gate_a_b'], 'gate_x_w': out['gate_x_w'], 'gate_x_b': out['gate_x_b'], 'lru_lambda': out['lru_lambda'], 'attn_sinks': out['attn_sinks'], 'norm_rec_g': out['norm_rec_g'], 'norm_attn_g': out['norm_attn_g'], 'w_out': out['w_out'], 'ln2_g': out['ln2_g'], 'ln2_b': out['ln2_b'], 'ffn2_w_gate': out['ffn2_w_gate'], 'ffn2_w_up': out['ffn2_w_up'], 'ffn2_w_down': out['ffn2_w_down'], 'ln3_g': out['ln3_g'], 'ln3_b': out['ln3_b'], 'loss_target': out['loss_target'], 'm_meta_tokens': out['m_meta_tokens'], 'm_ffn1_w_gate': out['m_ffn1_w_gate'], 'm_ffn1_w_up': out['m_ffn1_w_up'], 'm_ffn1_w_down': out['m_ffn1_w_down'], 'm_ln1_g': out['m_ln1_g'], 'm_ln1_b': out['m_ln1_b'], 'm_w_in': out['m_w_in'], 'm_conv_w': out['m_conv_w'], 'm_conv_b': out['m_conv_b'], 'm_gate_a_w': out['m_gate_a_w'], 'm_gate_a_b': out['m_gate_a_b'], 'm_gate_x_w': out['m_gate_x_w'], 'm_gate_x_b': out['m_gate_x_b'], 'm_lru_lambda': out['m_lru_lambda'], 'm_attn_sinks': out['m_attn_sinks'], 'm_norm_rec_g': out['m_norm_rec_g'], 'm_norm_attn_g': out['m_norm_attn_g'], 'm_w_out': out['m_w_out'], 'm_ln2_g': out['m_ln2_g'], 'm_ln2_b': out['m_ln2_b'], 'm_ffn2_w_gate': out['m_ffn2_w_gate'], 'm_ffn2_w_up': out['m_ffn2_w_up'], 'm_ffn2_w_down': out['m_ffn2_w_down'], 'm_ln3_g': out['m_ln3_g'], 'm_ln3_b': out['m_ln3_b'], 'v_meta_tokens': out['v_meta_tokens'], 'v_ffn1_w_gate': out['v_ffn1_w_gate'], 'v_ffn1_w_up': out['v_ffn1_w_up'], 'v_ffn1_w_down': out['v_ffn1_w_down'], 'v_ln1_g': out['v_ln1_g'], 'v_ln1_b': out['v_ln1_b'], 'v_w_in': out['v_w_in'], 'v_conv_w': out['v_conv_w'], 'v_conv_b': out['v_conv_b'], 'v_gate_a_w': out['v_gate_a_w'], 'v_gate_a_b': out['v_gate_a_b'], 'v_gate_x_w': out['v_gate_x_w'], 'v_gate_x_b': out['v_gate_x_b'], 'v_lru_lambda': out['v_lru_lambda'], 'v_attn_sinks': out['v_attn_sinks'], 'v_norm_rec_g': out['v_norm_rec_g'], 'v_norm_attn_g': out['v_norm_attn_g'], 'v_w_out': out['v_w_out'], 'v_ln2_g': out['v_ln2_g'], 'v_ln2_b': out['v_ln2_b'], 'v_ffn2_w_gate': out['v_ffn2_w_gate'], 'v_ffn2_w_up': out['v_ffn2_w_up'], 'v_ffn2_w_down': out['v_ffn2_w_down'], 'v_ln3_g': out['v_ln3_g'], 'v_ln3_b': out['v_ln3_b']}


def _loss(weights, diff, rest, loss_target):
    with _jax.named_scope("forward"):
        args = {**rest, TWIN_DIFF_INPUT: diff, **{k: w.astype(_WEIGHT_DTYPES[k]) for k, w in weights.items()}}
        y = _forward(args)
    with _jax.named_scope("loss_head"):
        err = _jnp.square(y.astype(_jnp.float32) - loss_target)
        return 0.5 * _jnp.sum(_jnp.mean(err, axis=-1)) if err.ndim else 0.5 * err


def _adamw(w, g, m, v):
    m = ADAM_B1 * m + (1.0 - ADAM_B1) * g
    v = ADAM_B2 * v + (1.0 - ADAM_B2) * _jnp.square(g)
    m_hat = m / (1.0 - ADAM_B1 ** ADAM_STEP)
    v_hat = v / (1.0 - ADAM_B2 ** ADAM_STEP)
    delta = -ADAM_LR * (m_hat / (_jnp.sqrt(v_hat) + ADAM_EPS) + ADAM_WD * w)
    return delta, m, v


def reference(x, meta_tokens, ffn1_w_gate, ffn1_w_up, ffn1_w_down, ln1_g, ln1_b, w_in, conv_w, conv_b, gate_a_w, gate_a_b, gate_x_w, gate_x_b, lru_lambda, attn_sinks, norm_rec_g, norm_attn_g, w_out, ln2_g, ln2_b, ffn2_w_gate, ffn2_w_up, ffn2_w_down, ln3_g, ln3_b, loss_target, m_meta_tokens, m_ffn1_w_gate, m_ffn1_w_up, m_ffn1_w_down, m_ln1_g, m_ln1_b, m_w_in, m_conv_w, m_conv_b, m_gate_a_w, m_gate_a_b, m_gate_x_w, m_gate_x_b, m_lru_lambda, m_attn_sinks, m_norm_rec_g, m_norm_attn_g, m_w_out, m_ln2_g, m_ln2_b, m_ffn2_w_gate, m_ffn2_w_up, m_ffn2_w_down, m_ln3_g, m_ln3_b, v_meta_tokens, v_ffn1_w_gate, v_ffn1_w_up, v_ffn1_w_down, v_ln1_g, v_ln1_b, v_w_in, v_conv_w, v_conv_b, v_gate_a_w, v_gate_a_b, v_gate_x_w, v_gate_x_b, v_lru_lambda, v_attn_sinks, v_norm_rec_g, v_norm_attn_g, v_w_out, v_ln2_g, v_ln2_b, v_ffn2_w_gate, v_ffn2_w_up, v_ffn2_w_down, v_ln3_g, v_ln3_b):
    given = dict(x=x, meta_tokens=meta_tokens, ffn1_w_gate=ffn1_w_gate, ffn1_w_up=ffn1_w_up, ffn1_w_down=ffn1_w_down, ln1_g=ln1_g, ln1_b=ln1_b, w_in=w_in, conv_w=conv_w, conv_b=conv_b, gate_a_w=gate_a_w, gate_a_b=gate_a_b, gate_x_w=gate_x_w, gate_x_b=gate_x_b, lru_lambda=lru_lambda, attn_sinks=attn_sinks, norm_rec_g=norm_rec_g, norm_attn_g=norm_attn_g, w_out=w_out, ln2_g=ln2_g, ln2_b=ln2_b, ffn2_w_gate=ffn2_w_gate, ffn2_w_up=ffn2_w_up, ffn2_w_down=ffn2_w_down, ln3_g=ln3_g, ln3_b=ln3_b, loss_target=loss_target, m_meta_tokens=m_meta_tokens, m_ffn1_w_gate=m_ffn1_w_gate, m_ffn1_w_up=m_ffn1_w_up, m_ffn1_w_down=m_ffn1_w_down, m_ln1_g=m_ln1_g, m_ln1_b=m_ln1_b, m_w_in=m_w_in, m_conv_w=m_conv_w, m_conv_b=m_conv_b, m_gate_a_w=m_gate_a_w, m_gate_a_b=m_gate_a_b, m_gate_x_w=m_gate_x_w, m_gate_x_b=m_gate_x_b, m_lru_lambda=m_lru_lambda, m_attn_sinks=m_attn_sinks, m_norm_rec_g=m_norm_rec_g, m_norm_attn_g=m_norm_attn_g, m_w_out=m_w_out, m_ln2_g=m_ln2_g, m_ln2_b=m_ln2_b, m_ffn2_w_gate=m_ffn2_w_gate, m_ffn2_w_up=m_ffn2_w_up, m_ffn2_w_down=m_ffn2_w_down, m_ln3_g=m_ln3_g, m_ln3_b=m_ln3_b, v_meta_tokens=v_meta_tokens, v_ffn1_w_gate=v_ffn1_w_gate, v_ffn1_w_up=v_ffn1_w_up, v_ffn1_w_down=v_ffn1_w_down, v_ln1_g=v_ln1_g, v_ln1_b=v_ln1_b, v_w_in=v_w_in, v_conv_w=v_conv_w, v_conv_b=v_conv_b, v_gate_a_w=v_gate_a_w, v_gate_a_b=v_gate_a_b, v_gate_x_w=v_gate_x_w, v_gate_x_b=v_gate_x_b, v_lru_lambda=v_lru_lambda, v_attn_sinks=v_attn_sinks, v_norm_rec_g=v_norm_rec_g, v_norm_attn_g=v_norm_attn_g, v_w_out=v_w_out, v_ln2_g=v_ln2_g, v_ln2_b=v_ln2_b, v_ffn2_w_gate=v_ffn2_w_gate, v_ffn2_w_up=v_ffn2_w_up, v_ffn2_w_down=v_ffn2_w_down, v_ln3_g=v_ln3_g, v_ln3_b=v_ln3_b)
    weights = {n: given[n] for n in TWIN_WEIGHTS}
    shared = {n: given[n] for n in SHARED_INPUTS}
    per_example = {n: given[n] for n in ['x']}
    grad_fn = _jax.value_and_grad(_loss, argnums=(0, 1))

    def one_microbatch(ex, loss_target):
        ex = dict(ex)
        diff = ex.pop(TWIN_DIFF_INPUT)
        return grad_fn(weights, diff, {**shared, **ex}, loss_target)

    if N_MICROBATCH == 1:
        loss, (grad_w, grad_x) = one_microbatch(per_example, given["loss_target"])
    else:
        def body(carry, xs):
            loss_sum, grad_sum = carry
            l_k, (gw_k, gx_k) = one_microbatch(xs[0], xs[1])
            with _jax.named_scope("update"):
                return (loss_sum + l_k, _jax.tree.map(_jnp.add, grad_sum, gw_k)), gx_k

        init = (_jnp.zeros((), _jnp.float32), _jax.tree.map(_jnp.zeros_like, weights))
        (loss, grad_w), grad_x = _jax.lax.scan(body, init, (per_example, given["loss_target"]))
    with _jax.named_scope("update"):
        delta_w, new_m, new_v = {}, {}, {}
        for n in TWIN_WEIGHTS:
            delta_w[n], new_m[n], new_v[n] = _adamw(weights[n], grad_w[n], given["m_" + n], given["v_" + n])
    return (loss, grad_x, *[grad_w[n] for n in TWIN_WEIGHTS], *[delta_w[n] for n in TWIN_WEIGHTS],
            *[new_m[n] for n in TWIN_WEIGHTS], *[new_v[n] for n in TWIN_WEIGHTS])
```

```python
import functools

import jax
import jax.numpy as jnp
from jax import lax
from jax.experimental import pallas as pl
from jax.experimental.pallas import tpu as pltpu

F32 = jnp.float32
BF16 = jnp.bfloat16
MESH = pl.DeviceIdType.MESH

N_DEV = 8
N_META = 16
BLOCK = 128
WINDOW = 128
REC_WIDTH = 512
ATTN_WIDTH = 512
KV_WIDTH = 128
HEAD_DIM = 64
N_Q_HEADS = 8
ROPE_DIM = 16
ROPE_THETA = 500000.0
CONV_WIDTH = 4
LRU_C = 8.0
LN_EPS = 1e-5
RMS_EPS = 1e-6
NEG_INF = -1e30
ADAM_LR, ADAM_B1, ADAM_B2, ADAM_EPS, ADAM_WD, ADAM_STEP = 0.001, 0.9, 0.999, 1e-08, 0.01, 10

NT = (((1,), (1,)), ((), ()))
TN = (((0,), (0,)), ((), ()))
VMEM_BIG = 56 * 1024 * 1024


def _params(vmem=None):
    return pltpu.CompilerParams(vmem_limit_bytes=vmem)


def _row_tile(rows):
    return 256 if rows % 256 == 0 else 128


def _sigmoid(x):
    return 1.0 / (1.0 + jnp.exp(-x))


def _expm1_nonpos(x):
    series = x * (1.0 + 0.5 * x * (1.0 + x / 3.0 * (1.0 + 0.25 * x * (1.0 + 0.2 * x))))
    return jnp.where(x > -0.05, series, jnp.exp(x) - 1.0)


def _softplus(x):
    e = jnp.exp(-jnp.abs(x))
    log1p = jnp.where(e < 1e-3, e * (1.0 - e * (0.5 - e / 3.0)), jnp.log(1.0 + e))
    return jnp.maximum(x, 0.0) + log1p


def _gelu_and_grad(x):
    c0 = 0.7978845608028654
    x2 = x * x
    t = jnp.tanh(c0 * (x + 0.044715 * x * x2))
    gelu = 0.5 * x * (1.0 + t)
    grad = 0.5 * (1.0 + t) + 0.5 * x * (1.0 - t * t) * c0 * (1.0 + 3.0 * 0.044715 * x2)
    return gelu, grad


def _ln_stats(z):
    mu = jnp.mean(z, axis=-1, keepdims=True)
    zc = z - mu
    var = jnp.mean(zc * zc, axis=-1, keepdims=True)
    rstd = lax.rsqrt(var + LN_EPS)
    return zc * rstd, rstd


def _ln_bwd(dy, z, gamma):
    xhat, rstd = _ln_stats(z)
    dxh = dy * gamma
    m1 = jnp.mean(dxh, axis=-1, keepdims=True)
    m2 = jnp.mean(dxh * xhat, axis=-1, keepdims=True)
    dz = rstd * (dxh - m1 - xhat * m2)
    return dz, jnp.sum(dy * xhat, axis=0, keepdims=True), jnp.sum(dy, axis=0, keepdims=True)


def _rms(y, gamma):
    r = lax.rsqrt(jnp.mean(y * y, axis=-1, keepdims=True) + RMS_EPS)
    return y * r * gamma


def _rms_bwd(dout, y, gamma):
    r = lax.rsqrt(jnp.mean(y * y, axis=-1, keepdims=True) + RMS_EPS)
    n = y * r
    dn = dout * gamma
    dy = r * (dn - n * jnp.mean(dn * n, axis=-1, keepdims=True))
    return dy, jnp.sum(dout * n, axis=0, keepdims=True), n * gamma


def _load_resident(pairs, sem):
    @pl.when(pl.program_id(0) == 0)
    def _():
        copies = [pltpu.make_async_copy(src, dst, sem.at[k]) for k, (src, dst) in enumerate(pairs)]
        for cp in copies:
            cp.start()
        for cp in copies:
            cp.wait()


def _full(shape):
    return pl.BlockSpec(shape, lambda *_: (0,) * len(shape))


ANY = pl.BlockSpec(memory_space=pl.ANY)


def _all_gather(xs, name):
    n = len(xs)

    def body(*refs):
        ins, outs = refs[:n], refs[n : 2 * n]
        send_sems, recv_sems, local_sems = refs[2 * n :]
        x, y, c = lax.axis_index("x"), lax.axis_index("y"), lax.axis_index("c")
        me, sibling = (x, y, c), (x, y, 1 - c)
        chips = [(1 - x, y), (x, 1 - y), (1 - x, 1 - y)]

        def slot(i, dev):
            return outs[i].at[4 * dev[0] + 2 * dev[1] + dev[2]]

        def copy(i, k, block, to, src=None):
            return pltpu.make_async_remote_copy(
                src_ref=slot(i, block) if src is None else src,
                dst_ref=slot(i, block),
                send_sem=send_sems.at[i, k],
                recv_sem=recv_sems.at[i, k],
                device_id=to,
                device_id_type=MESH,
            )

        mine = [pltpu.make_async_copy(ins[i], slot(i, me), local_sems.at[i]) for i in range(n)]
        for cp in mine:
            cp.start()
        first = []
        for i in range(n):
            first.append(copy(i, 0, me, sibling, src=ins[i]))
            first += [copy(i, 1 + j, me, (*chip, c), src=ins[i]) for j, chip in enumerate(chips)]
        for cp in first:
            cp.start()
        passed = []
        for j, chip in enumerate(chips):
            for i in range(n):
                copy(i, 1 + j, (*chip, c), me).wait_recv()
                fwd = copy(i, 4 + j, (*chip, c), sibling)
                fwd.start()
                passed.append(fwd)
        for i in range(n):
            copy(i, 0, sibling, me).wait_recv()
            for j, chip in enumerate(chips):
                copy(i, 4 + j, (*chip, 1 - c), me).wait_recv()
        for cp in first + passed:
            cp.wait_send()
        for cp in mine:
            cp.wait()

    return pl.pallas_call(
        body,
        name=name,
        out_shape=[jax.ShapeDtypeStruct((N_DEV,) + a.shape, a.dtype) for a in xs],
        in_specs=[ANY] * n,
        out_specs=[ANY] * n,
        scratch_shapes=[pltpu.SemaphoreType.DMA((n, 7)), pltpu.SemaphoreType.DMA((n, 7)), pltpu.SemaphoreType.DMA((n,))],
    )(*xs)


def _all_to_all(gs, name):
    n = len(gs)

    def body(*refs):
        ins, outs = refs[:n], refs[n : 2 * n]
        send_sems, recv_sems, local_sems = refs[2 * n :]
        x, y, c = lax.axis_index("x"), lax.axis_index("y"), lax.axis_index("c")
        me = 4 * x + 2 * y + c
        mine = [pltpu.make_async_copy(ins[i].at[me], outs[i].at[me], local_sems.at[i]) for i in range(n)]
        for cp in mine:
            cp.start()
        copies = []
        for k in range(1, N_DEV):
            kx, ky, kc = (k >> 2) & 1, (k >> 1) & 1, k & 1
            px = 1 - x if kx else x
            py = 1 - y if ky else y
            pc = 1 - c if kc else c
            peer = 4 * px + 2 * py + pc
            for i in range(n):
                copies.append(
                    pltpu.make_async_remote_copy(
                        src_ref=ins[i].at[peer],
                        dst_ref=outs[i].at[me],
                        send_sem=send_sems.at[i, k - 1],
                        recv_sem=recv_sems.at[i, k - 1],
                        device_id=(px, py, pc),
                        device_id_type=MESH,
                    )
                )
        for cp in copies:
            cp.start()
        for cp in copies:
            cp.wait_recv()
        for cp in copies:
            cp.wait_send()
        for cp in mine:
            cp.wait()

    return pl.pallas_call(
        body,
        name=name,
        out_shape=[jax.ShapeDtypeStruct(a.shape, a.dtype) for a in gs],
        in_specs=[ANY] * n,
        out_specs=[ANY] * n,
        scratch_shapes=[pltpu.SemaphoreType.DMA((n, 7)), pltpu.SemaphoreType.DMA((n, 7)), pltpu.SemaphoreType.DMA((n,))],
    )(*gs)


def _sum8(parts, name):
    _, rows, cols = parts.shape
    tr = rows
    for cand in (512, 256, 128, 64, 32, 16):
        if rows % cand == 0 and rows > cand:
            tr = cand
            break

    def body(p_ref, o_ref):
        acc = p_ref[0].astype(F32)
        for d in range(1, N_DEV):
            acc = acc + p_ref[d].astype(F32)
        o_ref[...] = acc

    return pl.pallas_call(
        body,
        name=name,
        grid=(rows // tr,),
        in_specs=[pl.BlockSpec((N_DEV, tr, cols), lambda i: (0, i, 0))],
        out_specs=pl.BlockSpec((tr, cols), lambda i: (i, 0)),
        out_shape=jax.ShapeDtypeStruct((rows, cols), F32),
    )(parts)


def _mm_tn(a, b, out_dtype, name):
    rows, m = a.shape
    n = b.shape[1]
    tm = m // 2 if (m // 2) % 128 == 0 and m > 512 else m
    tr = _row_tile(rows)
    nk = rows // tr

    def body(a_ref, b_ref, o_ref, acc):
        k = pl.program_id(1)
        part = lax.dot_general(a_ref[...].astype(BF16), b_ref[...].astype(BF16), TN, preferred_element_type=F32)

        @pl.when(k == 0)
        def _():
            acc[...] = part

        @pl.when(k > 0)
        def _():
            acc[...] += part

        @pl.when(k == nk - 1)
        def _():
            o_ref[...] = acc[...].astype(o_ref.dtype)

    return pl.pallas_call(
        body,
        name=name,
        grid=(m // tm, nk),
        in_specs=[pl.BlockSpec((tr, tm), lambda i, k: (k, i)), pl.BlockSpec((tr, n), lambda i, k: (k, 0))],
        out_specs=pl.BlockSpec((tm, n), lambda i, k: (i, 0)),
        out_shape=jax.ShapeDtypeStruct((m, n), out_dtype),
        scratch_shapes=[pltpu.VMEM((tm, n), F32)],
        compiler_params=_params(VMEM_BIG),
    )(a, b)


def _transpose_to_bf16(w, name):
    layers, a_dim, b_dim = w.shape

    def body(w_ref, o_ref):
        eye = (lax.broadcasted_iota(jnp.int32, (a_dim, a_dim), 0) == lax.broadcasted_iota(jnp.int32, (a_dim, a_dim), 1)).astype(BF16)
        o_ref[0] = lax.dot_general(w_ref[0].astype(BF16), eye, TN, preferred_element_type=F32).astype(BF16)

    return pl.pallas_call(
        body,
        name=name,
        grid=(layers,),
        in_specs=[pl.BlockSpec((1, a_dim, b_dim), lambda l: (l, 0, 0))],
        out_specs=pl.BlockSpec((1, b_dim, a_dim), lambda l: (l, 0, 0)),
        out_shape=jax.ShapeDtypeStruct((layers, b_dim, a_dim), BF16),
    )(w)


def _transpose_f32(g, name):
    a_dim, b_dim = g.shape

    def body(g_ref, o_ref):
        eye = (lax.broadcasted_iota(jnp.int32, (a_dim, a_dim), 0) == lax.broadcasted_iota(jnp.int32, (a_dim, a_dim), 1)).astype(BF16)
        v = g_ref[...]
        hi = v.astype(BF16)
        r1 = v - hi.astype(F32)
        mid = r1.astype(BF16)
        lo = (r1 - mid.astype(F32)).astype(BF16)
        out = lax.dot_general(hi, eye, TN, preferred_element_type=F32)
        out = out + lax.dot_general(mid, eye, TN, preferred_element_type=F32)
        out = out + lax.dot_general(lo, eye, TN, preferred_element_type=F32)
        o_ref[...] = out

    return pl.pallas_call(
        body,
        name=name,
        in_specs=[_full((a_dim, b_dim))],
        out_specs=_full((b_dim, a_dim)),
        out_shape=jax.ShapeDtypeStruct((b_dim, a_dim), F32),
        grid=(1,),
    )(g)


def _proj_fwd(h, w_in_t, name):
    rows, d = h.shape
    n = w_in_t.shape[0]
    tm = _row_tile(rows)

    def body(h_ref, w_ref, o_ref):
        o_ref[...] = lax.dot_general(h_ref[...].astype(BF16), w_ref[...], NT, preferred_element_type=F32)

    return pl.pallas_call(
        body,
        name=name,
        grid=(rows // tm,),
        in_specs=[pl.BlockSpec((tm, d), lambda i: (i, 0)), _full((n, d))],
        out_specs=pl.BlockSpec((tm, n), lambda i: (i, 0)),
        out_shape=jax.ShapeDtypeStruct((rows, n), F32),
        compiler_params=_params(VMEM_BIG),
    )(h, w_in_t)


def _proj_bwd(dproj, w_in_t, dz, alpha, name):
    rows, n = dproj.shape
    d = w_in_t.shape[1]
    tm = _row_tile(rows)

    def body(dp_ref, w_ref, dz_ref, o_ref):
        o_ref[...] = alpha * dz_ref[...] + jnp.dot(dp_ref[...], w_ref[...], preferred_element_type=F32)

    return pl.pallas_call(
        body,
        name=name,
        grid=(rows // tm,),
        in_specs=[pl.BlockSpec((tm, n), lambda i: (i, 0)), _full((n, d)), pl.BlockSpec((tm, d), lambda i: (i, 0))],
        out_specs=pl.BlockSpec((tm, d), lambda i: (i, 0)),
        out_shape=jax.ShapeDtypeStruct((rows, d), F32),
        compiler_params=_params(VMEM_BIG),
    )(dproj, w_in_t, dz)


FFN_CHUNKS = 2


def _ffn_fwd(h, wg_t, wu_t, wd, ln_g, ln_b, alpha, name):
    rows, d = h.shape
    f = wd.shape[0]
    tm = _row_tile(rows)
    tf = f // FFN_CHUNKS

    def body(h_ref, wg_hbm, wu_hbm, wd_hbm, lg_ref, lb_ref, ho_ref, g_ref, u_ref, z_ref, wg, wu, wdv, sem):
        _load_resident(((wg_hbm, wg), (wu_hbm, wu), (wd_hbm, wdv)), sem)
        x = h_ref[...]
        xb = x.astype(BF16)
        y = None
        for c in range(FFN_CHUNKS):
            sl = slice(c * tf, (c + 1) * tf)
            g = lax.dot_general(xb, wg[sl, :], NT, preferred_element_type=F32)
            u = lax.dot_general(xb, wu[sl, :], NT, preferred_element_type=F32)
            g_ref[:, sl] = g
            u_ref[:, sl] = u
            a = (g * _sigmoid(g) * u).astype(BF16)
            part = jnp.dot(a, wdv[sl, :], preferred_element_type=F32)
            y = part if y is None else y + part
        z = alpha * x + 0.5 * y
        z_ref[...] = z
        xhat, _ = _ln_stats(z)
        ho_ref[...] = xhat * lg_ref[...] + lb_ref[...]

    row_d = pl.BlockSpec((tm, d), lambda i: (i, 0))
    row_f = pl.BlockSpec((tm, f), lambda i: (i, 0))
    return pl.pallas_call(
        body,
        name=name,
        grid=(rows // tm,),
        in_specs=[row_d, ANY, ANY, ANY, _full((1, d)), _full((1, d))],
        out_specs=[row_d, row_f, row_f, row_d],
        out_shape=[
            jax.ShapeDtypeStruct((rows, d), F32),
            jax.ShapeDtypeStruct((rows, f), F32),
            jax.ShapeDtypeStruct((rows, f), F32),
            jax.ShapeDtypeStruct((rows, d), F32),
        ],
        scratch_shapes=[pltpu.VMEM((f, d), BF16), pltpu.VMEM((f, d), BF16), pltpu.VMEM((f, d), BF16), pltpu.SemaphoreType.DMA((3,))],
        compiler_params=_params(VMEM_BIG),
    )(h, wg_t, wu_t, wd, ln_g, ln_b)


def _ffn_bwd(dh, z, g, u, wg_t, wu_t, wd, ln_g, alpha, name):
    rows, d = dh.shape
    f = wd.shape[0]
    tm = _row_tile(rows)
    tf = f // FFN_CHUNKS

    def body(dh_ref, z_ref, g_ref, u_ref, wg_hbm, wu_hbm, wd_hbm, lg_ref, dx_ref, dg_ref, du_ref, a_ref, dyb_ref, dgam_ref, dbet_ref, wg, wu, wdv, sem):
        _load_resident(((wg_hbm, wg), (wu_hbm, wu), (wd_hbm, wdv)), sem)

        @pl.when(pl.program_id(0) == 0)
        def _():
            dgam_ref[...] = jnp.zeros_like(dgam_ref)
            dbet_ref[...] = jnp.zeros_like(dbet_ref)

        dz, dgam, dbet = _ln_bwd(dh_ref[...], z_ref[...], lg_ref[...])
        dgam_ref[...] += dgam
        dbet_ref[...] += dbet
        dyb = (0.5 * dz).astype(BF16)
        dyb_ref[...] = dyb
        dx = alpha * dz
        for c in range(FFN_CHUNKS):
            sl = slice(c * tf, (c + 1) * tf)
            da = lax.dot_general(dyb, wdv[sl, :], NT, preferred_element_type=F32)
            gc = g_ref[:, sl]
            uc = u_ref[:, sl]
            sg = _sigmoid(gc)
            silu = gc * sg
            dgc = (da * uc * (sg * (1.0 + gc * (1.0 - sg)))).astype(BF16)
            duc = (da * silu).astype(BF16)
            a_ref[:, sl] = (silu * uc).astype(BF16)
            dg_ref[:, sl] = dgc
            du_ref[:, sl] = duc
            dx = dx + jnp.dot(dgc, wg[sl, :], preferred_element_type=F32) + jnp.dot(duc, wu[sl, :], preferred_element_type=F32)
        dx_ref[...] = dx

    row_d = pl.BlockSpec((tm, d), lambda i: (i, 0))
    row_f = pl.BlockSpec((tm, f), lambda i: (i, 0))
    return pl.pallas_call(
        body,
        name=name,
        grid=(rows // tm,),
        in_specs=[row_d, row_d, row_f, row_f, ANY, ANY, ANY, _full((1, d))],
        out_specs=[row_d, row_f, row_f, row_f, row_d, _full((1, d)), _full((1, d))],
        out_shape=[
            jax.ShapeDtypeStruct((rows, d), F32),
            jax.ShapeDtypeStruct((rows, f), BF16),
            jax.ShapeDtypeStruct((rows, f), BF16),
            jax.ShapeDtypeStruct((rows, f), BF16),
            jax.ShapeDtypeStruct((rows, d), BF16),
            jax.ShapeDtypeStruct((1, d), F32),
            jax.ShapeDtypeStruct((1, d), F32),
        ],
        scratch_shapes=[pltpu.VMEM((f, d), BF16), pltpu.VMEM((f, d), BF16), pltpu.VMEM((f, d), BF16), pltpu.SemaphoreType.DMA((3,))],
        compiler_params=_params(VMEM_BIG),
    )(dh, z, g, u, wg_t, wu_t, wd, ln_g)


LANES = 128


def _gates(xc, wa_ref, ba, wx_ref, bx, lam):
    xcb = xc.astype(BF16)
    r = _sigmoid(jnp.dot(xcb, wa_ref[0], preferred_element_type=F32) + ba)
    i = _sigmoid(jnp.dot(xcb, wx_ref[0], preferred_element_type=F32) + bx)
    sp = _softplus(-lam)
    log_a = -LRU_C * sp * r
    a = jnp.exp(log_a)
    s = jnp.sqrt(-_expm1_nonpos(2.0 * log_a))
    return xcb, r, i, sp, a, s


def _conv(xr, cw, cb):
    return cb + cw[3:4] * xr + cw[2:3] * pltpu.roll(xr, 1, 0) + cw[1:2] * pltpu.roll(xr, 2, 0) + cw[0:1] * pltpu.roll(xr, 3, 0)


def _rglru_fwd(proj, cw, cb, wa, ba, wx, bx, lam, t_len, pad, name):
    rows = proj.shape[0]
    n_seq = rows // t_len
    n_tile = REC_WIDTH // LANES
    groups = t_len // 8

    def body(xr_ref, gt_ref, cw_ref, cb_ref, wa_ref, ba_ref, wx_ref, bx_ref, lam_ref, xc_ref, h_ref, y_ref, a_s, b_s):
        valid = lax.broadcasted_iota(jnp.int32, (t_len, 1), 0) >= pad
        xr = jnp.where(valid, xr_ref[...], 0.0)
        xc = _conv(xr, cw_ref[...], cb_ref[...])
        xc_ref[...] = xc
        _, _, i, _, a, s = _gates(xc, wa_ref, ba_ref[...], wx_ref, bx_ref[...], lam_ref[...])
        a_s[...] = a
        b_s[...] = jnp.where(valid, s * (i * xc), 0.0)
        sub = lax.broadcasted_iota(jnp.int32, (8, LANES), 0)

        def step(gi, carry):
            r0 = pl.multiple_of(gi * 8, 8)
            av = a_s[pl.ds(r0, 8), :]
            bv = b_s[pl.ds(r0, 8), :]
            for d in (1, 2, 4):
                m = sub >= d
                a_sh = pltpu.roll(av, d, 0)
                b_sh = pltpu.roll(bv, d, 0)
                bv = jnp.where(m, av * b_sh + bv, bv)
                av = jnp.where(m, av * a_sh, av)
            hv = av * carry + bv
            h_ref[pl.ds(r0, 8), :] = hv
            return jnp.broadcast_to(hv[7:8, :], (8, LANES))

        lax.fori_loop(0, groups, step, jnp.zeros((8, LANES), F32), unroll=4)
        gelu, _ = _gelu_and_grad(gt_ref[...])
        y_ref[...] = h_ref[...] * gelu

    seq_tile = lambda col0: pl.BlockSpec((t_len, LANES), lambda s, j: (s, col0 + j))
    vec = pl.BlockSpec((1, LANES), lambda s, j: (0, j))
    out = jax.ShapeDtypeStruct((rows, REC_WIDTH), F32)
    return pl.pallas_call(
        body,
        name=name,
        grid=(n_seq, n_tile),
        in_specs=[
            seq_tile(0),
            seq_tile(n_tile),
            pl.BlockSpec((CONV_WIDTH, LANES), lambda s, j: (0, j)),
            vec,
            pl.BlockSpec((1, LANES, LANES), lambda s, j: (j, 0, 0)),
            vec,
            pl.BlockSpec((1, LANES, LANES), lambda s, j: (j, 0, 0)),
            vec,
            vec,
        ],
        out_specs=[seq_tile(0)] * 3,
        out_shape=[out, out, out],
        scratch_shapes=[pltpu.VMEM((t_len, LANES), F32), pltpu.VMEM((t_len, LANES), F32)],
        compiler_params=_params(VMEM_BIG),
    )(proj, proj, cw, cb, wa, ba, wx, bx, lam)


def _rglru_bwd(dy, h, proj, xc, cw, wa, ba, wx, bx, lam, t_len, pad, name):
    rows = proj.shape[0]
    n_seq = rows // t_len
    n_tile = REC_WIDTH // LANES
    groups = t_len // 8

    def body(dy_ref, h_ref, xr_ref, gt_ref, xc_ref, cw_ref, wa_ref, ba_ref, wx_ref, bx_ref, lam_ref,
             dxr_ref, dgt_ref, dcw_ref, dcb_ref, dwa_ref, dba_ref, dwx_ref, dbx_ref, dlam_ref, a_s, c_s, db_s, da_s):
        first_seq = pl.program_id(1) == 0

        @pl.when(first_seq)
        def _():
            for ref in (dcw_ref, dcb_ref, dwa_ref, dba_ref, dwx_ref, dbx_ref, dlam_ref):
                ref[...] = jnp.zeros_like(ref)

        valid = lax.broadcasted_iota(jnp.int32, (t_len, 1), 0) >= pad
        gelu, dgelu = _gelu_and_grad(gt_ref[...])
        dyv = dy_ref[...]
        dho = dyv * gelu
        dgt_ref[...] = (dyv * h_ref[...] * dgelu).astype(BF16)
        xc = xc_ref[...]
        lam = lam_ref[...]
        xcb, r, i, sp, a, s = _gates(xc, wa_ref, ba_ref[...], wx_ref, bx_ref[...], lam)
        a_s[...] = a
        c_s[...] = a * dho
        db_s[...] = dho
        sub = lax.broadcasted_iota(jnp.int32, (8, LANES), 0)

        def step(k, carry):
            gi = groups - 1 - k
            r0 = pl.multiple_of(gi * 8, 8)
            av = a_s[pl.ds(r0, 8), :]
            cv = c_s[pl.ds(r0, 8), :]
            for d in (1, 2, 4):
                m = sub < 8 - d
                a_sh = pltpu.roll(av, 8 - d, 0)
                c_sh = pltpu.roll(cv, 8 - d, 0)
                cv = jnp.where(m, av * c_sh + cv, cv)
                av = jnp.where(m, av * a_sh, av)
            ev = av * carry + cv
            e_next = jnp.where(sub < 7, pltpu.roll(ev, 7, 0), carry)
            dht = db_s[pl.ds(r0, 8), :] + e_next
            hv = h_ref[pl.ds(r0, 8), :]
            rp = pl.multiple_of(jnp.maximum(gi - 1, 0) * 8, 8)
            h_before = jnp.where(gi > 0, jnp.broadcast_to(h_ref[pl.ds(rp, 8), :][7:8, :], (8, LANES)), 0.0)
            h_prev = jnp.where(sub >= 1, pltpu.roll(hv, 1, 0), h_before)
            db_s[pl.ds(r0, 8), :] = dht
            da_s[pl.ds(r0, 8), :] = dht * h_prev
            return jnp.broadcast_to(ev[0:1, :], (8, LANES))

        lax.fori_loop(0, groups, step, jnp.zeros((8, LANES), F32), unroll=4)

        db = jnp.where(valid, db_s[...], 0.0)
        da = da_s[...]
        ds = db * (i * xc)
        di = db * s * xc
        dxc = db * s * i
        dlog_a = da * a - ds * (a * a) / jnp.maximum(s, 1e-30)
        dr = dlog_a * (-LRU_C * sp)
        dsp = jnp.sum(dlog_a * (-LRU_C) * r, axis=0, keepdims=True)
        dlam_ref[...] += dsp * (-_sigmoid(-lam))
        dpr = dr * r * (1.0 - r)
        dpi = di * i * (1.0 - i)
        dprb = dpr.astype(BF16)
        dpib = dpi.astype(BF16)
        dxc = dxc + lax.dot_general(dprb, wa_ref[0], NT, preferred_element_type=F32) + lax.dot_general(dpib, wx_ref[0], NT, preferred_element_type=F32)
        dwa_ref[0] += lax.dot_general(xcb, dprb, TN, preferred_element_type=F32)
        dwx_ref[0] += lax.dot_general(xcb, dpib, TN, preferred_element_type=F32)
        dba_ref[...] += jnp.sum(dpr, axis=0, keepdims=True)
        dbx_ref[...] += jnp.sum(dpi, axis=0, keepdims=True)
        dxc = jnp.where(valid, dxc, 0.0)
        xr = jnp.where(valid, xr_ref[...], 0.0)
        dcb_ref[...] += jnp.sum(dxc, axis=0, keepdims=True)
        for k in range(CONV_WIDTH):
            shifted = xr if k == CONV_WIDTH - 1 else pltpu.roll(xr, CONV_WIDTH - 1 - k, 0)
            dcw_ref[k : k + 1, :] += jnp.sum(dxc * shifted, axis=0, keepdims=True)
        cw = cw_ref[...]
        dxr = cw[3:4] * dxc + cw[2:3] * pltpu.roll(dxc, t_len - 1, 0) + cw[1:2] * pltpu.roll(dxc, t_len - 2, 0) + cw[0:1] * pltpu.roll(dxc, t_len - 3, 0)
        dxr_ref[...] = jnp.where(valid, dxr, 0.0).astype(BF16)

    seq_tile = lambda col0: pl.BlockSpec((t_len, LANES), lambda j, s: (s, col0 + j))
    vec = pl.BlockSpec((1, LANES), lambda j, s: (0, j))
    mat = pl.BlockSpec((1, LANES, LANES), lambda j, s: (j, 0, 0))
    cwb = pl.BlockSpec((CONV_WIDTH, LANES), lambda j, s: (0, j))
    big = jax.ShapeDtypeStruct((rows, REC_WIDTH), BF16)
    vec_shape = jax.ShapeDtypeStruct((1, REC_WIDTH), F32)
    mat_shape = jax.ShapeDtypeStruct((n_tile, LANES, LANES), F32)
    return pl.pallas_call(
        body,
        name=name,
        grid=(n_tile, n_seq),
        in_specs=[seq_tile(0), seq_tile(0), seq_tile(0), seq_tile(n_tile), seq_tile(0), cwb, mat, vec, mat, vec, vec],
        out_specs=[seq_tile(0), seq_tile(0), cwb, vec, mat, vec, mat, vec, vec],
        out_shape=[big, big, jax.ShapeDtypeStruct((CONV_WIDTH, REC_WIDTH), F32), vec_shape, mat_shape, vec_shape, mat_shape, vec_shape, vec_shape],
        scratch_shapes=[pltpu.VMEM((t_len, LANES), F32)] * 4,
        compiler_params=_params(VMEM_BIG),
    )(dy, h, proj, proj, xc, cw, wa, ba, wx, bx, lam)


QKV_WIDTH = ATTN_WIDTH + 2 * KV_WIDTH


def _rotate(v, cos, sin_up, sin_down):
    width = v.shape[1]
    return v * cos + pltpu.roll(v, width - ROPE_DIM // 2, 1) * sin_up + pltpu.roll(v, ROPE_DIM // 2, 1) * sin_down


def _rope_fwd(proj, cos, sin_up, sin_down, t_len, name):
    rows = proj.shape[0]
    nb = t_len // BLOCK
    n_seq = rows // t_len
    q0 = 2 * REC_WIDTH

    def body(q_ref, k_ref, v_ref, cos_ref, up_ref, down_ref, o_ref):
        cos_t, up_t, down_t = cos_ref[...], up_ref[...], down_ref[...]
        o_ref[:, 0:ATTN_WIDTH] = _rotate(q_ref[...], cos_t, up_t, down_t).astype(BF16)
        o_ref[:, ATTN_WIDTH : ATTN_WIDTH + KV_WIDTH] = _rotate(k_ref[...], cos_t[:, :KV_WIDTH], up_t[:, :KV_WIDTH], down_t[:, :KV_WIDTH]).astype(BF16)
        o_ref[:, ATTN_WIDTH + KV_WIDTH :] = v_ref[...].astype(BF16)

    tab = pl.BlockSpec((BLOCK, ATTN_WIDTH), lambda s, n: (n, 0))
    return pl.pallas_call(
        body,
        name=name,
        grid=(n_seq, nb),
        in_specs=[
            pl.BlockSpec((BLOCK, ATTN_WIDTH), lambda s, n: (s * nb + n, q0 // ATTN_WIDTH)),
            pl.BlockSpec((BLOCK, KV_WIDTH), lambda s, n: (s * nb + n, (q0 + ATTN_WIDTH) // KV_WIDTH)),
            pl.BlockSpec((BLOCK, KV_WIDTH), lambda s, n: (s * nb + n, (q0 + ATTN_WIDTH) // KV_WIDTH + 1)),
            tab,
            tab,
            tab,
        ],
        out_specs=pl.BlockSpec((BLOCK, QKV_WIDTH), lambda s, n: (s * nb + n, 0)),
        out_shape=jax.ShapeDtypeStruct((rows, QKV_WIDTH), BF16),
    )(proj, proj, proj, cos, sin_up, sin_down)


def _rope_bwd(dq, dk, dv, cos, sin_up, sin_down, t_len, name):
    rows = dq.shape[0]
    nb = t_len // BLOCK
    n_seq = rows // t_len

    def body(dq_ref, dk_ref, dv_ref, cos_ref, up_ref, down_ref, o_ref):
        cos_t, up_t, down_t = cos_ref[...], -up_ref[...], -down_ref[...]
        o_ref[:, 0:ATTN_WIDTH] = _rotate(dq_ref[...], cos_t, up_t, down_t).astype(BF16)
        o_ref[:, ATTN_WIDTH : ATTN_WIDTH + KV_WIDTH] = _rotate(dk_ref[...], cos_t[:, :KV_WIDTH], up_t[:, :KV_WIDTH], down_t[:, :KV_WIDTH]).astype(BF16)
        o_ref[:, ATTN_WIDTH + KV_WIDTH :] = dv_ref[...].astype(BF16)

    tab = pl.BlockSpec((BLOCK, ATTN_WIDTH), lambda s, n: (n, 0))
    blk = lambda w: pl.BlockSpec((BLOCK, w), lambda s, n: (s * nb + n, 0))
    return pl.pallas_call(
        body,
        name=name,
        grid=(n_seq, nb),
        in_specs=[blk(ATTN_WIDTH), blk(KV_WIDTH), blk(KV_WIDTH), tab, tab, tab],
        out_specs=blk(QKV_WIDTH),
        out_shape=jax.ShapeDtypeStruct((rows, QKV_WIDTH), BF16),
    )(dq, dk, dv, cos, sin_up, sin_down)


def _attn_mask(n, pad):
    q_pos = n * BLOCK + lax.broadcasted_iota(jnp.int32, (BLOCK, 1), 0) - pad
    col = lax.broadcasted_iota(jnp.int32, (1, 3 * BLOCK), 1)
    first = col < BLOCK
    k_pos = col - pad + jnp.where(first, 0, (n - 2) * BLOCK)
    dist = q_pos - k_pos
    band = (dist >= 0) & (dist < WINDOW) & (k_pos >= N_META) & jnp.logical_not(first)
    meta = (k_pos >= 0) & (k_pos < N_META) & (k_pos <= q_pos) & first
    return band | meta


def _in_half(e):
    lane = lax.broadcasted_iota(jnp.int32, (1, BLOCK), 1)
    return lane >= HEAD_DIM if e else lane < HEAD_DIM


def _head_views(q2, e, g):
    v = jnp.where(_in_half(e), q2, 0.0)
    return pltpu.roll(v, HEAD_DIM, 1) if e != g else v


def _softmax_with_sink(scores, mask, sink):
    s = jnp.where(mask, scores * (HEAD_DIM**-0.5), NEG_INF)
    m = jnp.maximum(jnp.max(s, axis=-1, keepdims=True), sink)
    p = jnp.exp(s - m)
    p_sink = jnp.exp(sink - m)
    inv = 1.0 / (jnp.sum(p, axis=-1, keepdims=True) + p_sink)
    return p * inv, p_sink * inv


def _kv_specs(nb):
    k_col = ATTN_WIDTH // KV_WIDTH
    specs = []
    for col in (k_col, k_col + 1):
        specs += [
            pl.BlockSpec((BLOCK, KV_WIDTH), lambda s, n, col=col: (s * nb, col)),
            pl.BlockSpec((BLOCK, KV_WIDTH), lambda s, n, col=col: (s * nb + jnp.maximum(n - 1, 0), col)),
            pl.BlockSpec((BLOCK, KV_WIDTH), lambda s, n, col=col: (s * nb + n, col)),
        ]
    return specs


def _attn_fwd(qkv, sinks, t_len, pad, name):
    rows = qkv.shape[0]
    nb = t_len // BLOCK
    n_seq = rows // t_len

    def body(q_ref, km_ref, kp_ref, kc_ref, vm_ref, vp_ref, vc_ref, sink_ref, o_ref):
        n = pl.program_id(1)
        mask = _attn_mask(n, pad)
        keys = jnp.concatenate([km_ref[...], kp_ref[...], kc_ref[...]], axis=0)
        vals = jnp.concatenate([vm_ref[...], vp_ref[...], vc_ref[...]], axis=0)
        for j in range(ATTN_WIDTH // BLOCK):
            q2 = q_ref[:, j * BLOCK : (j + 1) * BLOCK].astype(F32)
            g = j // 2
            tile = jnp.zeros((BLOCK, BLOCK), F32)
            for e in range(2):
                head = 2 * j + e
                qh = _head_views(q2, e, g).astype(BF16)
                scores = lax.dot_general(qh, keys, NT, preferred_element_type=F32)
                p, _ = _softmax_with_sink(scores, mask, sink_ref[head : head + 1, 0:1])
                o = jnp.dot(p.astype(BF16), vals, preferred_element_type=F32)
                o = jnp.where(_in_half(g), o, 0.0)
                tile = tile + (pltpu.roll(o, HEAD_DIM, 1) if e != g else o)
            o_ref[:, j * BLOCK : (j + 1) * BLOCK] = tile

    return pl.pallas_call(
        body,
        name=name,
        grid=(n_seq, nb),
        in_specs=[pl.BlockSpec((BLOCK, ATTN_WIDTH), lambda s, n: (s * nb + n, 0))] + _kv_specs(nb) + [_full((N_Q_HEADS, BLOCK))],
        out_specs=pl.BlockSpec((BLOCK, ATTN_WIDTH), lambda s, n: (s * nb + n, 0)),
        out_shape=jax.ShapeDtypeStruct((rows, ATTN_WIDTH), F32),
    )(qkv, qkv, qkv, qkv, qkv, qkv, qkv, sinks)


def _attn_bwd(qkv, sinks, o, do, t_len, pad, name):
    rows = qkv.shape[0]
    nb = t_len // BLOCK
    n_seq = rows // t_len

    def body(q_ref, km_ref, kp_ref, kc_ref, vm_ref, vp_ref, vc_ref, sink_ref, o_ref, do_ref, dq_ref, dk_ref, dv_ref, dsink_ref):
        s_id, n = pl.program_id(0), pl.program_id(1)

        @pl.when(n == 0)
        def _():
            dk_ref[...] = jnp.zeros_like(dk_ref)
            dv_ref[...] = jnp.zeros_like(dv_ref)

        @pl.when((n == 0) & (s_id == 0))
        def _():
            dsink_ref[...] = jnp.zeros_like(dsink_ref)

        mask = _attn_mask(n, pad)
        keys = jnp.concatenate([km_ref[...], kp_ref[...], kc_ref[...]], axis=0)
        vals = jnp.concatenate([vm_ref[...], vp_ref[...], vc_ref[...]], axis=0)
        dkeys = jnp.zeros((3 * BLOCK, KV_WIDTH), F32)
        dvals = jnp.zeros((3 * BLOCK, KV_WIDTH), F32)
        for j in range(ATTN_WIDTH // BLOCK):
            sl = slice(j * BLOCK, (j + 1) * BLOCK)
            q2 = q_ref[:, sl].astype(F32)
            o2 = o_ref[:, sl]
            do2 = do_ref[:, sl]
            g = j // 2
            dq_tile = jnp.zeros((BLOCK, BLOCK), F32)
            for e in range(2):
                head = 2 * j + e
                qh = _head_views(q2, e, g).astype(BF16)
                scores = lax.dot_general(qh, keys, NT, preferred_element_type=F32)
                p, p_sink = _softmax_with_sink(scores, mask, sink_ref[head : head + 1, 0:1])
                do_head = jnp.where(_in_half(e), do2, 0.0)
                delta = jnp.sum(do_head * o2, axis=-1, keepdims=True)
                doh = (pltpu.roll(do_head, HEAD_DIM, 1) if e != g else do_head).astype(BF16)
                dp = lax.dot_general(doh, vals, NT, preferred_element_type=F32)
                ds = (p * (dp - delta) * (HEAD_DIM**-0.5)).astype(BF16)
                dsink_ref[head : head + 1, :] += jnp.broadcast_to(-jnp.sum(p_sink * delta, axis=0, keepdims=True), (1, BLOCK))
                dqh = jnp.where(_in_half(g), jnp.dot(ds, keys, preferred_element_type=F32), 0.0)
                dq_tile = dq_tile + (pltpu.roll(dqh, HEAD_DIM, 1) if e != g else dqh)
                dkeys = dkeys + lax.dot_general(ds, qh, TN, preferred_element_type=F32)
                dvals = dvals + lax.dot_general(p.astype(BF16), doh, TN, preferred_element_type=F32)
            dq_ref[:, sl] = dq_tile
        r_prev = pl.multiple_of(jnp.maximum(n - 1, 0) * BLOCK, BLOCK)
        r_cur = pl.multiple_of(n * BLOCK, BLOCK)
        for acc, d in ((dk_ref, dkeys), (dv_ref, dvals)):
            acc[0:BLOCK, :] += d[0:BLOCK]
            acc[pl.ds(r_prev, BLOCK), :] += d[BLOCK : 2 * BLOCK]
            acc[pl.ds(r_cur, BLOCK), :] += d[2 * BLOCK :]

    q_blk = pl.BlockSpec((BLOCK, ATTN_WIDTH), lambda s, n: (s * nb + n, 0))
    seq_kv = pl.BlockSpec((t_len, KV_WIDTH), lambda s, n: (s, 0))
    return pl.pallas_call(
        body,
        name=name,
        grid=(n_seq, nb),
        in_specs=[q_blk] + _kv_specs(nb) + [_full((N_Q_HEADS, BLOCK)), q_blk, q_blk],
        out_specs=[q_blk, seq_kv, seq_kv, _full((N_Q_HEADS, BLOCK))],
        out_shape=[
            jax.ShapeDtypeStruct((rows, ATTN_WIDTH), F32),
            jax.ShapeDtypeStruct((rows, KV_WIDTH), F32),
            jax.ShapeDtypeStruct((rows, KV_WIDTH), F32),
            jax.ShapeDtypeStruct((N_Q_HEADS, BLOCK), F32),
        ],
    )(qkv, qkv, qkv, qkv, qkv, qkv, qkv, sinks, o, do)


def _mix_out_fwd(y_rec, y_attn, h, w_out, g_rec, g_attn, ln_g, ln_b, alpha, name):
    rows, d = h.shape
    tm = _row_tile(rows)

    def body(yr_ref, ya_ref, h_ref, w_ref, gr_ref, ga_ref, lg_ref, lb_ref, ho_ref, z_ref):
        nr = _rms(yr_ref[...], gr_ref[...]).astype(BF16)
        na = _rms(ya_ref[...], ga_ref[...]).astype(BF16)
        m = jnp.dot(nr, w_ref[0:REC_WIDTH, :], preferred_element_type=F32) + jnp.dot(na, w_ref[REC_WIDTH:, :], preferred_element_type=F32)
        z = alpha * h_ref[...] + m
        z_ref[...] = z
        xhat, _ = _ln_stats(z)
        ho_ref[...] = xhat * lg_ref[...] + lb_ref[...]

    row_d = pl.BlockSpec((tm, d), lambda i: (i, 0))
    row_h = pl.BlockSpec((tm, REC_WIDTH), lambda i: (i, 0))
    return pl.pallas_call(
        body,
        name=name,
        grid=(rows // tm,),
        in_specs=[row_h, row_h, row_d, _full(w_out.shape), _full((1, REC_WIDTH)), _full((1, ATTN_WIDTH)), _full((1, d)), _full((1, d))],
        out_specs=[row_d, row_d],
        out_shape=[jax.ShapeDtypeStruct((rows, d), F32)] * 2,
    )(y_rec, y_attn, h, w_out, g_rec, g_attn, ln_g, ln_b)


def _mix_out_bwd(dh, z, y_rec, y_attn, w_out, g_rec, g_attn, ln_g, name):
    rows, d = dh.shape
    tm = _row_tile(rows)
    mix = REC_WIDTH + ATTN_WIDTH

    def body(dh_ref, z_ref, yr_ref, ya_ref, w_ref, gr_ref, ga_ref, lg_ref, dz_ref, dzb_ref, dyr_ref, dya_ref, yn_ref, dgam_ref, dbet_ref, dgr_ref, dga_ref):
        @pl.when(pl.program_id(0) == 0)
        def _():
            for ref in (dgam_ref, dbet_ref, dgr_ref, dga_ref):
                ref[...] = jnp.zeros_like(ref)

        dz, dgam, dbet = _ln_bwd(dh_ref[...], z_ref[...], lg_ref[...])
        dgam_ref[...] += dgam
        dbet_ref[...] += dbet
        dz_ref[...] = dz
        dzb = dz.astype(BF16)
        dzb_ref[...] = dzb
        dyn = lax.dot_general(dzb, w_ref[...], NT, preferred_element_type=F32)
        dyr, dgr, nr = _rms_bwd(dyn[:, 0:REC_WIDTH], yr_ref[...], gr_ref[...])
        dya, dga, na = _rms_bwd(dyn[:, REC_WIDTH:], ya_ref[...], ga_ref[...])
        dyr_ref[...] = dyr
        dya_ref[...] = dya
        dgr_ref[...] += dgr
        dga_ref[...] += dga
        yn_ref[:, 0:REC_WIDTH] = nr.astype(BF16)
        yn_ref[:, REC_WIDTH:] = na.astype(BF16)

    row_d = pl.BlockSpec((tm, d), lambda i: (i, 0))
    row_h = pl.BlockSpec((tm, REC_WIDTH), lambda i: (i, 0))
    row_m = pl.BlockSpec((tm, mix), lambda i: (i, 0))
    return pl.pallas_call(
        body,
        name=name,
        grid=(rows // tm,),
        in_specs=[row_d, row_d, row_h, row_h, _full(w_out.shape), _full((1, REC_WIDTH)), _full((1, ATTN_WIDTH)), _full((1, d))],
        out_specs=[row_d, row_d, row_h, row_h, row_m, _full((1, d)), _full((1, d)), _full((1, REC_WIDTH)), _full((1, ATTN_WIDTH))],
        out_shape=[
            jax.ShapeDtypeStruct((rows, d), F32),
            jax.ShapeDtypeStruct((rows, d), BF16),
            jax.ShapeDtypeStruct((rows, REC_WIDTH), F32),
            jax.ShapeDtypeStruct((rows, ATTN_WIDTH), F32),
            jax.ShapeDtypeStruct((rows, mix), BF16),
            jax.ShapeDtypeStruct((1, d), F32),
            jax.ShapeDtypeStruct((1, d), F32),
            jax.ShapeDtypeStruct((1, REC_WIDTH), F32),
            jax.ShapeDtypeStruct((1, ATTN_WIDTH), F32),
        ],
    )(dh, z, y_rec, y_attn, w_out, g_rec, g_attn, ln_g)


def _loss_head(y, target, t_len, first_token, name):
    rows, d = y.shape
    tm = _row_tile(rows)

    def body(y_ref, t_ref, loss_ref, dy_ref):
        i = pl.program_id(0)

        @pl.when(i == 0)
        def _():
            loss_ref[...] = jnp.zeros_like(loss_ref)

        row = i * tm + lax.broadcasted_iota(jnp.int32, (tm, 1), 0)
        is_token = lax.rem(row, t_len) >= first_token
        err = jnp.where(is_token, y_ref[...] - t_ref[...], 0.0)
        dy_ref[...] = err / d
        per_row = jnp.sum(err * err, axis=-1, keepdims=True) / d
        loss_ref[...] += jnp.broadcast_to(0.5 * jnp.sum(per_row, axis=0, keepdims=True), (1, BLOCK))

    row_d = pl.BlockSpec((tm, d), lambda i: (i, 0))
    return pl.pallas_call(
        body,
        name=name,
        grid=(rows // tm,),
        in_specs=[row_d, row_d],
        out_specs=[_full((1, BLOCK)), row_d],
        out_shape=[jax.ShapeDtypeStruct((1, BLOCK), F32), jax.ShapeDtypeStruct((rows, d), F32)],
    )(y, target)


def _meta_grad(dh0, t_len, pad, name):
    rows, d = dh0.shape
    nb = t_len // BLOCK
    n_seq = rows // t_len

    def body(dh_ref, o_ref):
        @pl.when(pl.program_id(0) == 0)
        def _():
            o_ref[...] = jnp.zeros_like(o_ref)

        o_ref[...] += dh_ref[pad : pad + N_META, :]

    return pl.pallas_call(
        body,
        name=name,
        grid=(n_seq,),
        in_specs=[pl.BlockSpec((BLOCK, d), lambda s: (s * nb, 0))],
        out_specs=_full((N_META, d)),
        out_shape=jax.ShapeDtypeStruct((N_META, d), F32),
    )(dh0)


def _adamw(w, g, m, v, name):
    shape = w.shape
    cols = shape[-1]
    rows = w.size // cols
    tr = rows
    for cand in (512, 256, 128, 64):
        if rows % cand == 0 and rows > cand:
            tr = cand
            break

    def body(w_ref, g_ref, m_ref, v_ref, d_ref, nm_ref, nv_ref):
        gv = g_ref[...]
        nm = ADAM_B1 * m_ref[...] + (1.0 - ADAM_B1) * gv
        nv = ADAM_B2 * v_ref[...] + (1.0 - ADAM_B2) * (gv * gv)
        m_hat = nm / (1.0 - ADAM_B1**ADAM_STEP)
        v_hat = nv / (1.0 - ADAM_B2**ADAM_STEP)
        d_ref[...] = -ADAM_LR * (m_hat / (jnp.sqrt(v_hat) + ADAM_EPS) + ADAM_WD * w_ref[...])
        nm_ref[...] = nm
        nv_ref[...] = nv

    blk = pl.BlockSpec((tr, cols), lambda i: (i, 0))
    flat = [a.reshape(rows, cols) for a in (w, g, m, v)]
    outs = pl.pallas_call(
        body,
        name=name,
        grid=(rows // tr,),
        in_specs=[blk] * 4,
        out_specs=[blk] * 3,
        out_shape=[jax.ShapeDtypeStruct((rows, cols), F32)] * 3,
    )(*flat)
    return [o.reshape(shape) for o in outs]


WEIGHTS = ["meta_tokens", "ffn1_w_gate", "ffn1_w_up", "ffn1_w_down", "ln1_g", "ln1_b", "w_in", "conv_w", "conv_b", "gate_a_w", "gate_a_b",
           "gate_x_w", "gate_x_b", "lru_lambda", "attn_sinks", "norm_rec_g", "norm_attn_g", "w_out", "ln2_g", "ln2_b", "ffn2_w_gate",
           "ffn2_w_up", "ffn2_w_down", "ln3_g", "ln3_b"]
BIG = [("ffn1_w_gate", True), ("ffn1_w_up", True), ("ffn1_w_down", False), ("w_in", True), ("w_out", False),
       ("ffn2_w_gate", True), ("ffn2_w_up", True), ("ffn2_w_down", False)]


def _rope_tables(t_len, pad):
    pos = (jnp.arange(t_len) - pad).astype(F32)
    inv_freq = ROPE_THETA ** (-jnp.arange(0, ROPE_DIM, 2, dtype=F32) / ROPE_DIM)
    ang = pos[:, None] * inv_freq[None, :]
    cos, sin = jnp.cos(ang), jnp.sin(ang)
    half = ROPE_DIM // 2
    rest = jnp.zeros((t_len, HEAD_DIM - ROPE_DIM), F32)
    zero = jnp.zeros((t_len, half), F32)
    cos_h = jnp.concatenate([cos, cos, rest + 1.0], axis=1)
    up_h = jnp.concatenate([-sin, zero, rest], axis=1)
    down_h = jnp.concatenate([zero, sin, rest], axis=1)
    return [jnp.tile(t, (1, ATTN_WIDTH // HEAD_DIM)) for t in (cos_h, up_h, down_h)]


def _gate_tiles(w):
    z = jnp.zeros((HEAD_DIM, HEAD_DIM), w.dtype)
    tiles = [jnp.block([[w[2 * j], z], [z, w[2 * j + 1]]]) for j in range(w.shape[0] // 2)]
    return jnp.stack(tiles).astype(BF16)


def _gate_blocks(tiles):
    out = []
    for j in range(tiles.shape[0]):
        out += [tiles[j, :HEAD_DIM, :HEAD_DIM], tiles[j, HEAD_DIM:, HEAD_DIM:]]
    return jnp.stack(out)


def kernel(x, meta_tokens, ffn1_w_gate, ffn1_w_up, ffn1_w_down, ln1_g, ln1_b, w_in, conv_w, conv_b, gate_a_w, gate_a_b, gate_x_w, gate_x_b, lru_lambda, attn_sinks, norm_rec_g, norm_attn_g, w_out, ln2_g, ln2_b, ffn2_w_gate, ffn2_w_up, ffn2_w_down, ln3_g, ln3_b, loss_target, m_meta_tokens, m_ffn1_w_gate, m_ffn1_w_up, m_ffn1_w_down, m_ln1_g, m_ln1_b, m_w_in, m_conv_w, m_conv_b, m_gate_a_w, m_gate_a_b, m_gate_x_w, m_gate_x_b, m_lru_lambda, m_attn_sinks, m_norm_rec_g, m_norm_attn_g, m_w_out, m_ln2_g, m_ln2_b, m_ffn2_w_gate, m_ffn2_w_up, m_ffn2_w_down, m_ln3_g, m_ln3_b, v_meta_tokens, v_ffn1_w_gate, v_ffn1_w_up, v_ffn1_w_down, v_ln1_g, v_ln1_b, v_w_in, v_conv_w, v_conv_b, v_gate_a_w, v_gate_a_b, v_gate_x_w, v_gate_x_b, v_lru_lambda, v_attn_sinks, v_norm_rec_g, v_norm_attn_g, v_w_out, v_ln2_g, v_ln2_b, v_ffn2_w_gate, v_ffn2_w_up, v_ffn2_w_down, v_ln3_g, v_ln3_b):
    given = dict(locals())
    w = {k: given[k] for k in WEIGHTS}
    n_seq, seq, d = x.shape
    depth = ln1_g.shape[0]
    alpha = (2.0 * depth) ** 0.25
    pad = (-(N_META + seq)) % BLOCK
    t_len = pad + N_META + seq
    rows = n_seq * t_len
    me = 4 * lax.axis_index("x") + 2 * lax.axis_index("y") + lax.axis_index("c")

    sent = {name: (_transpose_to_bf16(w[name], "t_" + name) if transposed else w[name].astype(BF16)) for name, transposed in BIG}
    shard_rows = [sent[name].shape[1] for name, _ in BIG]
    wcat = jnp.concatenate([sent[name][l] for l in range(depth) for name, _ in BIG], axis=0)
    small = jnp.concatenate([meta_tokens, conv_w.reshape(-1, BLOCK)], axis=0)
    wall, small_all = _all_gather([wcat, small], "gather_weights")
    full = []
    off = 0
    for l in range(depth):
        layer = {}
        for (name, _), nr in zip(BIG, shard_rows):
            layer[name] = wall[:, off : off + nr, :].reshape(N_DEV * nr, d)
            off += nr
        full.append(layer)
    meta_full = small_all[:, :N_META, :].transpose(1, 0, 2).reshape(N_META, d)
    conv_shard = conv_w.shape[-1]
    conv_full = small_all[:, N_META:, :].reshape(N_DEV, depth, CONV_WIDTH, conv_shard).transpose(1, 2, 0, 3).reshape(depth, CONV_WIDTH, REC_WIDTH)

    cos, sin_up, sin_down = _rope_tables(t_len, pad)
    row1 = lambda a: a.reshape(1, -1)

    h = jnp.concatenate([jnp.zeros((n_seq, pad, d), F32), jnp.broadcast_to(meta_full[None], (n_seq, N_META, d)), x], axis=1).reshape(rows, d)
    target = jnp.pad(loss_target, ((0, 0), (pad + N_META, 0), (0, 0))).reshape(rows, d)
    saved = []
    for l in range(depth):
        fl = full[l]
        s = {"h0": h}
        h, s["g1"], s["u1"], s["z1"] = _ffn_fwd(h, fl["ffn1_w_gate"], fl["ffn1_w_up"], fl["ffn1_w_down"], row1(ln1_g[l]), row1(ln1_b[l]), alpha, "ffn1_fwd")
        s["h1"] = h
        s["wa"], s["wx"] = _gate_tiles(gate_a_w[l]), _gate_tiles(gate_x_w[l])
        s["sinks"] = jnp.broadcast_to(attn_sinks[l][:, None], (N_Q_HEADS, BLOCK))
        proj = _proj_fwd(h, fl["w_in"], "proj_fwd")
        s["proj"] = proj
        s["xc"], s["hrec"], s["y_rec"] = _rglru_fwd(proj, conv_full[l], row1(conv_b[l]), s["wa"], row1(gate_a_b[l]), s["wx"], row1(gate_x_b[l]),
                                                     row1(lru_lambda[l]), t_len, pad, "rglru_fwd")
        s["qkv"] = _rope_fwd(proj, cos, sin_up, sin_down, t_len, "rope_fwd")
        s["y_attn"] = _attn_fwd(s["qkv"], s["sinks"], t_len, pad, "attn_fwd")
        h, s["z2"] = _mix_out_fwd(s["y_rec"], s["y_attn"], h, fl["w_out"], row1(norm_rec_g[l]), row1(norm_attn_g[l]), row1(ln2_g[l]), row1(ln2_b[l]), alpha, "mix_out_fwd")
        s["h2"] = h
        h, s["g2"], s["u2"], s["z3"] = _ffn_fwd(h, fl["ffn2_w_gate"], fl["ffn2_w_up"], fl["ffn2_w_down"], row1(ln3_g[l]), row1(ln3_b[l]), alpha, "ffn2_fwd")
        saved.append(s)
    loss_part, dh = _loss_head(h, target, t_len, pad + N_META, "loss_head")

    big_grads = [None] * depth
    small_grads = [None] * depth
    for l in reversed(range(depth)):
        fl, s = full[l], saved[l]
        bg, sg = {}, {}
        dh, dg, du, act, dyb, sg["ln3_g"], sg["ln3_b"] = _ffn_bwd(dh, s["z3"], s["g2"], s["u2"], fl["ffn2_w_gate"], fl["ffn2_w_up"], fl["ffn2_w_down"], row1(ln3_g[l]), alpha, "ffn2_bwd")
        bg["ffn2_w_gate"] = _mm_tn(dg, s["h2"], BF16, "ffn_wgrad_in")
        bg["ffn2_w_up"] = _mm_tn(du, s["h2"], BF16, "ffn_wgrad_in")
        bg["ffn2_w_down"] = _mm_tn(act, dyb, BF16, "ffn_wgrad_down")
        dz, dzb, dy_rec, dy_attn, yn, sg["ln2_g"], sg["ln2_b"], sg["norm_rec_g"], sg["norm_attn_g"] = _mix_out_bwd(
            dh, s["z2"], s["y_rec"], s["y_attn"], fl["w_out"], row1(norm_rec_g[l]), row1(norm_attn_g[l]), row1(ln2_g[l]), "mix_out_bwd")
        bg["w_out"] = _mm_tn(yn, dzb, BF16, "w_out_wgrad")
        dxr, dgt, sg["conv_w"], sg["conv_b"], dwa, sg["gate_a_b"], dwx, sg["gate_x_b"], sg["lru_lambda"] = _rglru_bwd(
            dy_rec, s["hrec"], s["proj"], s["xc"], conv_full[l], s["wa"], row1(gate_a_b[l]), s["wx"], row1(gate_x_b[l]), row1(lru_lambda[l]), t_len, pad, "rglru_bwd")
        sg["gate_a_w"], sg["gate_x_w"] = _gate_blocks(dwa), _gate_blocks(dwx)
        dq, dk, dv, dsink = _attn_bwd(s["qkv"], s["sinks"], s["y_attn"], dy_attn, t_len, pad, "attn_bwd")
        sg["attn_sinks"] = dsink[:, 0]
        dqkv = _rope_bwd(dq, dk, dv, cos, sin_up, sin_down, t_len, "rope_bwd")
        dproj = jnp.concatenate([dxr, dgt, dqkv], axis=1)
        bg["w_in"] = _mm_tn(dproj, s["h1"], BF16, "w_in_wgrad")
        dh = _proj_bwd(dproj, fl["w_in"], dz, alpha, "proj_bwd")
        dh, dg, du, act, dyb, sg["ln1_g"], sg["ln1_b"] = _ffn_bwd(dh, s["z1"], s["g1"], s["u1"], fl["ffn1_w_gate"], fl["ffn1_w_up"], fl["ffn1_w_down"], row1(ln1_g[l]), alpha, "ffn1_bwd")
        bg["ffn1_w_gate"] = _mm_tn(dg, s["h0"], BF16, "ffn_wgrad_in")
        bg["ffn1_w_up"] = _mm_tn(du, s["h0"], BF16, "ffn_wgrad_in")
        bg["ffn1_w_down"] = _mm_tn(act, dyb, BF16, "ffn_wgrad_down")
        big_grads[l], small_grads[l] = bg, sg
    grad_x = dh.reshape(n_seq, t_len, d)[:, pad + N_META :, :]
    dmeta = _meta_grad(dh, t_len, pad, "meta_grad")

    order = [(l, name) for l in range(depth) for name, _ in BIG]
    parts = _all_to_all([big_grads[l][name].reshape(N_DEV, -1, d) for l, name in order], "exchange_grads")
    reduced = {key: _sum8(p, "sum_grads") for key, p in zip(order, parts)}
    grads = {}
    for name, transposed in BIG:
        per_layer = [reduced[(l, name)] for l in range(depth)]
        if transposed:
            per_layer = [_transpose_f32(g, "t_grad") for g in per_layer]
        grads[name] = jnp.stack(per_layer)

    small_names = ["ln1_g", "ln1_b", "ln2_g", "ln2_b", "ln3_g", "ln3_b", "conv_b", "gate_a_b", "gate_x_b", "lru_lambda", "norm_rec_g",
                   "norm_attn_g", "conv_w", "gate_a_w", "gate_x_w", "attn_sinks"]
    pieces = [small_grads[l][name].reshape(-1) for l in range(depth) for name in small_names] + [dmeta.reshape(-1), loss_part[0, :1]]
    sizes = [p.shape[0] for p in pieces]
    flat = jnp.concatenate(pieces)
    width = 1024
    n_rows = -(-flat.shape[0] // (8 * width)) * 8
    flat = jnp.pad(flat, (0, n_rows * width - flat.shape[0])).reshape(n_rows, width)
    (flat_all,) = _all_gather([flat], "gather_small_grads")
    total = _sum8(flat_all, "sum_small_grads").reshape(-1)
    offs = [0]
    for sz in sizes:
        offs.append(offs[-1] + sz)
    taken = [total[offs[k] : offs[k + 1]] for k in range(len(sizes))]
    for j, name in enumerate(small_names):
        full_shape = (depth,) + ((CONV_WIDTH, REC_WIDTH) if name == "conv_w" else w[name].shape[1:])
        grads[name] = jnp.stack([taken[l * len(small_names) + j] for l in range(depth)]).reshape(full_shape)
    grads["conv_w"] = lax.dynamic_slice_in_dim(grads["conv_w"], me * conv_shard, conv_shard, axis=2)
    meta_shard = meta_tokens.shape[1]
    grads["meta_tokens"] = lax.dynamic_slice_in_dim(taken[-2].reshape(N_META, d), me * meta_shard, meta_shard, axis=1)
    loss = taken[-1][0]

    deltas, new_m, new_v = {}, {}, {}
    for name in WEIGHTS:
        deltas[name], new_m[name], new_v[name] = _adamw(w[name], grads[name], given["m_" + name], given["v_" + name], "adamw")
    return (loss, grad_x, *[grads[k] for k in WEIGHTS], *[deltas[k] for k in WEIGHTS], *[new_m[k] for k in WEIGHTS], *[new_v[k] for k in WEIGHTS])
```

```python
import functools

import jax
import jax.numpy as jnp
from jax import lax
from jax.experimental import pallas as pl
from jax.experimental.pallas import tpu as pltpu

F32 = jnp.float32
BF16 = jnp.bfloat16
MESH = pl.DeviceIdType.MESH

N_DEV = 8
N_META = 16
BLOCK = 128
WINDOW = 128
REC_WIDTH = 512
ATTN_WIDTH = 512
KV_WIDTH = 128
HEAD_DIM = 64
N_Q_HEADS = 8
ROPE_DIM = 16
ROPE_THETA = 500000.0
CONV_WIDTH = 4
LRU_C = 8.0
LN_EPS = 1e-5
RMS_EPS = 1e-6
NEG_INF = -1e30
ADAM_LR, ADAM_B1, ADAM_B2, ADAM_EPS, ADAM_WD, ADAM_STEP = 0.001, 0.9, 0.999, 1e-08, 0.01, 10

NT = (((1,), (1,)), ((), ()))
TN = (((0,), (0,)), ((), ()))
VMEM_BIG = 56 * 1024 * 1024


def _params(vmem=None):
    return pltpu.CompilerParams(vmem_limit_bytes=vmem)


def _row_tile(rows):
    return 256 if rows % 256 == 0 else 128


def _sigmoid(x):
    return 1.0 / (1.0 + jnp.exp(-x))


def _expm1_nonpos(x):
    series = x * (1.0 + 0.5 * x * (1.0 + x / 3.0 * (1.0 + 0.25 * x * (1.0 + 0.2 * x))))
    return jnp.where(x > -0.05, series, jnp.exp(x) - 1.0)


def _softplus(x):
    e = jnp.exp(-jnp.abs(x))
    log1p = jnp.where(e < 1e-3, e * (1.0 - e * (0.5 - e / 3.0)), jnp.log(1.0 + e))
    return jnp.maximum(x, 0.0) + log1p


def _gelu_and_grad(x):
    c0 = 0.7978845608028654
    x2 = x * x
    t = jnp.tanh(c0 * (x + 0.044715 * x * x2))
    gelu = 0.5 * x * (1.0 + t)
    grad = 0.5 * (1.0 + t) + 0.5 * x * (1.0 - t * t) * c0 * (1.0 + 3.0 * 0.044715 * x2)
    return gelu, grad


def _ln_stats(z):
    mu = jnp.mean(z, axis=-1, keepdims=True)
    zc = z - mu
    var = jnp.mean(zc * zc, axis=-1, keepdims=True)
    rstd = lax.rsqrt(var + LN_EPS)
    return zc * rstd, rstd


def _ln_bwd(dy, z, gamma):
    xhat, rstd = _ln_stats(z)
    dxh = dy * gamma
    m1 = jnp.mean(dxh, axis=-1, keepdims=True)
    m2 = jnp.mean(dxh * xhat, axis=-1, keepdims=True)
    dz = rstd * (dxh - m1 - xhat * m2)
    return dz, jnp.sum(dy * xhat, axis=0, keepdims=True), jnp.sum(dy, axis=0, keepdims=True)


def _rms(y, gamma):
    r = lax.rsqrt(jnp.mean(y * y, axis=-1, keepdims=True) + RMS_EPS)
    return y * r * gamma


def _rms_bwd(dout, y, gamma):
    r = lax.rsqrt(jnp.mean(y * y, axis=-1, keepdims=True) + RMS_EPS)
    n = y * r
    dn = dout * gamma
    dy = r * (dn - n * jnp.mean(dn * n, axis=-1, keepdims=True))
    return dy, jnp.sum(dout * n, axis=0, keepdims=True), n * gamma


def _load_blocks(w_hbm, items, sems):
    @pl.when(pl.program_id(0) == 0)
    def _():
        copies = []
        for k, (off, nr, dst) in enumerate(items):
            for dev in range(N_DEV):
                copies.append(pltpu.make_async_copy(w_hbm.at[dev, pl.ds(off, nr), :], dst.at[pl.ds(dev * nr, nr), :], sems.at[k, dev]))
        for cp in copies:
            cp.start()
        for cp in copies:
            cp.wait()


def _full(shape):
    return pl.BlockSpec(shape, lambda *_: (0,) * len(shape))


ANY = pl.BlockSpec(memory_space=pl.ANY)


def _all_gather(xs, name):
    n = len(xs)

    def body(*refs):
        ins, outs = refs[:n], refs[n : 2 * n]
        send_sems, recv_sems, local_sems = refs[2 * n :]
        x, y, c = lax.axis_index("x"), lax.axis_index("y"), lax.axis_index("c")
        me, sibling = (x, y, c), (x, y, 1 - c)
        chips = [(1 - x, y), (x, 1 - y), (1 - x, 1 - y)]

        def slot(i, dev):
            return outs[i].at[4 * dev[0] + 2 * dev[1] + dev[2]]

        def copy(i, k, block, to, src=None):
            return pltpu.make_async_remote_copy(
                src_ref=slot(i, block) if src is None else src,
                dst_ref=slot(i, block),
                send_sem=send_sems.at[i, k],
                recv_sem=recv_sems.at[i, k],
                device_id=to,
                device_id_type=MESH,
            )

        mine = [pltpu.make_async_copy(ins[i], slot(i, me), local_sems.at[i]) for i in range(n)]
        for cp in mine:
            cp.start()
        first = []
        for i in range(n):
            first.append(copy(i, 0, me, sibling, src=ins[i]))
            first += [copy(i, 1 + j, me, (*chip, c), src=ins[i]) for j, chip in enumerate(chips)]
        for cp in first:
            cp.start()
        passed = []
        for j, chip in enumerate(chips):
            for i in range(n):
                copy(i, 1 + j, (*chip, c), me).wait_recv()
                fwd = copy(i, 4 + j, (*chip, c), sibling)
                fwd.start()
                passed.append(fwd)
        for i in range(n):
            copy(i, 0, sibling, me).wait_recv()
            for j, chip in enumerate(chips):
                copy(i, 4 + j, (*chip, 1 - c), me).wait_recv()
        for cp in first + passed:
            cp.wait_send()
        for cp in mine:
            cp.wait()

    return pl.pallas_call(
        body,
        name=name,
        out_shape=[jax.ShapeDtypeStruct((N_DEV,) + a.shape, a.dtype) for a in xs],
        in_specs=[ANY] * n,
        out_specs=[ANY] * n,
        scratch_shapes=[pltpu.SemaphoreType.DMA((n, 7)), pltpu.SemaphoreType.DMA((n, 7)), pltpu.SemaphoreType.DMA((n,))],
    )(*xs)


HBM = pl.BlockSpec(memory_space=pltpu.HBM)
SEM = pl.BlockSpec(memory_space=pltpu.SEMAPHORE)
EFFECT = pltpu.SideEffectType.DATAFLOW_SIDE_EFFECTING


def _push_copies(ins, lands, send_sems, recv_sems, scatter):
    x, y, c = lax.axis_index("x"), lax.axis_index("y"), lax.axis_index("c")
    me = 4 * x + 2 * y + c
    copies = []
    for k in range(1, N_DEV):
        px = 1 - x if (k >> 2) & 1 else x
        py = 1 - y if (k >> 1) & 1 else y
        pc = 1 - c if k & 1 else c
        for i in range(len(ins)):
            copies.append(
                pltpu.make_async_remote_copy(
                    src_ref=ins[i].at[4 * px + 2 * py + pc] if scatter else ins[i],
                    dst_ref=lands[i].at[me],
                    send_sem=send_sems.at[i * (N_DEV - 1) + k - 1],
                    recv_sem=recv_sems.at[i * (N_DEV - 1) + k - 1],
                    device_id=(px, py, pc),
                    device_id_type=MESH,
                )
            )
    return copies


def _push_start(xs, scatter, name):
    n = len(xs)

    def body(*refs):
        ins, lands = refs[:n], refs[n : 2 * n]
        send_sems, recv_sems = refs[2 * n], refs[2 * n + 1]
        token, local_sems = refs[4 * n + 2], refs[4 * n + 3]
        for cp in _push_copies(ins, lands, send_sems, recv_sems, scatter):
            cp.start()
        me = 4 * lax.axis_index("x") + 2 * lax.axis_index("y") + lax.axis_index("c")
        mine = [pltpu.make_async_copy(ins[i].at[me] if scatter else ins[i], lands[i].at[me], local_sems.at[i]) for i in range(n)]
        for cp in mine:
            cp.start()
        for cp in mine:
            cp.wait()
        token[...] = jnp.zeros_like(token)

    land_shapes = [a.shape if scatter else (N_DEV,) + a.shape for a in xs]
    outs = pl.pallas_call(
        body,
        name=name,
        out_shape=(
            pltpu.SemaphoreType.DMA((n * (N_DEV - 1),)),
            pltpu.SemaphoreType.DMA((n * (N_DEV - 1),)),
            *[pltpu.HBM(a.shape, a.dtype) for a in xs],
            *[pltpu.HBM(s, a.dtype) for s, a in zip(land_shapes, xs)],
            jax.ShapeDtypeStruct((8, BLOCK), F32),
        ),
        in_specs=[HBM] * (2 * n),
        out_specs=(SEM, SEM, *[HBM] * (2 * n), pl.BlockSpec(memory_space=pltpu.VMEM)),
        input_output_aliases={i: 2 + i for i in range(2 * n)},
        scratch_shapes=[pltpu.SemaphoreType.DMA((n,))],
        compiler_params=pltpu.CompilerParams(has_side_effects=EFFECT),
    )(*[pltpu.with_memory_space_constraint(a, pltpu.HBM) for a in xs], *[pltpu.with_memory_space_constraint(lax.empty(s, a.dtype), pltpu.HBM) for s, a in zip(land_shapes, xs)])
    return outs[0], outs[1], list(outs[2 : 2 + n]), list(outs[2 + n : 2 + 2 * n]), outs[2 + 2 * n]


def _push_wait(send_sems, recv_sems, srcs, lands, after, scatter, name):
    n = len(srcs)

    def body(*refs):
        ins, zones = refs[:n], refs[n : 2 * n]
        for cp in _push_copies(ins, zones, refs[2 * n], refs[2 * n + 1], scatter):
            cp.wait_send()
            cp.wait_recv()

    outs = pl.pallas_call(
        body,
        name=name,
        out_shape=[pltpu.HBM(a.shape, a.dtype) for a in srcs + lands],
        in_specs=[HBM] * (2 * n) + [SEM, SEM, ANY],
        out_specs=[HBM] * (2 * n),
        input_output_aliases={i: i for i in range(2 * n)},
        compiler_params=pltpu.CompilerParams(has_side_effects=EFFECT),
    )(*srcs, *lands, send_sems, recv_sems, after)
    return list(outs[n:])


def _sum8(parts, name):
    _, rows, cols = parts.shape
    tr = rows
    for cand in (512, 256, 128, 64, 32, 16):
        if rows % cand == 0 and rows > cand:
            tr = cand
            break

    def body(p_ref, o_ref):
        acc = p_ref[0].astype(F32)
        for d in range(1, N_DEV):
            acc = acc + p_ref[d].astype(F32)
        o_ref[...] = acc

    return pl.pallas_call(
        body,
        name=name,
        grid=(rows // tr,),
        in_specs=[pl.BlockSpec((N_DEV, tr, cols), lambda i: (0, i, 0))],
        out_specs=pl.BlockSpec((tr, cols), lambda i: (i, 0)),
        out_shape=jax.ShapeDtypeStruct((rows, cols), F32),
    )(parts)


def _mm_tn(a, b, out_dtype, name):
    rows, m = a.shape
    n = b.shape[1]
    tm = m // 2 if (m // 2) % 128 == 0 and m > 512 else m
    tr = next(t for t in (1088, 1024, 512, 256, 128) if rows % t == 0)
    nk = rows // tr

    def body(a_ref, b_ref, o_ref, acc):
        k = pl.program_id(1)
        part = lax.dot_general(a_ref[...].astype(BF16), b_ref[...].astype(BF16), TN, preferred_element_type=F32)

        @pl.when(k == 0)
        def _():
            acc[...] = part

        @pl.when(k > 0)
        def _():
            acc[...] += part

        @pl.when(k == nk - 1)
        def _():
            o_ref[...] = acc[...].astype(o_ref.dtype)

    return pl.pallas_call(
        body,
        name=name,
        grid=(m // tm, nk),
        in_specs=[pl.BlockSpec((tr, tm), lambda i, k: (k, i)), pl.BlockSpec((tr, n), lambda i, k: (k, 0))],
        out_specs=pl.BlockSpec((tm, n), lambda i, k: (i, 0)),
        out_shape=jax.ShapeDtypeStruct((m, n), out_dtype),
        scratch_shapes=[pltpu.VMEM((tm, n), F32)],
        compiler_params=_params(VMEM_BIG),
    )(a, b)


def _transpose_to_bf16(w, name):
    layers, a_dim, b_dim = w.shape

    def body(w_ref, o_ref):
        eye = (lax.broadcasted_iota(jnp.int32, (a_dim, a_dim), 0) == lax.broadcasted_iota(jnp.int32, (a_dim, a_dim), 1)).astype(BF16)
        o_ref[0] = lax.dot_general(w_ref[0].astype(BF16), eye, TN, preferred_element_type=F32).astype(BF16)

    return pl.pallas_call(
        body,
        name=name,
        grid=(layers,),
        in_specs=[pl.BlockSpec((1, a_dim, b_dim), lambda l: (l, 0, 0))],
        out_specs=pl.BlockSpec((1, b_dim, a_dim), lambda l: (l, 0, 0)),
        out_shape=jax.ShapeDtypeStruct((layers, b_dim, a_dim), BF16),
    )(w)


def _transpose_f32(g, name):
    a_dim, b_dim = g.shape

    def body(g_ref, o_ref):
        eye = (lax.broadcasted_iota(jnp.int32, (a_dim, a_dim), 0) == lax.broadcasted_iota(jnp.int32, (a_dim, a_dim), 1)).astype(BF16)
        v = g_ref[...]
        hi = v.astype(BF16)
        r1 = v - hi.astype(F32)
        mid = r1.astype(BF16)
        lo = (r1 - mid.astype(F32)).astype(BF16)
        out = lax.dot_general(hi, eye, TN, preferred_element_type=F32)
        out = out + lax.dot_general(mid, eye, TN, preferred_element_type=F32)
        out = out + lax.dot_general(lo, eye, TN, preferred_element_type=F32)
        o_ref[...] = out

    return pl.pallas_call(
        body,
        name=name,
        in_specs=[_full((a_dim, b_dim))],
        out_specs=_full((b_dim, a_dim)),
        out_shape=jax.ShapeDtypeStruct((b_dim, a_dim), F32),
        grid=(1,),
    )(g)


def _weight_scratch(locs, d):
    return [pltpu.VMEM((N_DEV * nr, d), BF16) for _, nr in locs] + [pltpu.SemaphoreType.DMA((len(locs), N_DEV))]


def _proj_fwd(h, wbuf, loc, name):
    rows, d = h.shape
    n = N_DEV * loc[1]
    tm = _row_tile(rows)

    def body(h_ref, w_hbm, o_ref, w, sems):
        _load_blocks(w_hbm, [(*loc, w)], sems)
        o_ref[...] = lax.dot_general(h_ref[...].astype(BF16), w[...], NT, preferred_element_type=F32)

    return pl.pallas_call(
        body,
        name=name,
        grid=(rows // tm,),
        in_specs=[pl.BlockSpec((tm, d), lambda i: (i, 0)), ANY],
        out_specs=pl.BlockSpec((tm, n), lambda i: (i, 0)),
        out_shape=jax.ShapeDtypeStruct((rows, n), F32),
        scratch_shapes=_weight_scratch([loc], d),
        compiler_params=_params(VMEM_BIG),
    )(h, wbuf)


def _proj_bwd(dproj, wbuf, loc, dz, alpha, name):
    rows, n = dproj.shape
    d = dz.shape[1]
    tm = _row_tile(rows)

    def body(dp_ref, w_hbm, dz_ref, o_ref, w, sems):
        _load_blocks(w_hbm, [(*loc, w)], sems)
        o_ref[...] = alpha * dz_ref[...] + jnp.dot(dp_ref[...], w[...], preferred_element_type=F32)

    return pl.pallas_call(
        body,
        name=name,
        grid=(rows // tm,),
        in_specs=[pl.BlockSpec((tm, n), lambda i: (i, 0)), ANY, pl.BlockSpec((tm, d), lambda i: (i, 0))],
        out_specs=pl.BlockSpec((tm, d), lambda i: (i, 0)),
        out_shape=jax.ShapeDtypeStruct((rows, d), F32),
        scratch_shapes=_weight_scratch([loc], d),
        compiler_params=_params(VMEM_BIG),
    )(dproj, wbuf, dz)


FFN_CHUNKS = 2


def _ffn_fwd(h, wbuf, locs, ln_g, ln_b, alpha, name):
    rows, d = h.shape
    f = N_DEV * locs[0][1]
    tm = _row_tile(rows)
    tf = f // FFN_CHUNKS

    def body(h_ref, w_hbm, lg_ref, lb_ref, ho_ref, g_ref, u_ref, z_ref, wg, wu, wdv, sems):
        _load_blocks(w_hbm, [(*locs[0], wg), (*locs[1], wu), (*locs[2], wdv)], sems)
        x = h_ref[...]
        xb = x.astype(BF16)
        y = None
        for c in range(FFN_CHUNKS):
            sl = slice(c * tf, (c + 1) * tf)
            g = lax.dot_general(xb, wg[sl, :], NT, preferred_element_type=F32)
            u = lax.dot_general(xb, wu[sl, :], NT, preferred_element_type=F32)
            g_ref[:, sl] = g
            u_ref[:, sl] = u
            a = (g * _sigmoid(g) * u).astype(BF16)
            part = jnp.dot(a, wdv[sl, :], preferred_element_type=F32)
            y = part if y is None else y + part
        z = alpha * x + 0.5 * y
        z_ref[...] = z
        xhat, _ = _ln_stats(z)
        ho_ref[...] = xhat * lg_ref[...] + lb_ref[...]

    row_d = pl.BlockSpec((tm, d), lambda i: (i, 0))
    row_f = pl.BlockSpec((tm, f), lambda i: (i, 0))
    return pl.pallas_call(
        body,
        name=name,
        grid=(rows // tm,),
        in_specs=[row_d, ANY, _full((1, d)), _full((1, d))],
        out_specs=[row_d, row_f, row_f, row_d],
        out_shape=[
            jax.ShapeDtypeStruct((rows, d), F32),
            jax.ShapeDtypeStruct((rows, f), F32),
            jax.ShapeDtypeStruct((rows, f), F32),
            jax.ShapeDtypeStruct((rows, d), F32),
        ],
        scratch_shapes=_weight_scratch(locs, d),
        compiler_params=_params(VMEM_BIG),
    )(h, wbuf, ln_g, ln_b)


def _ffn_bwd(dh, z, g, u, wbuf, locs, ln_g, alpha, name):
    rows, d = dh.shape
    f = N_DEV * locs[0][1]
    tm = _row_tile(rows)
    tf = f // FFN_CHUNKS

    def body(dh_ref, z_ref, g_ref, u_ref, w_hbm, lg_ref, dx_ref, dg_ref, du_ref, a_ref, dyb_ref, dgam_ref, dbet_ref, wg, wu, wdv, sems):
        _load_blocks(w_hbm, [(*locs[0], wg), (*locs[1], wu), (*locs[2], wdv)], sems)

        @pl.when(pl.program_id(0) == 0)
        def _():
            dgam_ref[...] = jnp.zeros_like(dgam_ref)
            dbet_ref[...] = jnp.zeros_like(dbet_ref)

        dz, dgam, dbet = _ln_bwd(dh_ref[...], z_ref[...], lg_ref[...])
        dgam_ref[...] += dgam
        dbet_ref[...] += dbet
        dyb = (0.5 * dz).astype(BF16)
        dyb_ref[...] = dyb
        dx = alpha * dz
        for c in range(FFN_CHUNKS):
            sl = slice(c * tf, (c + 1) * tf)
            da = lax.dot_general(dyb, wdv[sl, :], NT, preferred_element_type=F32)
            gc = g_ref[:, sl]
            uc = u_ref[:, sl]
            sg = _sigmoid(gc)
            silu = gc * sg
            dgc = (da * uc * (sg * (1.0 + gc * (1.0 - sg)))).astype(BF16)
            duc = (da * silu).astype(BF16)
            a_ref[:, sl] = (silu * uc).astype(BF16)
            dg_ref[:, sl] = dgc
            du_ref[:, sl] = duc
            dx = dx + jnp.dot(dgc, wg[sl, :], preferred_element_type=F32) + jnp.dot(duc, wu[sl, :], preferred_element_type=F32)
        dx_ref[...] = dx

    row_d = pl.BlockSpec((tm, d), lambda i: (i, 0))
    row_f = pl.BlockSpec((tm, f), lambda i: (i, 0))
    return pl.pallas_call(
        body,
        name=name,
        grid=(rows // tm,),
        in_specs=[row_d, row_d, row_f, row_f, ANY, _full((1, d))],
        out_specs=[row_d, row_f, row_f, row_f, row_d, _full((1, d)), _full((1, d))],
        out_shape=[
            jax.ShapeDtypeStruct((rows, d), F32),
            jax.ShapeDtypeStruct((rows, f), BF16),
            jax.ShapeDtypeStruct((rows, f), BF16),
            jax.ShapeDtypeStruct((rows, f), BF16),
            jax.ShapeDtypeStruct((rows, d), BF16),
            jax.ShapeDtypeStruct((1, d), F32),
            jax.ShapeDtypeStruct((1, d), F32),
        ],
        scratch_shapes=_weight_scratch(locs, d),
        compiler_params=_params(VMEM_BIG),
    )(dh, z, g, u, wbuf, ln_g)


LANES = 128


def _gates(xc, wa_ref, ba, wx_ref, bx, lam):
    xcb = xc.astype(BF16)
    r = _sigmoid(jnp.dot(xcb, wa_ref[0], preferred_element_type=F32) + ba)
    i = _sigmoid(jnp.dot(xcb, wx_ref[0], preferred_element_type=F32) + bx)
    sp = _softplus(-lam)
    log_a = -LRU_C * sp * r
    a = jnp.exp(log_a)
    s = jnp.sqrt(-_expm1_nonpos(2.0 * log_a))
    return xcb, r, i, sp, a, s


def _conv(xr, cw, cb):
    return cb + cw[3:4] * xr + cw[2:3] * pltpu.roll(xr, 1, 0) + cw[1:2] * pltpu.roll(xr, 2, 0) + cw[0:1] * pltpu.roll(xr, 3, 0)


def _rglru_fwd(proj, cw, cb, wa, ba, wx, bx, lam, t_len, pad, name):
    rows = proj.shape[0]
    n_seq = rows // t_len
    n_tile = REC_WIDTH // LANES
    groups = t_len // 8

    def body(xr_ref, gt_ref, cw_ref, cb_ref, wa_ref, ba_ref, wx_ref, bx_ref, lam_ref, xc_ref, h_ref, y_ref, a_s, b_s):
        valid = lax.broadcasted_iota(jnp.int32, (t_len, 1), 0) >= pad
        xr = jnp.where(valid, xr_ref[...], 0.0)
        xc = _conv(xr, cw_ref[...], cb_ref[...])
        xc_ref[...] = xc
        _, _, i, _, a, s = _gates(xc, wa_ref, ba_ref[...], wx_ref, bx_ref[...], lam_ref[...])
        a_s[...] = a
        b_s[...] = jnp.where(valid, s * (i * xc), 0.0)
        sub = lax.broadcasted_iota(jnp.int32, (8, LANES), 0)

        def step(gi, carry):
            r0 = pl.multiple_of(gi * 8, 8)
            av = a_s[pl.ds(r0, 8), :]
            bv = b_s[pl.ds(r0, 8), :]
            for d in (1, 2, 4):
                m = sub >= d
                a_sh = pltpu.roll(av, d, 0)
                b_sh = pltpu.roll(bv, d, 0)
                bv = jnp.where(m, av * b_sh + bv, bv)
                av = jnp.where(m, av * a_sh, av)
            hv = av * carry + bv
            h_ref[pl.ds(r0, 8), :] = hv
            return jnp.broadcast_to(hv[7:8, :], (8, LANES))

        lax.fori_loop(0, groups, step, jnp.zeros((8, LANES), F32), unroll=4)
        gelu, _ = _gelu_and_grad(gt_ref[...])
        y_ref[...] = h_ref[...] * gelu

    seq_tile = lambda col0: pl.BlockSpec((t_len, LANES), lambda s, j: (s, col0 + j))
    vec = pl.BlockSpec((1, LANES), lambda s, j: (0, j))
    out = jax.ShapeDtypeStruct((rows, REC_WIDTH), F32)
    return pl.pallas_call(
        body,
        name=name,
        grid=(n_seq, n_tile),
        in_specs=[
            seq_tile(0),
            seq_tile(n_tile),
            pl.BlockSpec((CONV_WIDTH, LANES), lambda s, j: (0, j)),
            vec,
            pl.BlockSpec((1, LANES, LANES), lambda s, j: (j, 0, 0)),
            vec,
            pl.BlockSpec((1, LANES, LANES), lambda s, j: (j, 0, 0)),
            vec,
            vec,
        ],
        out_specs=[seq_tile(0)] * 3,
        out_shape=[out, out, out],
        scratch_shapes=[pltpu.VMEM((t_len, LANES), F32), pltpu.VMEM((t_len, LANES), F32)],
        compiler_params=_params(VMEM_BIG),
    )(proj, proj, cw, cb, wa, ba, wx, bx, lam)


def _rglru_bwd(dy, h, proj, xc, cw, wa, ba, wx, bx, lam, t_len, pad, name):
    rows = proj.shape[0]
    n_seq = rows // t_len
    n_tile = REC_WIDTH // LANES
    groups = t_len // 8

    def body(dy_ref, h_ref, xr_ref, gt_ref, xc_ref, cw_ref, wa_ref, ba_ref, wx_ref, bx_ref, lam_ref,
             dxr_ref, dgt_ref, dcw_ref, dcb_ref, dwa_ref, dba_ref, dwx_ref, dbx_ref, dlam_ref, a_s, c_s, db_s, da_s):
        first_seq = pl.program_id(1) == 0

        @pl.when(first_seq)
        def _():
            for ref in (dcw_ref, dcb_ref, dwa_ref, dba_ref, dwx_ref, dbx_ref, dlam_ref):
                ref[...] = jnp.zeros_like(ref)

        valid = lax.broadcasted_iota(jnp.int32, (t_len, 1), 0) >= pad
        gelu, dgelu = _gelu_and_grad(gt_ref[...])
        dyv = dy_ref[...]
        dho = dyv * gelu
        dgt_ref[...] = (dyv * h_ref[...] * dgelu).astype(BF16)
        xc = xc_ref[...]
        lam = lam_ref[...]
        xcb, r, i, sp, a, s = _gates(xc, wa_ref, ba_ref[...], wx_ref, bx_ref[...], lam)
        a_s[...] = a
        c_s[...] = a * dho
        db_s[...] = dho
        sub = lax.broadcasted_iota(jnp.int32, (8, LANES), 0)

        def step(k, carry):
            gi = groups - 1 - k
            r0 = pl.multiple_of(gi * 8, 8)
            av = a_s[pl.ds(r0, 8), :]
            cv = c_s[pl.ds(r0, 8), :]
            for d in (1, 2, 4):
                m = sub < 8 - d
                a_sh = pltpu.roll(av, 8 - d, 0)
                c_sh = pltpu.roll(cv, 8 - d, 0)
                cv = jnp.where(m, av * c_sh + cv, cv)
                av = jnp.where(m, av * a_sh, av)
            ev = av * carry + cv
            e_next = jnp.where(sub < 7, pltpu.roll(ev, 7, 0), carry)
            dht = db_s[pl.ds(r0, 8), :] + e_next
            hv = h_ref[pl.ds(r0, 8), :]
            rp = pl.multiple_of(jnp.maximum(gi - 1, 0) * 8, 8)
            h_before = jnp.where(gi > 0, jnp.broadcast_to(h_ref[pl.ds(rp, 8), :][7:8, :], (8, LANES)), 0.0)
            h_prev = jnp.where(sub >= 1, pltpu.roll(hv, 1, 0), h_before)
            db_s[pl.ds(r0, 8), :] = dht
            da_s[pl.ds(r0, 8), :] = dht * h_prev
            return jnp.broadcast_to(ev[0:1, :], (8, LANES))

        lax.fori_loop(0, groups, step, jnp.zeros((8, LANES), F32), unroll=4)

        db = jnp.where(valid, db_s[...], 0.0)
        da = da_s[...]
        ds = db * (i * xc)
        di = db * s * xc
        dxc = db * s * i
        dlog_a = da * a - ds * (a * a) / jnp.maximum(s, 1e-30)
        dr = dlog_a * (-LRU_C * sp)
        dsp = jnp.sum(dlog_a * (-LRU_C) * r, axis=0, keepdims=True)
        dlam_ref[...] += dsp * (-_sigmoid(-lam))
        dpr = dr * r * (1.0 - r)
        dpi = di * i * (1.0 - i)
        dprb = dpr.astype(BF16)
        dpib = dpi.astype(BF16)
        dxc = dxc + lax.dot_general(dprb, wa_ref[0], NT, preferred_element_type=F32) + lax.dot_general(dpib, wx_ref[0], NT, preferred_element_type=F32)
        dwa_ref[0] += lax.dot_general(xcb, dprb, TN, preferred_element_type=F32)
        dwx_ref[0] += lax.dot_general(xcb, dpib, TN, preferred_element_type=F32)
        dba_ref[...] += jnp.sum(dpr, axis=0, keepdims=True)
        dbx_ref[...] += jnp.sum(dpi, axis=0, keepdims=True)
        dxc = jnp.where(valid, dxc, 0.0)
        xr = jnp.where(valid, xr_ref[...], 0.0)
        dcb_ref[...] += jnp.sum(dxc, axis=0, keepdims=True)
        for k in range(CONV_WIDTH):
            shifted = xr if k == CONV_WIDTH - 1 else pltpu.roll(xr, CONV_WIDTH - 1 - k, 0)
            dcw_ref[k : k + 1, :] += jnp.sum(dxc * shifted, axis=0, keepdims=True)
        cw = cw_ref[...]
        dxr = cw[3:4] * dxc + cw[2:3] * pltpu.roll(dxc, t_len - 1, 0) + cw[1:2] * pltpu.roll(dxc, t_len - 2, 0) + cw[0:1] * pltpu.roll(dxc, t_len - 3, 0)
        dxr_ref[...] = jnp.where(valid, dxr, 0.0).astype(BF16)

    seq_tile = lambda col0: pl.BlockSpec((t_len, LANES), lambda j, s: (s, col0 + j))
    vec = pl.BlockSpec((1, LANES), lambda j, s: (0, j))
    mat = pl.BlockSpec((1, LANES, LANES), lambda j, s: (j, 0, 0))
    cwb = pl.BlockSpec((CONV_WIDTH, LANES), lambda j, s: (0, j))
    big = jax.ShapeDtypeStruct((rows, REC_WIDTH), BF16)
    vec_shape = jax.ShapeDtypeStruct((1, REC_WIDTH), F32)
    mat_shape = jax.ShapeDtypeStruct((n_tile, LANES, LANES), F32)
    return pl.pallas_call(
        body,
        name=name,
        grid=(n_tile, n_seq),
        in_specs=[seq_tile(0), seq_tile(0), seq_tile(0), seq_tile(n_tile), seq_tile(0), cwb, mat, vec, mat, vec, vec],
        out_specs=[seq_tile(0), seq_tile(0), cwb, vec, mat, vec, mat, vec, vec],
        out_shape=[big, big, jax.ShapeDtypeStruct((CONV_WIDTH, REC_WIDTH), F32), vec_shape, mat_shape, vec_shape, mat_shape, vec_shape, vec_shape],
        scratch_shapes=[pltpu.VMEM((t_len, LANES), F32)] * 4,
        compiler_params=_params(VMEM_BIG),
    )(dy, h, proj, proj, xc, cw, wa, ba, wx, bx, lam)


QKV_WIDTH = ATTN_WIDTH + 2 * KV_WIDTH


def _rotate(v, cos, sin_up, sin_down):
    width = v.shape[1]
    return v * cos + pltpu.roll(v, width - ROPE_DIM // 2, 1) * sin_up + pltpu.roll(v, ROPE_DIM // 2, 1) * sin_down


def _rope_fwd(proj, cos, sin_up, sin_down, t_len, name):
    rows = proj.shape[0]
    nb = t_len // BLOCK
    n_seq = rows // t_len
    q0 = 2 * REC_WIDTH

    def body(q_ref, k_ref, v_ref, cos_ref, up_ref, down_ref, o_ref):
        cos_t, up_t, down_t = cos_ref[...], up_ref[...], down_ref[...]
        o_ref[:, 0:ATTN_WIDTH] = _rotate(q_ref[...], cos_t, up_t, down_t).astype(BF16)
        o_ref[:, ATTN_WIDTH : ATTN_WIDTH + KV_WIDTH] = _rotate(k_ref[...], cos_t[:, :KV_WIDTH], up_t[:, :KV_WIDTH], down_t[:, :KV_WIDTH]).astype(BF16)
        o_ref[:, ATTN_WIDTH + KV_WIDTH :] = v_ref[...].astype(BF16)

    tab = pl.BlockSpec((BLOCK, ATTN_WIDTH), lambda s, n: (n, 0))
    return pl.pallas_call(
        body,
        name=name,
        grid=(n_seq, nb),
        in_specs=[
            pl.BlockSpec((BLOCK, ATTN_WIDTH), lambda s, n: (s * nb + n, q0 // ATTN_WIDTH)),
            pl.BlockSpec((BLOCK, KV_WIDTH), lambda s, n: (s * nb + n, (q0 + ATTN_WIDTH) // KV_WIDTH)),
            pl.BlockSpec((BLOCK, KV_WIDTH), lambda s, n: (s * nb + n, (q0 + ATTN_WIDTH) // KV_WIDTH + 1)),
            tab,
            tab,
            tab,
        ],
        out_specs=pl.BlockSpec((BLOCK, QKV_WIDTH), lambda s, n: (s * nb + n, 0)),
        out_shape=jax.ShapeDtypeStruct((rows, QKV_WIDTH), BF16),
    )(proj, proj, proj, cos, sin_up, sin_down)


def _rope_bwd(dq, dk, dv, cos, sin_up, sin_down, t_len, name):
    rows = dq.shape[0]
    nb = t_len // BLOCK
    n_seq = rows // t_len

    def body(dq_ref, dk_ref, dv_ref, cos_ref, up_ref, down_ref, o_ref):
        cos_t, up_t, down_t = cos_ref[...], -up_ref[...], -down_ref[...]
        o_ref[:, 0:ATTN_WIDTH] = _rotate(dq_ref[...], cos_t, up_t, down_t).astype(BF16)
        o_ref[:, ATTN_WIDTH : ATTN_WIDTH + KV_WIDTH] = _rotate(dk_ref[...], cos_t[:, :KV_WIDTH], up_t[:, :KV_WIDTH], down_t[:, :KV_WIDTH]).astype(BF16)
        o_ref[:, ATTN_WIDTH + KV_WIDTH :] = dv_ref[...].astype(BF16)

    tab = pl.BlockSpec((BLOCK, ATTN_WIDTH), lambda s, n: (n, 0))
    blk = lambda w: pl.BlockSpec((BLOCK, w), lambda s, n: (s * nb + n, 0))
    return pl.pallas_call(
        body,
        name=name,
        grid=(n_seq, nb),
        in_specs=[blk(ATTN_WIDTH), blk(KV_WIDTH), blk(KV_WIDTH), tab, tab, tab],
        out_specs=blk(QKV_WIDTH),
        out_shape=jax.ShapeDtypeStruct((rows, QKV_WIDTH), BF16),
    )(dq, dk, dv, cos, sin_up, sin_down)


GROUP = 4


def _attn_mask(n, pad):
    q_pos = n * BLOCK + (lax.broadcasted_iota(jnp.int32, (GROUP * BLOCK, 1), 0) & (BLOCK - 1)) - pad
    col = lax.broadcasted_iota(jnp.int32, (1, 3 * BLOCK), 1)
    first = col < BLOCK
    k_pos = col - pad + jnp.where(first, 0, (n - 2) * BLOCK)
    dist = q_pos - k_pos
    band = (dist >= 0) & (dist < WINDOW) & (k_pos >= N_META) & jnp.logical_not(first)
    meta = (k_pos >= 0) & (k_pos < N_META) & (k_pos <= q_pos) & first
    return band | meta


def _in_half(e):
    lane = lax.broadcasted_iota(jnp.int32, (1, BLOCK), 1)
    return lane >= HEAD_DIM if e else lane < HEAD_DIM


def _head_views(q2, e, g):
    v = jnp.where(_in_half(e), q2, 0.0)
    return pltpu.roll(v, HEAD_DIM, 1) if e != g else v


def _tile(g, t):
    j = (GROUP // 2) * g + t
    return slice(j * BLOCK, (j + 1) * BLOCK)


def _stack_rows(ref, g):
    parts = []
    for t in range(GROUP // 2):
        tile = ref[:, _tile(g, t)].astype(F32)
        parts += [_head_views(tile, e, g) for e in range(2)]
    return jnp.concatenate(parts, axis=0)


def _stack_group(q_ref, sink_ref, g):
    head = lax.broadcasted_iota(jnp.int32, (GROUP * BLOCK, 1), 0) >> (BLOCK.bit_length() - 1)
    sink = jnp.zeros((GROUP * BLOCK, 1), F32)
    for hh in range(GROUP):
        sink = jnp.where(head == hh, sink_ref[GROUP * g + hh : GROUP * g + hh + 1, 0:1], sink)
    return _stack_rows(q_ref, g), sink


def _unstack_pair(v, g, t):
    out = None
    for e in range(2):
        blk = v[(2 * t + e) * BLOCK : (2 * t + e + 1) * BLOCK, :]
        blk = pltpu.roll(blk, HEAD_DIM, 1) if e != g else blk
        out = blk if out is None else out + blk
    return out


def _softmax_with_sink(scores, mask, sink):
    s = jnp.where(mask, scores * (HEAD_DIM**-0.5), NEG_INF)
    m = jnp.maximum(jnp.max(s, axis=-1, keepdims=True), sink)
    p = jnp.exp(s - m)
    p_sink = jnp.exp(sink - m)
    inv = 1.0 / (jnp.sum(p, axis=-1, keepdims=True) + p_sink)
    return p * inv, p_sink * inv


def _kv_specs(nb):
    k_col = ATTN_WIDTH // KV_WIDTH
    specs = []
    for col in (k_col, k_col + 1):
        specs += [
            pl.BlockSpec((BLOCK, KV_WIDTH), lambda s, n, col=col: (s * nb, col)),
            pl.BlockSpec((BLOCK, KV_WIDTH), lambda s, n, col=col: (s * nb + jnp.maximum(n - 1, 0), col)),
            pl.BlockSpec((BLOCK, KV_WIDTH), lambda s, n, col=col: (s * nb + n, col)),
        ]
    return specs


def _attn_fwd(qkv, sinks, t_len, pad, name):
    rows = qkv.shape[0]
    nb = t_len // BLOCK
    n_seq = rows // t_len

    def body(q_ref, km_ref, kp_ref, kc_ref, vm_ref, vp_ref, vc_ref, sink_ref, o_ref):
        n = pl.program_id(1)
        mask = _attn_mask(n, pad)
        keys = jnp.concatenate([km_ref[...], kp_ref[...], kc_ref[...]], axis=0)
        vals = jnp.concatenate([vm_ref[...], vp_ref[...], vc_ref[...]], axis=0)
        for g in range(N_Q_HEADS // GROUP):
            qs, sink = _stack_group(q_ref, sink_ref, g)
            scores = lax.dot_general(qs.astype(BF16), keys, NT, preferred_element_type=F32)
            p, _ = _softmax_with_sink(scores, mask, sink)
            o = jnp.where(_in_half(g), jnp.dot(p.astype(BF16), vals, preferred_element_type=F32), 0.0)
            for t in range(GROUP // 2):
                o_ref[:, _tile(g, t)] = _unstack_pair(o, g, t)

    return pl.pallas_call(
        body,
        name=name,
        grid=(n_seq, nb),
        in_specs=[pl.BlockSpec((BLOCK, ATTN_WIDTH), lambda s, n: (s * nb + n, 0))] + _kv_specs(nb) + [_full((N_Q_HEADS, BLOCK))],
        out_specs=pl.BlockSpec((BLOCK, ATTN_WIDTH), lambda s, n: (s * nb + n, 0)),
        out_shape=jax.ShapeDtypeStruct((rows, ATTN_WIDTH), F32),
    )(qkv, qkv, qkv, qkv, qkv, qkv, qkv, sinks)


def _attn_bwd(qkv, sinks, o, do, t_len, pad, name):
    rows = qkv.shape[0]
    nb = t_len // BLOCK
    n_seq = rows // t_len

    def body(q_ref, km_ref, kp_ref, kc_ref, vm_ref, vp_ref, vc_ref, sink_ref, o_ref, do_ref, dq_ref, dk_ref, dv_ref, dsink_ref):
        s_id, n = pl.program_id(0), pl.program_id(1)

        @pl.when(n == 0)
        def _():
            dk_ref[...] = jnp.zeros_like(dk_ref)
            dv_ref[...] = jnp.zeros_like(dv_ref)

        @pl.when((n == 0) & (s_id == 0))
        def _():
            dsink_ref[...] = jnp.zeros_like(dsink_ref)

        mask = _attn_mask(n, pad)
        keys = jnp.concatenate([km_ref[...], kp_ref[...], kc_ref[...]], axis=0)
        vals = jnp.concatenate([vm_ref[...], vp_ref[...], vc_ref[...]], axis=0)
        dkeys = jnp.zeros((3 * BLOCK, KV_WIDTH), F32)
        dvals = jnp.zeros((3 * BLOCK, KV_WIDTH), F32)
        for g in range(N_Q_HEADS // GROUP):
            qs, sink = _stack_group(q_ref, sink_ref, g)
            qsb = qs.astype(BF16)
            scores = lax.dot_general(qsb, keys, NT, preferred_element_type=F32)
            p, p_sink = _softmax_with_sink(scores, mask, sink)
            dos = _stack_rows(do_ref, g)
            delta = jnp.sum(dos * _stack_rows(o_ref, g), axis=-1, keepdims=True)
            dosb = dos.astype(BF16)
            dp = lax.dot_general(dosb, vals, NT, preferred_element_type=F32)
            ds = (p * (dp - delta) * (HEAD_DIM**-0.5)).astype(BF16)
            dqs = jnp.where(_in_half(g), jnp.dot(ds, keys, preferred_element_type=F32), 0.0)
            for t in range(GROUP // 2):
                dq_ref[:, _tile(g, t)] = _unstack_pair(dqs, g, t)
            dkeys = dkeys + lax.dot_general(ds, qsb, TN, preferred_element_type=F32)
            dvals = dvals + lax.dot_general(p.astype(BF16), dosb, TN, preferred_element_type=F32)
            sink_term = p_sink * delta
            for hh in range(GROUP):
                head = GROUP * g + hh
                part = -jnp.sum(sink_term[hh * BLOCK : (hh + 1) * BLOCK, :], axis=0, keepdims=True)
                dsink_ref[head : head + 1, :] += jnp.broadcast_to(part, (1, BLOCK))
        r_prev = pl.multiple_of(jnp.maximum(n - 1, 0) * BLOCK, BLOCK)
        r_cur = pl.multiple_of(n * BLOCK, BLOCK)
        for acc, d in ((dk_ref, dkeys), (dv_ref, dvals)):
            acc[0:BLOCK, :] += d[0:BLOCK]
            acc[pl.ds(r_prev, BLOCK), :] += d[BLOCK : 2 * BLOCK]
            acc[pl.ds(r_cur, BLOCK), :] += d[2 * BLOCK :]

    q_blk = pl.BlockSpec((BLOCK, ATTN_WIDTH), lambda s, n: (s * nb + n, 0))
    seq_kv = pl.BlockSpec((t_len, KV_WIDTH), lambda s, n: (s, 0))
    return pl.pallas_call(
        body,
        name=name,
        grid=(n_seq, nb),
        in_specs=[q_blk] + _kv_specs(nb) + [_full((N_Q_HEADS, BLOCK)), q_blk, q_blk],
        out_specs=[q_blk, seq_kv, seq_kv, _full((N_Q_HEADS, BLOCK))],
        out_shape=[
            jax.ShapeDtypeStruct((rows, ATTN_WIDTH), F32),
            jax.ShapeDtypeStruct((rows, KV_WIDTH), F32),
            jax.ShapeDtypeStruct((rows, KV_WIDTH), F32),
            jax.ShapeDtypeStruct((N_Q_HEADS, BLOCK), F32),
        ],
    )(qkv, qkv, qkv, qkv, qkv, qkv, qkv, sinks, o, do)


def _mix_out_fwd(y_rec, y_attn, h, wbuf, loc, g_rec, g_attn, ln_g, ln_b, alpha, name):
    rows, d = h.shape
    tm = _row_tile(rows)

    def body(yr_ref, ya_ref, h_ref, w_hbm, gr_ref, ga_ref, lg_ref, lb_ref, ho_ref, z_ref, w_ref, sems):
        _load_blocks(w_hbm, [(*loc, w_ref)], sems)
        nr = _rms(yr_ref[...], gr_ref[...]).astype(BF16)
        na = _rms(ya_ref[...], ga_ref[...]).astype(BF16)
        m = jnp.dot(nr, w_ref[0:REC_WIDTH, :], preferred_element_type=F32) + jnp.dot(na, w_ref[REC_WIDTH:, :], preferred_element_type=F32)
        z = alpha * h_ref[...] + m
        z_ref[...] = z
        xhat, _ = _ln_stats(z)
        ho_ref[...] = xhat * lg_ref[...] + lb_ref[...]

    row_d = pl.BlockSpec((tm, d), lambda i: (i, 0))
    row_h = pl.BlockSpec((tm, REC_WIDTH), lambda i: (i, 0))
    return pl.pallas_call(
        body,
        name=name,
        grid=(rows // tm,),
        in_specs=[row_h, row_h, row_d, ANY, _full((1, REC_WIDTH)), _full((1, ATTN_WIDTH)), _full((1, d)), _full((1, d))],
        out_specs=[row_d, row_d],
        out_shape=[jax.ShapeDtypeStruct((rows, d), F32)] * 2,
        scratch_shapes=_weight_scratch([loc], d),
    )(y_rec, y_attn, h, wbuf, g_rec, g_attn, ln_g, ln_b)


def _mix_out_bwd(dh, z, y_rec, y_attn, wbuf, loc, g_rec, g_attn, ln_g, name):
    rows, d = dh.shape
    tm = _row_tile(rows)
    mix = REC_WIDTH + ATTN_WIDTH

    def body(dh_ref, z_ref, yr_ref, ya_ref, w_hbm, gr_ref, ga_ref, lg_ref, dz_ref, dzb_ref, dyr_ref, dya_ref, yn_ref, dgam_ref, dbet_ref, dgr_ref, dga_ref, w_ref, sems):
        _load_blocks(w_hbm, [(*loc, w_ref)], sems)

        @pl.when(pl.program_id(0) == 0)
        def _():
            for ref in (dgam_ref, dbet_ref, dgr_ref, dga_ref):
                ref[...] = jnp.zeros_like(ref)

        dz, dgam, dbet = _ln_bwd(dh_ref[...], z_ref[...], lg_ref[...])
        dgam_ref[...] += dgam
        dbet_ref[...] += dbet
        dz_ref[...] = dz
        dzb = dz.astype(BF16)
        dzb_ref[...] = dzb
        dyn = lax.dot_general(dzb, w_ref[...], NT, preferred_element_type=F32)
        dyr, dgr, nr = _rms_bwd(dyn[:, 0:REC_WIDTH], yr_ref[...], gr_ref[...])
        dya, dga, na = _rms_bwd(dyn[:, REC_WIDTH:], ya_ref[...], ga_ref[...])
        dyr_ref[...] = dyr
        dya_ref[...] = dya
        dgr_ref[...] += dgr
        dga_ref[...] += dga
        yn_ref[:, 0:REC_WIDTH] = nr.astype(BF16)
        yn_ref[:, REC_WIDTH:] = na.astype(BF16)

    row_d = pl.BlockSpec((tm, d), lambda i: (i, 0))
    row_h = pl.BlockSpec((tm, REC_WIDTH), lambda i: (i, 0))
    row_m = pl.BlockSpec((tm, mix), lambda i: (i, 0))
    return pl.pallas_call(
        body,
        name=name,
        grid=(rows // tm,),
        in_specs=[row_d, row_d, row_h, row_h, ANY, _full((1, REC_WIDTH)), _full((1, ATTN_WIDTH)), _full((1, d))],
        out_specs=[row_d, row_d, row_h, row_h, row_m, _full((1, d)), _full((1, d)), _full((1, REC_WIDTH)), _full((1, ATTN_WIDTH))],
        out_shape=[
            jax.ShapeDtypeStruct((rows, d), F32),
            jax.ShapeDtypeStruct((rows, d), BF16),
            jax.ShapeDtypeStruct((rows, REC_WIDTH), F32),
            jax.ShapeDtypeStruct((rows, ATTN_WIDTH), F32),
            jax.ShapeDtypeStruct((rows, mix), BF16),
            jax.ShapeDtypeStruct((1, d), F32),
            jax.ShapeDtypeStruct((1, d), F32),
            jax.ShapeDtypeStruct((1, REC_WIDTH), F32),
            jax.ShapeDtypeStruct((1, ATTN_WIDTH), F32),
        ],
        scratch_shapes=_weight_scratch([loc], d),
    )(dh, z, y_rec, y_attn, wbuf, g_rec, g_attn, ln_g)


def _loss_head(y, target, t_len, first_token, name):
    rows, d = y.shape
    tm = _row_tile(rows)

    def body(y_ref, t_ref, loss_ref, dy_ref):
        i = pl.program_id(0)

        @pl.when(i == 0)
        def _():
            loss_ref[...] = jnp.zeros_like(loss_ref)

        row = i * tm + lax.broadcasted_iota(jnp.int32, (tm, 1), 0)
        is_token = lax.rem(row, t_len) >= first_token
        err = jnp.where(is_token, y_ref[...] - t_ref[...], 0.0)
        dy_ref[...] = err / d
        per_row = jnp.sum(err * err, axis=-1, keepdims=True) / d
        loss_ref[...] += jnp.broadcast_to(0.5 * jnp.sum(per_row, axis=0, keepdims=True), (1, BLOCK))

    row_d = pl.BlockSpec((tm, d), lambda i: (i, 0))
    return pl.pallas_call(
        body,
        name=name,
        grid=(rows // tm,),
        in_specs=[row_d, row_d],
        out_specs=[_full((1, BLOCK)), row_d],
        out_shape=[jax.ShapeDtypeStruct((1, BLOCK), F32), jax.ShapeDtypeStruct((rows, d), F32)],
    )(y, target)


def _meta_grad(dh0, t_len, pad, name):
    rows, d = dh0.shape
    nb = t_len // BLOCK
    n_seq = rows // t_len

    def body(dh_ref, o_ref):
        @pl.when(pl.program_id(0) == 0)
        def _():
            o_ref[...] = jnp.zeros_like(o_ref)

        o_ref[...] += dh_ref[pad : pad + N_META, :]

    return pl.pallas_call(
        body,
        name=name,
        grid=(n_seq,),
        in_specs=[pl.BlockSpec((BLOCK, d), lambda s: (s * nb, 0))],
        out_specs=_full((N_META, d)),
        out_shape=jax.ShapeDtypeStruct((N_META, d), F32),
    )(dh0)


def _adamw(w, g, m, v, name):
    shape = w.shape
    cols = shape[-1]
    rows = w.size // cols
    tr = rows
    for cand in (512, 256, 128, 64):
        if rows % cand == 0 and rows > cand:
            tr = cand
            break

    def body(w_ref, g_ref, m_ref, v_ref, d_ref, nm_ref, nv_ref):
        gv = g_ref[...]
        nm = ADAM_B1 * m_ref[...] + (1.0 - ADAM_B1) * gv
        nv = ADAM_B2 * v_ref[...] + (1.0 - ADAM_B2) * (gv * gv)
        m_hat = nm / (1.0 - ADAM_B1**ADAM_STEP)
        v_hat = nv / (1.0 - ADAM_B2**ADAM_STEP)
        d_ref[...] = -ADAM_LR * (m_hat / (jnp.sqrt(v_hat) + ADAM_EPS) + ADAM_WD * w_ref[...])
        nm_ref[...] = nm
        nv_ref[...] = nv

    blk = pl.BlockSpec((tr, cols), lambda i: (i, 0))
    flat = [a.reshape(rows, cols) for a in (w, g, m, v)]
    outs = pl.pallas_call(
        body,
        name=name,
        grid=(rows // tr,),
        in_specs=[blk] * 4,
        out_specs=[blk] * 3,
        out_shape=[jax.ShapeDtypeStruct((rows, cols), F32)] * 3,
    )(*flat)
    return [o.reshape(shape) for o in outs]


WEIGHTS = ["meta_tokens", "ffn1_w_gate", "ffn1_w_up", "ffn1_w_down", "ln1_g", "ln1_b", "w_in", "conv_w", "conv_b", "gate_a_w", "gate_a_b",
           "gate_x_w", "gate_x_b", "lru_lambda", "attn_sinks", "norm_rec_g", "norm_attn_g", "w_out", "ln2_g", "ln2_b", "ffn2_w_gate",
           "ffn2_w_up", "ffn2_w_down", "ln3_g", "ln3_b"]
BIG = [("ffn1_w_gate", True), ("ffn1_w_up", True), ("ffn1_w_down", False), ("w_in", True), ("w_out", False),
       ("ffn2_w_gate", True), ("ffn2_w_up", True), ("ffn2_w_down", False)]


def _rope_tables(t_len, pad):
    pos = (jnp.arange(t_len) - pad).astype(F32)
    inv_freq = ROPE_THETA ** (-jnp.arange(0, ROPE_DIM, 2, dtype=F32) / ROPE_DIM)
    ang = pos[:, None] * inv_freq[None, :]
    cos, sin = jnp.cos(ang), jnp.sin(ang)
    half = ROPE_DIM // 2
    rest = jnp.zeros((t_len, HEAD_DIM - ROPE_DIM), F32)
    zero = jnp.zeros((t_len, half), F32)
    cos_h = jnp.concatenate([cos, cos, rest + 1.0], axis=1)
    up_h = jnp.concatenate([-sin, zero, rest], axis=1)
    down_h = jnp.concatenate([zero, sin, rest], axis=1)
    return [jnp.tile(t, (1, ATTN_WIDTH // HEAD_DIM)) for t in (cos_h, up_h, down_h)]


def _gate_tiles(w):
    z = jnp.zeros((HEAD_DIM, HEAD_DIM), w.dtype)
    tiles = [jnp.block([[w[2 * j], z], [z, w[2 * j + 1]]]) for j in range(w.shape[0] // 2)]
    return jnp.stack(tiles).astype(BF16)


def _gate_blocks(tiles):
    out = []
    for j in range(tiles.shape[0]):
        out += [tiles[j, :HEAD_DIM, :HEAD_DIM], tiles[j, HEAD_DIM:, HEAD_DIM:]]
    return jnp.stack(out)


def kernel(x, meta_tokens, ffn1_w_gate, ffn1_w_up, ffn1_w_down, ln1_g, ln1_b, w_in, conv_w, conv_b, gate_a_w, gate_a_b, gate_x_w, gate_x_b, lru_lambda, attn_sinks, norm_rec_g, norm_attn_g, w_out, ln2_g, ln2_b, ffn2_w_gate, ffn2_w_up, ffn2_w_down, ln3_g, ln3_b, loss_target, m_meta_tokens, m_ffn1_w_gate, m_ffn1_w_up, m_ffn1_w_down, m_ln1_g, m_ln1_b, m_w_in, m_conv_w, m_conv_b, m_gate_a_w, m_gate_a_b, m_gate_x_w, m_gate_x_b, m_lru_lambda, m_attn_sinks, m_norm_rec_g, m_norm_attn_g, m_w_out, m_ln2_g, m_ln2_b, m_ffn2_w_gate, m_ffn2_w_up, m_ffn2_w_down, m_ln3_g, m_ln3_b, v_meta_tokens, v_ffn1_w_gate, v_ffn1_w_up, v_ffn1_w_down, v_ln1_g, v_ln1_b, v_w_in, v_conv_w, v_conv_b, v_gate_a_w, v_gate_a_b, v_gate_x_w, v_gate_x_b, v_lru_lambda, v_attn_sinks, v_norm_rec_g, v_norm_attn_g, v_w_out, v_ln2_g, v_ln2_b, v_ffn2_w_gate, v_ffn2_w_up, v_ffn2_w_down, v_ln3_g, v_ln3_b):
    given = dict(locals())
    w = {k: given[k] for k in WEIGHTS}
    n_seq, seq, d = x.shape
    depth = ln1_g.shape[0]
    alpha = (2.0 * depth) ** 0.25
    pad = (-(N_META + seq)) % BLOCK
    t_len = pad + N_META + seq
    rows = n_seq * t_len
    me = 4 * lax.axis_index("x") + 2 * lax.axis_index("y") + lax.axis_index("c")

    sent = {name: (_transpose_to_bf16(w[name], "t_" + name) if transposed else w[name].astype(BF16)) for name, transposed in BIG}
    loc, off = {}, 0
    for name, _ in BIG:
        loc[name] = (off, sent[name].shape[1])
        off += sent[name].shape[1]
    ffn1_loc = [loc["ffn1_w_gate"], loc["ffn1_w_up"], loc["ffn1_w_down"]]
    ffn2_loc = [loc["ffn2_w_gate"], loc["ffn2_w_up"], loc["ffn2_w_down"]]
    gathers = [_push_start([jnp.concatenate([sent[name][l] for name, _ in BIG], axis=0)], False, f"gather_start_{l}") for l in range(depth)]
    started = sum(g[4][0, 0] for g in gathers)
    small = jnp.concatenate([meta_tokens, conv_w.reshape(-1, BLOCK)], axis=0)
    (small_all,) = _all_gather([small], "gather_small")
    meta_full = small_all[:, :N_META, :].transpose(1, 0, 2).reshape(N_META, d)
    conv_shard = conv_w.shape[-1]
    conv_full = small_all[:, N_META:, :].reshape(N_DEV, depth, CONV_WIDTH, conv_shard).transpose(1, 2, 0, 3).reshape(depth, CONV_WIDTH, REC_WIDTH)

    cos, sin_up, sin_down = _rope_tables(t_len, pad)
    row1 = lambda a: a.reshape(1, -1)

    h = jnp.concatenate([jnp.zeros((n_seq, pad, d), F32), jnp.broadcast_to(meta_full[None], (n_seq, N_META, d)), x], axis=1).reshape(rows, d) + started
    target = jnp.pad(loss_target, ((0, 0), (pad + N_META, 0), (0, 0))).reshape(rows, d)
    saved, wbufs = [], []
    for l in range(depth):
        (wbuf,) = _push_wait(*gathers[l][:4], h, False, f"gather_wait_{l}")
        wbufs.append(wbuf)
        s = {"h0": h}
        h, s["g1"], s["u1"], s["z1"] = _ffn_fwd(h, wbuf, ffn1_loc, row1(ln1_g[l]), row1(ln1_b[l]), alpha, "ffn1_fwd")
        s["h1"] = h
        s["wa"], s["wx"] = _gate_tiles(gate_a_w[l]), _gate_tiles(gate_x_w[l])
        s["sinks"] = jnp.broadcast_to(attn_sinks[l][:, None], (N_Q_HEADS, BLOCK))
        proj = _proj_fwd(h, wbuf, loc["w_in"], "proj_fwd")
        s["proj"] = proj
        s["xc"], s["hrec"], s["y_rec"] = _rglru_fwd(proj, conv_full[l], row1(conv_b[l]), s["wa"], row1(gate_a_b[l]), s["wx"], row1(gate_x_b[l]),
                                                     row1(lru_lambda[l]), t_len, pad, "rglru_fwd")
        s["qkv"] = _rope_fwd(proj, cos, sin_up, sin_down, t_len, "rope_fwd")
        s["y_attn"] = _attn_fwd(s["qkv"], s["sinks"], t_len, pad, "attn_fwd")
        h, s["z2"] = _mix_out_fwd(s["y_rec"], s["y_attn"], h, wbuf, loc["w_out"], row1(norm_rec_g[l]), row1(norm_attn_g[l]), row1(ln2_g[l]), row1(ln2_b[l]), alpha, "mix_out_fwd")
        s["h2"] = h
        h, s["g2"], s["u2"], s["z3"] = _ffn_fwd(h, wbuf, ffn2_loc, row1(ln3_g[l]), row1(ln3_b[l]), alpha, "ffn2_fwd")
        saved.append(s)
    loss_part, dh = _loss_head(h, target, t_len, pad + N_META, "loss_head")

    exchanges = [None] * depth
    small_grads = [None] * depth
    sent_token = 0.0
    for l in reversed(range(depth)):
        wbuf, s = wbufs[l], saved[l]
        bg, sg = {}, {}
        dh, dg, du, act, dyb, sg["ln3_g"], sg["ln3_b"] = _ffn_bwd(dh, s["z3"], s["g2"], s["u2"], wbuf, ffn2_loc, row1(ln3_g[l]) + sent_token, alpha, "ffn2_bwd")
        bg["ffn2_w_gate"] = _mm_tn(dg, s["h2"], BF16, "ffn_wgrad_in")
        bg["ffn2_w_up"] = _mm_tn(du, s["h2"], BF16, "ffn_wgrad_in")
        bg["ffn2_w_down"] = _mm_tn(act, dyb, BF16, "ffn_wgrad_down")
        dz, dzb, dy_rec, dy_attn, yn, sg["ln2_g"], sg["ln2_b"], sg["norm_rec_g"], sg["norm_attn_g"] = _mix_out_bwd(
            dh, s["z2"], s["y_rec"], s["y_attn"], wbuf, loc["w_out"], row1(norm_rec_g[l]), row1(norm_attn_g[l]), row1(ln2_g[l]), "mix_out_bwd")
        bg["w_out"] = _mm_tn(yn, dzb, BF16, "w_out_wgrad")
        dxr, dgt, sg["conv_w"], sg["conv_b"], dwa, sg["gate_a_b"], dwx, sg["gate_x_b"], sg["lru_lambda"] = _rglru_bwd(
            dy_rec, s["hrec"], s["proj"], s["xc"], conv_full[l], s["wa"], row1(gate_a_b[l]), s["wx"], row1(gate_x_b[l]), row1(lru_lambda[l]), t_len, pad, "rglru_bwd")
        sg["gate_a_w"], sg["gate_x_w"] = _gate_blocks(dwa), _gate_blocks(dwx)
        dq, dk, dv, dsink = _attn_bwd(s["qkv"], s["sinks"], s["y_attn"], dy_attn, t_len, pad, "attn_bwd")
        sg["attn_sinks"] = dsink[:, 0]
        dqkv = _rope_bwd(dq, dk, dv, cos, sin_up, sin_down, t_len, "rope_bwd")
        dproj = jnp.concatenate([dxr, dgt, dqkv], axis=1)
        bg["w_in"] = _mm_tn(dproj, s["h1"], BF16, "w_in_wgrad")
        dh = _proj_bwd(dproj, wbuf, loc["w_in"], dz, alpha, "proj_bwd")
        dh, dg, du, act, dyb, sg["ln1_g"], sg["ln1_b"] = _ffn_bwd(dh, s["z1"], s["g1"], s["u1"], wbuf, ffn1_loc, row1(ln1_g[l]), alpha, "ffn1_bwd")
        bg["ffn1_w_gate"] = _mm_tn(dg, s["h0"], BF16, "ffn_wgrad_in")
        bg["ffn1_w_up"] = _mm_tn(du, s["h0"], BF16, "ffn_wgrad_in")
        bg["ffn1_w_down"] = _mm_tn(act, dyb, BF16, "ffn_wgrad_down")
        exchanges[l] = _push_start([bg[name].reshape(N_DEV, -1, d) for name, _ in BIG], True, f"exchange_start_{l}")
        sent_token = exchanges[l][4][0, 0]
        small_grads[l] = sg
    grad_x = dh.reshape(n_seq, t_len, d)[:, pad + N_META :, :]
    dmeta = _meta_grad(dh, t_len, pad, "meta_grad")

    reduced = {}
    after = exchanges[0][4]
    for l in reversed(range(depth)):
        parts = _push_wait(*exchanges[l][:4], after, True, f"exchange_wait_{l}")
        for (name, _), p in zip(BIG, parts):
            reduced[(l, name)] = _sum8(p, "sum_grads")
        after = reduced[(l, BIG[-1][0])]
    grads = {}
    for name, transposed in BIG:
        per_layer = [reduced[(l, name)] for l in range(depth)]
        if transposed:
            per_layer = [_transpose_f32(g, "t_grad") for g in per_layer]
        grads[name] = jnp.stack(per_layer)

    small_names = ["ln1_g", "ln1_b", "ln2_g", "ln2_b", "ln3_g", "ln3_b", "conv_b", "gate_a_b", "gate_x_b", "lru_lambda", "norm_rec_g",
                   "norm_attn_g", "conv_w", "gate_a_w", "gate_x_w", "attn_sinks"]
    pieces = [small_grads[l][name].reshape(-1) for l in range(depth) for name in small_names] + [dmeta.reshape(-1), loss_part[0, :1]]
    sizes = [p.shape[0] for p in pieces]
    flat = jnp.concatenate(pieces)
    width = 1024
    n_rows = -(-flat.shape[0] // (8 * width)) * 8
    flat = jnp.pad(flat, (0, n_rows * width - flat.shape[0])).reshape(n_rows, width)
    (flat_all,) = _all_gather([flat], "gather_small_grads")
    total = _sum8(flat_all, "sum_small_grads").reshape(-1)
    offs = [0]
    for sz in sizes:
        offs.append(offs[-1] + sz)
    taken = [total[offs[k] : offs[k + 1]] for k in range(len(sizes))]
    for j, name in enumerate(small_names):
        full_shape = (depth,) + ((CONV_WIDTH, REC_WIDTH) if name == "conv_w" else w[name].shape[1:])
        grads[name] = jnp.stack([taken[l * len(small_names) + j] for l in range(depth)]).reshape(full_shape)
    grads["conv_w"] = lax.dynamic_slice_in_dim(grads["conv_w"], me * conv_shard, conv_shard, axis=2)
    meta_shard = meta_tokens.shape[1]
    grads["meta_tokens"] = lax.dynamic_slice_in_dim(taken[-2].reshape(N_META, d), me * meta_shard, meta_shard, axis=1)
    loss = taken[-1][0]

    deltas, new_m, new_v = {}, {}, {}
    for name in WEIGHTS:
        deltas[name], new_m[name], new_v[name] = _adamw(w[name], grads[name], given["m_" + name], given["v_" + name], "adamw")
    return (loss, grad_x, *[grads[k] for k in WEIGHTS], *[deltas[k] for k in WEIGHTS], *[new_m[k] for k in WEIGHTS], *[new_v[k] for k in WEIGHTS])
```

```python
import functools

import jax
import jax.numpy as jnp
from jax import lax
from jax.experimental import pallas as pl
from jax.experimental.pallas import tpu as pltpu

F32 = jnp.float32
BF16 = jnp.bfloat16
MESH = pl.DeviceIdType.MESH

N_DEV = 8
N_META = 16
BLOCK = 128
WINDOW = 128
REC_WIDTH = 512
ATTN_WIDTH = 512
KV_WIDTH = 128
HEAD_DIM = 64
N_Q_HEADS = 8
ROPE_DIM = 16
ROPE_THETA = 500000.0
CONV_WIDTH = 4
LRU_C = 8.0
LN_EPS = 1e-5
RMS_EPS = 1e-6
NEG_INF = -1e30
ADAM_LR, ADAM_B1, ADAM_B2, ADAM_EPS, ADAM_WD, ADAM_STEP = 0.001, 0.9, 0.999, 1e-08, 0.01, 10

NT = (((1,), (1,)), ((), ()))
TN = (((0,), (0,)), ((), ()))
VMEM_BIG = 56 * 1024 * 1024


def _params(vmem=None):
    return pltpu.CompilerParams(vmem_limit_bytes=vmem)


def _row_tile(rows):
    return 256 if rows % 256 == 0 else 128


def _sigmoid(x):
    return 1.0 / (1.0 + jnp.exp(-x))


def _expm1_nonpos(x):
    series = x * (1.0 + 0.5 * x * (1.0 + x / 3.0 * (1.0 + 0.25 * x * (1.0 + 0.2 * x))))
    return jnp.where(x > -0.05, series, jnp.exp(x) - 1.0)


def _softplus(x):
    e = jnp.exp(-jnp.abs(x))
    log1p = jnp.where(e < 1e-3, e * (1.0 - e * (0.5 - e / 3.0)), jnp.log(1.0 + e))
    return jnp.maximum(x, 0.0) + log1p


def _gelu_and_grad(x):
    c0 = 0.7978845608028654
    x2 = x * x
    t = jnp.tanh(c0 * (x + 0.044715 * x * x2))
    gelu = 0.5 * x * (1.0 + t)
    grad = 0.5 * (1.0 + t) + 0.5 * x * (1.0 - t * t) * c0 * (1.0 + 3.0 * 0.044715 * x2)
    return gelu, grad


def _ln_stats(z):
    mu = jnp.mean(z, axis=-1, keepdims=True)
    zc = z - mu
    var = jnp.mean(zc * zc, axis=-1, keepdims=True)
    rstd = lax.rsqrt(var + LN_EPS)
    return zc * rstd, rstd


def _ln_bwd(dy, z, gamma):
    xhat, rstd = _ln_stats(z)
    dxh = dy * gamma
    m1 = jnp.mean(dxh, axis=-1, keepdims=True)
    m2 = jnp.mean(dxh * xhat, axis=-1, keepdims=True)
    dz = rstd * (dxh - m1 - xhat * m2)
    return dz, jnp.sum(dy * xhat, axis=0, keepdims=True), jnp.sum(dy, axis=0, keepdims=True)


def _rms(y, gamma):
    r = lax.rsqrt(jnp.mean(y * y, axis=-1, keepdims=True) + RMS_EPS)
    return y * r * gamma


def _rms_bwd(dout, y, gamma):
    r = lax.rsqrt(jnp.mean(y * y, axis=-1, keepdims=True) + RMS_EPS)
    n = y * r
    dn = dout * gamma
    dy = r * (dn - n * jnp.mean(dn * n, axis=-1, keepdims=True))
    return dy, jnp.sum(dout * n, axis=0, keepdims=True), n * gamma


def _load_blocks(w_hbm, items, sems):
    @pl.when(pl.program_id(0) == 0)
    def _():
        copies = []
        for k, (off, nr, dst) in enumerate(items):
            for dev in range(N_DEV):
                copies.append(pltpu.make_async_copy(w_hbm.at[dev, pl.ds(off, nr), :], dst.at[pl.ds(dev * nr, nr), :], sems.at[k, dev]))
        for cp in copies:
            cp.start()
        for cp in copies:
            cp.wait()


def _full(shape):
    return pl.BlockSpec(shape, lambda *_: (0,) * len(shape))


ANY = pl.BlockSpec(memory_space=pl.ANY)


def _all_gather(xs, name):
    n = len(xs)

    def body(*refs):
        ins, outs = refs[:n], refs[n : 2 * n]
        send_sems, recv_sems, local_sems = refs[2 * n :]
        x, y, c = lax.axis_index("x"), lax.axis_index("y"), lax.axis_index("c")
        me, sibling = (x, y, c), (x, y, 1 - c)
        chips = [(1 - x, y), (x, 1 - y), (1 - x, 1 - y)]

        def slot(i, dev):
            return outs[i].at[4 * dev[0] + 2 * dev[1] + dev[2]]

        def copy(i, k, block, to, src=None):
            return pltpu.make_async_remote_copy(
                src_ref=slot(i, block) if src is None else src,
                dst_ref=slot(i, block),
                send_sem=send_sems.at[i, k],
                recv_sem=recv_sems.at[i, k],
                device_id=to,
                device_id_type=MESH,
            )

        mine = [pltpu.make_async_copy(ins[i], slot(i, me), local_sems.at[i]) for i in range(n)]
        for cp in mine:
            cp.start()
        first = []
        for i in range(n):
            first.append(copy(i, 0, me, sibling, src=ins[i]))
            first += [copy(i, 1 + j, me, (*chip, c), src=ins[i]) for j, chip in enumerate(chips)]
        for cp in first:
            cp.start()
        passed = []
        for j, chip in enumerate(chips):
            for i in range(n):
                copy(i, 1 + j, (*chip, c), me).wait_recv()
                fwd = copy(i, 4 + j, (*chip, c), sibling)
                fwd.start()
                passed.append(fwd)
        for i in range(n):
            copy(i, 0, sibling, me).wait_recv()
            for j, chip in enumerate(chips):
                copy(i, 4 + j, (*chip, 1 - c), me).wait_recv()
        for cp in first + passed:
            cp.wait_send()
        for cp in mine:
            cp.wait()

    return pl.pallas_call(
        body,
        name=name,
        out_shape=[jax.ShapeDtypeStruct((N_DEV,) + a.shape, a.dtype) for a in xs],
        in_specs=[ANY] * n,
        out_specs=[ANY] * n,
        scratch_shapes=[pltpu.SemaphoreType.DMA((n, 7)), pltpu.SemaphoreType.DMA((n, 7)), pltpu.SemaphoreType.DMA((n,))],
    )(*xs)


HBM = pl.BlockSpec(memory_space=pltpu.HBM)
SEM = pl.BlockSpec(memory_space=pltpu.SEMAPHORE)
EFFECT = pltpu.SideEffectType.DATAFLOW_SIDE_EFFECTING


def _push_copies(ins, lands, send_sems, recv_sems, scatter):
    x, y, c = lax.axis_index("x"), lax.axis_index("y"), lax.axis_index("c")
    me = 4 * x + 2 * y + c
    copies = []
    for k in range(1, N_DEV):
        px = 1 - x if (k >> 2) & 1 else x
        py = 1 - y if (k >> 1) & 1 else y
        pc = 1 - c if k & 1 else c
        for i in range(len(ins)):
            copies.append(
                pltpu.make_async_remote_copy(
                    src_ref=ins[i].at[4 * px + 2 * py + pc] if scatter else ins[i],
                    dst_ref=lands[i].at[me],
                    send_sem=send_sems.at[i * (N_DEV - 1) + k - 1],
                    recv_sem=recv_sems.at[i * (N_DEV - 1) + k - 1],
                    device_id=(px, py, pc),
                    device_id_type=MESH,
                )
            )
    return copies


def _place_own(xs, scatter, name):
    n = len(xs)

    def body(*refs):
        ins, lands, sems = refs[:n], refs[n : 2 * n], refs[2 * n]
        me = 4 * lax.axis_index("x") + 2 * lax.axis_index("y") + lax.axis_index("c")
        mine = [pltpu.make_async_copy(ins[i].at[me] if scatter else ins[i], lands[i].at[me], sems.at[i]) for i in range(n)]
        for cp in mine:
            cp.start()
        for cp in mine:
            cp.wait()

    land_shapes = [a.shape if scatter else (N_DEV,) + a.shape for a in xs]
    return pl.pallas_call(
        body,
        name=name,
        out_shape=[jax.ShapeDtypeStruct(s, a.dtype) for s, a in zip(land_shapes, xs)],
        in_specs=[ANY] * n,
        out_specs=[ANY] * n,
        scratch_shapes=[pltpu.SemaphoreType.DMA((n,))],
    )(*xs)


def _push_start(xs, lands, chain, scatter, name):
    n = len(xs)

    def body(*refs):
        ins, zones = refs[:n], refs[n : 2 * n]
        for cp in _push_copies(ins, zones, refs[2 * n + 1], refs[2 * n + 2], scatter):
            cp.start()

    outs = pl.pallas_call(
        body,
        name=name,
        out_shape=(
            pltpu.SemaphoreType.DMA((n * (N_DEV - 1),)),
            pltpu.SemaphoreType.DMA((n * (N_DEV - 1),)),
            *[pltpu.HBM(a.shape, a.dtype) for a in list(xs) + list(lands)],
        ),
        in_specs=[HBM] * (2 * n) + [ANY],
        out_specs=(SEM, SEM, *[HBM] * (2 * n)),
        input_output_aliases={i: 2 + i for i in range(2 * n)},
        compiler_params=pltpu.CompilerParams(has_side_effects=EFFECT),
    )(*[pltpu.with_memory_space_constraint(a, pltpu.HBM) for a in list(xs) + list(lands)], chain)
    return outs[0], outs[1], list(outs[2 : 2 + n]), list(outs[2 + n : 2 + 2 * n])


def _push_wait(send_sems, recv_sems, srcs, lands, after, scatter, name):
    n = len(srcs)

    def body(*refs):
        ins, zones = refs[:n], refs[n : 2 * n]
        for cp in _push_copies(ins, zones, refs[2 * n], refs[2 * n + 1], scatter):
            cp.wait_send()
            cp.wait_recv()

    outs = pl.pallas_call(
        body,
        name=name,
        out_shape=[pltpu.HBM(a.shape, a.dtype) for a in srcs + lands],
        in_specs=[HBM] * (2 * n) + [SEM, SEM, ANY],
        out_specs=[HBM] * (2 * n),
        input_output_aliases={i: i for i in range(2 * n)},
        compiler_params=pltpu.CompilerParams(has_side_effects=EFFECT),
    )(*srcs, *lands, send_sems, recv_sems, after)
    return list(outs[n:])


def _sum8(parts, name):
    _, rows, cols = parts.shape
    tr = rows
    for cand in (512, 256, 128, 64, 32, 16):
        if rows % cand == 0 and rows > cand:
            tr = cand
            break

    def body(p_ref, o_ref):
        acc = p_ref[0].astype(F32)
        for d in range(1, N_DEV):
            acc = acc + p_ref[d].astype(F32)
        o_ref[...] = acc

    return pl.pallas_call(
        body,
        name=name,
        grid=(rows // tr,),
        in_specs=[pl.BlockSpec((N_DEV, tr, cols), lambda i: (0, i, 0))],
        out_specs=pl.BlockSpec((tr, cols), lambda i: (i, 0)),
        out_shape=jax.ShapeDtypeStruct((rows, cols), F32),
    )(parts)


def _mm_tn(a, b, out_dtype, name):
    rows, m = a.shape
    n = b.shape[1]
    tm = m // 2 if (m // 2) % 128 == 0 and m > 512 else m
    tr = next(t for t in (1088, 1024, 512, 256, 128) if rows % t == 0)
    nk = rows // tr

    def body(a_ref, b_ref, o_ref, acc):
        k = pl.program_id(1)
        part = lax.dot_general(a_ref[...].astype(BF16), b_ref[...].astype(BF16), TN, preferred_element_type=F32)

        @pl.when(k == 0)
        def _():
            acc[...] = part

        @pl.when(k > 0)
        def _():
            acc[...] += part

        @pl.when(k == nk - 1)
        def _():
            o_ref[...] = acc[...].astype(o_ref.dtype)

    return pl.pallas_call(
        body,
        name=name,
        grid=(m // tm, nk),
        in_specs=[pl.BlockSpec((tr, tm), lambda i, k: (k, i)), pl.BlockSpec((tr, n), lambda i, k: (k, 0))],
        out_specs=pl.BlockSpec((tm, n), lambda i, k: (i, 0)),
        out_shape=jax.ShapeDtypeStruct((m, n), out_dtype),
        scratch_shapes=[pltpu.VMEM((tm, n), F32)],
        compiler_params=_params(VMEM_BIG),
    )(a, b)


def _transpose_to_bf16(w, name):
    layers, a_dim, b_dim = w.shape

    def body(w_ref, o_ref):
        eye = (lax.broadcasted_iota(jnp.int32, (a_dim, a_dim), 0) == lax.broadcasted_iota(jnp.int32, (a_dim, a_dim), 1)).astype(BF16)
        o_ref[0] = lax.dot_general(w_ref[0].astype(BF16), eye, TN, preferred_element_type=F32).astype(BF16)

    return pl.pallas_call(
        body,
        name=name,
        grid=(layers,),
        in_specs=[pl.BlockSpec((1, a_dim, b_dim), lambda l: (l, 0, 0))],
        out_specs=pl.BlockSpec((1, b_dim, a_dim), lambda l: (l, 0, 0)),
        out_shape=jax.ShapeDtypeStruct((layers, b_dim, a_dim), BF16),
    )(w)


def _transpose_f32(g, name):
    a_dim, b_dim = g.shape

    def body(g_ref, o_ref):
        eye = (lax.broadcasted_iota(jnp.int32, (a_dim, a_dim), 0) == lax.broadcasted_iota(jnp.int32, (a_dim, a_dim), 1)).astype(BF16)
        v = g_ref[...]
        hi = v.astype(BF16)
        r1 = v - hi.astype(F32)
        mid = r1.astype(BF16)
        lo = (r1 - mid.astype(F32)).astype(BF16)
        out = lax.dot_general(hi, eye, TN, preferred_element_type=F32)
        out = out + lax.dot_general(mid, eye, TN, preferred_element_type=F32)
        out = out + lax.dot_general(lo, eye, TN, preferred_element_type=F32)
        o_ref[...] = out

    return pl.pallas_call(
        body,
        name=name,
        in_specs=[_full((a_dim, b_dim))],
        out_specs=_full((b_dim, a_dim)),
        out_shape=jax.ShapeDtypeStruct((b_dim, a_dim), F32),
        grid=(1,),
    )(g)


def _weight_scratch(locs, d):
    return [pltpu.VMEM((N_DEV * nr, d), BF16) for _, nr in locs] + [pltpu.SemaphoreType.DMA((len(locs), N_DEV))]


def _proj_fwd(h, wbuf, loc, name):
    rows, d = h.shape
    n = N_DEV * loc[1]
    tm = _row_tile(rows)

    def body(h_ref, w_hbm, o_ref, w, sems):
        _load_blocks(w_hbm, [(*loc, w)], sems)
        o_ref[...] = lax.dot_general(h_ref[...].astype(BF16), w[...], NT, preferred_element_type=F32)

    return pl.pallas_call(
        body,
        name=name,
        grid=(rows // tm,),
        in_specs=[pl.BlockSpec((tm, d), lambda i: (i, 0)), ANY],
        out_specs=pl.BlockSpec((tm, n), lambda i: (i, 0)),
        out_shape=jax.ShapeDtypeStruct((rows, n), F32),
        scratch_shapes=_weight_scratch([loc], d),
        compiler_params=_params(VMEM_BIG),
    )(h, wbuf)


def _proj_bwd(dproj, wbuf, loc, dz, alpha, name):
    rows, n = dproj.shape
    d = dz.shape[1]
    tm = _row_tile(rows)

    def body(dp_ref, w_hbm, dz_ref, o_ref, w, sems):
        _load_blocks(w_hbm, [(*loc, w)], sems)
        o_ref[...] = alpha * dz_ref[...] + jnp.dot(dp_ref[...], w[...], preferred_element_type=F32)

    return pl.pallas_call(
        body,
        name=name,
        grid=(rows // tm,),
        in_specs=[pl.BlockSpec((tm, n), lambda i: (i, 0)), ANY, pl.BlockSpec((tm, d), lambda i: (i, 0))],
        out_specs=pl.BlockSpec((tm, d), lambda i: (i, 0)),
        out_shape=jax.ShapeDtypeStruct((rows, d), F32),
        scratch_shapes=_weight_scratch([loc], d),
        compiler_params=_params(VMEM_BIG),
    )(dproj, wbuf, dz)


FFN_CHUNKS = 2


def _ffn_fwd(h, wbuf, locs, ln_g, ln_b, alpha, name):
    rows, d = h.shape
    f = N_DEV * locs[0][1]
    tm = _row_tile(rows)
    tf = f // FFN_CHUNKS

    def body(h_ref, w_hbm, lg_ref, lb_ref, ho_ref, g_ref, u_ref, z_ref, wg, wu, wdv, sems):
        _load_blocks(w_hbm, [(*locs[0], wg), (*locs[1], wu), (*locs[2], wdv)], sems)
        x = h_ref[...]
        xb = x.astype(BF16)
        y = None
        for c in range(FFN_CHUNKS):
            sl = slice(c * tf, (c + 1) * tf)
            g = lax.dot_general(xb, wg[sl, :], NT, preferred_element_type=F32)
            u = lax.dot_general(xb, wu[sl, :], NT, preferred_element_type=F32)
            g_ref[:, sl] = g
            u_ref[:, sl] = u
            a = (g * _sigmoid(g) * u).astype(BF16)
            part = jnp.dot(a, wdv[sl, :], preferred_element_type=F32)
            y = part if y is None else y + part
        z = alpha * x + 0.5 * y
        z_ref[...] = z
        xhat, _ = _ln_stats(z)
        ho_ref[...] = xhat * lg_ref[...] + lb_ref[...]

    row_d = pl.BlockSpec((tm, d), lambda i: (i, 0))
    row_f = pl.BlockSpec((tm, f), lambda i: (i, 0))
    return pl.pallas_call(
        body,
        name=name,
        grid=(rows // tm,),
        in_specs=[row_d, ANY, _full((1, d)), _full((1, d))],
        out_specs=[row_d, row_f, row_f, row_d],
        out_shape=[
            jax.ShapeDtypeStruct((rows, d), F32),
            jax.ShapeDtypeStruct((rows, f), F32),
            jax.ShapeDtypeStruct((rows, f), F32),
            jax.ShapeDtypeStruct((rows, d), F32),
        ],
        scratch_shapes=_weight_scratch(locs, d),
        compiler_params=_params(VMEM_BIG),
    )(h, wbuf, ln_g, ln_b)


def _ffn_bwd(dh, z, g, u, wbuf, locs, ln_g, after, alpha, name):
    rows, d = dh.shape
    f = N_DEV * locs[0][1]
    tm = _row_tile(rows)
    tf = f // FFN_CHUNKS

    def body(dh_ref, z_ref, g_ref, u_ref, w_hbm, lg_ref, after_ref, dx_ref, dg_ref, du_ref, a_ref, dyb_ref, dgam_ref, dbet_ref, wg, wu, wdv, sems):
        _load_blocks(w_hbm, [(*locs[0], wg), (*locs[1], wu), (*locs[2], wdv)], sems)

        @pl.when(pl.program_id(0) == 0)
        def _():
            dgam_ref[...] = jnp.zeros_like(dgam_ref)
            dbet_ref[...] = jnp.zeros_like(dbet_ref)

        dz, dgam, dbet = _ln_bwd(dh_ref[...], z_ref[...], lg_ref[...])
        dgam_ref[...] += dgam
        dbet_ref[...] += dbet
        dyb = (0.5 * dz).astype(BF16)
        dyb_ref[...] = dyb
        dx = alpha * dz
        for c in range(FFN_CHUNKS):
            sl = slice(c * tf, (c + 1) * tf)
            da = lax.dot_general(dyb, wdv[sl, :], NT, preferred_element_type=F32)
            gc = g_ref[:, sl]
            uc = u_ref[:, sl]
            sg = _sigmoid(gc)
            silu = gc * sg
            dgc = (da * uc * (sg * (1.0 + gc * (1.0 - sg)))).astype(BF16)
            duc = (da * silu).astype(BF16)
            a_ref[:, sl] = (silu * uc).astype(BF16)
            dg_ref[:, sl] = dgc
            du_ref[:, sl] = duc
            dx = dx + jnp.dot(dgc, wg[sl, :], preferred_element_type=F32) + jnp.dot(duc, wu[sl, :], preferred_element_type=F32)
        dx_ref[...] = dx

    row_d = pl.BlockSpec((tm, d), lambda i: (i, 0))
    row_f = pl.BlockSpec((tm, f), lambda i: (i, 0))
    return pl.pallas_call(
        body,
        name=name,
        grid=(rows // tm,),
        in_specs=[row_d, row_d, row_f, row_f, ANY, _full((1, d)), ANY],
        out_specs=[row_d, row_f, row_f, row_f, row_d, _full((1, d)), _full((1, d))],
        out_shape=[
            jax.ShapeDtypeStruct((rows, d), F32),
            jax.ShapeDtypeStruct((rows, f), BF16),
            jax.ShapeDtypeStruct((rows, f), BF16),
            jax.ShapeDtypeStruct((rows, f), BF16),
            jax.ShapeDtypeStruct((rows, d), BF16),
            jax.ShapeDtypeStruct((1, d), F32),
            jax.ShapeDtypeStruct((1, d), F32),
        ],
        scratch_shapes=_weight_scratch(locs, d),
        compiler_params=_params(VMEM_BIG),
    )(dh, z, g, u, wbuf, ln_g, after)


LANES = 128


def _gates(xc, wa_ref, ba, wx_ref, bx, lam):
    xcb = xc.astype(BF16)
    r = _sigmoid(jnp.dot(xcb, wa_ref[0], preferred_element_type=F32) + ba)
    i = _sigmoid(jnp.dot(xcb, wx_ref[0], preferred_element_type=F32) + bx)
    sp = _softplus(-lam)
    log_a = -LRU_C * sp * r
    a = jnp.exp(log_a)
    s = jnp.sqrt(-_expm1_nonpos(2.0 * log_a))
    return xcb, r, i, sp, a, s


def _conv(xr, cw, cb):
    return cb + cw[3:4] * xr + cw[2:3] * pltpu.roll(xr, 1, 0) + cw[1:2] * pltpu.roll(xr, 2, 0) + cw[0:1] * pltpu.roll(xr, 3, 0)


def _rglru_fwd(proj, cw, cb, wa, ba, wx, bx, lam, t_len, pad, name):
    rows = proj.shape[0]
    n_seq = rows // t_len
    n_tile = REC_WIDTH // LANES
    groups = t_len // 8

    def body(xr_ref, gt_ref, cw_ref, cb_ref, wa_ref, ba_ref, wx_ref, bx_ref, lam_ref, xc_ref, h_ref, y_ref, a_s, b_s):
        valid = lax.broadcasted_iota(jnp.int32, (t_len, 1), 0) >= pad
        xr = jnp.where(valid, xr_ref[...], 0.0)
        xc = _conv(xr, cw_ref[...], cb_ref[...])
        xc_ref[...] = xc
        _, _, i, _, a, s = _gates(xc, wa_ref, ba_ref[...], wx_ref, bx_ref[...], lam_ref[...])
        a_s[...] = a
        b_s[...] = jnp.where(valid, s * (i * xc), 0.0)
        sub = lax.broadcasted_iota(jnp.int32, (8, LANES), 0)

        def step(gi, carry):
            r0 = pl.multiple_of(gi * 8, 8)
            av = a_s[pl.ds(r0, 8), :]
            bv = b_s[pl.ds(r0, 8), :]
            for d in (1, 2, 4):
                m = sub >= d
                a_sh = pltpu.roll(av, d, 0)
                b_sh = pltpu.roll(bv, d, 0)
                bv = jnp.where(m, av * b_sh + bv, bv)
                av = jnp.where(m, av * a_sh, av)
            hv = av * carry + bv
            h_ref[pl.ds(r0, 8), :] = hv
            return jnp.broadcast_to(hv[7:8, :], (8, LANES))

        lax.fori_loop(0, groups, step, jnp.zeros((8, LANES), F32), unroll=4)
        gelu, _ = _gelu_and_grad(gt_ref[...])
        y_ref[...] = h_ref[...] * gelu

    seq_tile = lambda col0: pl.BlockSpec((t_len, LANES), lambda s, j: (s, col0 + j))
    vec = pl.BlockSpec((1, LANES), lambda s, j: (0, j))
    out = jax.ShapeDtypeStruct((rows, REC_WIDTH), F32)
    return pl.pallas_call(
        body,
        name=name,
        grid=(n_seq, n_tile),
        in_specs=[
            seq_tile(0),
            seq_tile(n_tile),
            pl.BlockSpec((CONV_WIDTH, LANES), lambda s, j: (0, j)),
            vec,
            pl.BlockSpec((1, LANES, LANES), lambda s, j: (j, 0, 0)),
            vec,
            pl.BlockSpec((1, LANES, LANES), lambda s, j: (j, 0, 0)),
            vec,
            vec,
        ],
        out_specs=[seq_tile(0)] * 3,
        out_shape=[out, out, out],
        scratch_shapes=[pltpu.VMEM((t_len, LANES), F32), pltpu.VMEM((t_len, LANES), F32)],
        compiler_params=_params(VMEM_BIG),
    )(proj, proj, cw, cb, wa, ba, wx, bx, lam)


def _rglru_bwd(dy, h, proj, xc, cw, wa, ba, wx, bx, lam, t_len, pad, name):
    rows = proj.shape[0]
    n_seq = rows // t_len
    n_tile = REC_WIDTH // LANES
    groups = t_len // 8

    def body(dy_ref, h_ref, xr_ref, gt_ref, xc_ref, cw_ref, wa_ref, ba_ref, wx_ref, bx_ref, lam_ref,
             dxr_ref, dgt_ref, dcw_ref, dcb_ref, dwa_ref, dba_ref, dwx_ref, dbx_ref, dlam_ref, a_s, c_s, db_s, da_s):
        first_seq = pl.program_id(1) == 0

        @pl.when(first_seq)
        def _():
            for ref in (dcw_ref, dcb_ref, dwa_ref, dba_ref, dwx_ref, dbx_ref, dlam_ref):
                ref[...] = jnp.zeros_like(ref)

        valid = lax.broadcasted_iota(jnp.int32, (t_len, 1), 0) >= pad
        gelu, dgelu = _gelu_and_grad(gt_ref[...])
        dyv = dy_ref[...]
        dho = dyv * gelu
        dgt_ref[...] = (dyv * h_ref[...] * dgelu).astype(BF16)
        xc = xc_ref[...]
        lam = lam_ref[...]
        xcb, r, i, sp, a, s = _gates(xc, wa_ref, ba_ref[...], wx_ref, bx_ref[...], lam)
        a_s[...] = a
        c_s[...] = a * dho
        db_s[...] = dho
        sub = lax.broadcasted_iota(jnp.int32, (8, LANES), 0)

        def step(k, carry):
            gi = groups - 1 - k
            r0 = pl.multiple_of(gi * 8, 8)
            av = a_s[pl.ds(r0, 8), :]
            cv = c_s[pl.ds(r0, 8), :]
            for d in (1, 2, 4):
                m = sub < 8 - d
                a_sh = pltpu.roll(av, 8 - d, 0)
                c_sh = pltpu.roll(cv, 8 - d, 0)
                cv = jnp.where(m, av * c_sh + cv, cv)
                av = jnp.where(m, av * a_sh, av)
            ev = av * carry + cv
            e_next = jnp.where(sub < 7, pltpu.roll(ev, 7, 0), carry)
            dht = db_s[pl.ds(r0, 8), :] + e_next
            hv = h_ref[pl.ds(r0, 8), :]
            rp = pl.multiple_of(jnp.maximum(gi - 1, 0) * 8, 8)
            h_before = jnp.where(gi > 0, jnp.broadcast_to(h_ref[pl.ds(rp, 8), :][7:8, :], (8, LANES)), 0.0)
            h_prev = jnp.where(sub >= 1, pltpu.roll(hv, 1, 0), h_before)
            db_s[pl.ds(r0, 8), :] = dht
            da_s[pl.ds(r0, 8), :] = dht * h_prev
            return jnp.broadcast_to(ev[0:1, :], (8, LANES))

        lax.fori_loop(0, groups, step, jnp.zeros((8, LANES), F32), unroll=4)

        db = jnp.where(valid, db_s[...], 0.0)
        da = da_s[...]
        ds = db * (i * xc)
        di = db * s * xc
        dxc = db * s * i
        dlog_a = da * a - ds * (a * a) / jnp.maximum(s, 1e-30)
        dr = dlog_a * (-LRU_C * sp)
        dsp = jnp.sum(dlog_a * (-LRU_C) * r, axis=0, keepdims=True)
        dlam_ref[...] += dsp * (-_sigmoid(-lam))
        dpr = dr * r * (1.0 - r)
        dpi = di * i * (1.0 - i)
        dprb = dpr.astype(BF16)
        dpib = dpi.astype(BF16)
        dxc = dxc + lax.dot_general(dprb, wa_ref[0], NT, preferred_element_type=F32) + lax.dot_general(dpib, wx_ref[0], NT, preferred_element_type=F32)
        dwa_ref[0] += lax.dot_general(xcb, dprb, TN, preferred_element_type=F32)
        dwx_ref[0] += lax.dot_general(xcb, dpib, TN, preferred_element_type=F32)
        dba_ref[...] += jnp.sum(dpr, axis=0, keepdims=True)
        dbx_ref[...] += jnp.sum(dpi, axis=0, keepdims=True)
        dxc = jnp.where(valid, dxc, 0.0)
        xr = jnp.where(valid, xr_ref[...], 0.0)
        dcb_ref[...] += jnp.sum(dxc, axis=0, keepdims=True)
        for k in range(CONV_WIDTH):
            shifted = xr if k == CONV_WIDTH - 1 else pltpu.roll(xr, CONV_WIDTH - 1 - k, 0)
            dcw_ref[k : k + 1, :] += jnp.sum(dxc * shifted, axis=0, keepdims=True)
        cw = cw_ref[...]
        dxr = cw[3:4] * dxc + cw[2:3] * pltpu.roll(dxc, t_len - 1, 0) + cw[1:2] * pltpu.roll(dxc, t_len - 2, 0) + cw[0:1] * pltpu.roll(dxc, t_len - 3, 0)
        dxr_ref[...] = jnp.where(valid, dxr, 0.0).astype(BF16)

    seq_tile = lambda col0: pl.BlockSpec((t_len, LANES), lambda j, s: (s, col0 + j))
    vec = pl.BlockSpec((1, LANES), lambda j, s: (0, j))
    mat = pl.BlockSpec((1, LANES, LANES), lambda j, s: (j, 0, 0))
    cwb = pl.BlockSpec((CONV_WIDTH, LANES), lambda j, s: (0, j))
    big = jax.ShapeDtypeStruct((rows, REC_WIDTH), BF16)
    vec_shape = jax.ShapeDtypeStruct((1, REC_WIDTH), F32)
    mat_shape = jax.ShapeDtypeStruct((n_tile, LANES, LANES), F32)
    return pl.pallas_call(
        body,
        name=name,
        grid=(n_tile, n_seq),
        in_specs=[seq_tile(0), seq_tile(0), seq_tile(0), seq_tile(n_tile), seq_tile(0), cwb, mat, vec, mat, vec, vec],
        out_specs=[seq_tile(0), seq_tile(0), cwb, vec, mat, vec, mat, vec, vec],
        out_shape=[big, big, jax.ShapeDtypeStruct((CONV_WIDTH, REC_WIDTH), F32), vec_shape, mat_shape, vec_shape, mat_shape, vec_shape, vec_shape],
        scratch_shapes=[pltpu.VMEM((t_len, LANES), F32)] * 4,
        compiler_params=_params(VMEM_BIG),
    )(dy, h, proj, proj, xc, cw, wa, ba, wx, bx, lam)


QKV_WIDTH = ATTN_WIDTH + 2 * KV_WIDTH


def _rotate(v, cos, sin_up, sin_down):
    width = v.shape[1]
    return v * cos + pltpu.roll(v, width - ROPE_DIM // 2, 1) * sin_up + pltpu.roll(v, ROPE_DIM // 2, 1) * sin_down


def _rope_fwd(proj, cos, sin_up, sin_down, t_len, name):
    rows = proj.shape[0]
    nb = t_len // BLOCK
    n_seq = rows // t_len
    q0 = 2 * REC_WIDTH

    def body(q_ref, k_ref, v_ref, cos_ref, up_ref, down_ref, o_ref):
        cos_t, up_t, down_t = cos_ref[...], up_ref[...], down_ref[...]
        o_ref[:, 0:ATTN_WIDTH] = _rotate(q_ref[...], cos_t, up_t, down_t).astype(BF16)
        o_ref[:, ATTN_WIDTH : ATTN_WIDTH + KV_WIDTH] = _rotate(k_ref[...], cos_t[:, :KV_WIDTH], up_t[:, :KV_WIDTH], down_t[:, :KV_WIDTH]).astype(BF16)
        o_ref[:, ATTN_WIDTH + KV_WIDTH :] = v_ref[...].astype(BF16)

    tab = pl.BlockSpec((BLOCK, ATTN_WIDTH), lambda s, n: (n, 0))
    return pl.pallas_call(
        body,
        name=name,
        grid=(n_seq, nb),
        in_specs=[
            pl.BlockSpec((BLOCK, ATTN_WIDTH), lambda s, n: (s * nb + n, q0 // ATTN_WIDTH)),
            pl.BlockSpec((BLOCK, KV_WIDTH), lambda s, n: (s * nb + n, (q0 + ATTN_WIDTH) // KV_WIDTH)),
            pl.BlockSpec((BLOCK, KV_WIDTH), lambda s, n: (s * nb + n, (q0 + ATTN_WIDTH) // KV_WIDTH + 1)),
            tab,
            tab,
            tab,
        ],
        out_specs=pl.BlockSpec((BLOCK, QKV_WIDTH), lambda s, n: (s * nb + n, 0)),
        out_shape=jax.ShapeDtypeStruct((rows, QKV_WIDTH), BF16),
    )(proj, proj, proj, cos, sin_up, sin_down)


def _rope_bwd(dq, dk, dv, cos, sin_up, sin_down, t_len, name):
    rows = dq.shape[0]
    nb = t_len // BLOCK
    n_seq = rows // t_len

    def body(dq_ref, dk_ref, dv_ref, cos_ref, up_ref, down_ref, o_ref):
        cos_t, up_t, down_t = cos_ref[...], -up_ref[...], -down_ref[...]
        o_ref[:, 0:ATTN_WIDTH] = _rotate(dq_ref[...], cos_t, up_t, down_t).astype(BF16)
        o_ref[:, ATTN_WIDTH : ATTN_WIDTH + KV_WIDTH] = _rotate(dk_ref[...], cos_t[:, :KV_WIDTH], up_t[:, :KV_WIDTH], down_t[:, :KV_WIDTH]).astype(BF16)
        o_ref[:, ATTN_WIDTH + KV_WIDTH :] = dv_ref[...].astype(BF16)

    tab = pl.BlockSpec((BLOCK, ATTN_WIDTH), lambda s, n: (n, 0))
    blk = lambda w: pl.BlockSpec((BLOCK, w), lambda s, n: (s * nb + n, 0))
    return pl.pallas_call(
        body,
        name=name,
        grid=(n_seq, nb),
        in_specs=[blk(ATTN_WIDTH), blk(KV_WIDTH), blk(KV_WIDTH), tab, tab, tab],
        out_specs=blk(QKV_WIDTH),
        out_shape=jax.ShapeDtypeStruct((rows, QKV_WIDTH), BF16),
    )(dq, dk, dv, cos, sin_up, sin_down)


GROUP = 4


def _attn_mask(n, pad):
    q_pos = n * BLOCK + (lax.broadcasted_iota(jnp.int32, (GROUP * BLOCK, 1), 0) & (BLOCK - 1)) - pad
    col = lax.broadcasted_iota(jnp.int32, (1, 3 * BLOCK), 1)
    first = col < BLOCK
    k_pos = col - pad + jnp.where(first, 0, (n - 2) * BLOCK)
    dist = q_pos - k_pos
    band = (dist >= 0) & (dist < WINDOW) & (k_pos >= N_META) & jnp.logical_not(first)
    meta = (k_pos >= 0) & (k_pos < N_META) & (k_pos <= q_pos) & first
    return band | meta


def _in_half(e):
    lane = lax.broadcasted_iota(jnp.int32, (1, BLOCK), 1)
    return lane >= HEAD_DIM if e else lane < HEAD_DIM


def _head_views(q2, e, g):
    v = jnp.where(_in_half(e), q2, 0.0)
    return pltpu.roll(v, HEAD_DIM, 1) if e != g else v


def _tile(g, t):
    j = (GROUP // 2) * g + t
    return slice(j * BLOCK, (j + 1) * BLOCK)


def _stack_rows(ref, g):
    parts = []
    for t in range(GROUP // 2):
        tile = ref[:, _tile(g, t)].astype(F32)
        parts += [_head_views(tile, e, g) for e in range(2)]
    return jnp.concatenate(parts, axis=0)


def _stack_group(q_ref, sink_ref, g):
    head = lax.broadcasted_iota(jnp.int32, (GROUP * BLOCK, 1), 0) >> (BLOCK.bit_length() - 1)
    sink = jnp.zeros((GROUP * BLOCK, 1), F32)
    for hh in range(GROUP):
        sink = jnp.where(head == hh, sink_ref[GROUP * g + hh : GROUP * g + hh + 1, 0:1], sink)
    return _stack_rows(q_ref, g), sink


def _unstack_pair(v, g, t):
    out = None
    for e in range(2):
        blk = v[(2 * t + e) * BLOCK : (2 * t + e + 1) * BLOCK, :]
        blk = pltpu.roll(blk, HEAD_DIM, 1) if e != g else blk
        out = blk if out is None else out + blk
    return out


def _softmax_with_sink(scores, mask, sink):
    s = jnp.where(mask, scores * (HEAD_DIM**-0.5), NEG_INF)
    m = jnp.maximum(jnp.max(s, axis=-1, keepdims=True), sink)
    p = jnp.exp(s - m)
    p_sink = jnp.exp(sink - m)
    inv = 1.0 / (jnp.sum(p, axis=-1, keepdims=True) + p_sink)
    return p * inv, p_sink * inv


def _kv_specs(nb):
    k_col = ATTN_WIDTH // KV_WIDTH
    specs = []
    for col in (k_col, k_col + 1):
        specs += [
            pl.BlockSpec((BLOCK, KV_WIDTH), lambda s, n, col=col: (s * nb, col)),
            pl.BlockSpec((BLOCK, KV_WIDTH), lambda s, n, col=col: (s * nb + jnp.maximum(n - 1, 0), col)),
            pl.BlockSpec((BLOCK, KV_WIDTH), lambda s, n, col=col: (s * nb + n, col)),
        ]
    return specs


def _attn_fwd(qkv, sinks, t_len, pad, name):
    rows = qkv.shape[0]
    nb = t_len // BLOCK
    n_seq = rows // t_len

    def body(q_ref, km_ref, kp_ref, kc_ref, vm_ref, vp_ref, vc_ref, sink_ref, o_ref):
        n = pl.program_id(1)
        mask = _attn_mask(n, pad)
        keys = jnp.concatenate([km_ref[...], kp_ref[...], kc_ref[...]], axis=0)
        vals = jnp.concatenate([vm_ref[...], vp_ref[...], vc_ref[...]], axis=0)
        for g in range(N_Q_HEADS // GROUP):
            qs, sink = _stack_group(q_ref, sink_ref, g)
            scores = lax.dot_general(qs.astype(BF16), keys, NT, preferred_element_type=F32)
            p, _ = _softmax_with_sink(scores, mask, sink)
            o = jnp.where(_in_half(g), jnp.dot(p.astype(BF16), vals, preferred_element_type=F32), 0.0)
            for t in range(GROUP // 2):
                o_ref[:, _tile(g, t)] = _unstack_pair(o, g, t)

    return pl.pallas_call(
        body,
        name=name,
        grid=(n_seq, nb),
        in_specs=[pl.BlockSpec((BLOCK, ATTN_WIDTH), lambda s, n: (s * nb + n, 0))] + _kv_specs(nb) + [_full((N_Q_HEADS, BLOCK))],
        out_specs=pl.BlockSpec((BLOCK, ATTN_WIDTH), lambda s, n: (s * nb + n, 0)),
        out_shape=jax.ShapeDtypeStruct((rows, ATTN_WIDTH), F32),
    )(qkv, qkv, qkv, qkv, qkv, qkv, qkv, sinks)


def _attn_bwd(qkv, sinks, o, do, t_len, pad, name):
    rows = qkv.shape[0]
    nb = t_len // BLOCK
    n_seq = rows // t_len

    def body(q_ref, km_ref, kp_ref, kc_ref, vm_ref, vp_ref, vc_ref, sink_ref, o_ref, do_ref, dq_ref, dk_ref, dv_ref, dsink_ref):
        s_id, n = pl.program_id(0), pl.program_id(1)

        @pl.when(n == 0)
        def _():
            dk_ref[...] = jnp.zeros_like(dk_ref)
            dv_ref[...] = jnp.zeros_like(dv_ref)

        @pl.when((n == 0) & (s_id == 0))
        def _():
            dsink_ref[...] = jnp.zeros_like(dsink_ref)

        mask = _attn_mask(n, pad)
        keys = jnp.concatenate([km_ref[...], kp_ref[...], kc_ref[...]], axis=0)
        vals = jnp.concatenate([vm_ref[...], vp_ref[...], vc_ref[...]], axis=0)
        dkeys = jnp.zeros((3 * BLOCK, KV_WIDTH), F32)
        dvals = jnp.zeros((3 * BLOCK, KV_WIDTH), F32)
        for g in range(N_Q_HEADS // GROUP):
            qs, sink = _stack_group(q_ref, sink_ref, g)
            qsb = qs.astype(BF16)
            scores = lax.dot_general(qsb, keys, NT, preferred_element_type=F32)
            p, p_sink = _softmax_with_sink(scores, mask, sink)
            dos = _stack_rows(do_ref, g)
            delta = jnp.sum(dos * _stack_rows(o_ref, g), axis=-1, keepdims=True)
            dosb = dos.astype(BF16)
            dp = lax.dot_general(dosb, vals, NT, preferred_element_type=F32)
            ds = (p * (dp - delta) * (HEAD_DIM**-0.5)).astype(BF16)
            dqs = jnp.where(_in_half(g), jnp.dot(ds, keys, preferred_element_type=F32), 0.0)
            for t in range(GROUP // 2):
                dq_ref[:, _tile(g, t)] = _unstack_pair(dqs, g, t)
            dkeys = dkeys + lax.dot_general(ds, qsb, TN, preferred_element_type=F32)
            dvals = dvals + lax.dot_general(p.astype(BF16), dosb, TN, preferred_element_type=F32)
            sink_term = p_sink * delta
            for hh in range(GROUP):
                head = GROUP * g + hh
                part = -jnp.sum(sink_term[hh * BLOCK : (hh + 1) * BLOCK, :], axis=0, keepdims=True)
                dsink_ref[head : head + 1, :] += jnp.broadcast_to(part, (1, BLOCK))
        r_prev = pl.multiple_of(jnp.maximum(n - 1, 0) * BLOCK, BLOCK)
        r_cur = pl.multiple_of(n * BLOCK, BLOCK)
        for acc, d in ((dk_ref, dkeys), (dv_ref, dvals)):
            acc[0:BLOCK, :] += d[0:BLOCK]
            acc[pl.ds(r_prev, BLOCK), :] += d[BLOCK : 2 * BLOCK]
            acc[pl.ds(r_cur, BLOCK), :] += d[2 * BLOCK :]

    q_blk = pl.BlockSpec((BLOCK, ATTN_WIDTH), lambda s, n: (s * nb + n, 0))
    seq_kv = pl.BlockSpec((t_len, KV_WIDTH), lambda s, n: (s, 0))
    return pl.pallas_call(
        body,
        name=name,
        grid=(n_seq, nb),
        in_specs=[q_blk] + _kv_specs(nb) + [_full((N_Q_HEADS, BLOCK)), q_blk, q_blk],
        out_specs=[q_blk, seq_kv, seq_kv, _full((N_Q_HEADS, BLOCK))],
        out_shape=[
            jax.ShapeDtypeStruct((rows, ATTN_WIDTH), F32),
            jax.ShapeDtypeStruct((rows, KV_WIDTH), F32),
            jax.ShapeDtypeStruct((rows, KV_WIDTH), F32),
            jax.ShapeDtypeStruct((N_Q_HEADS, BLOCK), F32),
        ],
    )(qkv, qkv, qkv, qkv, qkv, qkv, qkv, sinks, o, do)


def _mix_out_fwd(y_rec, y_attn, h, wbuf, loc, g_rec, g_attn, ln_g, ln_b, alpha, name):
    rows, d = h.shape
    tm = _row_tile(rows)

    def body(yr_ref, ya_ref, h_ref, w_hbm, gr_ref, ga_ref, lg_ref, lb_ref, ho_ref, z_ref, w_ref, sems):
        _load_blocks(w_hbm, [(*loc, w_ref)], sems)
        nr = _rms(yr_ref[...], gr_ref[...]).astype(BF16)
        na = _rms(ya_ref[...], ga_ref[...]).astype(BF16)
        m = jnp.dot(nr, w_ref[0:REC_WIDTH, :], preferred_element_type=F32) + jnp.dot(na, w_ref[REC_WIDTH:, :], preferred_element_type=F32)
        z = alpha * h_ref[...] + m
        z_ref[...] = z
        xhat, _ = _ln_stats(z)
        ho_ref[...] = xhat * lg_ref[...] + lb_ref[...]

    row_d = pl.BlockSpec((tm, d), lambda i: (i, 0))
    row_h = pl.BlockSpec((tm, REC_WIDTH), lambda i: (i, 0))
    return pl.pallas_call(
        body,
        name=name,
        grid=(rows // tm,),
        in_specs=[row_h, row_h, row_d, ANY, _full((1, REC_WIDTH)), _full((1, ATTN_WIDTH)), _full((1, d)), _full((1, d))],
        out_specs=[row_d, row_d],
        out_shape=[jax.ShapeDtypeStruct((rows, d), F32)] * 2,
        scratch_shapes=_weight_scratch([loc], d),
    )(y_rec, y_attn, h, wbuf, g_rec, g_attn, ln_g, ln_b)


def _mix_out_bwd(dh, z, y_rec, y_attn, wbuf, loc, g_rec, g_attn, ln_g, name):
    rows, d = dh.shape
    tm = _row_tile(rows)
    mix = REC_WIDTH + ATTN_WIDTH

    def body(dh_ref, z_ref, yr_ref, ya_ref, w_hbm, gr_ref, ga_ref, lg_ref, dz_ref, dzb_ref, dyr_ref, dya_ref, yn_ref, dgam_ref, dbet_ref, dgr_ref, dga_ref, w_ref, sems):
        _load_blocks(w_hbm, [(*loc, w_ref)], sems)

        @pl.when(pl.program_id(0) == 0)
        def _():
            for ref in (dgam_ref, dbet_ref, dgr_ref, dga_ref):
                ref[...] = jnp.zeros_like(ref)

        dz, dgam, dbet = _ln_bwd(dh_ref[...], z_ref[...], lg_ref[...])
        dgam_ref[...] += dgam
        dbet_ref[...] += dbet
        dz_ref[...] = dz
        dzb = dz.astype(BF16)
        dzb_ref[...] = dzb
        dyn = lax.dot_general(dzb, w_ref[...], NT, preferred_element_type=F32)
        dyr, dgr, nr = _rms_bwd(dyn[:, 0:REC_WIDTH], yr_ref[...], gr_ref[...])
        dya, dga, na = _rms_bwd(dyn[:, REC_WIDTH:], ya_ref[...], ga_ref[...])
        dyr_ref[...] = dyr
        dya_ref[...] = dya
        dgr_ref[...] += dgr
        dga_ref[...] += dga
        yn_ref[:, 0:REC_WIDTH] = nr.astype(BF16)
        yn_ref[:, REC_WIDTH:] = na.astype(BF16)

    row_d = pl.BlockSpec((tm, d), lambda i: (i, 0))
    row_h = pl.BlockSpec((tm, REC_WIDTH), lambda i: (i, 0))
    row_m = pl.BlockSpec((tm, mix), lambda i: (i, 0))
    return pl.pallas_call(
        body,
        name=name,
        grid=(rows // tm,),
        in_specs=[row_d, row_d, row_h, row_h, ANY, _full((1, REC_WIDTH)), _full((1, ATTN_WIDTH)), _full((1, d))],
        out_specs=[row_d, row_d, row_h, row_h, row_m, _full((1, d)), _full((1, d)), _full((1, REC_WIDTH)), _full((1, ATTN_WIDTH))],
        out_shape=[
            jax.ShapeDtypeStruct((rows, d), F32),
            jax.ShapeDtypeStruct((rows, d), BF16),
            jax.ShapeDtypeStruct((rows, REC_WIDTH), F32),
            jax.ShapeDtypeStruct((rows, ATTN_WIDTH), F32),
            jax.ShapeDtypeStruct((rows, mix), BF16),
            jax.ShapeDtypeStruct((1, d), F32),
            jax.ShapeDtypeStruct((1, d), F32),
            jax.ShapeDtypeStruct((1, REC_WIDTH), F32),
            jax.ShapeDtypeStruct((1, ATTN_WIDTH), F32),
        ],
        scratch_shapes=_weight_scratch([loc], d),
    )(dh, z, y_rec, y_attn, wbuf, g_rec, g_attn, ln_g)


def _loss_head(y, target, t_len, first_token, name):
    rows, d = y.shape
    tm = _row_tile(rows)

    def body(y_ref, t_ref, loss_ref, dy_ref):
        i = pl.program_id(0)

        @pl.when(i == 0)
        def _():
            loss_ref[...] = jnp.zeros_like(loss_ref)

        row = i * tm + lax.broadcasted_iota(jnp.int32, (tm, 1), 0)
        is_token = lax.rem(row, t_len) >= first_token
        err = jnp.where(is_token, y_ref[...] - t_ref[...], 0.0)
        dy_ref[...] = err / d
        per_row = jnp.sum(err * err, axis=-1, keepdims=True) / d
        loss_ref[...] += jnp.broadcast_to(0.5 * jnp.sum(per_row, axis=0, keepdims=True), (1, BLOCK))

    row_d = pl.BlockSpec((tm, d), lambda i: (i, 0))
    return pl.pallas_call(
        body,
        name=name,
        grid=(rows // tm,),
        in_specs=[row_d, row_d],
        out_specs=[_full((1, BLOCK)), row_d],
        out_shape=[jax.ShapeDtypeStruct((1, BLOCK), F32), jax.ShapeDtypeStruct((rows, d), F32)],
    )(y, target)


def _meta_grad(dh0, t_len, pad, name):
    rows, d = dh0.shape
    nb = t_len // BLOCK
    n_seq = rows // t_len

    def body(dh_ref, o_ref):
        @pl.when(pl.program_id(0) == 0)
        def _():
            o_ref[...] = jnp.zeros_like(o_ref)

        o_ref[...] += dh_ref[pad : pad + N_META, :]

    return pl.pallas_call(
        body,
        name=name,
        grid=(n_seq,),
        in_specs=[pl.BlockSpec((BLOCK, d), lambda s: (s * nb, 0))],
        out_specs=_full((N_META, d)),
        out_shape=jax.ShapeDtypeStruct((N_META, d), F32),
    )(dh0)


def _adamw(w, g, m, v, name):
    shape = w.shape
    cols = shape[-1]
    rows = w.size // cols
    tr = rows
    for cand in (512, 256, 128, 64):
        if rows % cand == 0 and rows > cand:
            tr = cand
            break

    def body(w_ref, g_ref, m_ref, v_ref, d_ref, nm_ref, nv_ref):
        gv = g_ref[...]
        nm = ADAM_B1 * m_ref[...] + (1.0 - ADAM_B1) * gv
        nv = ADAM_B2 * v_ref[...] + (1.0 - ADAM_B2) * (gv * gv)
        m_hat = nm / (1.0 - ADAM_B1**ADAM_STEP)
        v_hat = nv / (1.0 - ADAM_B2**ADAM_STEP)
        d_ref[...] = -ADAM_LR * (m_hat / (jnp.sqrt(v_hat) + ADAM_EPS) + ADAM_WD * w_ref[...])
        nm_ref[...] = nm
        nv_ref[...] = nv

    blk = pl.BlockSpec((tr, cols), lambda i: (i, 0))
    flat = [a.reshape(rows, cols) for a in (w, g, m, v)]
    outs = pl.pallas_call(
        body,
        name=name,
        grid=(rows // tr,),
        in_specs=[blk] * 4,
        out_specs=[blk] * 3,
        out_shape=[jax.ShapeDtypeStruct((rows, cols), F32)] * 3,
    )(*flat)
    return [o.reshape(shape) for o in outs]


WEIGHTS = ["meta_tokens", "ffn1_w_gate", "ffn1_w_up", "ffn1_w_down", "ln1_g", "ln1_b", "w_in", "conv_w", "conv_b", "gate_a_w", "gate_a_b",
           "gate_x_w", "gate_x_b", "lru_lambda", "attn_sinks", "norm_rec_g", "norm_attn_g", "w_out", "ln2_g", "ln2_b", "ffn2_w_gate",
           "ffn2_w_up", "ffn2_w_down", "ln3_g", "ln3_b"]
BIG = [("ffn1_w_gate", True), ("ffn1_w_up", True), ("ffn1_w_down", False), ("w_in", True), ("w_out", False),
       ("ffn2_w_gate", True), ("ffn2_w_up", True), ("ffn2_w_down", False)]


def _rope_tables(t_len, pad):
    pos = (jnp.arange(t_len) - pad).astype(F32)
    inv_freq = ROPE_THETA ** (-jnp.arange(0, ROPE_DIM, 2, dtype=F32) / ROPE_DIM)
    ang = pos[:, None] * inv_freq[None, :]
    cos, sin = jnp.cos(ang), jnp.sin(ang)
    half = ROPE_DIM // 2
    rest = jnp.zeros((t_len, HEAD_DIM - ROPE_DIM), F32)
    zero = jnp.zeros((t_len, half), F32)
    cos_h = jnp.concatenate([cos, cos, rest + 1.0], axis=1)
    up_h = jnp.concatenate([-sin, zero, rest], axis=1)
    down_h = jnp.concatenate([zero, sin, rest], axis=1)
    return [jnp.tile(t, (1, ATTN_WIDTH // HEAD_DIM)) for t in (cos_h, up_h, down_h)]


def _gate_tiles(w):
    z = jnp.zeros((HEAD_DIM, HEAD_DIM), w.dtype)
    tiles = [jnp.block([[w[2 * j], z], [z, w[2 * j + 1]]]) for j in range(w.shape[0] // 2)]
    return jnp.stack(tiles).astype(BF16)


def _gate_blocks(tiles):
    out = []
    for j in range(tiles.shape[0]):
        out += [tiles[j, :HEAD_DIM, :HEAD_DIM], tiles[j, HEAD_DIM:, HEAD_DIM:]]
    return jnp.stack(out)


def kernel(x, meta_tokens, ffn1_w_gate, ffn1_w_up, ffn1_w_down, ln1_g, ln1_b, w_in, conv_w, conv_b, gate_a_w, gate_a_b, gate_x_w, gate_x_b, lru_lambda, attn_sinks, norm_rec_g, norm_attn_g, w_out, ln2_g, ln2_b, ffn2_w_gate, ffn2_w_up, ffn2_w_down, ln3_g, ln3_b, loss_target, m_meta_tokens, m_ffn1_w_gate, m_ffn1_w_up, m_ffn1_w_down, m_ln1_g, m_ln1_b, m_w_in, m_conv_w, m_conv_b, m_gate_a_w, m_gate_a_b, m_gate_x_w, m_gate_x_b, m_lru_lambda, m_attn_sinks, m_norm_rec_g, m_norm_attn_g, m_w_out, m_ln2_g, m_ln2_b, m_ffn2_w_gate, m_ffn2_w_up, m_ffn2_w_down, m_ln3_g, m_ln3_b, v_meta_tokens, v_ffn1_w_gate, v_ffn1_w_up, v_ffn1_w_down, v_ln1_g, v_ln1_b, v_w_in, v_conv_w, v_conv_b, v_gate_a_w, v_gate_a_b, v_gate_x_w, v_gate_x_b, v_lru_lambda, v_attn_sinks, v_norm_rec_g, v_norm_attn_g, v_w_out, v_ln2_g, v_ln2_b, v_ffn2_w_gate, v_ffn2_w_up, v_ffn2_w_down, v_ln3_g, v_ln3_b):
    given = dict(locals())
    w = {k: given[k] for k in WEIGHTS}
    n_seq, seq, d = x.shape
    depth = ln1_g.shape[0]
    alpha = (2.0 * depth) ** 0.25
    pad = (-(N_META + seq)) % BLOCK
    t_len = pad + N_META + seq
    rows = n_seq * t_len
    me = 4 * lax.axis_index("x") + 2 * lax.axis_index("y") + lax.axis_index("c")

    sent = {name: (_transpose_to_bf16(w[name], "t_" + name) if transposed else w[name].astype(BF16)) for name, transposed in BIG}
    loc, off = {}, 0
    for name, _ in BIG:
        loc[name] = (off, sent[name].shape[1])
        off += sent[name].shape[1]
    ffn1_loc = [loc["ffn1_w_gate"], loc["ffn1_w_up"], loc["ffn1_w_down"]]
    ffn2_loc = [loc["ffn2_w_gate"], loc["ffn2_w_up"], loc["ffn2_w_down"]]
    shards = [jnp.concatenate([sent[name][l] for name, _ in BIG], axis=0) for l in range(depth)]
    zones = _place_own(shards, False, "gather_place")
    gathers, chain = [], zones[0]
    for l in range(depth):
        gathers.append(_push_start([shards[l]], [zones[l]], chain, False, f"gather_start_{l}"))
        chain = gathers[-1][2][0]
    small = jnp.concatenate([meta_tokens, conv_w.reshape(-1, BLOCK)], axis=0)
    (small_all,) = _all_gather([small], "gather_small")
    meta_full = small_all[:, :N_META, :].transpose(1, 0, 2).reshape(N_META, d)
    conv_shard = conv_w.shape[-1]
    conv_full = small_all[:, N_META:, :].reshape(N_DEV, depth, CONV_WIDTH, conv_shard).transpose(1, 2, 0, 3).reshape(depth, CONV_WIDTH, REC_WIDTH)

    cos, sin_up, sin_down = _rope_tables(t_len, pad)
    row1 = lambda a: a.reshape(1, -1)

    h = jnp.concatenate([jnp.zeros((n_seq, pad, d), F32), jnp.broadcast_to(meta_full[None], (n_seq, N_META, d)), x], axis=1).reshape(rows, d)
    target = jnp.pad(loss_target, ((0, 0), (pad + N_META, 0), (0, 0))).reshape(rows, d)
    saved, wbufs = [], []
    for l in range(depth):
        (wbuf,) = _push_wait(*gathers[l], chain if l == 0 else h, False, f"gather_wait_{l}")
        wbufs.append(wbuf)
        s = {"h0": h}
        h, s["g1"], s["u1"], s["z1"] = _ffn_fwd(h, wbuf, ffn1_loc, row1(ln1_g[l]), row1(ln1_b[l]), alpha, "ffn1_fwd")
        s["h1"] = h
        s["wa"], s["wx"] = _gate_tiles(gate_a_w[l]), _gate_tiles(gate_x_w[l])
        s["sinks"] = jnp.broadcast_to(attn_sinks[l][:, None], (N_Q_HEADS, BLOCK))
        proj = _proj_fwd(h, wbuf, loc["w_in"], "proj_fwd")
        s["proj"] = proj
        s["xc"], s["hrec"], s["y_rec"] = _rglru_fwd(proj, conv_full[l], row1(conv_b[l]), s["wa"], row1(gate_a_b[l]), s["wx"], row1(gate_x_b[l]),
                                                     row1(lru_lambda[l]), t_len, pad, "rglru_fwd")
        s["qkv"] = _rope_fwd(proj, cos, sin_up, sin_down, t_len, "rope_fwd")
        s["y_attn"] = _attn_fwd(s["qkv"], s["sinks"], t_len, pad, "attn_fwd")
        h, s["z2"] = _mix_out_fwd(s["y_rec"], s["y_attn"], h, wbuf, loc["w_out"], row1(norm_rec_g[l]), row1(norm_attn_g[l]), row1(ln2_g[l]), row1(ln2_b[l]), alpha, "mix_out_fwd")
        s["h2"] = h
        h, s["g2"], s["u2"], s["z3"] = _ffn_fwd(h, wbuf, ffn2_loc, row1(ln3_g[l]), row1(ln3_b[l]), alpha, "ffn2_fwd")
        saved.append(s)
    loss_part, dh = _loss_head(h, target, t_len, pad + N_META, "loss_head")

    exchanges = [None] * depth
    small_grads = [None] * depth
    behind = dh
    for l in reversed(range(depth)):
        wbuf, s = wbufs[l], saved[l]
        bg, sg = {}, {}
        dh, dg, du, act, dyb, sg["ln3_g"], sg["ln3_b"] = _ffn_bwd(dh, s["z3"], s["g2"], s["u2"], wbuf, ffn2_loc, row1(ln3_g[l]), behind, alpha, "ffn2_bwd")
        bg["ffn2_w_gate"] = _mm_tn(dg, s["h2"], BF16, "ffn_wgrad_in")
        bg["ffn2_w_up"] = _mm_tn(du, s["h2"], BF16, "ffn_wgrad_in")
        bg["ffn2_w_down"] = _mm_tn(act, dyb, BF16, "ffn_wgrad_down")
        dz, dzb, dy_rec, dy_attn, yn, sg["ln2_g"], sg["ln2_b"], sg["norm_rec_g"], sg["norm_attn_g"] = _mix_out_bwd(
            dh, s["z2"], s["y_rec"], s["y_attn"], wbuf, loc["w_out"], row1(norm_rec_g[l]), row1(norm_attn_g[l]), row1(ln2_g[l]), "mix_out_bwd")
        bg["w_out"] = _mm_tn(yn, dzb, BF16, "w_out_wgrad")
        dxr, dgt, sg["conv_w"], sg["conv_b"], dwa, sg["gate_a_b"], dwx, sg["gate_x_b"], sg["lru_lambda"] = _rglru_bwd(
            dy_rec, s["hrec"], s["proj"], s["xc"], conv_full[l], s["wa"], row1(gate_a_b[l]), s["wx"], row1(gate_x_b[l]), row1(lru_lambda[l]), t_len, pad, "rglru_bwd")
        sg["gate_a_w"], sg["gate_x_w"] = _gate_blocks(dwa), _gate_blocks(dwx)
        dq, dk, dv, dsink = _attn_bwd(s["qkv"], s["sinks"], s["y_attn"], dy_attn, t_len, pad, "attn_bwd")
        sg["attn_sinks"] = dsink[:, 0]
        dqkv = _rope_bwd(dq, dk, dv, cos, sin_up, sin_down, t_len, "rope_bwd")
        dproj = jnp.concatenate([dxr, dgt, dqkv], axis=1)
        bg["w_in"] = _mm_tn(dproj, s["h1"], BF16, "w_in_wgrad")
        dh = _proj_bwd(dproj, wbuf, loc["w_in"], dz, alpha, "proj_bwd")
        dh, dg, du, act, dyb, sg["ln1_g"], sg["ln1_b"] = _ffn_bwd(dh, s["z1"], s["g1"], s["u1"], wbuf, ffn1_loc, row1(ln1_g[l]), dz, alpha, "ffn1_bwd")
        bg["ffn1_w_gate"] = _mm_tn(dg, s["h0"], BF16, "ffn_wgrad_in")
        bg["ffn1_w_up"] = _mm_tn(du, s["h0"], BF16, "ffn_wgrad_in")
        bg["ffn1_w_down"] = _mm_tn(act, dyb, BF16, "ffn_wgrad_down")
        gs = [bg[name].reshape(N_DEV, -1, d) for name, _ in BIG]
        zones = _place_own(gs, True, "exchange_place")
        exchanges[l] = _push_start(gs, zones, zones[0], True, f"exchange_start_{l}")
        behind = exchanges[l][2][0]
        small_grads[l] = sg
    grad_x = dh.reshape(n_seq, t_len, d)[:, pad + N_META :, :]
    dmeta = _meta_grad(dh, t_len, pad, "meta_grad")

    grads = {}
    small_names = ["ln1_g", "ln1_b", "ln2_g", "ln2_b", "ln3_g", "ln3_b", "conv_b", "gate_a_b", "gate_x_b", "lru_lambda", "norm_rec_g",
                   "norm_attn_g", "conv_w", "gate_a_w", "gate_x_w", "attn_sinks"]
    pieces = [small_grads[l][name].reshape(-1) for l in range(depth) for name in small_names] + [dmeta.reshape(-1), loss_part[0, :1]]
    sizes = [p.shape[0] for p in pieces]
    flat = jnp.concatenate(pieces)
    width = 1024
    n_rows = -(-flat.shape[0] // (8 * width)) * 8
    flat = jnp.pad(flat, (0, n_rows * width - flat.shape[0])).reshape(n_rows, width)
    (flat_all,) = _all_gather([flat], "gather_small_grads")
    total = _sum8(flat_all, "sum_small_grads").reshape(-1)
    offs = [0]
    for sz in sizes:
        offs.append(offs[-1] + sz)
    taken = [total[offs[k] : offs[k + 1]] for k in range(len(sizes))]
    for j, name in enumerate(small_names):
        full_shape = (depth,) + ((CONV_WIDTH, REC_WIDTH) if name == "conv_w" else w[name].shape[1:])
        grads[name] = jnp.stack([taken[l * len(small_names) + j] for l in range(depth)]).reshape(full_shape)
    grads["conv_w"] = lax.dynamic_slice_in_dim(grads["conv_w"], me * conv_shard, conv_shard, axis=2)
    meta_shard = meta_tokens.shape[1]
    grads["meta_tokens"] = lax.dynamic_slice_in_dim(taken[-2].reshape(N_META, d), me * meta_shard, meta_shard, axis=1)
    loss = taken[-1][0]
    deltas, new_m, new_v = {}, {}, {}
    for name in WEIGHTS:
        if name not in dict(BIG):
            deltas[name], new_m[name], new_v[name] = _adamw(w[name], grads[name], given["m_" + name], given["v_" + name], "adamw")

    reduced = {}
    after = new_v[WEIGHTS[-1]]
    for l in reversed(range(depth)):
        parts = _push_wait(*exchanges[l], after, True, f"exchange_wait_{l}")
        for (name, transposed), p in zip(BIG, parts):
            reduced[(l, name)] = _sum8(p, "sum_grads")
            if transposed:
                reduced[(l, name)] = _transpose_f32(reduced[(l, name)], "t_grad")
        after = reduced[(l, BIG[-1][0])]
    for name, _ in BIG:
        grads[name] = jnp.stack([reduced[(l, name)] for l in range(depth)])
        deltas[name], new_m[name], new_v[name] = _adamw(w[name], grads[name], given["m_" + name], given["v_" + name], "adamw")
    return (loss, grad_x, *[grads[k] for k in WEIGHTS], *[deltas[k] for k in WEIGHTS], *[new_m[k] for k in WEIGHTS], *[new_v[k] for k in WEIGHTS])
```

```python
import functools

import jax
import jax.numpy as jnp
from jax import lax
from jax.experimental import pallas as pl
from jax.experimental.pallas import tpu as pltpu

F32 = jnp.float32
BF16 = jnp.bfloat16
MESH = pl.DeviceIdType.MESH

N_DEV = 8
N_META = 16
BLOCK = 128
WINDOW = 128
REC_WIDTH = 512
ATTN_WIDTH = 512
KV_WIDTH = 128
HEAD_DIM = 64
N_Q_HEADS = 8
ROPE_DIM = 16
ROPE_THETA = 500000.0
CONV_WIDTH = 4
LRU_C = 8.0
LN_EPS = 1e-5
RMS_EPS = 1e-6
NEG_INF = -1e30
ADAM_LR, ADAM_B1, ADAM_B2, ADAM_EPS, ADAM_WD, ADAM_STEP = 0.001, 0.9, 0.999, 1e-08, 0.01, 10

NT = (((1,), (1,)), ((), ()))
TN = (((0,), (0,)), ((), ()))
VMEM_BIG = 56 * 1024 * 1024


def _params(vmem=None):
    return pltpu.CompilerParams(vmem_limit_bytes=vmem)


def _row_tile(rows):
    return 256 if rows % 256 == 0 else 128


def _sigmoid(x):
    return 1.0 / (1.0 + jnp.exp(-x))


def _expm1_nonpos(x):
    series = x * (1.0 + 0.5 * x * (1.0 + x / 3.0 * (1.0 + 0.25 * x * (1.0 + 0.2 * x))))
    return jnp.where(x > -0.05, series, jnp.exp(x) - 1.0)


def _softplus(x):
    e = jnp.exp(-jnp.abs(x))
    log1p = jnp.where(e < 1e-3, e * (1.0 - e * (0.5 - e / 3.0)), jnp.log(1.0 + e))
    return jnp.maximum(x, 0.0) + log1p


def _gelu_and_grad(x):
    c0 = 0.7978845608028654
    x2 = x * x
    t = jnp.tanh(c0 * (x + 0.044715 * x * x2))
    gelu = 0.5 * x * (1.0 + t)
    grad = 0.5 * (1.0 + t) + 0.5 * x * (1.0 - t * t) * c0 * (1.0 + 3.0 * 0.044715 * x2)
    return gelu, grad


def _ln_stats(z):
    mu = jnp.mean(z, axis=-1, keepdims=True)
    zc = z - mu
    var = jnp.mean(zc * zc, axis=-1, keepdims=True)
    rstd = lax.rsqrt(var + LN_EPS)
    return zc * rstd, rstd


def _ln_bwd(dy, z, gamma):
    xhat, rstd = _ln_stats(z)
    dxh = dy * gamma
    m1 = jnp.mean(dxh, axis=-1, keepdims=True)
    m2 = jnp.mean(dxh * xhat, axis=-1, keepdims=True)
    dz = rstd * (dxh - m1 - xhat * m2)
    return dz, jnp.sum(dy * xhat, axis=0, keepdims=True), jnp.sum(dy, axis=0, keepdims=True)


def _rms(y, gamma):
    r = lax.rsqrt(jnp.mean(y * y, axis=-1, keepdims=True) + RMS_EPS)
    return y * r * gamma


def _rms_bwd(dout, y, gamma):
    r = lax.rsqrt(jnp.mean(y * y, axis=-1, keepdims=True) + RMS_EPS)
    n = y * r
    dn = dout * gamma
    dy = r * (dn - n * jnp.mean(dn * n, axis=-1, keepdims=True))
    return dy, jnp.sum(dout * n, axis=0, keepdims=True), n * gamma


def _load_blocks(w_hbm, items, sems):
    @pl.when(pl.program_id(0) == 0)
    def _():
        copies = []
        for k, (off, nr, dst) in enumerate(items):
            for dev in range(N_DEV):
                copies.append(pltpu.make_async_copy(w_hbm.at[dev, pl.ds(off, nr), :], dst.at[pl.ds(dev * nr, nr), :], sems.at[k, dev]))
        for cp in copies:
            cp.start()
        for cp in copies:
            cp.wait()


def _full(shape):
    return pl.BlockSpec(shape, lambda *_: (0,) * len(shape))


ANY = pl.BlockSpec(memory_space=pl.ANY)


def _all_gather(xs, after, name):
    n = len(xs)

    def body(*refs):
        ins, outs = refs[:n], refs[n + 1 : 2 * n + 1]
        send_sems, recv_sems, local_sems = refs[2 * n + 1 :]
        x, y, c = lax.axis_index("x"), lax.axis_index("y"), lax.axis_index("c")
        me, sibling = (x, y, c), (x, y, 1 - c)
        chips = [(1 - x, y), (x, 1 - y), (1 - x, 1 - y)]

        def slot(i, dev):
            return outs[i].at[4 * dev[0] + 2 * dev[1] + dev[2]]

        def copy(i, k, block, to, src=None):
            return pltpu.make_async_remote_copy(
                src_ref=slot(i, block) if src is None else src,
                dst_ref=slot(i, block),
                send_sem=send_sems.at[i, k],
                recv_sem=recv_sems.at[i, k],
                device_id=to,
                device_id_type=MESH,
            )

        mine = [pltpu.make_async_copy(ins[i], slot(i, me), local_sems.at[i]) for i in range(n)]
        for cp in mine:
            cp.start()
        first = []
        for i in range(n):
            first.append(copy(i, 0, me, sibling, src=ins[i]))
            first += [copy(i, 1 + j, me, (*chip, c), src=ins[i]) for j, chip in enumerate(chips)]
        for cp in first:
            cp.start()
        passed = []
        for j, chip in enumerate(chips):
            for i in range(n):
                copy(i, 1 + j, (*chip, c), me).wait_recv()
                fwd = copy(i, 4 + j, (*chip, c), sibling)
                fwd.start()
                passed.append(fwd)
        for i in range(n):
            copy(i, 0, sibling, me).wait_recv()
            for j, chip in enumerate(chips):
                copy(i, 4 + j, (*chip, 1 - c), me).wait_recv()
        for cp in first + passed:
            cp.wait_send()
        for cp in mine:
            cp.wait()

    return pl.pallas_call(
        body,
        name=name,
        out_shape=[jax.ShapeDtypeStruct((N_DEV,) + a.shape, a.dtype) for a in xs],
        in_specs=[ANY] * (n + 1),
        out_specs=[ANY] * n,
        scratch_shapes=[pltpu.SemaphoreType.DMA((n, 7)), pltpu.SemaphoreType.DMA((n, 7)), pltpu.SemaphoreType.DMA((n,))],
    )(*xs, after)


HBM = pl.BlockSpec(memory_space=pltpu.HBM)
SEM = pl.BlockSpec(memory_space=pltpu.SEMAPHORE)
EFFECT = pltpu.SideEffectType.DATAFLOW_SIDE_EFFECTING


def _push_copies(ins, lands, send_sems, recv_sems, scatter):
    x, y, c = lax.axis_index("x"), lax.axis_index("y"), lax.axis_index("c")
    me = 4 * x + 2 * y + c
    copies = []
    for k in range(1, N_DEV):
        px = 1 - x if (k >> 2) & 1 else x
        py = 1 - y if (k >> 1) & 1 else y
        pc = 1 - c if k & 1 else c
        for i in range(len(ins)):
            copies.append(
                pltpu.make_async_remote_copy(
                    src_ref=ins[i].at[4 * px + 2 * py + pc] if scatter else ins[i],
                    dst_ref=lands[i].at[me],
                    send_sem=send_sems.at[i * (N_DEV - 1) + k - 1],
                    recv_sem=recv_sems.at[i * (N_DEV - 1) + k - 1],
                    device_id=(px, py, pc),
                    device_id_type=MESH,
                )
            )
    return copies


def _push_start(xs, lands, chain, scatter, name):
    n = len(xs)

    def body(*refs):
        ins, zones = refs[:n], refs[n : 2 * n]
        for cp in _push_copies(ins, zones, refs[2 * n + 1], refs[2 * n + 2], scatter):
            cp.start()

    outs = pl.pallas_call(
        body,
        name=name,
        out_shape=(
            pltpu.SemaphoreType.DMA((n * (N_DEV - 1),)),
            pltpu.SemaphoreType.DMA((n * (N_DEV - 1),)),
            *[pltpu.HBM(a.shape, a.dtype) for a in list(xs) + list(lands)],
        ),
        in_specs=[HBM] * (2 * n) + [ANY],
        out_specs=(SEM, SEM, *[HBM] * (2 * n)),
        input_output_aliases={i: 2 + i for i in range(2 * n)},
        compiler_params=pltpu.CompilerParams(has_side_effects=EFFECT),
    )(*[pltpu.with_memory_space_constraint(a, pltpu.HBM) for a in list(xs) + list(lands)], chain)
    return outs[0], outs[1], list(outs[2 : 2 + n]), list(outs[2 + n : 2 + 2 * n])


def _push_wait(send_sems, recv_sems, srcs, lands, after, scatter, name):
    n = len(srcs)

    def body(*refs):
        ins, zones = refs[:n], refs[n : 2 * n]
        for cp in _push_copies(ins, zones, refs[2 * n], refs[2 * n + 1], scatter):
            cp.wait_send()
            cp.wait_recv()

    outs = pl.pallas_call(
        body,
        name=name,
        out_shape=[pltpu.HBM(a.shape, a.dtype) for a in srcs + lands],
        in_specs=[HBM] * (2 * n) + [SEM, SEM, ANY],
        out_specs=[HBM] * (2 * n),
        input_output_aliases={i: i for i in range(2 * n)},
        compiler_params=pltpu.CompilerParams(has_side_effects=EFFECT),
    )(*srcs, *lands, send_sems, recv_sems, after)
    return list(outs[:n]), list(outs[n:])


def _sum8(parts, name):
    _, rows, cols = parts.shape
    tr = rows
    for cand in (512, 256, 128, 64, 32, 16):
        if rows % cand == 0 and rows > cand:
            tr = cand
            break

    def body(p_ref, o_ref):
        acc = p_ref[0].astype(F32)
        for d in range(1, N_DEV):
            acc = acc + p_ref[d].astype(F32)
        o_ref[...] = acc

    return pl.pallas_call(
        body,
        name=name,
        grid=(rows // tr,),
        in_specs=[pl.BlockSpec((N_DEV, tr, cols), lambda i: (0, i, 0))],
        out_specs=pl.BlockSpec((tr, cols), lambda i: (i, 0)),
        out_shape=jax.ShapeDtypeStruct((rows, cols), F32),
    )(parts)


def _sum_exchanged(parts, own, name):
    _, rows, cols = parts.shape

    def body(p_ref, own_hbm, o_ref, own_v, sem):
        me = 4 * lax.axis_index("x") + 2 * lax.axis_index("y") + lax.axis_index("c")
        cp = pltpu.make_async_copy(own_hbm.at[me], own_v, sem)
        cp.start()
        cp.wait()
        o_ref[...] = own_v[...].astype(F32)
        for dev in range(N_DEV):

            @pl.when(me != dev)
            def _():
                o_ref[...] += p_ref[dev].astype(F32)

    return pl.pallas_call(
        body,
        name=name,
        grid=(1,),
        in_specs=[_full((N_DEV, rows, cols)), ANY],
        out_specs=_full((rows, cols)),
        out_shape=jax.ShapeDtypeStruct((rows, cols), F32),
        scratch_shapes=[pltpu.VMEM((rows, cols), parts.dtype), pltpu.SemaphoreType.DMA(())],
        compiler_params=_params(VMEM_BIG),
    )(parts, own)


def _mm_tn(a, b, out_dtype, name):
    rows, m = a.shape
    n = b.shape[1]
    tm = m // 2 if (m // 2) % 128 == 0 and m > 512 else m
    tr = next(t for t in (1088, 1024, 512, 256, 128) if rows % t == 0)
    nk = rows // tr

    def body(a_ref, b_ref, o_ref, acc):
        k = pl.program_id(1)
        part = lax.dot_general(a_ref[...].astype(BF16), b_ref[...].astype(BF16), TN, preferred_element_type=F32)

        @pl.when(k == 0)
        def _():
            acc[...] = part

        @pl.when(k > 0)
        def _():
            acc[...] += part

        @pl.when(k == nk - 1)
        def _():
            o_ref[...] = acc[...].astype(o_ref.dtype)

    return pl.pallas_call(
        body,
        name=name,
        grid=(m // tm, nk),
        in_specs=[pl.BlockSpec((tr, tm), lambda i, k: (k, i)), pl.BlockSpec((tr, n), lambda i, k: (k, 0))],
        out_specs=pl.BlockSpec((tm, n), lambda i, k: (i, 0)),
        out_shape=jax.ShapeDtypeStruct((m, n), out_dtype),
        scratch_shapes=[pltpu.VMEM((tm, n), F32)],
        compiler_params=_params(VMEM_BIG),
    )(a, b)


def _transpose_to_bf16(w, name):
    layers, a_dim, b_dim = w.shape

    def body(w_ref, o_ref):
        eye = (lax.broadcasted_iota(jnp.int32, (a_dim, a_dim), 0) == lax.broadcasted_iota(jnp.int32, (a_dim, a_dim), 1)).astype(BF16)
        o_ref[0] = lax.dot_general(w_ref[0].astype(BF16), eye, TN, preferred_element_type=F32).astype(BF16)

    return pl.pallas_call(
        body,
        name=name,
        grid=(layers,),
        in_specs=[pl.BlockSpec((1, a_dim, b_dim), lambda l: (l, 0, 0))],
        out_specs=pl.BlockSpec((1, b_dim, a_dim), lambda l: (l, 0, 0)),
        out_shape=jax.ShapeDtypeStruct((layers, b_dim, a_dim), BF16),
    )(w)


def _transpose_f32(g, name):
    a_dim, b_dim = g.shape

    def body(g_ref, o_ref):
        eye = (lax.broadcasted_iota(jnp.int32, (a_dim, a_dim), 0) == lax.broadcasted_iota(jnp.int32, (a_dim, a_dim), 1)).astype(BF16)
        v = g_ref[...]
        hi = v.astype(BF16)
        r1 = v - hi.astype(F32)
        mid = r1.astype(BF16)
        lo = (r1 - mid.astype(F32)).astype(BF16)
        out = lax.dot_general(hi, eye, TN, preferred_element_type=F32)
        out = out + lax.dot_general(mid, eye, TN, preferred_element_type=F32)
        out = out + lax.dot_general(lo, eye, TN, preferred_element_type=F32)
        o_ref[...] = out

    return pl.pallas_call(
        body,
        name=name,
        in_specs=[_full((a_dim, b_dim))],
        out_specs=_full((b_dim, a_dim)),
        out_shape=jax.ShapeDtypeStruct((b_dim, a_dim), F32),
        grid=(1,),
    )(g)


def _weight_scratch(locs, d):
    return [pltpu.VMEM((N_DEV * nr, d), BF16) for _, nr in locs] + [pltpu.SemaphoreType.DMA((len(locs), N_DEV))]


def _proj_fwd(h, wbuf, loc, name):
    rows, d = h.shape
    n = N_DEV * loc[1]
    tm = _row_tile(rows)

    def body(h_ref, w_hbm, o_ref, w, sems):
        _load_blocks(w_hbm, [(*loc, w)], sems)
        o_ref[...] = lax.dot_general(h_ref[...].astype(BF16), w[...], NT, preferred_element_type=F32)

    return pl.pallas_call(
        body,
        name=name,
        grid=(rows // tm,),
        in_specs=[pl.BlockSpec((tm, d), lambda i: (i, 0)), ANY],
        out_specs=pl.BlockSpec((tm, n), lambda i: (i, 0)),
        out_shape=jax.ShapeDtypeStruct((rows, n), F32),
        scratch_shapes=_weight_scratch([loc], d),
        compiler_params=_params(VMEM_BIG),
    )(h, wbuf)


def _proj_bwd(dproj, wbuf, loc, dz, alpha, name):
    rows, n = dproj.shape
    d = dz.shape[1]
    tm = _row_tile(rows)

    def body(dp_ref, w_hbm, dz_ref, o_ref, w, sems):
        _load_blocks(w_hbm, [(*loc, w)], sems)
        o_ref[...] = alpha * dz_ref[...] + jnp.dot(dp_ref[...], w[...], preferred_element_type=F32)

    return pl.pallas_call(
        body,
        name=name,
        grid=(rows // tm,),
        in_specs=[pl.BlockSpec((tm, n), lambda i: (i, 0)), ANY, pl.BlockSpec((tm, d), lambda i: (i, 0))],
        out_specs=pl.BlockSpec((tm, d), lambda i: (i, 0)),
        out_shape=jax.ShapeDtypeStruct((rows, d), F32),
        scratch_shapes=_weight_scratch([loc], d),
        compiler_params=_params(VMEM_BIG),
    )(dproj, wbuf, dz)


FFN_CHUNKS = 2


def _ffn_fwd(h, wbuf, locs, ln_g, ln_b, alpha, name):
    rows, d = h.shape
    f = N_DEV * locs[0][1]
    tm = _row_tile(rows)
    tf = f // FFN_CHUNKS

    def body(h_ref, w_hbm, lg_ref, lb_ref, ho_ref, g_ref, u_ref, z_ref, wg, wu, wdv, sems):
        _load_blocks(w_hbm, [(*locs[0], wg), (*locs[1], wu), (*locs[2], wdv)], sems)
        x = h_ref[...]
        xb = x.astype(BF16)
        y = None
        for c in range(FFN_CHUNKS):
            sl = slice(c * tf, (c + 1) * tf)
            g = lax.dot_general(xb, wg[sl, :], NT, preferred_element_type=F32)
            u = lax.dot_general(xb, wu[sl, :], NT, preferred_element_type=F32)
            g_ref[:, sl] = g
            u_ref[:, sl] = u
            a = (g * _sigmoid(g) * u).astype(BF16)
            part = jnp.dot(a, wdv[sl, :], preferred_element_type=F32)
            y = part if y is None else y + part
        z = alpha * x + 0.5 * y
        z_ref[...] = z
        xhat, _ = _ln_stats(z)
        ho_ref[...] = xhat * lg_ref[...] + lb_ref[...]

    row_d = pl.BlockSpec((tm, d), lambda i: (i, 0))
    row_f = pl.BlockSpec((tm, f), lambda i: (i, 0))
    return pl.pallas_call(
        body,
        name=name,
        grid=(rows // tm,),
        in_specs=[row_d, ANY, _full((1, d)), _full((1, d))],
        out_specs=[row_d, row_f, row_f, row_d],
        out_shape=[
            jax.ShapeDtypeStruct((rows, d), F32),
            jax.ShapeDtypeStruct((rows, f), F32),
            jax.ShapeDtypeStruct((rows, f), F32),
            jax.ShapeDtypeStruct((rows, d), F32),
        ],
        scratch_shapes=_weight_scratch(locs, d),
        compiler_params=_params(VMEM_BIG),
    )(h, wbuf, ln_g, ln_b)


def _ffn_bwd(dh, z, g, u, wbuf, locs, ln_g, after, alpha, name):
    rows, d = dh.shape
    f = N_DEV * locs[0][1]
    tm = _row_tile(rows)
    tf = f // FFN_CHUNKS

    def body(dh_ref, z_ref, g_ref, u_ref, w_hbm, lg_ref, after_ref, dx_ref, dg_ref, du_ref, a_ref, dyb_ref, dgam_ref, dbet_ref, wg, wu, wdv, sems):
        _load_blocks(w_hbm, [(*locs[0], wg), (*locs[1], wu), (*locs[2], wdv)], sems)

        @pl.when(pl.program_id(0) == 0)
        def _():
            dgam_ref[...] = jnp.zeros_like(dgam_ref)
            dbet_ref[...] = jnp.zeros_like(dbet_ref)

        dz, dgam, dbet = _ln_bwd(dh_ref[...], z_ref[...], lg_ref[...])
        dgam_ref[...] += dgam
        dbet_ref[...] += dbet
        dyb = (0.5 * dz).astype(BF16)
        dyb_ref[...] = dyb
        dx = alpha * dz
        for c in range(FFN_CHUNKS):
            sl = slice(c * tf, (c + 1) * tf)
            da = lax.dot_general(dyb, wdv[sl, :], NT, preferred_element_type=F32)
            gc = g_ref[:, sl]
            uc = u_ref[:, sl]
            sg = _sigmoid(gc)
            silu = gc * sg
            dgc = (da * uc * (sg * (1.0 + gc * (1.0 - sg)))).astype(BF16)
            duc = (da * silu).astype(BF16)
            a_ref[:, sl] = (silu * uc).astype(BF16)
            dg_ref[:, sl] = dgc
            du_ref[:, sl] = duc
            dx = dx + jnp.dot(dgc, wg[sl, :], preferred_element_type=F32) + jnp.dot(duc, wu[sl, :], preferred_element_type=F32)
        dx_ref[...] = dx

    row_d = pl.BlockSpec((tm, d), lambda i: (i, 0))
    row_f = pl.BlockSpec((tm, f), lambda i: (i, 0))
    return pl.pallas_call(
        body,
        name=name,
        grid=(rows // tm,),
        in_specs=[row_d, row_d, row_f, row_f, ANY, _full((1, d)), ANY],
        out_specs=[row_d, row_f, row_f, row_f, row_d, _full((1, d)), _full((1, d))],
        out_shape=[
            jax.ShapeDtypeStruct((rows, d), F32),
            jax.ShapeDtypeStruct((rows, f), BF16),
            jax.ShapeDtypeStruct((rows, f), BF16),
            jax.ShapeDtypeStruct((rows, f), BF16),
            jax.ShapeDtypeStruct((rows, d), BF16),
            jax.ShapeDtypeStruct((1, d), F32),
            jax.ShapeDtypeStruct((1, d), F32),
        ],
        scratch_shapes=_weight_scratch(locs, d),
        compiler_params=_params(VMEM_BIG),
    )(dh, z, g, u, wbuf, ln_g, after)


LANES = 128


def _gates(xc, wa_ref, ba, wx_ref, bx, lam):
    xcb = xc.astype(BF16)
    r = _sigmoid(jnp.dot(xcb, wa_ref[0], preferred_element_type=F32) + ba)
    i = _sigmoid(jnp.dot(xcb, wx_ref[0], preferred_element_type=F32) + bx)
    sp = _softplus(-lam)
    log_a = -LRU_C * sp * r
    a = jnp.exp(log_a)
    s = jnp.sqrt(-_expm1_nonpos(2.0 * log_a))
    return xcb, r, i, sp, a, s


def _conv(xr, cw, cb):
    return cb + cw[3:4] * xr + cw[2:3] * pltpu.roll(xr, 1, 0) + cw[1:2] * pltpu.roll(xr, 2, 0) + cw[0:1] * pltpu.roll(xr, 3, 0)


def _rglru_fwd(proj, cw, cb, wa, ba, wx, bx, lam, t_len, pad, name):
    rows = proj.shape[0]
    n_seq = rows // t_len
    n_tile = REC_WIDTH // LANES
    groups = t_len // 8

    def body(xr_ref, gt_ref, cw_ref, cb_ref, wa_ref, ba_ref, wx_ref, bx_ref, lam_ref, xc_ref, h_ref, y_ref, a_s, b_s):
        valid = lax.broadcasted_iota(jnp.int32, (t_len, 1), 0) >= pad
        xr = jnp.where(valid, xr_ref[...], 0.0)
        xc = _conv(xr, cw_ref[...], cb_ref[...])
        xc_ref[...] = xc
        _, _, i, _, a, s = _gates(xc, wa_ref, ba_ref[...], wx_ref, bx_ref[...], lam_ref[...])
        a_s[...] = a
        b_s[...] = jnp.where(valid, s * (i * xc), 0.0)
        sub = lax.broadcasted_iota(jnp.int32, (8, LANES), 0)

        def step(gi, carry):
            r0 = pl.multiple_of(gi * 8, 8)
            av = a_s[pl.ds(r0, 8), :]
            bv = b_s[pl.ds(r0, 8), :]
            for d in (1, 2, 4):
                m = sub >= d
                a_sh = pltpu.roll(av, d, 0)
                b_sh = pltpu.roll(bv, d, 0)
                bv = jnp.where(m, av * b_sh + bv, bv)
                av = jnp.where(m, av * a_sh, av)
            hv = av * carry + bv
            h_ref[pl.ds(r0, 8), :] = hv
            return jnp.broadcast_to(hv[7:8, :], (8, LANES))

        lax.fori_loop(0, groups, step, jnp.zeros((8, LANES), F32), unroll=4)
        gelu, _ = _gelu_and_grad(gt_ref[...])
        y_ref[...] = h_ref[...] * gelu

    seq_tile = lambda col0: pl.BlockSpec((t_len, LANES), lambda s, j: (s, col0 + j))
    vec = pl.BlockSpec((1, LANES), lambda s, j: (0, j))
    out = jax.ShapeDtypeStruct((rows, REC_WIDTH), F32)
    return pl.pallas_call(
        body,
        name=name,
        grid=(n_seq, n_tile),
        in_specs=[
            seq_tile(0),
            seq_tile(n_tile),
            pl.BlockSpec((CONV_WIDTH, LANES), lambda s, j: (0, j)),
            vec,
            pl.BlockSpec((1, LANES, LANES), lambda s, j: (j, 0, 0)),
            vec,
            pl.BlockSpec((1, LANES, LANES), lambda s, j: (j, 0, 0)),
            vec,
            vec,
        ],
        out_specs=[seq_tile(0)] * 3,
        out_shape=[out, out, out],
        scratch_shapes=[pltpu.VMEM((t_len, LANES), F32), pltpu.VMEM((t_len, LANES), F32)],
        compiler_params=_params(VMEM_BIG),
    )(proj, proj, cw, cb, wa, ba, wx, bx, lam)


def _rglru_bwd(dy, h, proj, xc, cw, wa, ba, wx, bx, lam, t_len, pad, name):
    rows = proj.shape[0]
    n_seq = rows // t_len
    n_tile = REC_WIDTH // LANES
    groups = t_len // 8

    def body(dy_ref, h_ref, xr_ref, gt_ref, xc_ref, cw_ref, wa_ref, ba_ref, wx_ref, bx_ref, lam_ref,
             dxr_ref, dgt_ref, dcw_ref, dcb_ref, dwa_ref, dba_ref, dwx_ref, dbx_ref, dlam_ref, a_s, c_s, db_s, da_s):
        first_seq = pl.program_id(1) == 0

        @pl.when(first_seq)
        def _():
            for ref in (dcw_ref, dcb_ref, dwa_ref, dba_ref, dwx_ref, dbx_ref, dlam_ref):
                ref[...] = jnp.zeros_like(ref)

        valid = lax.broadcasted_iota(jnp.int32, (t_len, 1), 0) >= pad
        gelu, dgelu = _gelu_and_grad(gt_ref[...])
        dyv = dy_ref[...]
        dho = dyv * gelu
        dgt_ref[...] = (dyv * h_ref[...] * dgelu).astype(BF16)
        xc = xc_ref[...]
        lam = lam_ref[...]
        xcb, r, i, sp, a, s = _gates(xc, wa_ref, ba_ref[...], wx_ref, bx_ref[...], lam)
        a_s[...] = a
        c_s[...] = a * dho
        db_s[...] = dho
        sub = lax.broadcasted_iota(jnp.int32, (8, LANES), 0)

        def step(k, carry):
            gi = groups - 1 - k
            r0 = pl.multiple_of(gi * 8, 8)
            av = a_s[pl.ds(r0, 8), :]
            cv = c_s[pl.ds(r0, 8), :]
            for d in (1, 2, 4):
                m = sub < 8 - d
                a_sh = pltpu.roll(av, 8 - d, 0)
                c_sh = pltpu.roll(cv, 8 - d, 0)
                cv = jnp.where(m, av * c_sh + cv, cv)
                av = jnp.where(m, av * a_sh, av)
            ev = av * carry + cv
            e_next = jnp.where(sub < 7, pltpu.roll(ev, 7, 0), carry)
            dht = db_s[pl.ds(r0, 8), :] + e_next
            hv = h_ref[pl.ds(r0, 8), :]
            rp = pl.multiple_of(jnp.maximum(gi - 1, 0) * 8, 8)
            h_before = jnp.where(gi > 0, jnp.broadcast_to(h_ref[pl.ds(rp, 8), :][7:8, :], (8, LANES)), 0.0)
            h_prev = jnp.where(sub >= 1, pltpu.roll(hv, 1, 0), h_before)
            db_s[pl.ds(r0, 8), :] = dht
            da_s[pl.ds(r0, 8), :] = dht * h_prev
            return jnp.broadcast_to(ev[0:1, :], (8, LANES))

        lax.fori_loop(0, groups, step, jnp.zeros((8, LANES), F32), unroll=4)

        db = jnp.where(valid, db_s[...], 0.0)
        da = da_s[...]
        ds = db * (i * xc)
        di = db * s * xc
        dxc = db * s * i
        dlog_a = da * a - ds * (a * a) / jnp.maximum(s, 1e-30)
        dr = dlog_a * (-LRU_C * sp)
        dsp = jnp.sum(dlog_a * (-LRU_C) * r, axis=0, keepdims=True)
        dlam_ref[...] += dsp * (-_sigmoid(-lam))
        dpr = dr * r * (1.0 - r)
        dpi = di * i * (1.0 - i)
        dprb = dpr.astype(BF16)
        dpib = dpi.astype(BF16)
        dxc = dxc + lax.dot_general(dprb, wa_ref[0], NT, preferred_element_type=F32) + lax.dot_general(dpib, wx_ref[0], NT, preferred_element_type=F32)
        dwa_ref[0] += lax.dot_general(xcb, dprb, TN, preferred_element_type=F32)
        dwx_ref[0] += lax.dot_general(xcb, dpib, TN, preferred_element_type=F32)
        dba_ref[...] += jnp.sum(dpr, axis=0, keepdims=True)
        dbx_ref[...] += jnp.sum(dpi, axis=0, keepdims=True)
        dxc = jnp.where(valid, dxc, 0.0)
        xr = jnp.where(valid, xr_ref[...], 0.0)
        dcb_ref[...] += jnp.sum(dxc, axis=0, keepdims=True)
        for k in range(CONV_WIDTH):
            shifted = xr if k == CONV_WIDTH - 1 else pltpu.roll(xr, CONV_WIDTH - 1 - k, 0)
            dcw_ref[k : k + 1, :] += jnp.sum(dxc * shifted, axis=0, keepdims=True)
        cw = cw_ref[...]
        dxr = cw[3:4] * dxc + cw[2:3] * pltpu.roll(dxc, t_len - 1, 0) + cw[1:2] * pltpu.roll(dxc, t_len - 2, 0) + cw[0:1] * pltpu.roll(dxc, t_len - 3, 0)
        dxr_ref[...] = jnp.where(valid, dxr, 0.0).astype(BF16)

    seq_tile = lambda col0: pl.BlockSpec((t_len, LANES), lambda j, s: (s, col0 + j))
    vec = pl.BlockSpec((1, LANES), lambda j, s: (0, j))
    mat = pl.BlockSpec((1, LANES, LANES), lambda j, s: (j, 0, 0))
    cwb = pl.BlockSpec((CONV_WIDTH, LANES), lambda j, s: (0, j))
    big = jax.ShapeDtypeStruct((rows, REC_WIDTH), BF16)
    vec_shape = jax.ShapeDtypeStruct((1, REC_WIDTH), F32)
    mat_shape = jax.ShapeDtypeStruct((n_tile, LANES, LANES), F32)
    return pl.pallas_call(
        body,
        name=name,
        grid=(n_tile, n_seq),
        in_specs=[seq_tile(0), seq_tile(0), seq_tile(0), seq_tile(n_tile), seq_tile(0), cwb, mat, vec, mat, vec, vec],
        out_specs=[seq_tile(0), seq_tile(0), cwb, vec, mat, vec, mat, vec, vec],
        out_shape=[big, big, jax.ShapeDtypeStruct((CONV_WIDTH, REC_WIDTH), F32), vec_shape, mat_shape, vec_shape, mat_shape, vec_shape, vec_shape],
        scratch_shapes=[pltpu.VMEM((t_len, LANES), F32)] * 4,
        compiler_params=_params(VMEM_BIG),
    )(dy, h, proj, proj, xc, cw, wa, ba, wx, bx, lam)


QKV_WIDTH = ATTN_WIDTH + 2 * KV_WIDTH


def _rotate(v, cos, sin_up, sin_down):
    width = v.shape[1]
    return v * cos + pltpu.roll(v, width - ROPE_DIM // 2, 1) * sin_up + pltpu.roll(v, ROPE_DIM // 2, 1) * sin_down


def _rope_fwd(proj, cos, sin_up, sin_down, t_len, name):
    rows = proj.shape[0]
    nb = t_len // BLOCK
    n_seq = rows // t_len
    q0 = 2 * REC_WIDTH

    def body(q_ref, k_ref, v_ref, cos_ref, up_ref, down_ref, o_ref):
        cos_t, up_t, down_t = cos_ref[...], up_ref[...], down_ref[...]
        o_ref[:, 0:ATTN_WIDTH] = _rotate(q_ref[...], cos_t, up_t, down_t).astype(BF16)
        o_ref[:, ATTN_WIDTH : ATTN_WIDTH + KV_WIDTH] = _rotate(k_ref[...], cos_t[:, :KV_WIDTH], up_t[:, :KV_WIDTH], down_t[:, :KV_WIDTH]).astype(BF16)
        o_ref[:, ATTN_WIDTH + KV_WIDTH :] = v_ref[...].astype(BF16)

    tab = pl.BlockSpec((BLOCK, ATTN_WIDTH), lambda s, n: (n, 0))
    return pl.pallas_call(
        body,
        name=name,
        grid=(n_seq, nb),
        in_specs=[
            pl.BlockSpec((BLOCK, ATTN_WIDTH), lambda s, n: (s * nb + n, q0 // ATTN_WIDTH)),
            pl.BlockSpec((BLOCK, KV_WIDTH), lambda s, n: (s * nb + n, (q0 + ATTN_WIDTH) // KV_WIDTH)),
            pl.BlockSpec((BLOCK, KV_WIDTH), lambda s, n: (s * nb + n, (q0 + ATTN_WIDTH) // KV_WIDTH + 1)),
            tab,
            tab,
            tab,
        ],
        out_specs=pl.BlockSpec((BLOCK, QKV_WIDTH), lambda s, n: (s * nb + n, 0)),
        out_shape=jax.ShapeDtypeStruct((rows, QKV_WIDTH), BF16),
    )(proj, proj, proj, cos, sin_up, sin_down)


def _rope_bwd(dq, dk, dv, cos, sin_up, sin_down, t_len, name):
    rows = dq.shape[0]
    nb = t_len // BLOCK
    n_seq = rows // t_len

    def body(dq_ref, dk_ref, dv_ref, cos_ref, up_ref, down_ref, o_ref):
        cos_t, up_t, down_t = cos_ref[...], -up_ref[...], -down_ref[...]
        o_ref[:, 0:ATTN_WIDTH] = _rotate(dq_ref[...], cos_t, up_t, down_t).astype(BF16)
        o_ref[:, ATTN_WIDTH : ATTN_WIDTH + KV_WIDTH] = _rotate(dk_ref[...], cos_t[:, :KV_WIDTH], up_t[:, :KV_WIDTH], down_t[:, :KV_WIDTH]).astype(BF16)
        o_ref[:, ATTN_WIDTH + KV_WIDTH :] = dv_ref[...].astype(BF16)

    tab = pl.BlockSpec((BLOCK, ATTN_WIDTH), lambda s, n: (n, 0))
    blk = lambda w: pl.BlockSpec((BLOCK, w), lambda s, n: (s * nb + n, 0))
    return pl.pallas_call(
        body,
        name=name,
        grid=(n_seq, nb),
        in_specs=[blk(ATTN_WIDTH), blk(KV_WIDTH), blk(KV_WIDTH), tab, tab, tab],
        out_specs=blk(QKV_WIDTH),
        out_shape=jax.ShapeDtypeStruct((rows, QKV_WIDTH), BF16),
    )(dq, dk, dv, cos, sin_up, sin_down)


GROUP = 4


def _attn_mask(n, pad):
    q_pos = n * BLOCK + (lax.broadcasted_iota(jnp.int32, (GROUP * BLOCK, 1), 0) & (BLOCK - 1)) - pad
    col = lax.broadcasted_iota(jnp.int32, (1, 3 * BLOCK), 1)
    first = col < BLOCK
    k_pos = col - pad + jnp.where(first, 0, (n - 2) * BLOCK)
    dist = q_pos - k_pos
    band = (dist >= 0) & (dist < WINDOW) & (k_pos >= N_META) & jnp.logical_not(first)
    meta = (k_pos >= 0) & (k_pos < N_META) & (k_pos <= q_pos) & first
    return band | meta


def _in_half(e):
    lane = lax.broadcasted_iota(jnp.int32, (1, BLOCK), 1)
    return lane >= HEAD_DIM if e else lane < HEAD_DIM


def _head_views(q2, e, g):
    v = jnp.where(_in_half(e), q2, 0.0)
    return pltpu.roll(v, HEAD_DIM, 1) if e != g else v


def _tile(g, t):
    j = (GROUP // 2) * g + t
    return slice(j * BLOCK, (j + 1) * BLOCK)


def _stack_rows(ref, g):
    parts = []
    for t in range(GROUP // 2):
        tile = ref[:, _tile(g, t)].astype(F32)
        parts += [_head_views(tile, e, g) for e in range(2)]
    return jnp.concatenate(parts, axis=0)


def _stack_group(q_ref, sink_ref, g):
    head = lax.broadcasted_iota(jnp.int32, (GROUP * BLOCK, 1), 0) >> (BLOCK.bit_length() - 1)
    sink = jnp.zeros((GROUP * BLOCK, 1), F32)
    for hh in range(GROUP):
        sink = jnp.where(head == hh, sink_ref[GROUP * g + hh : GROUP * g + hh + 1, 0:1], sink)
    return _stack_rows(q_ref, g), sink


def _unstack_pair(v, g, t):
    out = None
    for e in range(2):
        blk = v[(2 * t + e) * BLOCK : (2 * t + e + 1) * BLOCK, :]
        blk = pltpu.roll(blk, HEAD_DIM, 1) if e != g else blk
        out = blk if out is None else out + blk
    return out


def _softmax_with_sink(scores, mask, sink):
    s = jnp.where(mask, scores * (HEAD_DIM**-0.5), NEG_INF)
    m = jnp.maximum(jnp.max(s, axis=-1, keepdims=True), sink)
    p = jnp.exp(s - m)
    p_sink = jnp.exp(sink - m)
    inv = 1.0 / (jnp.sum(p, axis=-1, keepdims=True) + p_sink)
    return p * inv, p_sink * inv


def _kv_specs(nb):
    k_col = ATTN_WIDTH // KV_WIDTH
    specs = []
    for col in (k_col, k_col + 1):
        specs += [
            pl.BlockSpec((BLOCK, KV_WIDTH), lambda s, n, col=col: (s * nb, col)),
            pl.BlockSpec((BLOCK, KV_WIDTH), lambda s, n, col=col: (s * nb + jnp.maximum(n - 1, 0), col)),
            pl.BlockSpec((BLOCK, KV_WIDTH), lambda s, n, col=col: (s * nb + n, col)),
        ]
    return specs


def _attn_fwd(qkv, sinks, t_len, pad, name):
    rows = qkv.shape[0]
    nb = t_len // BLOCK
    n_seq = rows // t_len

    def body(q_ref, km_ref, kp_ref, kc_ref, vm_ref, vp_ref, vc_ref, sink_ref, o_ref):
        n = pl.program_id(1)
        mask = _attn_mask(n, pad)
        keys = jnp.concatenate([km_ref[...], kp_ref[...], kc_ref[...]], axis=0)
        vals = jnp.concatenate([vm_ref[...], vp_ref[...], vc_ref[...]], axis=0)
        for g in range(N_Q_HEADS // GROUP):
            qs, sink = _stack_group(q_ref, sink_ref, g)
            scores = lax.dot_general(qs.astype(BF16), keys, NT, preferred_element_type=F32)
            p, _ = _softmax_with_sink(scores, mask, sink)
            o = jnp.where(_in_half(g), jnp.dot(p.astype(BF16), vals, preferred_element_type=F32), 0.0)
            for t in range(GROUP // 2):
                o_ref[:, _tile(g, t)] = _unstack_pair(o, g, t)

    return pl.pallas_call(
        body,
        name=name,
        grid=(n_seq, nb),
        in_specs=[pl.BlockSpec((BLOCK, ATTN_WIDTH), lambda s, n: (s * nb + n, 0))] + _kv_specs(nb) + [_full((N_Q_HEADS, BLOCK))],
        out_specs=pl.BlockSpec((BLOCK, ATTN_WIDTH), lambda s, n: (s * nb + n, 0)),
        out_shape=jax.ShapeDtypeStruct((rows, ATTN_WIDTH), F32),
    )(qkv, qkv, qkv, qkv, qkv, qkv, qkv, sinks)


def _attn_bwd(qkv, sinks, o, do, t_len, pad, name):
    rows = qkv.shape[0]
    nb = t_len // BLOCK
    n_seq = rows // t_len

    def body(q_ref, km_ref, kp_ref, kc_ref, vm_ref, vp_ref, vc_ref, sink_ref, o_ref, do_ref, dq_ref, dk_ref, dv_ref, dsink_ref):
        s_id, n = pl.program_id(0), pl.program_id(1)

        @pl.when(n == 0)
        def _():
            dk_ref[...] = jnp.zeros_like(dk_ref)
            dv_ref[...] = jnp.zeros_like(dv_ref)

        @pl.when((n == 0) & (s_id == 0))
        def _():
            dsink_ref[...] = jnp.zeros_like(dsink_ref)

        mask = _attn_mask(n, pad)
        keys = jnp.concatenate([km_ref[...], kp_ref[...], kc_ref[...]], axis=0)
        vals = jnp.concatenate([vm_ref[...], vp_ref[...], vc_ref[...]], axis=0)
        dkeys = jnp.zeros((3 * BLOCK, KV_WIDTH), F32)
        dvals = jnp.zeros((3 * BLOCK, KV_WIDTH), F32)
        for g in range(N_Q_HEADS // GROUP):
            qs, sink = _stack_group(q_ref, sink_ref, g)
            qsb = qs.astype(BF16)
            scores = lax.dot_general(qsb, keys, NT, preferred_element_type=F32)
            p, p_sink = _softmax_with_sink(scores, mask, sink)
            dos = _stack_rows(do_ref, g)
            delta = jnp.sum(dos * _stack_rows(o_ref, g), axis=-1, keepdims=True)
            dosb = dos.astype(BF16)
            dp = lax.dot_general(dosb, vals, NT, preferred_element_type=F32)
            ds = (p * (dp - delta) * (HEAD_DIM**-0.5)).astype(BF16)
            dqs = jnp.where(_in_half(g), jnp.dot(ds, keys, preferred_element_type=F32), 0.0)
            for t in range(GROUP // 2):
                dq_ref[:, _tile(g, t)] = _unstack_pair(dqs, g, t)
            dkeys = dkeys + lax.dot_general(ds, qsb, TN, preferred_element_type=F32)
            dvals = dvals + lax.dot_general(p.astype(BF16), dosb, TN, preferred_element_type=F32)
            sink_term = p_sink * delta
            for hh in range(GROUP):
                head = GROUP * g + hh
                part = -jnp.sum(sink_term[hh * BLOCK : (hh + 1) * BLOCK, :], axis=0, keepdims=True)
                dsink_ref[head : head + 1, :] += jnp.broadcast_to(part, (1, BLOCK))
        r_prev = pl.multiple_of(jnp.maximum(n - 1, 0) * BLOCK, BLOCK)
        r_cur = pl.multiple_of(n * BLOCK, BLOCK)
        for acc, d in ((dk_ref, dkeys), (dv_ref, dvals)):
            acc[0:BLOCK, :] += d[0:BLOCK]
            acc[pl.ds(r_prev, BLOCK), :] += d[BLOCK : 2 * BLOCK]
            acc[pl.ds(r_cur, BLOCK), :] += d[2 * BLOCK :]

    q_blk = pl.BlockSpec((BLOCK, ATTN_WIDTH), lambda s, n: (s * nb + n, 0))
    seq_kv = pl.BlockSpec((t_len, KV_WIDTH), lambda s, n: (s, 0))
    return pl.pallas_call(
        body,
        name=name,
        grid=(n_seq, nb),
        in_specs=[q_blk] + _kv_specs(nb) + [_full((N_Q_HEADS, BLOCK)), q_blk, q_blk],
        out_specs=[q_blk, seq_kv, seq_kv, _full((N_Q_HEADS, BLOCK))],
        out_shape=[
            jax.ShapeDtypeStruct((rows, ATTN_WIDTH), F32),
            jax.ShapeDtypeStruct((rows, KV_WIDTH), F32),
            jax.ShapeDtypeStruct((rows, KV_WIDTH), F32),
            jax.ShapeDtypeStruct((N_Q_HEADS, BLOCK), F32),
        ],
    )(qkv, qkv, qkv, qkv, qkv, qkv, qkv, sinks, o, do)


def _mix_out_fwd(y_rec, y_attn, h, wbuf, loc, g_rec, g_attn, ln_g, ln_b, alpha, name):
    rows, d = h.shape
    tm = _row_tile(rows)

    def body(yr_ref, ya_ref, h_ref, w_hbm, gr_ref, ga_ref, lg_ref, lb_ref, ho_ref, z_ref, w_ref, sems):
        _load_blocks(w_hbm, [(*loc, w_ref)], sems)
        nr = _rms(yr_ref[...], gr_ref[...]).astype(BF16)
        na = _rms(ya_ref[...], ga_ref[...]).astype(BF16)
        m = jnp.dot(nr, w_ref[0:REC_WIDTH, :], preferred_element_type=F32) + jnp.dot(na, w_ref[REC_WIDTH:, :], preferred_element_type=F32)
        z = alpha * h_ref[...] + m
        z_ref[...] = z
        xhat, _ = _ln_stats(z)
        ho_ref[...] = xhat * lg_ref[...] + lb_ref[...]

    row_d = pl.BlockSpec((tm, d), lambda i: (i, 0))
    row_h = pl.BlockSpec((tm, REC_WIDTH), lambda i: (i, 0))
    return pl.pallas_call(
        body,
        name=name,
        grid=(rows // tm,),
        in_specs=[row_h, row_h, row_d, ANY, _full((1, REC_WIDTH)), _full((1, ATTN_WIDTH)), _full((1, d)), _full((1, d))],
        out_specs=[row_d, row_d],
        out_shape=[jax.ShapeDtypeStruct((rows, d), F32)] * 2,
        scratch_shapes=_weight_scratch([loc], d),
    )(y_rec, y_attn, h, wbuf, g_rec, g_attn, ln_g, ln_b)


def _mix_out_bwd(dh, z, y_rec, y_attn, wbuf, loc, g_rec, g_attn, ln_g, name):
    rows, d = dh.shape
    tm = _row_tile(rows)
    mix = REC_WIDTH + ATTN_WIDTH

    def body(dh_ref, z_ref, yr_ref, ya_ref, w_hbm, gr_ref, ga_ref, lg_ref, dz_ref, dzb_ref, dyr_ref, dya_ref, yn_ref, dgam_ref, dbet_ref, dgr_ref, dga_ref, w_ref, sems):
        _load_blocks(w_hbm, [(*loc, w_ref)], sems)

        @pl.when(pl.program_id(0) == 0)
        def _():
            for ref in (dgam_ref, dbet_ref, dgr_ref, dga_ref):
                ref[...] = jnp.zeros_like(ref)

        dz, dgam, dbet = _ln_bwd(dh_ref[...], z_ref[...], lg_ref[...])
        dgam_ref[...] += dgam
        dbet_ref[...] += dbet
        dz_ref[...] = dz
        dzb = dz.astype(BF16)
        dzb_ref[...] = dzb
        dyn = lax.dot_general(dzb, w_ref[...], NT, preferred_element_type=F32)
        dyr, dgr, nr = _rms_bwd(dyn[:, 0:REC_WIDTH], yr_ref[...], gr_ref[...])
        dya, dga, na = _rms_bwd(dyn[:, REC_WIDTH:], ya_ref[...], ga_ref[...])
        dyr_ref[...] = dyr
        dya_ref[...] = dya
        dgr_ref[...] += dgr
        dga_ref[...] += dga
        yn_ref[:, 0:REC_WIDTH] = nr.astype(BF16)
        yn_ref[:, REC_WIDTH:] = na.astype(BF16)

    row_d = pl.BlockSpec((tm, d), lambda i: (i, 0))
    row_h = pl.BlockSpec((tm, REC_WIDTH), lambda i: (i, 0))
    row_m = pl.BlockSpec((tm, mix), lambda i: (i, 0))
    return pl.pallas_call(
        body,
        name=name,
        grid=(rows // tm,),
        in_specs=[row_d, row_d, row_h, row_h, ANY, _full((1, REC_WIDTH)), _full((1, ATTN_WIDTH)), _full((1, d))],
        out_specs=[row_d, row_d, row_h, row_h, row_m, _full((1, d)), _full((1, d)), _full((1, REC_WIDTH)), _full((1, ATTN_WIDTH))],
        out_shape=[
            jax.ShapeDtypeStruct((rows, d), F32),
            jax.ShapeDtypeStruct((rows, d), BF16),
            jax.ShapeDtypeStruct((rows, REC_WIDTH), F32),
            jax.ShapeDtypeStruct((rows, ATTN_WIDTH), F32),
            jax.ShapeDtypeStruct((rows, mix), BF16),
            jax.ShapeDtypeStruct((1, d), F32),
            jax.ShapeDtypeStruct((1, d), F32),
            jax.ShapeDtypeStruct((1, REC_WIDTH), F32),
            jax.ShapeDtypeStruct((1, ATTN_WIDTH), F32),
        ],
        scratch_shapes=_weight_scratch([loc], d),
    )(dh, z, y_rec, y_attn, wbuf, g_rec, g_attn, ln_g)


def _loss_head(y, target, t_len, first_token, name):
    rows, d = y.shape
    tm = _row_tile(rows)

    def body(y_ref, t_ref, loss_ref, dy_ref):
        i = pl.program_id(0)

        @pl.when(i == 0)
        def _():
            loss_ref[...] = jnp.zeros_like(loss_ref)

        row = i * tm + lax.broadcasted_iota(jnp.int32, (tm, 1), 0)
        is_token = lax.rem(row, t_len) >= first_token
        err = jnp.where(is_token, y_ref[...] - t_ref[...], 0.0)
        dy_ref[...] = err / d
        per_row = jnp.sum(err * err, axis=-1, keepdims=True) / d
        loss_ref[...] += jnp.broadcast_to(0.5 * jnp.sum(per_row, axis=0, keepdims=True), (1, BLOCK))

    row_d = pl.BlockSpec((tm, d), lambda i: (i, 0))
    return pl.pallas_call(
        body,
        name=name,
        grid=(rows // tm,),
        in_specs=[row_d, row_d],
        out_specs=[_full((1, BLOCK)), row_d],
        out_shape=[jax.ShapeDtypeStruct((1, BLOCK), F32), jax.ShapeDtypeStruct((rows, d), F32)],
    )(y, target)


def _meta_grad(dh0, t_len, pad, name):
    rows, d = dh0.shape
    nb = t_len // BLOCK
    n_seq = rows // t_len

    def body(dh_ref, o_ref):
        @pl.when(pl.program_id(0) == 0)
        def _():
            o_ref[...] = jnp.zeros_like(o_ref)

        o_ref[...] += dh_ref[pad : pad + N_META, :]

    return pl.pallas_call(
        body,
        name=name,
        grid=(n_seq,),
        in_specs=[pl.BlockSpec((BLOCK, d), lambda s: (s * nb, 0))],
        out_specs=_full((N_META, d)),
        out_shape=jax.ShapeDtypeStruct((N_META, d), F32),
    )(dh0)


def _adamw(w, g, m, v, name):
    shape = w.shape
    cols = shape[-1]
    rows = w.size // cols
    tr = rows
    for cand in (512, 256, 128, 64):
        if rows % cand == 0 and rows > cand:
            tr = cand
            break

    def body(w_ref, g_ref, m_ref, v_ref, d_ref, nm_ref, nv_ref):
        gv = g_ref[...]
        nm = ADAM_B1 * m_ref[...] + (1.0 - ADAM_B1) * gv
        nv = ADAM_B2 * v_ref[...] + (1.0 - ADAM_B2) * (gv * gv)
        m_hat = nm / (1.0 - ADAM_B1**ADAM_STEP)
        v_hat = nv / (1.0 - ADAM_B2**ADAM_STEP)
        d_ref[...] = -ADAM_LR * (m_hat / (jnp.sqrt(v_hat) + ADAM_EPS) + ADAM_WD * w_ref[...])
        nm_ref[...] = nm
        nv_ref[...] = nv

    blk = pl.BlockSpec((tr, cols), lambda i: (i, 0))
    flat = [a.reshape(rows, cols) for a in (w, g, m, v)]
    outs = pl.pallas_call(
        body,
        name=name,
        grid=(rows // tr,),
        in_specs=[blk] * 4,
        out_specs=[blk] * 3,
        out_shape=[jax.ShapeDtypeStruct((rows, cols), F32)] * 3,
    )(*flat)
    return [o.reshape(shape) for o in outs]


WEIGHTS = ["meta_tokens", "ffn1_w_gate", "ffn1_w_up", "ffn1_w_down", "ln1_g", "ln1_b", "w_in", "conv_w", "conv_b", "gate_a_w", "gate_a_b",
           "gate_x_w", "gate_x_b", "lru_lambda", "attn_sinks", "norm_rec_g", "norm_attn_g", "w_out", "ln2_g", "ln2_b", "ffn2_w_gate",
           "ffn2_w_up", "ffn2_w_down", "ln3_g", "ln3_b"]
BIG = [("ffn1_w_gate", True), ("ffn1_w_up", True), ("ffn1_w_down", False), ("w_in", True), ("w_out", False),
       ("ffn2_w_gate", True), ("ffn2_w_up", True), ("ffn2_w_down", False)]


GATHER_GROUPS = [["ffn1_w_gate", "ffn1_w_up", "ffn1_w_down"], ["w_in", "w_out"], ["ffn2_w_gate", "ffn2_w_up", "ffn2_w_down"]]
EXCHANGE_GROUPS = [["ffn2_w_gate", "ffn2_w_up", "ffn2_w_down", "w_out"], ["w_in", "ffn1_w_gate", "ffn1_w_up", "ffn1_w_down"]]


def _rope_tables(t_len, pad):
    pos = (jnp.arange(t_len) - pad).astype(F32)
    inv_freq = ROPE_THETA ** (-jnp.arange(0, ROPE_DIM, 2, dtype=F32) / ROPE_DIM)
    ang = pos[:, None] * inv_freq[None, :]
    cos, sin = jnp.cos(ang), jnp.sin(ang)
    half = ROPE_DIM // 2
    rest = jnp.zeros((t_len, HEAD_DIM - ROPE_DIM), F32)
    zero = jnp.zeros((t_len, half), F32)
    cos_h = jnp.concatenate([cos, cos, rest + 1.0], axis=1)
    up_h = jnp.concatenate([-sin, zero, rest], axis=1)
    down_h = jnp.concatenate([zero, sin, rest], axis=1)
    return [jnp.tile(t, (1, ATTN_WIDTH // HEAD_DIM)) for t in (cos_h, up_h, down_h)]


def _gate_tiles(w):
    z = jnp.zeros((HEAD_DIM, HEAD_DIM), w.dtype)
    tiles = [jnp.block([[w[2 * j], z], [z, w[2 * j + 1]]]) for j in range(w.shape[0] // 2)]
    return jnp.stack(tiles).astype(BF16)


def _gate_blocks(tiles):
    out = []
    for j in range(tiles.shape[0]):
        out += [tiles[j, :HEAD_DIM, :HEAD_DIM], tiles[j, HEAD_DIM:, HEAD_DIM:]]
    return jnp.stack(out)


def kernel(x, meta_tokens, ffn1_w_gate, ffn1_w_up, ffn1_w_down, ln1_g, ln1_b, w_in, conv_w, conv_b, gate_a_w, gate_a_b, gate_x_w, gate_x_b, lru_lambda, attn_sinks, norm_rec_g, norm_attn_g, w_out, ln2_g, ln2_b, ffn2_w_gate, ffn2_w_up, ffn2_w_down, ln3_g, ln3_b, loss_target, m_meta_tokens, m_ffn1_w_gate, m_ffn1_w_up, m_ffn1_w_down, m_ln1_g, m_ln1_b, m_w_in, m_conv_w, m_conv_b, m_gate_a_w, m_gate_a_b, m_gate_x_w, m_gate_x_b, m_lru_lambda, m_attn_sinks, m_norm_rec_g, m_norm_attn_g, m_w_out, m_ln2_g, m_ln2_b, m_ffn2_w_gate, m_ffn2_w_up, m_ffn2_w_down, m_ln3_g, m_ln3_b, v_meta_tokens, v_ffn1_w_gate, v_ffn1_w_up, v_ffn1_w_down, v_ln1_g, v_ln1_b, v_w_in, v_conv_w, v_conv_b, v_gate_a_w, v_gate_a_b, v_gate_x_w, v_gate_x_b, v_lru_lambda, v_attn_sinks, v_norm_rec_g, v_norm_attn_g, v_w_out, v_ln2_g, v_ln2_b, v_ffn2_w_gate, v_ffn2_w_up, v_ffn2_w_down, v_ln3_g, v_ln3_b):
    given = dict(locals())
    w = {k: given[k] for k in WEIGHTS}
    n_seq, seq, d = x.shape
    depth = ln1_g.shape[0]
    alpha = (2.0 * depth) ** 0.25
    pad = (-(N_META + seq)) % BLOCK
    t_len = pad + N_META + seq
    rows = n_seq * t_len
    me = 4 * lax.axis_index("x") + 2 * lax.axis_index("y") + lax.axis_index("c")

    sent = {name: (_transpose_to_bf16(w[name], "t_" + name) if transposed else w[name].astype(BF16)) for name, transposed in BIG}
    loc = {}
    for group in GATHER_GROUPS:
        off = 0
        for name in group:
            loc[name] = (off, sent[name].shape[1])
            off += sent[name].shape[1]
    ffn1_loc = [loc["ffn1_w_gate"], loc["ffn1_w_up"], loc["ffn1_w_down"]]
    ffn2_loc = [loc["ffn2_w_gate"], loc["ffn2_w_up"], loc["ffn2_w_down"]]
    gathers, chain = [], meta_tokens
    for l in range(depth):
        for k, group in enumerate(GATHER_GROUPS):
            shard = jnp.concatenate([sent[name][l] for name in group], axis=0)
            zone = lax.dynamic_update_slice(lax.empty((N_DEV,) + shard.shape, BF16), shard[None], (me, 0, 0))
            gathers.append(_push_start([shard], [zone], chain, False, f"gather_start_{l}_{k}"))
            chain = gathers[-1][2][0]
    small = jnp.concatenate([meta_tokens, conv_w.reshape(-1, BLOCK)], axis=0)
    (small_all,) = _all_gather([small], small, "gather_small")
    meta_full = small_all[:, :N_META, :].transpose(1, 0, 2).reshape(N_META, d)
    conv_shard = conv_w.shape[-1]
    conv_full = small_all[:, N_META:, :].reshape(N_DEV, depth, CONV_WIDTH, conv_shard).transpose(1, 2, 0, 3).reshape(depth, CONV_WIDTH, REC_WIDTH)

    cos, sin_up, sin_down = _rope_tables(t_len, pad)
    row1 = lambda a: a.reshape(1, -1)

    h = jnp.concatenate([jnp.zeros((n_seq, pad, d), F32), jnp.broadcast_to(meta_full[None], (n_seq, N_META, d)), x], axis=1).reshape(rows, d)
    target = jnp.pad(loss_target, ((0, 0), (pad + N_META, 0), (0, 0))).reshape(rows, d)
    saved, wbufs = [], []

    def gathered(l, k, after):
        _, (buf,) = _push_wait(*gathers[len(GATHER_GROUPS) * l + k], after, False, f"gather_wait_{l}_{k}")
        return buf

    for l in range(depth):
        w_ffn1 = gathered(l, 0, chain if l == 0 else h)
        s = {"h0": h}
        h, s["g1"], s["u1"], s["z1"] = _ffn_fwd(h, w_ffn1, ffn1_loc, row1(ln1_g[l]), row1(ln1_b[l]), alpha, "ffn1_fwd")
        s["h1"] = h
        s["wa"], s["wx"] = _gate_tiles(gate_a_w[l]), _gate_tiles(gate_x_w[l])
        s["sinks"] = jnp.broadcast_to(attn_sinks[l][:, None], (N_Q_HEADS, BLOCK))
        wbuf = gathered(l, 1, h)
        proj = _proj_fwd(h, wbuf, loc["w_in"], "proj_fwd")
        s["proj"] = proj
        s["xc"], s["hrec"], s["y_rec"] = _rglru_fwd(proj, conv_full[l], row1(conv_b[l]), s["wa"], row1(gate_a_b[l]), s["wx"], row1(gate_x_b[l]),
                                                     row1(lru_lambda[l]), t_len, pad, "rglru_fwd")
        s["qkv"] = _rope_fwd(proj, cos, sin_up, sin_down, t_len, "rope_fwd")
        s["y_attn"] = _attn_fwd(s["qkv"], s["sinks"], t_len, pad, "attn_fwd")
        h, s["z2"] = _mix_out_fwd(s["y_rec"], s["y_attn"], h, wbuf, loc["w_out"], row1(norm_rec_g[l]), row1(norm_attn_g[l]), row1(ln2_g[l]), row1(ln2_b[l]), alpha, "mix_out_fwd")
        s["h2"] = h
        w_ffn2 = gathered(l, 2, h)
        h, s["g2"], s["u2"], s["z3"] = _ffn_fwd(h, w_ffn2, ffn2_loc, row1(ln3_g[l]), row1(ln3_b[l]), alpha, "ffn2_fwd")
        saved.append(s)
        wbufs.append((w_ffn1, wbuf, w_ffn2))
    loss_part, dh = _loss_head(h, target, t_len, pad + N_META, "loss_head")

    exchanges = {}
    small_grads = [None] * depth
    behind = dh

    def exchange(l, k, bg):
        gs = [bg[name].reshape(N_DEV, -1, d) for name in EXCHANGE_GROUPS[k]]
        zones = [lax.empty(g.shape, g.dtype) for g in gs]
        exchanges[(l, k)] = _push_start(gs, zones, gs[0], True, f"exchange_start_{l}_{k}")
        return exchanges[(l, k)][2][0]

    for l in reversed(range(depth)):
        (w_ffn1, wbuf, w_ffn2), s = wbufs[l], saved[l]
        bg, sg = {}, {}
        dh, dg, du, act, dyb, sg["ln3_g"], sg["ln3_b"] = _ffn_bwd(dh, s["z3"], s["g2"], s["u2"], w_ffn2, ffn2_loc, row1(ln3_g[l]), behind, alpha, "ffn2_bwd")
        bg["ffn2_w_gate"] = _mm_tn(dg, s["h2"], BF16, "ffn_wgrad_in")
        bg["ffn2_w_up"] = _mm_tn(du, s["h2"], BF16, "ffn_wgrad_in")
        bg["ffn2_w_down"] = _mm_tn(act, dyb, BF16, "ffn_wgrad_down")
        dz, dzb, dy_rec, dy_attn, yn, sg["ln2_g"], sg["ln2_b"], sg["norm_rec_g"], sg["norm_attn_g"] = _mix_out_bwd(
            dh, s["z2"], s["y_rec"], s["y_attn"], wbuf, loc["w_out"], row1(norm_rec_g[l]), row1(norm_attn_g[l]), row1(ln2_g[l]), "mix_out_bwd")
        bg["w_out"] = _mm_tn(yn, dzb, BF16, "w_out_wgrad")
        behind = exchange(l, 0, bg)
        dxr, dgt, sg["conv_w"], sg["conv_b"], dwa, sg["gate_a_b"], dwx, sg["gate_x_b"], sg["lru_lambda"] = _rglru_bwd(
            dy_rec, s["hrec"], s["proj"], s["xc"], conv_full[l], s["wa"], row1(gate_a_b[l]), s["wx"], row1(gate_x_b[l]), row1(lru_lambda[l]), t_len, pad, "rglru_bwd")
        sg["gate_a_w"], sg["gate_x_w"] = _gate_blocks(dwa), _gate_blocks(dwx)
        dq, dk, dv, dsink = _attn_bwd(s["qkv"], s["sinks"], s["y_attn"], dy_attn, t_len, pad, "attn_bwd")
        sg["attn_sinks"] = dsink[:, 0]
        dqkv = _rope_bwd(dq, dk, dv, cos, sin_up, sin_down, t_len, "rope_bwd")
        dproj = jnp.concatenate([dxr, dgt, dqkv], axis=1)
        bg["w_in"] = _mm_tn(dproj, s["h1"], BF16, "w_in_wgrad")
        dh = _proj_bwd(dproj, wbuf, loc["w_in"], dz, alpha, "proj_bwd")
        dh, dg, du, act, dyb, sg["ln1_g"], sg["ln1_b"] = _ffn_bwd(dh, s["z1"], s["g1"], s["u1"], w_ffn1, ffn1_loc, row1(ln1_g[l]), behind, alpha, "ffn1_bwd")
        bg["ffn1_w_gate"] = _mm_tn(dg, s["h0"], BF16, "ffn_wgrad_in")
        bg["ffn1_w_up"] = _mm_tn(du, s["h0"], BF16, "ffn_wgrad_in")
        bg["ffn1_w_down"] = _mm_tn(act, dyb, BF16, "ffn_wgrad_down")
        behind = exchange(l, 1, bg)
        small_grads[l] = sg
    grad_x = dh.reshape(n_seq, t_len, d)[:, pad + N_META :, :]
    dmeta = _meta_grad(dh, t_len, pad, "meta_grad")

    grads = {}
    small_names = ["ln1_g", "ln1_b", "ln2_g", "ln2_b", "ln3_g", "ln3_b", "conv_b", "gate_a_b", "gate_x_b", "lru_lambda", "norm_rec_g",
                   "norm_attn_g", "conv_w", "gate_a_w", "gate_x_w", "attn_sinks"]
    pieces = [small_grads[l][name].reshape(-1) for l in range(depth) for name in small_names] + [dmeta.reshape(-1), loss_part[0, :1]]
    sizes = [p.shape[0] for p in pieces]
    flat = jnp.concatenate(pieces)
    width = 1024
    n_rows = -(-flat.shape[0] // (8 * width)) * 8
    flat = jnp.pad(flat, (0, n_rows * width - flat.shape[0])).reshape(n_rows, width)
    (flat_all,) = _all_gather([flat], behind, "gather_small_grads")
    total = _sum8(flat_all, "sum_small_grads").reshape(-1)
    offs = [0]
    for sz in sizes:
        offs.append(offs[-1] + sz)
    taken = [total[offs[k] : offs[k + 1]] for k in range(len(sizes))]
    for j, name in enumerate(small_names):
        full_shape = (depth,) + ((CONV_WIDTH, REC_WIDTH) if name == "conv_w" else w[name].shape[1:])
        grads[name] = jnp.stack([taken[l * len(small_names) + j] for l in range(depth)]).reshape(full_shape)
    grads["conv_w"] = lax.dynamic_slice_in_dim(grads["conv_w"], me * conv_shard, conv_shard, axis=2)
    meta_shard = meta_tokens.shape[1]
    grads["meta_tokens"] = lax.dynamic_slice_in_dim(taken[-2].reshape(N_META, d), me * meta_shard, meta_shard, axis=1)
    loss = taken[-1][0]
    deltas, new_m, new_v = {}, {}, {}
    for name in WEIGHTS:
        if name not in dict(BIG):
            deltas[name], new_m[name], new_v[name] = _adamw(w[name], grads[name], given["m_" + name], given["v_" + name], "adamw")

    reduced = {}
    after = new_v[WEIGHTS[-1]]
    transposed = dict(BIG)
    for l in reversed(range(depth)):
        for k, group in enumerate(EXCHANGE_GROUPS):
            send_sems, recv_sems, srcs, zones = exchanges[(l, k)]
            owns, parts = _push_wait(send_sems, recv_sems, srcs, zones, after, True, f"exchange_wait_{l}_{k}")
            for name, own, p in zip(group, owns, parts):
                reduced[(l, name)] = _sum_exchanged(p, own, "sum_grads")
                if transposed[name]:
                    reduced[(l, name)] = _transpose_f32(reduced[(l, name)], "t_grad")
            after = reduced[(l, group[-1])]
    for name, _ in BIG:
        grads[name] = jnp.stack([reduced[(l, name)] for l in range(depth)])
        deltas[name], new_m[name], new_v[name] = _adamw(w[name], grads[name], given["m_" + name], given["v_" + name], "adamw")
    return (loss, grad_x, *[grads[k] for k in WEIGHTS], *[deltas[k] for k in WEIGHTS], *[new_m[k] for k in WEIGHTS], *[new_v[k] for k in WEIGHTS])
```

```python
import functools

import jax
import jax.numpy as jnp
from jax import lax
from jax.experimental import pallas as pl
from jax.experimental.pallas import tpu as pltpu

F32 = jnp.float32
BF16 = jnp.bfloat16
MESH = pl.DeviceIdType.MESH

N_DEV = 8
N_META = 16
BLOCK = 128
WINDOW = 128
REC_WIDTH = 512
ATTN_WIDTH = 512
KV_WIDTH = 128
HEAD_DIM = 64
N_Q_HEADS = 8
ROPE_DIM = 16
ROPE_THETA = 500000.0
CONV_WIDTH = 4
LRU_C = 8.0
LN_EPS = 1e-5
RMS_EPS = 1e-6
NEG_INF = -1e30
ADAM_LR, ADAM_B1, ADAM_B2, ADAM_EPS, ADAM_WD, ADAM_STEP = 0.001, 0.9, 0.999, 1e-08, 0.01, 10

NT = (((1,), (1,)), ((), ()))
TN = (((0,), (0,)), ((), ()))
VMEM_BIG = 56 * 1024 * 1024


def _params(vmem=None):
    return pltpu.CompilerParams(vmem_limit_bytes=vmem)


def _row_tile(rows):
    return 256 if rows % 256 == 0 else 128


def _sigmoid(x):
    return 1.0 / (1.0 + jnp.exp(-x))


def _expm1_nonpos(x):
    series = x * (1.0 + 0.5 * x * (1.0 + x / 3.0 * (1.0 + 0.25 * x * (1.0 + 0.2 * x))))
    return jnp.where(x > -0.05, series, jnp.exp(x) - 1.0)


def _softplus(x):
    e = jnp.exp(-jnp.abs(x))
    log1p = jnp.where(e < 1e-3, e * (1.0 - e * (0.5 - e / 3.0)), jnp.log(1.0 + e))
    return jnp.maximum(x, 0.0) + log1p


def _gelu_and_grad(x):
    c0 = 0.7978845608028654
    x2 = x * x
    t = jnp.tanh(c0 * (x + 0.044715 * x * x2))
    gelu = 0.5 * x * (1.0 + t)
    grad = 0.5 * (1.0 + t) + 0.5 * x * (1.0 - t * t) * c0 * (1.0 + 3.0 * 0.044715 * x2)
    return gelu, grad


def _ln_stats(z):
    mu = jnp.mean(z, axis=-1, keepdims=True)
    zc = z - mu
    var = jnp.mean(zc * zc, axis=-1, keepdims=True)
    rstd = lax.rsqrt(var + LN_EPS)
    return zc * rstd, rstd


def _ln_bwd(dy, z, gamma):
    xhat, rstd = _ln_stats(z)
    dxh = dy * gamma
    m1 = jnp.mean(dxh, axis=-1, keepdims=True)
    m2 = jnp.mean(dxh * xhat, axis=-1, keepdims=True)
    dz = rstd * (dxh - m1 - xhat * m2)
    return dz, jnp.sum(dy * xhat, axis=0, keepdims=True), jnp.sum(dy, axis=0, keepdims=True)


def _rms(y, gamma):
    r = lax.rsqrt(jnp.mean(y * y, axis=-1, keepdims=True) + RMS_EPS)
    return y * r * gamma


def _rms_bwd(dout, y, gamma):
    r = lax.rsqrt(jnp.mean(y * y, axis=-1, keepdims=True) + RMS_EPS)
    n = y * r
    dn = dout * gamma
    dy = r * (dn - n * jnp.mean(dn * n, axis=-1, keepdims=True))
    return dy, jnp.sum(dout * n, axis=0, keepdims=True), n * gamma


def _load_blocks(w_hbm, items, sems):
    @pl.when(pl.program_id(0) == 0)
    def _():
        copies = []
        for k, (off, nr, dst) in enumerate(items):
            for dev in range(N_DEV):
                copies.append(pltpu.make_async_copy(w_hbm.at[dev, pl.ds(off, nr), :], dst.at[pl.ds(dev * nr, nr), :], sems.at[k, dev]))
        for cp in copies:
            cp.start()
        for cp in copies:
            cp.wait()


def _full(shape):
    return pl.BlockSpec(shape, lambda *_: (0,) * len(shape))


ANY = pl.BlockSpec(memory_space=pl.ANY)


def _all_gather(xs, after, name):
    n = len(xs)

    def body(*refs):
        ins, outs = refs[:n], refs[n + 1 : 2 * n + 1]
        send_sems, recv_sems, local_sems = refs[2 * n + 1 :]
        x, y, c = lax.axis_index("x"), lax.axis_index("y"), lax.axis_index("c")
        me, sibling = (x, y, c), (x, y, 1 - c)
        chips = [(1 - x, y), (x, 1 - y), (1 - x, 1 - y)]

        def slot(i, dev):
            return outs[i].at[4 * dev[0] + 2 * dev[1] + dev[2]]

        def copy(i, k, block, to, src=None):
            return pltpu.make_async_remote_copy(
                src_ref=slot(i, block) if src is None else src,
                dst_ref=slot(i, block),
                send_sem=send_sems.at[i, k],
                recv_sem=recv_sems.at[i, k],
                device_id=to,
                device_id_type=MESH,
            )

        mine = [pltpu.make_async_copy(ins[i], slot(i, me), local_sems.at[i]) for i in range(n)]
        for cp in mine:
            cp.start()
        first = []
        for i in range(n):
            first.append(copy(i, 0, me, sibling, src=ins[i]))
            first += [copy(i, 1 + j, me, (*chip, c), src=ins[i]) for j, chip in enumerate(chips)]
        for cp in first:
            cp.start()
        passed = []
        for j, chip in enumerate(chips):
            for i in range(n):
                copy(i, 1 + j, (*chip, c), me).wait_recv()
                fwd = copy(i, 4 + j, (*chip, c), sibling)
                fwd.start()
                passed.append(fwd)
        for i in range(n):
            copy(i, 0, sibling, me).wait_recv()
            for j, chip in enumerate(chips):
                copy(i, 4 + j, (*chip, 1 - c), me).wait_recv()
        for cp in first + passed:
            cp.wait_send()
        for cp in mine:
            cp.wait()

    return pl.pallas_call(
        body,
        name=name,
        out_shape=[jax.ShapeDtypeStruct((N_DEV,) + a.shape, a.dtype) for a in xs],
        in_specs=[ANY] * (n + 1),
        out_specs=[ANY] * n,
        scratch_shapes=[pltpu.SemaphoreType.DMA((n, 7)), pltpu.SemaphoreType.DMA((n, 7)), pltpu.SemaphoreType.DMA((n,))],
    )(*xs, after)


HBM = pl.BlockSpec(memory_space=pltpu.HBM)
SEM = pl.BlockSpec(memory_space=pltpu.SEMAPHORE)
EFFECT = pltpu.SideEffectType.DATAFLOW_SIDE_EFFECTING


def _push_copies(ins, lands, send_sems, recv_sems, scatter):
    x, y, c = lax.axis_index("x"), lax.axis_index("y"), lax.axis_index("c")
    me = 4 * x + 2 * y + c
    copies = []
    for k in range(1, N_DEV):
        px = 1 - x if (k >> 2) & 1 else x
        py = 1 - y if (k >> 1) & 1 else y
        pc = 1 - c if k & 1 else c
        for i in range(len(ins)):
            copies.append(
                pltpu.make_async_remote_copy(
                    src_ref=ins[i].at[4 * px + 2 * py + pc] if scatter else ins[i],
                    dst_ref=lands[i].at[me],
                    send_sem=send_sems.at[i * (N_DEV - 1) + k - 1],
                    recv_sem=recv_sems.at[i * (N_DEV - 1) + k - 1],
                    device_id=(px, py, pc),
                    device_id_type=MESH,
                )
            )
    return copies


def _push_start(xs, lands, chain, scatter, name):
    n = len(xs)

    def body(*refs):
        ins, zones = refs[:n], refs[n : 2 * n]
        for cp in _push_copies(ins, zones, refs[2 * n + 1], refs[2 * n + 2], scatter):
            cp.start()

    outs = pl.pallas_call(
        body,
        name=name,
        out_shape=(
            pltpu.SemaphoreType.DMA((n * (N_DEV - 1),)),
            pltpu.SemaphoreType.DMA((n * (N_DEV - 1),)),
            *[pltpu.HBM(a.shape, a.dtype) for a in list(xs) + list(lands)],
        ),
        in_specs=[HBM] * (2 * n) + [ANY],
        out_specs=(SEM, SEM, *[HBM] * (2 * n)),
        input_output_aliases={i: 2 + i for i in range(2 * n)},
        compiler_params=pltpu.CompilerParams(has_side_effects=EFFECT),
    )(*[pltpu.with_memory_space_constraint(a, pltpu.HBM) for a in list(xs) + list(lands)], chain)
    return outs[0], outs[1], list(outs[2 : 2 + n]), list(outs[2 + n : 2 + 2 * n])


def _push_wait(send_sems, recv_sems, srcs, lands, after, scatter, name):
    n = len(srcs)

    def body(*refs):
        ins, zones = refs[:n], refs[n : 2 * n]
        for cp in _push_copies(ins, zones, refs[2 * n], refs[2 * n + 1], scatter):
            cp.wait_send()
            cp.wait_recv()

    outs = pl.pallas_call(
        body,
        name=name,
        out_shape=[pltpu.HBM(a.shape, a.dtype) for a in srcs + lands],
        in_specs=[HBM] * (2 * n) + [SEM, SEM, ANY],
        out_specs=[HBM] * (2 * n),
        input_output_aliases={i: i for i in range(2 * n)},
        compiler_params=pltpu.CompilerParams(has_side_effects=EFFECT),
    )(*srcs, *lands, send_sems, recv_sems, after)
    return list(outs[:n]), list(outs[n:])


def _sum8(parts, name):
    _, rows, cols = parts.shape
    tr = rows
    for cand in (512, 256, 128, 64, 32, 16):
        if rows % cand == 0 and rows > cand:
            tr = cand
            break

    def body(p_ref, o_ref):
        acc = p_ref[0].astype(F32)
        for d in range(1, N_DEV):
            acc = acc + p_ref[d].astype(F32)
        o_ref[...] = acc

    return pl.pallas_call(
        body,
        name=name,
        grid=(rows // tr,),
        in_specs=[pl.BlockSpec((N_DEV, tr, cols), lambda i: (0, i, 0))],
        out_specs=pl.BlockSpec((tr, cols), lambda i: (i, 0)),
        out_shape=jax.ShapeDtypeStruct((rows, cols), F32),
    )(parts)


def _sum_exchanged(parts, own, name):
    _, rows, cols = parts.shape

    def body(p_ref, own_hbm, o_ref, own_v, sem):
        me = 4 * lax.axis_index("x") + 2 * lax.axis_index("y") + lax.axis_index("c")
        cp = pltpu.make_async_copy(own_hbm.at[me], own_v, sem)
        cp.start()
        cp.wait()
        o_ref[...] = own_v[...].astype(F32)
        for dev in range(N_DEV):

            @pl.when(me != dev)
            def _():
                o_ref[...] += p_ref[dev].astype(F32)

    return pl.pallas_call(
        body,
        name=name,
        grid=(1,),
        in_specs=[_full((N_DEV, rows, cols)), ANY],
        out_specs=_full((rows, cols)),
        out_shape=jax.ShapeDtypeStruct((rows, cols), F32),
        scratch_shapes=[pltpu.VMEM((rows, cols), parts.dtype), pltpu.SemaphoreType.DMA(())],
        compiler_params=_params(VMEM_BIG),
    )(parts, own)


def _mm_tn(a, b, out_dtype, name):
    rows, m = a.shape
    n = b.shape[1]
    tm = m // 2 if (m // 2) % 128 == 0 and m > 512 else m
    tr = next(t for t in (2176, 1024, 512, 256, 128) if rows % t == 0)
    nk = rows // tr

    def body(a_ref, b_ref, o_ref, acc):
        k = pl.program_id(1)
        part = lax.dot_general(a_ref[...].astype(BF16), b_ref[...].astype(BF16), TN, preferred_element_type=F32)

        @pl.when(k == 0)
        def _():
            acc[...] = part

        @pl.when(k > 0)
        def _():
            acc[...] += part

        @pl.when(k == nk - 1)
        def _():
            o_ref[...] = acc[...].astype(o_ref.dtype)

    return pl.pallas_call(
        body,
        name=name,
        grid=(m // tm, nk),
        in_specs=[pl.BlockSpec((tr, tm), lambda i, k: (k, i)), pl.BlockSpec((tr, n), lambda i, k: (k, 0))],
        out_specs=pl.BlockSpec((tm, n), lambda i, k: (i, 0)),
        out_shape=jax.ShapeDtypeStruct((m, n), out_dtype),
        scratch_shapes=[pltpu.VMEM((tm, n), F32)],
        compiler_params=_params(VMEM_BIG),
    )(a, b)


def _transpose_to_bf16(w, name):
    layers, a_dim, b_dim = w.shape

    def body(w_ref, o_ref):
        eye = (lax.broadcasted_iota(jnp.int32, (a_dim, a_dim), 0) == lax.broadcasted_iota(jnp.int32, (a_dim, a_dim), 1)).astype(BF16)
        o_ref[0] = lax.dot_general(w_ref[0].astype(BF16), eye, TN, preferred_element_type=F32).astype(BF16)

    return pl.pallas_call(
        body,
        name=name,
        grid=(layers,),
        in_specs=[pl.BlockSpec((1, a_dim, b_dim), lambda l: (l, 0, 0))],
        out_specs=pl.BlockSpec((1, b_dim, a_dim), lambda l: (l, 0, 0)),
        out_shape=jax.ShapeDtypeStruct((layers, b_dim, a_dim), BF16),
    )(w)


def _transpose_f32(g, name):
    a_dim, b_dim = g.shape

    def body(g_ref, o_ref):
        eye = (lax.broadcasted_iota(jnp.int32, (a_dim, a_dim), 0) == lax.broadcasted_iota(jnp.int32, (a_dim, a_dim), 1)).astype(BF16)
        v = g_ref[...]
        hi = v.astype(BF16)
        r1 = v - hi.astype(F32)
        mid = r1.astype(BF16)
        lo = (r1 - mid.astype(F32)).astype(BF16)
        out = lax.dot_general(hi, eye, TN, preferred_element_type=F32)
        out = out + lax.dot_general(mid, eye, TN, preferred_element_type=F32)
        out = out + lax.dot_general(lo, eye, TN, preferred_element_type=F32)
        o_ref[...] = out

    return pl.pallas_call(
        body,
        name=name,
        in_specs=[_full((a_dim, b_dim))],
        out_specs=_full((b_dim, a_dim)),
        out_shape=jax.ShapeDtypeStruct((b_dim, a_dim), F32),
        grid=(1,),
    )(g)


def _weight_scratch(locs, d):
    return [pltpu.VMEM((N_DEV * nr, d), BF16) for _, nr in locs] + [pltpu.SemaphoreType.DMA((len(locs), N_DEV))]


def _proj_fwd(h, wbuf, loc, name):
    rows, d = h.shape
    n = N_DEV * loc[1]
    tm = _row_tile(rows)

    def body(h_ref, w_hbm, o_ref, w, sems):
        _load_blocks(w_hbm, [(*loc, w)], sems)
        o_ref[...] = lax.dot_general(h_ref[...].astype(BF16), w[...], NT, preferred_element_type=F32)

    return pl.pallas_call(
        body,
        name=name,
        grid=(rows // tm,),
        in_specs=[pl.BlockSpec((tm, d), lambda i: (i, 0)), ANY],
        out_specs=pl.BlockSpec((tm, n), lambda i: (i, 0)),
        out_shape=jax.ShapeDtypeStruct((rows, n), F32),
        scratch_shapes=_weight_scratch([loc], d),
        compiler_params=_params(VMEM_BIG),
    )(h, wbuf)


def _proj_bwd(dproj, wbuf, loc, dz, alpha, name):
    rows, n = dproj.shape
    d = dz.shape[1]
    tm = _row_tile(rows)

    def body(dp_ref, w_hbm, dz_ref, o_ref, w, sems):
        _load_blocks(w_hbm, [(*loc, w)], sems)
        o_ref[...] = alpha * dz_ref[...] + jnp.dot(dp_ref[...], w[...], preferred_element_type=F32)

    return pl.pallas_call(
        body,
        name=name,
        grid=(rows // tm,),
        in_specs=[pl.BlockSpec((tm, n), lambda i: (i, 0)), ANY, pl.BlockSpec((tm, d), lambda i: (i, 0))],
        out_specs=pl.BlockSpec((tm, d), lambda i: (i, 0)),
        out_shape=jax.ShapeDtypeStruct((rows, d), F32),
        scratch_shapes=_weight_scratch([loc], d),
        compiler_params=_params(VMEM_BIG),
    )(dproj, wbuf, dz)


FFN_CHUNKS = 2


def _ffn_fwd(h, wbuf, locs, ln_g, ln_b, alpha, name):
    rows, d = h.shape
    f = N_DEV * locs[0][1]
    tm = _row_tile(rows)
    tf = f // FFN_CHUNKS

    def body(h_ref, w_hbm, lg_ref, lb_ref, ho_ref, g_ref, u_ref, z_ref, hb_ref, wg, wu, wdv, sems):
        _load_blocks(w_hbm, [(*locs[0], wg), (*locs[1], wu), (*locs[2], wdv)], sems)
        x = h_ref[...]
        xb = x.astype(BF16)
        y = None
        for c in range(FFN_CHUNKS):
            sl = slice(c * tf, (c + 1) * tf)
            g = lax.dot_general(xb, wg[sl, :], NT, preferred_element_type=F32)
            u = lax.dot_general(xb, wu[sl, :], NT, preferred_element_type=F32)
            g_ref[:, sl] = g
            u_ref[:, sl] = u
            a = (g * _sigmoid(g) * u).astype(BF16)
            part = jnp.dot(a, wdv[sl, :], preferred_element_type=F32)
            y = part if y is None else y + part
        z = alpha * x + 0.5 * y
        z_ref[...] = z
        xhat, _ = _ln_stats(z)
        ho = xhat * lg_ref[...] + lb_ref[...]
        ho_ref[...] = ho
        hb_ref[...] = ho.astype(BF16)

    row_d = pl.BlockSpec((tm, d), lambda i: (i, 0))
    row_f = pl.BlockSpec((tm, f), lambda i: (i, 0))
    return pl.pallas_call(
        body,
        name=name,
        grid=(rows // tm,),
        in_specs=[row_d, ANY, _full((1, d)), _full((1, d))],
        out_specs=[row_d, row_f, row_f, row_d, row_d],
        out_shape=[
            jax.ShapeDtypeStruct((rows, d), F32),
            jax.ShapeDtypeStruct((rows, f), F32),
            jax.ShapeDtypeStruct((rows, f), F32),
            jax.ShapeDtypeStruct((rows, d), F32),
            jax.ShapeDtypeStruct((rows, d), BF16),
        ],
        scratch_shapes=_weight_scratch(locs, d),
        compiler_params=_params(VMEM_BIG),
    )(h, wbuf, ln_g, ln_b)


def _ffn_bwd(dh, z, g, u, wbuf, locs, ln_g, after, alpha, name):
    rows, d = dh.shape
    f = N_DEV * locs[0][1]
    tm = _row_tile(rows)
    tf = f // FFN_CHUNKS

    def body(dh_ref, z_ref, g_ref, u_ref, w_hbm, lg_ref, after_ref, dx_ref, dg_ref, du_ref, a_ref, dyb_ref, dgam_ref, dbet_ref, wg, wu, wdv, sems):
        _load_blocks(w_hbm, [(*locs[0], wg), (*locs[1], wu), (*locs[2], wdv)], sems)

        @pl.when(pl.program_id(0) == 0)
        def _():
            dgam_ref[...] = jnp.zeros_like(dgam_ref)
            dbet_ref[...] = jnp.zeros_like(dbet_ref)

        dz, dgam, dbet = _ln_bwd(dh_ref[...], z_ref[...], lg_ref[...])
        dgam_ref[...] += dgam
        dbet_ref[...] += dbet
        dyb = (0.5 * dz).astype(BF16)
        dyb_ref[...] = dyb
        dx = alpha * dz
        for c in range(FFN_CHUNKS):
            sl = slice(c * tf, (c + 1) * tf)
            da = lax.dot_general(dyb, wdv[sl, :], NT, preferred_element_type=F32)
            gc = g_ref[:, sl]
            uc = u_ref[:, sl]
            sg = _sigmoid(gc)
            silu = gc * sg
            dgc = (da * uc * (sg * (1.0 + gc * (1.0 - sg)))).astype(BF16)
            duc = (da * silu).astype(BF16)
            a_ref[:, sl] = (silu * uc).astype(BF16)
            dg_ref[:, sl] = dgc
            du_ref[:, sl] = duc
            dx = dx + jnp.dot(dgc, wg[sl, :], preferred_element_type=F32) + jnp.dot(duc, wu[sl, :], preferred_element_type=F32)
        dx_ref[...] = dx

    row_d = pl.BlockSpec((tm, d), lambda i: (i, 0))
    row_f = pl.BlockSpec((tm, f), lambda i: (i, 0))
    return pl.pallas_call(
        body,
        name=name,
        grid=(rows // tm,),
        in_specs=[row_d, row_d, row_f, row_f, ANY, _full((1, d)), ANY],
        out_specs=[row_d, row_f, row_f, row_f, row_d, _full((1, d)), _full((1, d))],
        out_shape=[
            jax.ShapeDtypeStruct((rows, d), F32),
            jax.ShapeDtypeStruct((rows, f), BF16),
            jax.ShapeDtypeStruct((rows, f), BF16),
            jax.ShapeDtypeStruct((rows, f), BF16),
            jax.ShapeDtypeStruct((rows, d), BF16),
            jax.ShapeDtypeStruct((1, d), F32),
            jax.ShapeDtypeStruct((1, d), F32),
        ],
        scratch_shapes=_weight_scratch(locs, d),
        compiler_params=_params(VMEM_BIG),
    )(dh, z, g, u, wbuf, ln_g, after)


LANES = 128


def _gates(xc, wa_ref, ba, wx_ref, bx, lam):
    xcb = xc.astype(BF16)
    r = _sigmoid(jnp.dot(xcb, wa_ref[0], preferred_element_type=F32) + ba)
    i = _sigmoid(jnp.dot(xcb, wx_ref[0], preferred_element_type=F32) + bx)
    sp = _softplus(-lam)
    log_a = -LRU_C * sp * r
    a = jnp.exp(log_a)
    s = jnp.sqrt(-_expm1_nonpos(2.0 * log_a))
    return xcb, r, i, sp, a, s


def _conv(xr, cw, cb):
    return cb + cw[3:4] * xr + cw[2:3] * pltpu.roll(xr, 1, 0) + cw[1:2] * pltpu.roll(xr, 2, 0) + cw[0:1] * pltpu.roll(xr, 3, 0)


def _rglru_fwd(proj, cw, cb, wa, ba, wx, bx, lam, t_len, pad, name):
    rows = proj.shape[0]
    n_seq = rows // t_len
    n_tile = REC_WIDTH // LANES
    groups = t_len // 8

    def body(xr_ref, gt_ref, cw_ref, cb_ref, wa_ref, ba_ref, wx_ref, bx_ref, lam_ref, xc_ref, h_ref, y_ref, a_s, b_s):
        valid = lax.broadcasted_iota(jnp.int32, (t_len, 1), 0) >= pad
        xr = jnp.where(valid, xr_ref[...], 0.0)
        xc = _conv(xr, cw_ref[...], cb_ref[...])
        xc_ref[...] = xc
        _, _, i, _, a, s = _gates(xc, wa_ref, ba_ref[...], wx_ref, bx_ref[...], lam_ref[...])
        a_s[...] = a
        b_s[...] = jnp.where(valid, s * (i * xc), 0.0)
        sub = lax.broadcasted_iota(jnp.int32, (8, LANES), 0)

        def step(gi, carry):
            r0 = pl.multiple_of(gi * 8, 8)
            av = a_s[pl.ds(r0, 8), :]
            bv = b_s[pl.ds(r0, 8), :]
            for d in (1, 2, 4):
                m = sub >= d
                a_sh = pltpu.roll(av, d, 0)
                b_sh = pltpu.roll(bv, d, 0)
                bv = jnp.where(m, av * b_sh + bv, bv)
                av = jnp.where(m, av * a_sh, av)
            hv = av * carry + bv
            h_ref[pl.ds(r0, 8), :] = hv
            return jnp.broadcast_to(hv[7:8, :], (8, LANES))

        lax.fori_loop(0, groups, step, jnp.zeros((8, LANES), F32), unroll=4)
        gelu, _ = _gelu_and_grad(gt_ref[...])
        y_ref[...] = h_ref[...] * gelu

    seq_tile = lambda col0: pl.BlockSpec((t_len, LANES), lambda s, j: (s, col0 + j))
    vec = pl.BlockSpec((1, LANES), lambda s, j: (0, j))
    out = jax.ShapeDtypeStruct((rows, REC_WIDTH), F32)
    return pl.pallas_call(
        body,
        name=name,
        grid=(n_seq, n_tile),
        in_specs=[
            seq_tile(0),
            seq_tile(n_tile),
            pl.BlockSpec((CONV_WIDTH, LANES), lambda s, j: (0, j)),
            vec,
            pl.BlockSpec((1, LANES, LANES), lambda s, j: (j, 0, 0)),
            vec,
            pl.BlockSpec((1, LANES, LANES), lambda s, j: (j, 0, 0)),
            vec,
            vec,
        ],
        out_specs=[seq_tile(0)] * 3,
        out_shape=[out, out, out],
        scratch_shapes=[pltpu.VMEM((t_len, LANES), F32), pltpu.VMEM((t_len, LANES), F32)],
        compiler_params=_params(VMEM_BIG),
    )(proj, proj, cw, cb, wa, ba, wx, bx, lam)


def _rglru_bwd(dy, h, proj, xc, cw, wa, ba, wx, bx, lam, t_len, pad, name):
    rows = proj.shape[0]
    n_seq = rows // t_len
    n_tile = REC_WIDTH // LANES
    groups = t_len // 8

    def body(dy_ref, h_ref, xr_ref, gt_ref, xc_ref, cw_ref, wa_ref, ba_ref, wx_ref, bx_ref, lam_ref,
             dxr_ref, dgt_ref, dcw_ref, dcb_ref, dwa_ref, dba_ref, dwx_ref, dbx_ref, dlam_ref, a_s, c_s, db_s, da_s):
        first_seq = pl.program_id(1) == 0

        @pl.when(first_seq)
        def _():
            for ref in (dcw_ref, dcb_ref, dwa_ref, dba_ref, dwx_ref, dbx_ref, dlam_ref):
                ref[...] = jnp.zeros_like(ref)

        valid = lax.broadcasted_iota(jnp.int32, (t_len, 1), 0) >= pad
        gelu, dgelu = _gelu_and_grad(gt_ref[...])
        dyv = dy_ref[...]
        dho = dyv * gelu
        dgt_ref[...] = (dyv * h_ref[...] * dgelu).astype(BF16)
        xc = xc_ref[...]
        lam = lam_ref[...]
        xcb, r, i, sp, a, s = _gates(xc, wa_ref, ba_ref[...], wx_ref, bx_ref[...], lam)
        a_s[...] = a
        c_s[...] = a * dho
        db_s[...] = dho
        sub = lax.broadcasted_iota(jnp.int32, (8, LANES), 0)

        def step(k, carry):
            gi = groups - 1 - k
            r0 = pl.multiple_of(gi * 8, 8)
            av = a_s[pl.ds(r0, 8), :]
            cv = c_s[pl.ds(r0, 8), :]
            for d in (1, 2, 4):
                m = sub < 8 - d
                a_sh = pltpu.roll(av, 8 - d, 0)
                c_sh = pltpu.roll(cv, 8 - d, 0)
                cv = jnp.where(m, av * c_sh + cv, cv)
                av = jnp.where(m, av * a_sh, av)
            ev = av * carry + cv
            e_next = jnp.where(sub < 7, pltpu.roll(ev, 7, 0), carry)
            dht = db_s[pl.ds(r0, 8), :] + e_next
            hv = h_ref[pl.ds(r0, 8), :]
            rp = pl.multiple_of(jnp.maximum(gi - 1, 0) * 8, 8)
            h_before = jnp.where(gi > 0, jnp.broadcast_to(h_ref[pl.ds(rp, 8), :][7:8, :], (8, LANES)), 0.0)
            h_prev = jnp.where(sub >= 1, pltpu.roll(hv, 1, 0), h_before)
            db_s[pl.ds(r0, 8), :] = dht
            da_s[pl.ds(r0, 8), :] = dht * h_prev
            return jnp.broadcast_to(ev[0:1, :], (8, LANES))

        lax.fori_loop(0, groups, step, jnp.zeros((8, LANES), F32), unroll=4)

        db = jnp.where(valid, db_s[...], 0.0)
        da = da_s[...]
        ds = db * (i * xc)
        di = db * s * xc
        dxc = db * s * i
        dlog_a = da * a - ds * (a * a) / jnp.maximum(s, 1e-30)
        dr = dlog_a * (-LRU_C * sp)
        dsp = jnp.sum(dlog_a * (-LRU_C) * r, axis=0, keepdims=True)
        dlam_ref[...] += dsp * (-_sigmoid(-lam))
        dpr = dr * r * (1.0 - r)
        dpi = di * i * (1.0 - i)
        dprb = dpr.astype(BF16)
        dpib = dpi.astype(BF16)
        dxc = dxc + lax.dot_general(dprb, wa_ref[0], NT, preferred_element_type=F32) + lax.dot_general(dpib, wx_ref[0], NT, preferred_element_type=F32)
        dwa_ref[0] += lax.dot_general(xcb, dprb, TN, preferred_element_type=F32)
        dwx_ref[0] += lax.dot_general(xcb, dpib, TN, preferred_element_type=F32)
        dba_ref[...] += jnp.sum(dpr, axis=0, keepdims=True)
        dbx_ref[...] += jnp.sum(dpi, axis=0, keepdims=True)
        dxc = jnp.where(valid, dxc, 0.0)
        xr = jnp.where(valid, xr_ref[...], 0.0)
        dcb_ref[...] += jnp.sum(dxc, axis=0, keepdims=True)
        for k in range(CONV_WIDTH):
            shifted = xr if k == CONV_WIDTH - 1 else pltpu.roll(xr, CONV_WIDTH - 1 - k, 0)
            dcw_ref[k : k + 1, :] += jnp.sum(dxc * shifted, axis=0, keepdims=True)
        cw = cw_ref[...]
        dxr = cw[3:4] * dxc + cw[2:3] * pltpu.roll(dxc, t_len - 1, 0) + cw[1:2] * pltpu.roll(dxc, t_len - 2, 0) + cw[0:1] * pltpu.roll(dxc, t_len - 3, 0)
        dxr_ref[...] = jnp.where(valid, dxr, 0.0).astype(BF16)

    seq_tile = lambda col0: pl.BlockSpec((t_len, LANES), lambda j, s: (s, col0 + j))
    vec = pl.BlockSpec((1, LANES), lambda j, s: (0, j))
    mat = pl.BlockSpec((1, LANES, LANES), lambda j, s: (j, 0, 0))
    cwb = pl.BlockSpec((CONV_WIDTH, LANES), lambda j, s: (0, j))
    big = jax.ShapeDtypeStruct((rows, REC_WIDTH), BF16)
    vec_shape = jax.ShapeDtypeStruct((1, REC_WIDTH), F32)
    mat_shape = jax.ShapeDtypeStruct((n_tile, LANES, LANES), F32)
    return pl.pallas_call(
        body,
        name=name,
        grid=(n_tile, n_seq),
        in_specs=[seq_tile(0), seq_tile(0), seq_tile(0), seq_tile(n_tile), seq_tile(0), cwb, mat, vec, mat, vec, vec],
        out_specs=[seq_tile(0), seq_tile(0), cwb, vec, mat, vec, mat, vec, vec],
        out_shape=[big, big, jax.ShapeDtypeStruct((CONV_WIDTH, REC_WIDTH), F32), vec_shape, mat_shape, vec_shape, mat_shape, vec_shape, vec_shape],
        scratch_shapes=[pltpu.VMEM((t_len, LANES), F32)] * 4,
        compiler_params=_params(VMEM_BIG),
    )(dy, h, proj, proj, xc, cw, wa, ba, wx, bx, lam)


QKV_WIDTH = ATTN_WIDTH + 2 * KV_WIDTH


def _rotate(v, cos, sin_up, sin_down):
    width = v.shape[1]
    return v * cos + pltpu.roll(v, width - ROPE_DIM // 2, 1) * sin_up + pltpu.roll(v, ROPE_DIM // 2, 1) * sin_down


def _rope_rows(t_len):
    return t_len // 4 if t_len % 64 == 0 else BLOCK


def _rope_fwd(proj, cos, sin_up, sin_down, t_len, name):
    rows = proj.shape[0]
    rb = _rope_rows(t_len)
    nb = t_len // rb
    n_seq = rows // t_len
    q0 = 2 * REC_WIDTH

    def body(q_ref, k_ref, v_ref, cos_ref, up_ref, down_ref, o_ref):
        cos_t, up_t, down_t = cos_ref[...], up_ref[...], down_ref[...]
        o_ref[:, 0:ATTN_WIDTH] = _rotate(q_ref[...], cos_t, up_t, down_t).astype(BF16)
        o_ref[:, ATTN_WIDTH : ATTN_WIDTH + KV_WIDTH] = _rotate(k_ref[...], cos_t[:, :KV_WIDTH], up_t[:, :KV_WIDTH], down_t[:, :KV_WIDTH]).astype(BF16)
        o_ref[:, ATTN_WIDTH + KV_WIDTH :] = v_ref[...].astype(BF16)

    tab = pl.BlockSpec((rb, ATTN_WIDTH), lambda s, n: (n, 0))
    return pl.pallas_call(
        body,
        name=name,
        grid=(n_seq, nb),
        in_specs=[
            pl.BlockSpec((rb, ATTN_WIDTH), lambda s, n: (s * nb + n, q0 // ATTN_WIDTH)),
            pl.BlockSpec((rb, KV_WIDTH), lambda s, n: (s * nb + n, (q0 + ATTN_WIDTH) // KV_WIDTH)),
            pl.BlockSpec((rb, KV_WIDTH), lambda s, n: (s * nb + n, (q0 + ATTN_WIDTH) // KV_WIDTH + 1)),
            tab,
            tab,
            tab,
        ],
        out_specs=pl.BlockSpec((rb, QKV_WIDTH), lambda s, n: (s * nb + n, 0)),
        out_shape=jax.ShapeDtypeStruct((rows, QKV_WIDTH), BF16),
    )(proj, proj, proj, cos, sin_up, sin_down)


def _rope_bwd(dq, dk, dv, cos, sin_up, sin_down, t_len, name):
    rows = dq.shape[0]
    rb = _rope_rows(t_len)
    nb = t_len // rb
    n_seq = rows // t_len

    def body(dq_ref, dk_ref, dv_ref, cos_ref, up_ref, down_ref, o_ref):
        cos_t, up_t, down_t = cos_ref[...], -up_ref[...], -down_ref[...]
        o_ref[:, 0:ATTN_WIDTH] = _rotate(dq_ref[...], cos_t, up_t, down_t).astype(BF16)
        o_ref[:, ATTN_WIDTH : ATTN_WIDTH + KV_WIDTH] = _rotate(dk_ref[...], cos_t[:, :KV_WIDTH], up_t[:, :KV_WIDTH], down_t[:, :KV_WIDTH]).astype(BF16)
        o_ref[:, ATTN_WIDTH + KV_WIDTH :] = dv_ref[...].astype(BF16)

    tab = pl.BlockSpec((rb, ATTN_WIDTH), lambda s, n: (n, 0))
    blk = lambda w: pl.BlockSpec((rb, w), lambda s, n: (s * nb + n, 0))
    return pl.pallas_call(
        body,
        name=name,
        grid=(n_seq, nb),
        in_specs=[blk(ATTN_WIDTH), blk(KV_WIDTH), blk(KV_WIDTH), tab, tab, tab],
        out_specs=blk(QKV_WIDTH),
        out_shape=jax.ShapeDtypeStruct((rows, QKV_WIDTH), BF16),
    )(dq, dk, dv, cos, sin_up, sin_down)


GROUP = 4


def _attn_mask(n, pad):
    q_pos = n * BLOCK + (lax.broadcasted_iota(jnp.int32, (GROUP * BLOCK, 1), 0) & (BLOCK - 1)) - pad
    col = lax.broadcasted_iota(jnp.int32, (1, 3 * BLOCK), 1)
    first = col < BLOCK
    k_pos = col - pad + jnp.where(first, 0, (n - 2) * BLOCK)
    dist = q_pos - k_pos
    band = (dist >= 0) & (dist < WINDOW) & (k_pos >= N_META) & jnp.logical_not(first)
    meta = (k_pos >= 0) & (k_pos < N_META) & (k_pos <= q_pos) & first
    return band | meta


def _in_half(e):
    lane = lax.broadcasted_iota(jnp.int32, (1, BLOCK), 1)
    return lane >= HEAD_DIM if e else lane < HEAD_DIM


def _head_views(q2, e, g):
    v = jnp.where(_in_half(e), q2, 0.0)
    return pltpu.roll(v, HEAD_DIM, 1) if e != g else v


def _tile(g, t):
    j = (GROUP // 2) * g + t
    return slice(j * BLOCK, (j + 1) * BLOCK)


def _stack_rows(ref, g):
    parts = []
    for t in range(GROUP // 2):
        tile = ref[:, _tile(g, t)].astype(F32)
        parts += [_head_views(tile, e, g) for e in range(2)]
    return jnp.concatenate(parts, axis=0)


def _stack_group(q_ref, sink_ref, g):
    head = lax.broadcasted_iota(jnp.int32, (GROUP * BLOCK, 1), 0) >> (BLOCK.bit_length() - 1)
    sink = jnp.zeros((GROUP * BLOCK, 1), F32)
    for hh in range(GROUP):
        sink = jnp.where(head == hh, sink_ref[GROUP * g + hh : GROUP * g + hh + 1, 0:1], sink)
    return _stack_rows(q_ref, g), sink


def _unstack_pair(v, g, t):
    out = None
    for e in range(2):
        blk = v[(2 * t + e) * BLOCK : (2 * t + e + 1) * BLOCK, :]
        blk = pltpu.roll(blk, HEAD_DIM, 1) if e != g else blk
        out = blk if out is None else out + blk
    return out


def _softmax_with_sink(scores, mask, sink):
    s = jnp.where(mask, scores * (HEAD_DIM**-0.5), NEG_INF)
    m = jnp.maximum(jnp.max(s, axis=-1, keepdims=True), sink)
    p = jnp.exp(s - m)
    p_sink = jnp.exp(sink - m)
    inv = 1.0 / (jnp.sum(p, axis=-1, keepdims=True) + p_sink)
    return p * inv, p_sink * inv


def _kv_specs(nb):
    k_col = ATTN_WIDTH // KV_WIDTH
    specs = []
    for col in (k_col, k_col + 1):
        specs += [
            pl.BlockSpec((BLOCK, KV_WIDTH), lambda s, n, col=col: (s * nb, col)),
            pl.BlockSpec((BLOCK, KV_WIDTH), lambda s, n, col=col: (s * nb + jnp.maximum(n - 1, 0), col)),
            pl.BlockSpec((BLOCK, KV_WIDTH), lambda s, n, col=col: (s * nb + n, col)),
        ]
    return specs


def _attn_fwd(qkv, sinks, t_len, pad, name):
    rows = qkv.shape[0]
    nb = t_len // BLOCK
    n_seq = rows // t_len

    def body(q_ref, km_ref, kp_ref, kc_ref, vm_ref, vp_ref, vc_ref, sink_ref, o_ref):
        n = pl.program_id(1)
        mask = _attn_mask(n, pad)
        keys = jnp.concatenate([km_ref[...], kp_ref[...], kc_ref[...]], axis=0)
        vals = jnp.concatenate([vm_ref[...], vp_ref[...], vc_ref[...]], axis=0)
        for g in range(N_Q_HEADS // GROUP):
            qs, sink = _stack_group(q_ref, sink_ref, g)
            scores = lax.dot_general(qs.astype(BF16), keys, NT, preferred_element_type=F32)
            p, _ = _softmax_with_sink(scores, mask, sink)
            o = jnp.where(_in_half(g), jnp.dot(p.astype(BF16), vals, preferred_element_type=F32), 0.0)
            for t in range(GROUP // 2):
                o_ref[:, _tile(g, t)] = _unstack_pair(o, g, t)

    return pl.pallas_call(
        body,
        name=name,
        grid=(n_seq, nb),
        in_specs=[pl.BlockSpec((BLOCK, ATTN_WIDTH), lambda s, n: (s * nb + n, 0))] + _kv_specs(nb) + [_full((N_Q_HEADS, BLOCK))],
        out_specs=pl.BlockSpec((BLOCK, ATTN_WIDTH), lambda s, n: (s * nb + n, 0)),
        out_shape=jax.ShapeDtypeStruct((rows, ATTN_WIDTH), F32),
    )(qkv, qkv, qkv, qkv, qkv, qkv, qkv, sinks)


def _attn_bwd(qkv, sinks, o, do, t_len, pad, name):
    rows = qkv.shape[0]
    nb = t_len // BLOCK
    n_seq = rows // t_len

    def body(q_ref, km_ref, kp_ref, kc_ref, vm_ref, vp_ref, vc_ref, sink_ref, o_ref, do_ref, dq_ref, dk_ref, dv_ref, dsink_ref):
        s_id, n = pl.program_id(0), pl.program_id(1)

        @pl.when(n == 0)
        def _():
            dk_ref[...] = jnp.zeros_like(dk_ref)
            dv_ref[...] = jnp.zeros_like(dv_ref)

        @pl.when((n == 0) & (s_id == 0))
        def _():
            dsink_ref[...] = jnp.zeros_like(dsink_ref)

        mask = _attn_mask(n, pad)
        keys = jnp.concatenate([km_ref[...], kp_ref[...], kc_ref[...]], axis=0)
        vals = jnp.concatenate([vm_ref[...], vp_ref[...], vc_ref[...]], axis=0)
        dkeys = jnp.zeros((3 * BLOCK, KV_WIDTH), F32)
        dvals = jnp.zeros((3 * BLOCK, KV_WIDTH), F32)
        for g in range(N_Q_HEADS // GROUP):
            qs, sink = _stack_group(q_ref, sink_ref, g)
            qsb = qs.astype(BF16)
            scores = lax.dot_general(qsb, keys, NT, preferred_element_type=F32)
            p, p_sink = _softmax_with_sink(scores, mask, sink)
            dos = _stack_rows(do_ref, g)
            delta = jnp.sum(dos * _stack_rows(o_ref, g), axis=-1, keepdims=True)
            dosb = dos.astype(BF16)
            dp = lax.dot_general(dosb, vals, NT, preferred_element_type=F32)
            ds = (p * (dp - delta) * (HEAD_DIM**-0.5)).astype(BF16)
            dqs = jnp.where(_in_half(g), jnp.dot(ds, keys, preferred_element_type=F32), 0.0)
            for t in range(GROUP // 2):
                dq_ref[:, _tile(g, t)] = _unstack_pair(dqs, g, t)
            dkeys = dkeys + lax.dot_general(ds, qsb, TN, preferred_element_type=F32)
            dvals = dvals + lax.dot_general(p.astype(BF16), dosb, TN, preferred_element_type=F32)
            sink_term = p_sink * delta
            for hh in range(GROUP):
                head = GROUP * g + hh
                part = -jnp.sum(sink_term[hh * BLOCK : (hh + 1) * BLOCK, :], axis=0, keepdims=True)
                dsink_ref[head : head + 1, :] += jnp.broadcast_to(part, (1, BLOCK))
        r_prev = pl.multiple_of(jnp.maximum(n - 1, 0) * BLOCK, BLOCK)
        r_cur = pl.multiple_of(n * BLOCK, BLOCK)
        for acc, d in ((dk_ref, dkeys), (dv_ref, dvals)):
            acc[0:BLOCK, :] += d[0:BLOCK]
            acc[pl.ds(r_prev, BLOCK), :] += d[BLOCK : 2 * BLOCK]
            acc[pl.ds(r_cur, BLOCK), :] += d[2 * BLOCK :]

    q_blk = pl.BlockSpec((BLOCK, ATTN_WIDTH), lambda s, n: (s * nb + n, 0))
    seq_kv = pl.BlockSpec((t_len, KV_WIDTH), lambda s, n: (s, 0))
    return pl.pallas_call(
        body,
        name=name,
        grid=(n_seq, nb),
        in_specs=[q_blk] + _kv_specs(nb) + [_full((N_Q_HEADS, BLOCK)), q_blk, q_blk],
        out_specs=[q_blk, seq_kv, seq_kv, _full((N_Q_HEADS, BLOCK))],
        out_shape=[
            jax.ShapeDtypeStruct((rows, ATTN_WIDTH), F32),
            jax.ShapeDtypeStruct((rows, KV_WIDTH), F32),
            jax.ShapeDtypeStruct((rows, KV_WIDTH), F32),
            jax.ShapeDtypeStruct((N_Q_HEADS, BLOCK), F32),
        ],
    )(qkv, qkv, qkv, qkv, qkv, qkv, qkv, sinks, o, do)


def _mix_out_fwd(y_rec, y_attn, h, wbuf, loc, g_rec, g_attn, ln_g, ln_b, alpha, name):
    rows, d = h.shape
    tm = _row_tile(rows)

    def body(yr_ref, ya_ref, h_ref, w_hbm, gr_ref, ga_ref, lg_ref, lb_ref, ho_ref, z_ref, hb_ref, w_ref, sems):
        _load_blocks(w_hbm, [(*loc, w_ref)], sems)
        nr = _rms(yr_ref[...], gr_ref[...]).astype(BF16)
        na = _rms(ya_ref[...], ga_ref[...]).astype(BF16)
        m = jnp.dot(nr, w_ref[0:REC_WIDTH, :], preferred_element_type=F32) + jnp.dot(na, w_ref[REC_WIDTH:, :], preferred_element_type=F32)
        z = alpha * h_ref[...] + m
        z_ref[...] = z
        xhat, _ = _ln_stats(z)
        ho = xhat * lg_ref[...] + lb_ref[...]
        ho_ref[...] = ho
        hb_ref[...] = ho.astype(BF16)

    row_d = pl.BlockSpec((tm, d), lambda i: (i, 0))
    row_h = pl.BlockSpec((tm, REC_WIDTH), lambda i: (i, 0))
    return pl.pallas_call(
        body,
        name=name,
        grid=(rows // tm,),
        in_specs=[row_h, row_h, row_d, ANY, _full((1, REC_WIDTH)), _full((1, ATTN_WIDTH)), _full((1, d)), _full((1, d))],
        out_specs=[row_d, row_d, row_d],
        out_shape=[jax.ShapeDtypeStruct((rows, d), F32)] * 2 + [jax.ShapeDtypeStruct((rows, d), BF16)],
        scratch_shapes=_weight_scratch([loc], d),
    )(y_rec, y_attn, h, wbuf, g_rec, g_attn, ln_g, ln_b)


def _mix_out_bwd(dh, z, y_rec, y_attn, wbuf, loc, g_rec, g_attn, ln_g, name):
    rows, d = dh.shape
    tm = _row_tile(rows)
    mix = REC_WIDTH + ATTN_WIDTH

    def body(dh_ref, z_ref, yr_ref, ya_ref, w_hbm, gr_ref, ga_ref, lg_ref, dz_ref, dzb_ref, dyr_ref, dya_ref, yn_ref, dgam_ref, dbet_ref, dgr_ref, dga_ref, w_ref, sems):
        _load_blocks(w_hbm, [(*loc, w_ref)], sems)

        @pl.when(pl.program_id(0) == 0)
        def _():
            for ref in (dgam_ref, dbet_ref, dgr_ref, dga_ref):
                ref[...] = jnp.zeros_like(ref)

        dz, dgam, dbet = _ln_bwd(dh_ref[...], z_ref[...], lg_ref[...])
        dgam_ref[...] += dgam
        dbet_ref[...] += dbet
        dz_ref[...] = dz
        dzb = dz.astype(BF16)
        dzb_ref[...] = dzb
        dyn = lax.dot_general(dzb, w_ref[...], NT, preferred_element_type=F32)
        dyr, dgr, nr = _rms_bwd(dyn[:, 0:REC_WIDTH], yr_ref[...], gr_ref[...])
        dya, dga, na = _rms_bwd(dyn[:, REC_WIDTH:], ya_ref[...], ga_ref[...])
        dyr_ref[...] = dyr
        dya_ref[...] = dya
        dgr_ref[...] += dgr
        dga_ref[...] += dga
        yn_ref[:, 0:REC_WIDTH] = nr.astype(BF16)
        yn_ref[:, REC_WIDTH:] = na.astype(BF16)

    row_d = pl.BlockSpec((tm, d), lambda i: (i, 0))
    row_h = pl.BlockSpec((tm, REC_WIDTH), lambda i: (i, 0))
    row_m = pl.BlockSpec((tm, mix), lambda i: (i, 0))
    return pl.pallas_call(
        body,
        name=name,
        grid=(rows // tm,),
        in_specs=[row_d, row_d, row_h, row_h, ANY, _full((1, REC_WIDTH)), _full((1, ATTN_WIDTH)), _full((1, d))],
        out_specs=[row_d, row_d, row_h, row_h, row_m, _full((1, d)), _full((1, d)), _full((1, REC_WIDTH)), _full((1, ATTN_WIDTH))],
        out_shape=[
            jax.ShapeDtypeStruct((rows, d), F32),
            jax.ShapeDtypeStruct((rows, d), BF16),
            jax.ShapeDtypeStruct((rows, REC_WIDTH), F32),
            jax.ShapeDtypeStruct((rows, ATTN_WIDTH), F32),
            jax.ShapeDtypeStruct((rows, mix), BF16),
            jax.ShapeDtypeStruct((1, d), F32),
            jax.ShapeDtypeStruct((1, d), F32),
            jax.ShapeDtypeStruct((1, REC_WIDTH), F32),
            jax.ShapeDtypeStruct((1, ATTN_WIDTH), F32),
        ],
        scratch_shapes=_weight_scratch([loc], d),
    )(dh, z, y_rec, y_attn, wbuf, g_rec, g_attn, ln_g)


def _loss_head(y, target, t_len, first_token, name):
    rows, d = y.shape
    tm = _row_tile(rows)

    def body(y_ref, t_ref, loss_ref, dy_ref):
        i = pl.program_id(0)

        @pl.when(i == 0)
        def _():
            loss_ref[...] = jnp.zeros_like(loss_ref)

        row = i * tm + lax.broadcasted_iota(jnp.int32, (tm, 1), 0)
        is_token = lax.rem(row, t_len) >= first_token
        err = jnp.where(is_token, y_ref[...] - t_ref[...], 0.0)
        dy_ref[...] = err / d
        per_row = jnp.sum(err * err, axis=-1, keepdims=True) / d
        loss_ref[...] += jnp.broadcast_to(0.5 * jnp.sum(per_row, axis=0, keepdims=True), (1, BLOCK))

    row_d = pl.BlockSpec((tm, d), lambda i: (i, 0))
    return pl.pallas_call(
        body,
        name=name,
        grid=(rows // tm,),
        in_specs=[row_d, row_d],
        out_specs=[_full((1, BLOCK)), row_d],
        out_shape=[jax.ShapeDtypeStruct((1, BLOCK), F32), jax.ShapeDtypeStruct((rows, d), F32)],
    )(y, target)


def _meta_grad(dh0, t_len, pad, name):
    rows, d = dh0.shape
    nb = t_len // BLOCK
    n_seq = rows // t_len

    def body(dh_ref, o_ref):
        @pl.when(pl.program_id(0) == 0)
        def _():
            o_ref[...] = jnp.zeros_like(o_ref)

        o_ref[...] += dh_ref[pad : pad + N_META, :]

    return pl.pallas_call(
        body,
        name=name,
        grid=(n_seq,),
        in_specs=[pl.BlockSpec((BLOCK, d), lambda s: (s * nb, 0))],
        out_specs=_full((N_META, d)),
        out_shape=jax.ShapeDtypeStruct((N_META, d), F32),
    )(dh0)


def _adamw(w, g, m, v, name):
    shape = w.shape
    cols = shape[-1]
    rows = w.size // cols
    tr = rows
    for cand in (512, 256, 128, 64):
        if rows % cand == 0 and rows > cand:
            tr = cand
            break

    def body(w_ref, g_ref, m_ref, v_ref, d_ref, nm_ref, nv_ref):
        gv = g_ref[...]
        nm = ADAM_B1 * m_ref[...] + (1.0 - ADAM_B1) * gv
        nv = ADAM_B2 * v_ref[...] + (1.0 - ADAM_B2) * (gv * gv)
        m_hat = nm / (1.0 - ADAM_B1**ADAM_STEP)
        v_hat = nv / (1.0 - ADAM_B2**ADAM_STEP)
        d_ref[...] = -ADAM_LR * (m_hat / (jnp.sqrt(v_hat) + ADAM_EPS) + ADAM_WD * w_ref[...])
        nm_ref[...] = nm
        nv_ref[...] = nv

    blk = pl.BlockSpec((tr, cols), lambda i: (i, 0))
    flat = [a.reshape(rows, cols) for a in (w, g, m, v)]
    outs = pl.pallas_call(
        body,
        name=name,
        grid=(rows // tr,),
        in_specs=[blk] * 4,
        out_specs=[blk] * 3,
        out_shape=[jax.ShapeDtypeStruct((rows, cols), F32)] * 3,
    )(*flat)
    return [o.reshape(shape) for o in outs]


WEIGHTS = ["meta_tokens", "ffn1_w_gate", "ffn1_w_up", "ffn1_w_down", "ln1_g", "ln1_b", "w_in", "conv_w", "conv_b", "gate_a_w", "gate_a_b",
           "gate_x_w", "gate_x_b", "lru_lambda", "attn_sinks", "norm_rec_g", "norm_attn_g", "w_out", "ln2_g", "ln2_b", "ffn2_w_gate",
           "ffn2_w_up", "ffn2_w_down", "ln3_g", "ln3_b"]
BIG = [("ffn1_w_gate", True), ("ffn1_w_up", True), ("ffn1_w_down", False), ("w_in", True), ("w_out", False),
       ("ffn2_w_gate", True), ("ffn2_w_up", True), ("ffn2_w_down", False)]


GATHER_GROUPS = [["ffn1_w_gate", "ffn1_w_up", "ffn1_w_down"], ["w_in", "w_out"], ["ffn2_w_gate", "ffn2_w_up", "ffn2_w_down"]]
EXCHANGE_GROUPS = [["ffn2_w_gate", "ffn2_w_up", "ffn2_w_down", "w_out"], ["w_in", "ffn1_w_gate", "ffn1_w_up", "ffn1_w_down"]]


def _rope_tables(t_len, pad):
    pos = (jnp.arange(t_len) - pad).astype(F32)
    inv_freq = ROPE_THETA ** (-jnp.arange(0, ROPE_DIM, 2, dtype=F32) / ROPE_DIM)
    ang = pos[:, None] * inv_freq[None, :]
    cos, sin = jnp.cos(ang), jnp.sin(ang)
    half = ROPE_DIM // 2
    rest = jnp.zeros((t_len, HEAD_DIM - ROPE_DIM), F32)
    zero = jnp.zeros((t_len, half), F32)
    cos_h = jnp.concatenate([cos, cos, rest + 1.0], axis=1)
    up_h = jnp.concatenate([-sin, zero, rest], axis=1)
    down_h = jnp.concatenate([zero, sin, rest], axis=1)
    return [jnp.tile(t, (1, ATTN_WIDTH // HEAD_DIM)) for t in (cos_h, up_h, down_h)]


def _gate_tiles(w):
    z = jnp.zeros((HEAD_DIM, HEAD_DIM), w.dtype)
    tiles = [jnp.block([[w[2 * j], z], [z, w[2 * j + 1]]]) for j in range(w.shape[0] // 2)]
    return jnp.stack(tiles).astype(BF16)


def _gate_blocks(tiles):
    out = []
    for j in range(tiles.shape[0]):
        out += [tiles[j, :HEAD_DIM, :HEAD_DIM], tiles[j, HEAD_DIM:, HEAD_DIM:]]
    return jnp.stack(out)


def kernel(x, meta_tokens, ffn1_w_gate, ffn1_w_up, ffn1_w_down, ln1_g, ln1_b, w_in, conv_w, conv_b, gate_a_w, gate_a_b, gate_x_w, gate_x_b, lru_lambda, attn_sinks, norm_rec_g, norm_attn_g, w_out, ln2_g, ln2_b, ffn2_w_gate, ffn2_w_up, ffn2_w_down, ln3_g, ln3_b, loss_target, m_meta_tokens, m_ffn1_w_gate, m_ffn1_w_up, m_ffn1_w_down, m_ln1_g, m_ln1_b, m_w_in, m_conv_w, m_conv_b, m_gate_a_w, m_gate_a_b, m_gate_x_w, m_gate_x_b, m_lru_lambda, m_attn_sinks, m_norm_rec_g, m_norm_attn_g, m_w_out, m_ln2_g, m_ln2_b, m_ffn2_w_gate, m_ffn2_w_up, m_ffn2_w_down, m_ln3_g, m_ln3_b, v_meta_tokens, v_ffn1_w_gate, v_ffn1_w_up, v_ffn1_w_down, v_ln1_g, v_ln1_b, v_w_in, v_conv_w, v_conv_b, v_gate_a_w, v_gate_a_b, v_gate_x_w, v_gate_x_b, v_lru_lambda, v_attn_sinks, v_norm_rec_g, v_norm_attn_g, v_w_out, v_ln2_g, v_ln2_b, v_ffn2_w_gate, v_ffn2_w_up, v_ffn2_w_down, v_ln3_g, v_ln3_b):
    given = dict(locals())
    w = {k: given[k] for k in WEIGHTS}
    n_seq, seq, d = x.shape
    depth = ln1_g.shape[0]
    alpha = (2.0 * depth) ** 0.25
    pad = (-(N_META + seq)) % BLOCK
    t_len = pad + N_META + seq
    rows = n_seq * t_len
    me = 4 * lax.axis_index("x") + 2 * lax.axis_index("y") + lax.axis_index("c")

    transposed = dict(BIG)
    sent, loc, gathers, chain = {}, {}, {}, meta_tokens

    def start_gather(l, k):
        shard = jnp.concatenate([sent[name][l] for name in GATHER_GROUPS[k]], axis=0)
        zone = lax.dynamic_update_slice(lax.empty((N_DEV,) + shard.shape, BF16), shard[None], (me, 0, 0))
        gathers[(l, k)] = _push_start([shard], [zone], chain, False, f"gather_start_{l}_{k}")
        return gathers[(l, k)][2][0]

    for k, group in enumerate(GATHER_GROUPS):
        off = 0
        for name in group:
            sent[name] = _transpose_to_bf16(w[name], "t_" + name) if transposed[name] else w[name].astype(BF16)
            loc[name] = (off, sent[name].shape[1])
            off += sent[name].shape[1]
        chain = start_gather(0, k)
    for l in range(1, depth):
        for k in range(len(GATHER_GROUPS)):
            chain = start_gather(l, k)
    ffn1_loc = [loc["ffn1_w_gate"], loc["ffn1_w_up"], loc["ffn1_w_down"]]
    ffn2_loc = [loc["ffn2_w_gate"], loc["ffn2_w_up"], loc["ffn2_w_down"]]
    small = jnp.concatenate([meta_tokens, conv_w.reshape(-1, BLOCK)], axis=0)
    (small_all,) = _all_gather([small], small, "gather_small")
    meta_full = small_all[:, :N_META, :].transpose(1, 0, 2).reshape(N_META, d)
    conv_shard = conv_w.shape[-1]
    conv_full = small_all[:, N_META:, :].reshape(N_DEV, depth, CONV_WIDTH, conv_shard).transpose(1, 2, 0, 3).reshape(depth, CONV_WIDTH, REC_WIDTH)

    cos, sin_up, sin_down = _rope_tables(t_len, pad)
    row1 = lambda a: a.reshape(1, -1)

    h = jnp.concatenate([jnp.zeros((n_seq, pad, d), F32), jnp.broadcast_to(meta_full[None], (n_seq, N_META, d)), x], axis=1).reshape(rows, d)
    target = jnp.pad(loss_target, ((0, 0), (pad + N_META, 0), (0, 0))).reshape(rows, d)
    hb = h.astype(BF16)
    saved, wbufs = [], []

    def gathered(l, k, after):
        _, (buf,) = _push_wait(*gathers[(l, k)], after, False, f"gather_wait_{l}_{k}")
        return buf

    for l in range(depth):
        w_ffn1 = gathered(l, 0, chain if l == 0 else h)
        s = {"h0": hb}
        h, s["g1"], s["u1"], s["z1"], hb = _ffn_fwd(h, w_ffn1, ffn1_loc, row1(ln1_g[l]), row1(ln1_b[l]), alpha, "ffn1_fwd")
        s["h1"] = hb
        s["wa"], s["wx"] = _gate_tiles(gate_a_w[l]), _gate_tiles(gate_x_w[l])
        s["sinks"] = jnp.broadcast_to(attn_sinks[l][:, None], (N_Q_HEADS, BLOCK))
        wbuf = gathered(l, 1, h)
        proj = _proj_fwd(h, wbuf, loc["w_in"], "proj_fwd")
        s["proj"] = proj
        s["xc"], s["hrec"], s["y_rec"] = _rglru_fwd(proj, conv_full[l], row1(conv_b[l]), s["wa"], row1(gate_a_b[l]), s["wx"], row1(gate_x_b[l]),
                                                     row1(lru_lambda[l]), t_len, pad, "rglru_fwd")
        s["qkv"] = _rope_fwd(proj, cos, sin_up, sin_down, t_len, "rope_fwd")
        s["y_attn"] = _attn_fwd(s["qkv"], s["sinks"], t_len, pad, "attn_fwd")
        h, s["z2"], hb = _mix_out_fwd(s["y_rec"], s["y_attn"], h, wbuf, loc["w_out"], row1(norm_rec_g[l]), row1(norm_attn_g[l]), row1(ln2_g[l]), row1(ln2_b[l]), alpha, "mix_out_fwd")
        s["h2"] = hb
        w_ffn2 = gathered(l, 2, h)
        h, s["g2"], s["u2"], s["z3"], hb = _ffn_fwd(h, w_ffn2, ffn2_loc, row1(ln3_g[l]), row1(ln3_b[l]), alpha, "ffn2_fwd")
        saved.append(s)
        wbufs.append((w_ffn1, wbuf, w_ffn2))
    loss_part, dh = _loss_head(h, target, t_len, pad + N_META, "loss_head")

    exchanges = {}
    small_grads = [None] * depth
    behind = dh

    def exchange(l, k, bg):
        gs = [bg[name].reshape(N_DEV, -1, d) for name in EXCHANGE_GROUPS[k]]
        zones = [lax.empty(g.shape, g.dtype) for g in gs]
        exchanges[(l, k)] = _push_start(gs, zones, gs[0], True, f"exchange_start_{l}_{k}")
        return exchanges[(l, k)][2][0]

    for l in reversed(range(depth)):
        (w_ffn1, wbuf, w_ffn2), s = wbufs[l], saved[l]
        bg, sg = {}, {}
        dh, dg, du, act, dyb, sg["ln3_g"], sg["ln3_b"] = _ffn_bwd(dh, s["z3"], s["g2"], s["u2"], w_ffn2, ffn2_loc, row1(ln3_g[l]), behind, alpha, "ffn2_bwd")
        bg["ffn2_w_gate"] = _mm_tn(dg, s["h2"], BF16, "ffn_wgrad_in")
        bg["ffn2_w_up"] = _mm_tn(du, s["h2"], BF16, "ffn_wgrad_in")
        bg["ffn2_w_down"] = _mm_tn(act, dyb, BF16, "ffn_wgrad_down")
        dz, dzb, dy_rec, dy_attn, yn, sg["ln2_g"], sg["ln2_b"], sg["norm_rec_g"], sg["norm_attn_g"] = _mix_out_bwd(
            dh, s["z2"], s["y_rec"], s["y_attn"], wbuf, loc["w_out"], row1(norm_rec_g[l]), row1(norm_attn_g[l]), row1(ln2_g[l]), "mix_out_bwd")
        bg["w_out"] = _mm_tn(yn, dzb, BF16, "w_out_wgrad")
        behind = exchange(l, 0, bg)
        dxr, dgt, sg["conv_w"], sg["conv_b"], dwa, sg["gate_a_b"], dwx, sg["gate_x_b"], sg["lru_lambda"] = _rglru_bwd(
            dy_rec, s["hrec"], s["proj"], s["xc"], conv_full[l], s["wa"], row1(gate_a_b[l]), s["wx"], row1(gate_x_b[l]), row1(lru_lambda[l]), t_len, pad, "rglru_bwd")
        sg["gate_a_w"], sg["gate_x_w"] = _gate_blocks(dwa), _gate_blocks(dwx)
        dq, dk, dv, dsink = _attn_bwd(s["qkv"], s["sinks"], s["y_attn"], dy_attn, t_len, pad, "attn_bwd")
        sg["attn_sinks"] = dsink[:, 0]
        dqkv = _rope_bwd(dq, dk, dv, cos, sin_up, sin_down, t_len, "rope_bwd")
        dproj = jnp.concatenate([dxr, dgt, dqkv], axis=1)
        bg["w_in"] = _mm_tn(dproj, s["h1"], BF16, "w_in_wgrad")
        dh = _proj_bwd(dproj, wbuf, loc["w_in"], dz, alpha, "proj_bwd")
        dh, dg, du, act, dyb, sg["ln1_g"], sg["ln1_b"] = _ffn_bwd(dh, s["z1"], s["g1"], s["u1"], w_ffn1, ffn1_loc, row1(ln1_g[l]), behind, alpha, "ffn1_bwd")
        bg["ffn1_w_gate"] = _mm_tn(dg, s["h0"], BF16, "ffn_wgrad_in")
        bg["ffn1_w_up"] = _mm_tn(du, s["h0"], BF16, "ffn_wgrad_in")
        bg["ffn1_w_down"] = _mm_tn(act, dyb, BF16, "ffn_wgrad_down")
        behind = exchange(l, 1, bg)
        small_grads[l] = sg
    grad_x = dh.reshape(n_seq, t_len, d)[:, pad + N_META :, :]
    dmeta = _meta_grad(dh, t_len, pad, "meta_grad")

    grads = {}
    small_names = ["ln1_g", "ln1_b", "ln2_g", "ln2_b", "ln3_g", "ln3_b", "conv_b", "gate_a_b", "gate_x_b", "lru_lambda", "norm_rec_g",
                   "norm_attn_g", "conv_w", "gate_a_w", "gate_x_w", "attn_sinks"]
    pieces = [small_grads[l][name].reshape(-1) for l in range(depth) for name in small_names] + [dmeta.reshape(-1), loss_part[0, :1]]
    sizes = [p.shape[0] for p in pieces]
    flat = jnp.concatenate(pieces)
    width = 1024
    n_rows = -(-flat.shape[0] // (8 * width)) * 8
    flat = jnp.pad(flat, (0, n_rows * width - flat.shape[0])).reshape(n_rows, width)
    zone = lax.dynamic_update_slice(lax.empty((N_DEV, n_rows, width), F32), flat[None], (me, 0, 0))
    small_gather = _push_start([flat], [zone], behind, False, "small_grads_start")

    reduced = {}
    transposed = dict(BIG)

    def reduce_group(l, k, after):
        send_sems, recv_sems, srcs, zones = exchanges[(l, k)]
        owns, parts = _push_wait(send_sems, recv_sems, srcs, zones, after, True, f"exchange_wait_{l}_{k}")
        for name, own, p in zip(EXCHANGE_GROUPS[k], owns, parts):
            reduced[(l, name)] = _sum_exchanged(p, own, "sum_grads")
            if transposed[name]:
                reduced[(l, name)] = _transpose_f32(reduced[(l, name)], "t_grad")
        return reduced[(l, EXCHANGE_GROUPS[k][-1])]

    order = [(l, k) for l in reversed(range(depth)) for k in range(len(EXCHANGE_GROUPS))]
    after = small_gather[2][0]
    for l, k in order[:-1]:
        after = reduce_group(l, k, after)
    _, (flat_all,) = _push_wait(*small_gather, after, False, "small_grads_wait")
    total = _sum8(flat_all, "sum_small_grads").reshape(-1)
    offs = [0]
    for sz in sizes:
        offs.append(offs[-1] + sz)
    taken = [total[offs[k] : offs[k + 1]] for k in range(len(sizes))]
    for j, name in enumerate(small_names):
        full_shape = (depth,) + ((CONV_WIDTH, REC_WIDTH) if name == "conv_w" else w[name].shape[1:])
        grads[name] = jnp.stack([taken[l * len(small_names) + j] for l in range(depth)]).reshape(full_shape)
    grads["conv_w"] = lax.dynamic_slice_in_dim(grads["conv_w"], me * conv_shard, conv_shard, axis=2)
    meta_shard = meta_tokens.shape[1]
    grads["meta_tokens"] = lax.dynamic_slice_in_dim(taken[-2].reshape(N_META, d), me * meta_shard, meta_shard, axis=1)
    loss = taken[-1][0]
    deltas, new_m, new_v = {}, {}, {}
    for name in WEIGHTS:
        if name not in dict(BIG):
            deltas[name], new_m[name], new_v[name] = _adamw(w[name], grads[name], given["m_" + name], given["v_" + name], "adamw")

    reduce_group(*order[-1], new_v[WEIGHTS[-1]])
    for name, _ in BIG:
        grads[name] = jnp.stack([reduced[(l, name)] for l in range(depth)])
        deltas[name], new_m[name], new_v[name] = _adamw(w[name], grads[name], given["m_" + name], given["v_" + name], "adamw")
    return (loss, grad_x, *[grads[k] for k in WEIGHTS], *[deltas[k] for k in WEIGHTS], *[new_m[k] for k in WEIGHTS], *[new_v[k] for k in WEIGHTS])
```

```python
import functools

import jax
import jax.numpy as jnp
from jax import lax
from jax.experimental import pallas as pl
from jax.experimental.pallas import tpu as pltpu

F32 = jnp.float32
BF16 = jnp.bfloat16
MESH = pl.DeviceIdType.MESH

N_DEV = 8
N_META = 16
BLOCK = 128
WINDOW = 128
REC_WIDTH = 512
ATTN_WIDTH = 512
KV_WIDTH = 128
HEAD_DIM = 64
N_Q_HEADS = 8
ROPE_DIM = 16
ROPE_THETA = 500000.0
CONV_WIDTH = 4
LRU_C = 8.0
LN_EPS = 1e-5
RMS_EPS = 1e-6
NEG_INF = -1e30
ADAM_LR, ADAM_B1, ADAM_B2, ADAM_EPS, ADAM_WD, ADAM_STEP = 0.001, 0.9, 0.999, 1e-08, 0.01, 10

NT = (((1,), (1,)), ((), ()))
TN = (((0,), (0,)), ((), ()))
VMEM_BIG = 56 * 1024 * 1024


def _params(vmem=None):
    return pltpu.CompilerParams(vmem_limit_bytes=vmem)


def _row_tile(rows):
    return 256 if rows % 256 == 0 else 128


def _sigmoid(x):
    return 1.0 / (1.0 + jnp.exp(-x))


def _expm1_nonpos(x):
    series = x * (1.0 + 0.5 * x * (1.0 + x / 3.0 * (1.0 + 0.25 * x * (1.0 + 0.2 * x))))
    return jnp.where(x > -0.05, series, jnp.exp(x) - 1.0)


def _softplus(x):
    e = jnp.exp(-jnp.abs(x))
    log1p = jnp.where(e < 1e-3, e * (1.0 - e * (0.5 - e / 3.0)), jnp.log(1.0 + e))
    return jnp.maximum(x, 0.0) + log1p


def _gelu_and_grad(x):
    c0 = 0.7978845608028654
    x2 = x * x
    t = jnp.tanh(c0 * (x + 0.044715 * x * x2))
    gelu = 0.5 * x * (1.0 + t)
    grad = 0.5 * (1.0 + t) + 0.5 * x * (1.0 - t * t) * c0 * (1.0 + 3.0 * 0.044715 * x2)
    return gelu, grad


def _ln_stats(z):
    mu = jnp.mean(z, axis=-1, keepdims=True)
    zc = z - mu
    var = jnp.mean(zc * zc, axis=-1, keepdims=True)
    rstd = lax.rsqrt(var + LN_EPS)
    return zc * rstd, rstd


def _ln_bwd(dy, z, gamma):
    xhat, rstd = _ln_stats(z)
    dxh = dy * gamma
    m1 = jnp.mean(dxh, axis=-1, keepdims=True)
    m2 = jnp.mean(dxh * xhat, axis=-1, keepdims=True)
    dz = rstd * (dxh - m1 - xhat * m2)
    return dz, jnp.sum(dy * xhat, axis=0, keepdims=True), jnp.sum(dy, axis=0, keepdims=True)


def _rms(y, gamma):
    r = lax.rsqrt(jnp.mean(y * y, axis=-1, keepdims=True) + RMS_EPS)
    return y * r * gamma


def _rms_bwd(dout, y, gamma):
    r = lax.rsqrt(jnp.mean(y * y, axis=-1, keepdims=True) + RMS_EPS)
    n = y * r
    dn = dout * gamma
    dy = r * (dn - n * jnp.mean(dn * n, axis=-1, keepdims=True))
    return dy, jnp.sum(dout * n, axis=0, keepdims=True), n * gamma


def _load_blocks(w_hbm, items, sems):
    @pl.when(pl.program_id(0) == 0)
    def _():
        copies = []
        for k, (off, nr, dst) in enumerate(items):
            for dev in range(N_DEV):
                copies.append(pltpu.make_async_copy(w_hbm.at[dev, pl.ds(off, nr), :], dst.at[pl.ds(dev * nr, nr), :], sems.at[k, dev]))
        for cp in copies:
            cp.start()
        for cp in copies:
            cp.wait()


def _full(shape):
    return pl.BlockSpec(shape, lambda *_: (0,) * len(shape))


ANY = pl.BlockSpec(memory_space=pl.ANY)


def _all_gather(xs, after, name):
    n = len(xs)

    def body(*refs):
        ins, outs = refs[:n], refs[n + 1 : 2 * n + 1]
        send_sems, recv_sems, local_sems = refs[2 * n + 1 :]
        x, y, c = lax.axis_index("x"), lax.axis_index("y"), lax.axis_index("c")
        me, sibling = (x, y, c), (x, y, 1 - c)
        chips = [(1 - x, y), (x, 1 - y), (1 - x, 1 - y)]

        def slot(i, dev):
            return outs[i].at[4 * dev[0] + 2 * dev[1] + dev[2]]

        def copy(i, k, block, to, src=None):
            return pltpu.make_async_remote_copy(
                src_ref=slot(i, block) if src is None else src,
                dst_ref=slot(i, block),
                send_sem=send_sems.at[i, k],
                recv_sem=recv_sems.at[i, k],
                device_id=to,
                device_id_type=MESH,
            )

        mine = [pltpu.make_async_copy(ins[i], slot(i, me), local_sems.at[i]) for i in range(n)]
        for cp in mine:
            cp.start()
        first = []
        for i in range(n):
            first.append(copy(i, 0, me, sibling, src=ins[i]))
            first += [copy(i, 1 + j, me, (*chip, c), src=ins[i]) for j, chip in enumerate(chips)]
        for cp in first:
            cp.start()
        passed = []
        for j, chip in enumerate(chips):
            for i in range(n):
                copy(i, 1 + j, (*chip, c), me).wait_recv()
                fwd = copy(i, 4 + j, (*chip, c), sibling)
                fwd.start()
                passed.append(fwd)
        for i in range(n):
            copy(i, 0, sibling, me).wait_recv()
            for j, chip in enumerate(chips):
                copy(i, 4 + j, (*chip, 1 - c), me).wait_recv()
        for cp in first + passed:
            cp.wait_send()
        for cp in mine:
            cp.wait()

    return pl.pallas_call(
        body,
        name=name,
        out_shape=[jax.ShapeDtypeStruct((N_DEV,) + a.shape, a.dtype) for a in xs],
        in_specs=[ANY] * (n + 1),
        out_specs=[ANY] * n,
        scratch_shapes=[pltpu.SemaphoreType.DMA((n, 7)), pltpu.SemaphoreType.DMA((n, 7)), pltpu.SemaphoreType.DMA((n,))],
    )(*xs, after)


HBM = pl.BlockSpec(memory_space=pltpu.HBM)
SEM = pl.BlockSpec(memory_space=pltpu.SEMAPHORE)
EFFECT = pltpu.SideEffectType.DATAFLOW_SIDE_EFFECTING


def _push_copies(ins, lands, send_sems, recv_sems, scatter):
    x, y, c = lax.axis_index("x"), lax.axis_index("y"), lax.axis_index("c")
    me = 4 * x + 2 * y + c
    copies = []
    for k in range(1, N_DEV):
        px = 1 - x if (k >> 2) & 1 else x
        py = 1 - y if (k >> 1) & 1 else y
        pc = 1 - c if k & 1 else c
        for i in range(len(ins)):
            copies.append(
                pltpu.make_async_remote_copy(
                    src_ref=ins[i].at[4 * px + 2 * py + pc] if scatter else ins[i],
                    dst_ref=lands[i].at[me],
                    send_sem=send_sems.at[i * (N_DEV - 1) + k - 1],
                    recv_sem=recv_sems.at[i * (N_DEV - 1) + k - 1],
                    device_id=(px, py, pc),
                    device_id_type=MESH,
                )
            )
    return copies


def _push_start(xs, lands, chain, scatter, name):
    n = len(xs)

    def body(*refs):
        ins, zones = refs[:n], refs[n : 2 * n]
        for cp in _push_copies(ins, zones, refs[2 * n + 1], refs[2 * n + 2], scatter):
            cp.start()

    outs = pl.pallas_call(
        body,
        name=name,
        out_shape=(
            pltpu.SemaphoreType.DMA((n * (N_DEV - 1),)),
            pltpu.SemaphoreType.DMA((n * (N_DEV - 1),)),
            *[pltpu.HBM(a.shape, a.dtype) for a in list(xs) + list(lands)],
        ),
        in_specs=[HBM] * (2 * n) + [ANY],
        out_specs=(SEM, SEM, *[HBM] * (2 * n)),
        input_output_aliases={i: 2 + i for i in range(2 * n)},
        compiler_params=pltpu.CompilerParams(has_side_effects=EFFECT),
    )(*[pltpu.with_memory_space_constraint(a, pltpu.HBM) for a in list(xs) + list(lands)], chain)
    return outs[0], outs[1], list(outs[2 : 2 + n]), list(outs[2 + n : 2 + 2 * n])


def _push_wait(send_sems, recv_sems, srcs, lands, after, scatter, name):
    n = len(srcs)

    def body(*refs):
        ins, zones = refs[:n], refs[n : 2 * n]
        for cp in _push_copies(ins, zones, refs[2 * n], refs[2 * n + 1], scatter):
            cp.wait_send()
            cp.wait_recv()

    outs = pl.pallas_call(
        body,
        name=name,
        out_shape=[pltpu.HBM(a.shape, a.dtype) for a in srcs + lands],
        in_specs=[HBM] * (2 * n) + [SEM, SEM, ANY],
        out_specs=[HBM] * (2 * n),
        input_output_aliases={i: i for i in range(2 * n)},
        compiler_params=pltpu.CompilerParams(has_side_effects=EFFECT),
    )(*srcs, *lands, send_sems, recv_sems, after)
    return list(outs[:n]), list(outs[n:])


def _sum8(parts, name):
    _, rows, cols = parts.shape
    tr = rows
    for cand in (512, 256, 128, 64, 32, 16):
        if rows % cand == 0 and rows > cand:
            tr = cand
            break

    def body(p_ref, o_ref):
        acc = p_ref[0].astype(F32)
        for d in range(1, N_DEV):
            acc = acc + p_ref[d].astype(F32)
        o_ref[...] = acc

    return pl.pallas_call(
        body,
        name=name,
        grid=(rows // tr,),
        in_specs=[pl.BlockSpec((N_DEV, tr, cols), lambda i: (0, i, 0))],
        out_specs=pl.BlockSpec((tr, cols), lambda i: (i, 0)),
        out_shape=jax.ShapeDtypeStruct((rows, cols), F32),
    )(parts)


def _mm_tn(a, b, out_dtype, name):
    rows, m = a.shape
    n = b.shape[1]
    tm = m // 2 if (m // 2) % 128 == 0 and m > 512 else m
    tr = next(t for t in (2176, 1024, 512, 256, 128) if rows % t == 0)
    nk = rows // tr

    def body(a_ref, b_ref, o_ref, acc):
        k = pl.program_id(1)
        part = lax.dot_general(a_ref[...].astype(BF16), b_ref[...].astype(BF16), TN, preferred_element_type=F32)

        @pl.when(k == 0)
        def _():
            acc[...] = part

        @pl.when(k > 0)
        def _():
            acc[...] += part

        @pl.when(k == nk - 1)
        def _():
            o_ref[...] = acc[...].astype(o_ref.dtype)

    return pl.pallas_call(
        body,
        name=name,
        grid=(m // tm, nk),
        in_specs=[pl.BlockSpec((tr, tm), lambda i, k: (k, i)), pl.BlockSpec((tr, n), lambda i, k: (k, 0))],
        out_specs=pl.BlockSpec((tm, n), lambda i, k: (i, 0)),
        out_shape=jax.ShapeDtypeStruct((m, n), out_dtype),
        scratch_shapes=[pltpu.VMEM((tm, n), F32)],
        compiler_params=_params(VMEM_BIG),
    )(a, b)


def _transpose_to_bf16(w, after, name):
    layers, a_dim, b_dim = w.shape

    def body(w_ref, after_ref, o_ref):
        eye = (lax.broadcasted_iota(jnp.int32, (a_dim, a_dim), 0) == lax.broadcasted_iota(jnp.int32, (a_dim, a_dim), 1)).astype(BF16)
        o_ref[0] = lax.dot_general(w_ref[0].astype(BF16), eye, TN, preferred_element_type=F32).astype(BF16)

    return pl.pallas_call(
        body,
        name=name,
        grid=(layers,),
        in_specs=[pl.BlockSpec((1, a_dim, b_dim), lambda l: (l, 0, 0)), ANY],
        out_specs=pl.BlockSpec((1, b_dim, a_dim), lambda l: (l, 0, 0)),
        out_shape=jax.ShapeDtypeStruct((layers, b_dim, a_dim), BF16),
    )(w, after)


def _weight_scratch(locs, d):
    return [pltpu.VMEM((N_DEV * nr, d), BF16) for _, nr in locs] + [pltpu.SemaphoreType.DMA((len(locs), N_DEV))]


def _proj_fwd(h, wbuf, loc, name):
    rows, d = h.shape
    n = N_DEV * loc[1]
    tm = _row_tile(rows)

    def body(h_ref, w_hbm, o_ref, w, sems):
        _load_blocks(w_hbm, [(*loc, w)], sems)
        o_ref[...] = lax.dot_general(h_ref[...].astype(BF16), w[...], NT, preferred_element_type=F32)

    return pl.pallas_call(
        body,
        name=name,
        grid=(rows // tm,),
        in_specs=[pl.BlockSpec((tm, d), lambda i: (i, 0)), ANY],
        out_specs=pl.BlockSpec((tm, n), lambda i: (i, 0)),
        out_shape=jax.ShapeDtypeStruct((rows, n), F32),
        scratch_shapes=_weight_scratch([loc], d),
        compiler_params=_params(VMEM_BIG),
    )(h, wbuf)


def _proj_bwd(dproj, wbuf, loc, dz, alpha, name):
    rows, n = dproj.shape
    d = dz.shape[1]
    tm = _row_tile(rows)

    def body(dp_ref, w_hbm, dz_ref, o_ref, w, sems):
        _load_blocks(w_hbm, [(*loc, w)], sems)
        o_ref[...] = alpha * dz_ref[...] + jnp.dot(dp_ref[...], w[...], preferred_element_type=F32)

    return pl.pallas_call(
        body,
        name=name,
        grid=(rows // tm,),
        in_specs=[pl.BlockSpec((tm, n), lambda i: (i, 0)), ANY, pl.BlockSpec((tm, d), lambda i: (i, 0))],
        out_specs=pl.BlockSpec((tm, d), lambda i: (i, 0)),
        out_shape=jax.ShapeDtypeStruct((rows, d), F32),
        scratch_shapes=_weight_scratch([loc], d),
        compiler_params=_params(VMEM_BIG),
    )(dproj, wbuf, dz)


FFN_CHUNKS = 2


def _ffn_fwd(h, wbuf, locs, ln_g, ln_b, alpha, name):
    rows, d = h.shape
    f = N_DEV * locs[0][1]
    tm = _row_tile(rows)
    tf = f // FFN_CHUNKS

    def body(h_ref, w_hbm, lg_ref, lb_ref, ho_ref, g_ref, u_ref, z_ref, hb_ref, wg, wu, wdv, sems):
        _load_blocks(w_hbm, [(*locs[0], wg), (*locs[1], wu), (*locs[2], wdv)], sems)
        x = h_ref[...]
        xb = x.astype(BF16)
        y = None
        for c in range(FFN_CHUNKS):
            sl = slice(c * tf, (c + 1) * tf)
            g = lax.dot_general(xb, wg[sl, :], NT, preferred_element_type=F32)
            u = lax.dot_general(xb, wu[sl, :], NT, preferred_element_type=F32)
            g_ref[:, sl] = g
            u_ref[:, sl] = u
            a = (g * _sigmoid(g) * u).astype(BF16)
            part = jnp.dot(a, wdv[sl, :], preferred_element_type=F32)
            y = part if y is None else y + part
        z = alpha * x + 0.5 * y
        z_ref[...] = z
        xhat, _ = _ln_stats(z)
        ho = xhat * lg_ref[...] + lb_ref[...]
        ho_ref[...] = ho
        hb_ref[...] = ho.astype(BF16)

    row_d = pl.BlockSpec((tm, d), lambda i: (i, 0))
    row_f = pl.BlockSpec((tm, f), lambda i: (i, 0))
    return pl.pallas_call(
        body,
        name=name,
        grid=(rows // tm,),
        in_specs=[row_d, ANY, _full((1, d)), _full((1, d))],
        out_specs=[row_d, row_f, row_f, row_d, row_d],
        out_shape=[
            jax.ShapeDtypeStruct((rows, d), F32),
            jax.ShapeDtypeStruct((rows, f), F32),
            jax.ShapeDtypeStruct((rows, f), F32),
            jax.ShapeDtypeStruct((rows, d), F32),
            jax.ShapeDtypeStruct((rows, d), BF16),
        ],
        scratch_shapes=_weight_scratch(locs, d),
        compiler_params=_params(VMEM_BIG),
    )(h, wbuf, ln_g, ln_b)


def _ffn_bwd(dh, z, g, u, wbuf, locs, ln_g, after, alpha, name):
    rows, d = dh.shape
    f = N_DEV * locs[0][1]
    tm = _row_tile(rows)
    tf = f // FFN_CHUNKS

    def body(dh_ref, z_ref, g_ref, u_ref, w_hbm, lg_ref, after_ref, dx_ref, dg_ref, du_ref, a_ref, dyb_ref, dgam_ref, dbet_ref, wg, wu, wdv, sems):
        _load_blocks(w_hbm, [(*locs[0], wg), (*locs[1], wu), (*locs[2], wdv)], sems)

        @pl.when(pl.program_id(0) == 0)
        def _():
            dgam_ref[...] = jnp.zeros_like(dgam_ref)
            dbet_ref[...] = jnp.zeros_like(dbet_ref)

        dz, dgam, dbet = _ln_bwd(dh_ref[...], z_ref[...], lg_ref[...])
        dgam_ref[...] += dgam
        dbet_ref[...] += dbet
        dyb = (0.5 * dz).astype(BF16)
        dyb_ref[...] = dyb
        dx = alpha * dz
        for c in range(FFN_CHUNKS):
            sl = slice(c * tf, (c + 1) * tf)
            da = lax.dot_general(dyb, wdv[sl, :], NT, preferred_element_type=F32)
            gc = g_ref[:, sl]
            uc = u_ref[:, sl]
            sg = _sigmoid(gc)
            silu = gc * sg
            dgc = (da * uc * (sg * (1.0 + gc * (1.0 - sg)))).astype(BF16)
            duc = (da * silu).astype(BF16)
            a_ref[:, sl] = (silu * uc).astype(BF16)
            dg_ref[:, sl] = dgc
            du_ref[:, sl] = duc
            dx = dx + jnp.dot(dgc, wg[sl, :], preferred_element_type=F32) + jnp.dot(duc, wu[sl, :], preferred_element_type=F32)
        dx_ref[...] = dx

    row_d = pl.BlockSpec((tm, d), lambda i: (i, 0))
    row_f = pl.BlockSpec((tm, f), lambda i: (i, 0))
    return pl.pallas_call(
        body,
        name=name,
        grid=(rows // tm,),
        in_specs=[row_d, row_d, row_f, row_f, ANY, _full((1, d)), ANY],
        out_specs=[row_d, row_f, row_f, row_f, row_d, _full((1, d)), _full((1, d))],
        out_shape=[
            jax.ShapeDtypeStruct((rows, d), F32),
            jax.ShapeDtypeStruct((rows, f), BF16),
            jax.ShapeDtypeStruct((rows, f), BF16),
            jax.ShapeDtypeStruct((rows, f), BF16),
            jax.ShapeDtypeStruct((rows, d), BF16),
            jax.ShapeDtypeStruct((1, d), F32),
            jax.ShapeDtypeStruct((1, d), F32),
        ],
        scratch_shapes=_weight_scratch(locs, d),
        compiler_params=_params(VMEM_BIG),
    )(dh, z, g, u, wbuf, ln_g, after)


LANES = 128


def _gates(xc, wa_ref, ba, wx_ref, bx, lam):
    xcb = xc.astype(BF16)
    r = _sigmoid(jnp.dot(xcb, wa_ref[0], preferred_element_type=F32) + ba)
    i = _sigmoid(jnp.dot(xcb, wx_ref[0], preferred_element_type=F32) + bx)
    sp = _softplus(-lam)
    log_a = -LRU_C * sp * r
    a = jnp.exp(log_a)
    s = jnp.sqrt(-_expm1_nonpos(2.0 * log_a))
    return xcb, r, i, sp, a, s


def _conv(xr, cw, cb):
    return cb + cw[3:4] * xr + cw[2:3] * pltpu.roll(xr, 1, 0) + cw[1:2] * pltpu.roll(xr, 2, 0) + cw[0:1] * pltpu.roll(xr, 3, 0)


def _rglru_fwd(proj, cw, cb, wa, ba, wx, bx, lam, t_len, pad, name):
    rows = proj.shape[0]
    n_seq = rows // t_len
    n_tile = REC_WIDTH // LANES
    groups = t_len // 8

    def body(xr_ref, gt_ref, cw_ref, cb_ref, wa_ref, ba_ref, wx_ref, bx_ref, lam_ref, xc_ref, h_ref, y_ref, a_s, b_s):
        valid = lax.broadcasted_iota(jnp.int32, (t_len, 1), 0) >= pad
        xr = jnp.where(valid, xr_ref[...], 0.0)
        xc = _conv(xr, cw_ref[...], cb_ref[...])
        xc_ref[...] = xc
        _, _, i, _, a, s = _gates(xc, wa_ref, ba_ref[...], wx_ref, bx_ref[...], lam_ref[...])
        a_s[...] = a
        b_s[...] = jnp.where(valid, s * (i * xc), 0.0)
        sub = lax.broadcasted_iota(jnp.int32, (8, LANES), 0)

        def step(gi, carry):
            r0 = pl.multiple_of(gi * 8, 8)
            av = a_s[pl.ds(r0, 8), :]
            bv = b_s[pl.ds(r0, 8), :]
            for d in (1, 2, 4):
                m = sub >= d
                a_sh = pltpu.roll(av, d, 0)
                b_sh = pltpu.roll(bv, d, 0)
                bv = jnp.where(m, av * b_sh + bv, bv)
                av = jnp.where(m, av * a_sh, av)
            hv = av * carry + bv
            h_ref[pl.ds(r0, 8), :] = hv
            return jnp.broadcast_to(hv[7:8, :], (8, LANES))

        lax.fori_loop(0, groups, step, jnp.zeros((8, LANES), F32), unroll=4)
        gelu, _ = _gelu_and_grad(gt_ref[...])
        y_ref[...] = h_ref[...] * gelu

    seq_tile = lambda col0: pl.BlockSpec((t_len, LANES), lambda s, j: (s, col0 + j))
    vec = pl.BlockSpec((1, LANES), lambda s, j: (0, j))
    out = jax.ShapeDtypeStruct((rows, REC_WIDTH), F32)
    return pl.pallas_call(
        body,
        name=name,
        grid=(n_seq, n_tile),
        in_specs=[
            seq_tile(0),
            seq_tile(n_tile),
            pl.BlockSpec((CONV_WIDTH, LANES), lambda s, j: (0, j)),
            vec,
            pl.BlockSpec((1, LANES, LANES), lambda s, j: (j, 0, 0)),
            vec,
            pl.BlockSpec((1, LANES, LANES), lambda s, j: (j, 0, 0)),
            vec,
            vec,
        ],
        out_specs=[seq_tile(0)] * 3,
        out_shape=[out, out, out],
        scratch_shapes=[pltpu.VMEM((t_len, LANES), F32), pltpu.VMEM((t_len, LANES), F32)],
        compiler_params=_params(VMEM_BIG),
    )(proj, proj, cw, cb, wa, ba, wx, bx, lam)


def _rglru_bwd(dy, h, proj, xc, cw, wa, ba, wx, bx, lam, t_len, pad, name):
    rows = proj.shape[0]
    n_seq = rows // t_len
    n_tile = REC_WIDTH // LANES
    groups = t_len // 8

    def body(dy_ref, h_ref, xr_ref, gt_ref, xc_ref, cw_ref, wa_ref, ba_ref, wx_ref, bx_ref, lam_ref,
             dxr_ref, dgt_ref, dcw_ref, dcb_ref, dwa_ref, dba_ref, dwx_ref, dbx_ref, dlam_ref, a_s, c_s, db_s, da_s):
        first_seq = pl.program_id(1) == 0

        @pl.when(first_seq)
        def _():
            for ref in (dcw_ref, dcb_ref, dwa_ref, dba_ref, dwx_ref, dbx_ref, dlam_ref):
                ref[...] = jnp.zeros_like(ref)

        valid = lax.broadcasted_iota(jnp.int32, (t_len, 1), 0) >= pad
        gelu, dgelu = _gelu_and_grad(gt_ref[...])
        dyv = dy_ref[...]
        dho = dyv * gelu
        dgt_ref[...] = (dyv * h_ref[...] * dgelu).astype(BF16)
        xc = xc_ref[...]
        lam = lam_ref[...]
        xcb, r, i, sp, a, s = _gates(xc, wa_ref, ba_ref[...], wx_ref, bx_ref[...], lam)
        a_s[...] = a
        c_s[...] = a * dho
        db_s[...] = dho
        sub = lax.broadcasted_iota(jnp.int32, (8, LANES), 0)

        def step(k, carry):
            gi = groups - 1 - k
            r0 = pl.multiple_of(gi * 8, 8)
            av = a_s[pl.ds(r0, 8), :]
            cv = c_s[pl.ds(r0, 8), :]
            for d in (1, 2, 4):
                m = sub < 8 - d
                a_sh = pltpu.roll(av, 8 - d, 0)
                c_sh = pltpu.roll(cv, 8 - d, 0)
                cv = jnp.where(m, av * c_sh + cv, cv)
                av = jnp.where(m, av * a_sh, av)
            ev = av * carry + cv
            e_next = jnp.where(sub < 7, pltpu.roll(ev, 7, 0), carry)
            dht = db_s[pl.ds(r0, 8), :] + e_next
            hv = h_ref[pl.ds(r0, 8), :]
            rp = pl.multiple_of(jnp.maximum(gi - 1, 0) * 8, 8)
            h_before = jnp.where(gi > 0, jnp.broadcast_to(h_ref[pl.ds(rp, 8), :][7:8, :], (8, LANES)), 0.0)
            h_prev = jnp.where(sub >= 1, pltpu.roll(hv, 1, 0), h_before)
            db_s[pl.ds(r0, 8), :] = dht
            da_s[pl.ds(r0, 8), :] = dht * h_prev
            return jnp.broadcast_to(ev[0:1, :], (8, LANES))

        lax.fori_loop(0, groups, step, jnp.zeros((8, LANES), F32), unroll=4)

        db = jnp.where(valid, db_s[...], 0.0)
        da = da_s[...]
        ds = db * (i * xc)
        di = db * s * xc
        dxc = db * s * i
        dlog_a = da * a - ds * (a * a) / jnp.maximum(s, 1e-30)
        dr = dlog_a * (-LRU_C * sp)
        dsp = jnp.sum(dlog_a * (-LRU_C) * r, axis=0, keepdims=True)
        dlam_ref[...] += dsp * (-_sigmoid(-lam))
        dpr = dr * r * (1.0 - r)
        dpi = di * i * (1.0 - i)
        dprb = dpr.astype(BF16)
        dpib = dpi.astype(BF16)
        dxc = dxc + lax.dot_general(dprb, wa_ref[0], NT, preferred_element_type=F32) + lax.dot_general(dpib, wx_ref[0], NT, preferred_element_type=F32)
        dwa_ref[0] += lax.dot_general(xcb, dprb, TN, preferred_element_type=F32)
        dwx_ref[0] += lax.dot_general(xcb, dpib, TN, preferred_element_type=F32)
        dba_ref[...] += jnp.sum(dpr, axis=0, keepdims=True)
        dbx_ref[...] += jnp.sum(dpi, axis=0, keepdims=True)
        dxc = jnp.where(valid, dxc, 0.0)
        xr = jnp.where(valid, xr_ref[...], 0.0)
        dcb_ref[...] += jnp.sum(dxc, axis=0, keepdims=True)
        for k in range(CONV_WIDTH):
            shifted = xr if k == CONV_WIDTH - 1 else pltpu.roll(xr, CONV_WIDTH - 1 - k, 0)
            dcw_ref[k : k + 1, :] += jnp.sum(dxc * shifted, axis=0, keepdims=True)
        cw = cw_ref[...]
        dxr = cw[3:4] * dxc + cw[2:3] * pltpu.roll(dxc, t_len - 1, 0) + cw[1:2] * pltpu.roll(dxc, t_len - 2, 0) + cw[0:1] * pltpu.roll(dxc, t_len - 3, 0)
        dxr_ref[...] = jnp.where(valid, dxr, 0.0).astype(BF16)

    seq_tile = lambda col0: pl.BlockSpec((t_len, LANES), lambda j, s: (s, col0 + j))
    vec = pl.BlockSpec((1, LANES), lambda j, s: (0, j))
    mat = pl.BlockSpec((1, LANES, LANES), lambda j, s: (j, 0, 0))
    cwb = pl.BlockSpec((CONV_WIDTH, LANES), lambda j, s: (0, j))
    big = jax.ShapeDtypeStruct((rows, REC_WIDTH), BF16)
    vec_shape = jax.ShapeDtypeStruct((1, REC_WIDTH), F32)
    mat_shape = jax.ShapeDtypeStruct((n_tile, LANES, LANES), F32)
    return pl.pallas_call(
        body,
        name=name,
        grid=(n_tile, n_seq),
        in_specs=[seq_tile(0), seq_tile(0), seq_tile(0), seq_tile(n_tile), seq_tile(0), cwb, mat, vec, mat, vec, vec],
        out_specs=[seq_tile(0), seq_tile(0), cwb, vec, mat, vec, mat, vec, vec],
        out_shape=[big, big, jax.ShapeDtypeStruct((CONV_WIDTH, REC_WIDTH), F32), vec_shape, mat_shape, vec_shape, mat_shape, vec_shape, vec_shape],
        scratch_shapes=[pltpu.VMEM((t_len, LANES), F32)] * 4,
        compiler_params=_params(VMEM_BIG),
    )(dy, h, proj, proj, xc, cw, wa, ba, wx, bx, lam)


QKV_WIDTH = ATTN_WIDTH + 2 * KV_WIDTH


def _rotate(v, cos, sin_up, sin_down):
    width = v.shape[1]
    return v * cos + pltpu.roll(v, width - ROPE_DIM // 2, 1) * sin_up + pltpu.roll(v, ROPE_DIM // 2, 1) * sin_down


def _rope_rows(t_len):
    return t_len // 4 if t_len % 64 == 0 else BLOCK


def _rope_fwd(proj, cos, sin_up, sin_down, t_len, name):
    rows = proj.shape[0]
    rb = _rope_rows(t_len)
    nb = t_len // rb
    n_seq = rows // t_len
    q0 = 2 * REC_WIDTH

    def body(q_ref, k_ref, v_ref, cos_ref, up_ref, down_ref, o_ref):
        cos_t, up_t, down_t = cos_ref[...], up_ref[...], down_ref[...]
        o_ref[:, 0:ATTN_WIDTH] = _rotate(q_ref[...], cos_t, up_t, down_t).astype(BF16)
        o_ref[:, ATTN_WIDTH : ATTN_WIDTH + KV_WIDTH] = _rotate(k_ref[...], cos_t[:, :KV_WIDTH], up_t[:, :KV_WIDTH], down_t[:, :KV_WIDTH]).astype(BF16)
        o_ref[:, ATTN_WIDTH + KV_WIDTH :] = v_ref[...].astype(BF16)

    tab = pl.BlockSpec((rb, ATTN_WIDTH), lambda s, n: (n, 0))
    return pl.pallas_call(
        body,
        name=name,
        grid=(n_seq, nb),
        in_specs=[
            pl.BlockSpec((rb, ATTN_WIDTH), lambda s, n: (s * nb + n, q0 // ATTN_WIDTH)),
            pl.BlockSpec((rb, KV_WIDTH), lambda s, n: (s * nb + n, (q0 + ATTN_WIDTH) // KV_WIDTH)),
            pl.BlockSpec((rb, KV_WIDTH), lambda s, n: (s * nb + n, (q0 + ATTN_WIDTH) // KV_WIDTH + 1)),
            tab,
            tab,
            tab,
        ],
        out_specs=pl.BlockSpec((rb, QKV_WIDTH), lambda s, n: (s * nb + n, 0)),
        out_shape=jax.ShapeDtypeStruct((rows, QKV_WIDTH), BF16),
    )(proj, proj, proj, cos, sin_up, sin_down)


def _rope_bwd(dq, dk, dv, cos, sin_up, sin_down, t_len, name):
    rows = dq.shape[0]
    rb = _rope_rows(t_len)
    nb = t_len // rb
    n_seq = rows // t_len

    def body(dq_ref, dk_ref, dv_ref, cos_ref, up_ref, down_ref, o_ref):
        cos_t, up_t, down_t = cos_ref[...], -up_ref[...], -down_ref[...]
        o_ref[:, 0:ATTN_WIDTH] = _rotate(dq_ref[...], cos_t, up_t, down_t).astype(BF16)
        o_ref[:, ATTN_WIDTH : ATTN_WIDTH + KV_WIDTH] = _rotate(dk_ref[...], cos_t[:, :KV_WIDTH], up_t[:, :KV_WIDTH], down_t[:, :KV_WIDTH]).astype(BF16)
        o_ref[:, ATTN_WIDTH + KV_WIDTH :] = dv_ref[...].astype(BF16)

    tab = pl.BlockSpec((rb, ATTN_WIDTH), lambda s, n: (n, 0))
    blk = lambda w: pl.BlockSpec((rb, w), lambda s, n: (s * nb + n, 0))
    return pl.pallas_call(
        body,
        name=name,
        grid=(n_seq, nb),
        in_specs=[blk(ATTN_WIDTH), blk(KV_WIDTH), blk(KV_WIDTH), tab, tab, tab],
        out_specs=blk(QKV_WIDTH),
        out_shape=jax.ShapeDtypeStruct((rows, QKV_WIDTH), BF16),
    )(dq, dk, dv, cos, sin_up, sin_down)


GROUP = 4


def _attn_mask(n, pad):
    q_pos = n * BLOCK + (lax.broadcasted_iota(jnp.int32, (GROUP * BLOCK, 1), 0) & (BLOCK - 1)) - pad
    col = lax.broadcasted_iota(jnp.int32, (1, 3 * BLOCK), 1)
    first = col < BLOCK
    k_pos = col - pad + jnp.where(first, 0, (n - 2) * BLOCK)
    dist = q_pos - k_pos
    band = (dist >= 0) & (dist < WINDOW) & (k_pos >= N_META) & jnp.logical_not(first)
    meta = (k_pos >= 0) & (k_pos < N_META) & (k_pos <= q_pos) & first
    return band | meta


def _in_half(e):
    lane = lax.broadcasted_iota(jnp.int32, (1, BLOCK), 1)
    return lane >= HEAD_DIM if e else lane < HEAD_DIM


def _head_views(q2, e, g):
    v = jnp.where(_in_half(e), q2, 0.0)
    return pltpu.roll(v, HEAD_DIM, 1) if e != g else v


def _tile(g, t):
    j = (GROUP // 2) * g + t
    return slice(j * BLOCK, (j + 1) * BLOCK)


def _stack_rows(ref, g):
    parts = []
    for t in range(GROUP // 2):
        tile = ref[:, _tile(g, t)].astype(F32)
        parts += [_head_views(tile, e, g) for e in range(2)]
    return jnp.concatenate(parts, axis=0)


def _stack_group(q_ref, sink_ref, g):
    head = lax.broadcasted_iota(jnp.int32, (GROUP * BLOCK, 1), 0) >> (BLOCK.bit_length() - 1)
    sink = jnp.zeros((GROUP * BLOCK, 1), F32)
    for hh in range(GROUP):
        sink = jnp.where(head == hh, sink_ref[GROUP * g + hh : GROUP * g + hh + 1, 0:1], sink)
    return _stack_rows(q_ref, g), sink


def _unstack_pair(v, g, t):
    out = None
    for e in range(2):
        blk = v[(2 * t + e) * BLOCK : (2 * t + e + 1) * BLOCK, :]
        blk = pltpu.roll(blk, HEAD_DIM, 1) if e != g else blk
        out = blk if out is None else out + blk
    return out


def _softmax_with_sink(scores, mask, sink):
    s = jnp.where(mask, scores * (HEAD_DIM**-0.5), NEG_INF)
    m = jnp.maximum(jnp.max(s, axis=-1, keepdims=True), sink)
    p = jnp.exp(s - m)
    p_sink = jnp.exp(sink - m)
    inv = 1.0 / (jnp.sum(p, axis=-1, keepdims=True) + p_sink)
    return p * inv, p_sink * inv


def _kv_specs(nb):
    k_col = ATTN_WIDTH // KV_WIDTH
    specs = []
    for col in (k_col, k_col + 1):
        specs += [
            pl.BlockSpec((BLOCK, KV_WIDTH), lambda s, n, col=col: (s * nb, col)),
            pl.BlockSpec((BLOCK, KV_WIDTH), lambda s, n, col=col: (s * nb + jnp.maximum(n - 1, 0), col)),
            pl.BlockSpec((BLOCK, KV_WIDTH), lambda s, n, col=col: (s * nb + n, col)),
        ]
    return specs


def _attn_fwd(qkv, sinks, t_len, pad, name):
    rows = qkv.shape[0]
    nb = t_len // BLOCK
    n_seq = rows // t_len

    def body(q_ref, km_ref, kp_ref, kc_ref, vm_ref, vp_ref, vc_ref, sink_ref, o_ref):
        n = pl.program_id(1)
        mask = _attn_mask(n, pad)
        keys = jnp.concatenate([km_ref[...], kp_ref[...], kc_ref[...]], axis=0)
        vals = jnp.concatenate([vm_ref[...], vp_ref[...], vc_ref[...]], axis=0)
        for g in range(N_Q_HEADS // GROUP):
            qs, sink = _stack_group(q_ref, sink_ref, g)
            scores = lax.dot_general(qs.astype(BF16), keys, NT, preferred_element_type=F32)
            p, _ = _softmax_with_sink(scores, mask, sink)
            o = jnp.where(_in_half(g), jnp.dot(p.astype(BF16), vals, preferred_element_type=F32), 0.0)
            for t in range(GROUP // 2):
                o_ref[:, _tile(g, t)] = _unstack_pair(o, g, t)

    return pl.pallas_call(
        body,
        name=name,
        grid=(n_seq, nb),
        in_specs=[pl.BlockSpec((BLOCK, ATTN_WIDTH), lambda s, n: (s * nb + n, 0))] + _kv_specs(nb) + [_full((N_Q_HEADS, BLOCK))],
        out_specs=pl.BlockSpec((BLOCK, ATTN_WIDTH), lambda s, n: (s * nb + n, 0)),
        out_shape=jax.ShapeDtypeStruct((rows, ATTN_WIDTH), F32),
    )(qkv, qkv, qkv, qkv, qkv, qkv, qkv, sinks)


def _attn_bwd(qkv, sinks, o, do, t_len, pad, name):
    rows = qkv.shape[0]
    nb = t_len // BLOCK
    n_seq = rows // t_len

    def body(q_ref, km_ref, kp_ref, kc_ref, vm_ref, vp_ref, vc_ref, sink_ref, o_ref, do_ref, dq_ref, dk_ref, dv_ref, dsink_ref):
        s_id, n = pl.program_id(0), pl.program_id(1)

        @pl.when(n == 0)
        def _():
            dk_ref[...] = jnp.zeros_like(dk_ref)
            dv_ref[...] = jnp.zeros_like(dv_ref)

        @pl.when((n == 0) & (s_id == 0))
        def _():
            dsink_ref[...] = jnp.zeros_like(dsink_ref)

        mask = _attn_mask(n, pad)
        keys = jnp.concatenate([km_ref[...], kp_ref[...], kc_ref[...]], axis=0)
        vals = jnp.concatenate([vm_ref[...], vp_ref[...], vc_ref[...]], axis=0)
        dkeys = jnp.zeros((3 * BLOCK, KV_WIDTH), F32)
        dvals = jnp.zeros((3 * BLOCK, KV_WIDTH), F32)
        for g in range(N_Q_HEADS // GROUP):
            qs, sink = _stack_group(q_ref, sink_ref, g)
            qsb = qs.astype(BF16)
            scores = lax.dot_general(qsb, keys, NT, preferred_element_type=F32)
            p, p_sink = _softmax_with_sink(scores, mask, sink)
            dos = _stack_rows(do_ref, g)
            delta = jnp.sum(dos * _stack_rows(o_ref, g), axis=-1, keepdims=True)
            dosb = dos.astype(BF16)
            dp = lax.dot_general(dosb, vals, NT, preferred_element_type=F32)
            ds = (p * (dp - delta) * (HEAD_DIM**-0.5)).astype(BF16)
            dqs = jnp.where(_in_half(g), jnp.dot(ds, keys, preferred_element_type=F32), 0.0)
            for t in range(GROUP // 2):
                dq_ref[:, _tile(g, t)] = _unstack_pair(dqs, g, t)
            dkeys = dkeys + lax.dot_general(ds, qsb, TN, preferred_element_type=F32)
            dvals = dvals + lax.dot_general(p.astype(BF16), dosb, TN, preferred_element_type=F32)
            sink_term = p_sink * delta
            for hh in range(GROUP):
                head = GROUP * g + hh
                part = -jnp.sum(sink_term[hh * BLOCK : (hh + 1) * BLOCK, :], axis=0, keepdims=True)
                dsink_ref[head : head + 1, :] += jnp.broadcast_to(part, (1, BLOCK))
        r_prev = pl.multiple_of(jnp.maximum(n - 1, 0) * BLOCK, BLOCK)
        r_cur = pl.multiple_of(n * BLOCK, BLOCK)
        for acc, d in ((dk_ref, dkeys), (dv_ref, dvals)):
            acc[0:BLOCK, :] += d[0:BLOCK]
            acc[pl.ds(r_prev, BLOCK), :] += d[BLOCK : 2 * BLOCK]
            acc[pl.ds(r_cur, BLOCK), :] += d[2 * BLOCK :]

    q_blk = pl.BlockSpec((BLOCK, ATTN_WIDTH), lambda s, n: (s * nb + n, 0))
    seq_kv = pl.BlockSpec((t_len, KV_WIDTH), lambda s, n: (s, 0))
    return pl.pallas_call(
        body,
        name=name,
        grid=(n_seq, nb),
        in_specs=[q_blk] + _kv_specs(nb) + [_full((N_Q_HEADS, BLOCK)), q_blk, q_blk],
        out_specs=[q_blk, seq_kv, seq_kv, _full((N_Q_HEADS, BLOCK))],
        out_shape=[
            jax.ShapeDtypeStruct((rows, ATTN_WIDTH), F32),
            jax.ShapeDtypeStruct((rows, KV_WIDTH), F32),
            jax.ShapeDtypeStruct((rows, KV_WIDTH), F32),
            jax.ShapeDtypeStruct((N_Q_HEADS, BLOCK), F32),
        ],
    )(qkv, qkv, qkv, qkv, qkv, qkv, qkv, sinks, o, do)


def _mix_out_fwd(y_rec, y_attn, h, wbuf, loc, g_rec, g_attn, ln_g, ln_b, alpha, name):
    rows, d = h.shape
    tm = _row_tile(rows)

    def body(yr_ref, ya_ref, h_ref, w_hbm, gr_ref, ga_ref, lg_ref, lb_ref, ho_ref, z_ref, hb_ref, w_ref, sems):
        _load_blocks(w_hbm, [(*loc, w_ref)], sems)
        nr = _rms(yr_ref[...], gr_ref[...]).astype(BF16)
        na = _rms(ya_ref[...], ga_ref[...]).astype(BF16)
        m = jnp.dot(nr, w_ref[0:REC_WIDTH, :], preferred_element_type=F32) + jnp.dot(na, w_ref[REC_WIDTH:, :], preferred_element_type=F32)
        z = alpha * h_ref[...] + m
        z_ref[...] = z
        xhat, _ = _ln_stats(z)
        ho = xhat * lg_ref[...] + lb_ref[...]
        ho_ref[...] = ho
        hb_ref[...] = ho.astype(BF16)

    row_d = pl.BlockSpec((tm, d), lambda i: (i, 0))
    row_h = pl.BlockSpec((tm, REC_WIDTH), lambda i: (i, 0))
    return pl.pallas_call(
        body,
        name=name,
        grid=(rows // tm,),
        in_specs=[row_h, row_h, row_d, ANY, _full((1, REC_WIDTH)), _full((1, ATTN_WIDTH)), _full((1, d)), _full((1, d))],
        out_specs=[row_d, row_d, row_d],
        out_shape=[jax.ShapeDtypeStruct((rows, d), F32)] * 2 + [jax.ShapeDtypeStruct((rows, d), BF16)],
        scratch_shapes=_weight_scratch([loc], d),
    )(y_rec, y_attn, h, wbuf, g_rec, g_attn, ln_g, ln_b)


def _mix_out_bwd(dh, z, y_rec, y_attn, wbuf, loc, g_rec, g_attn, ln_g, name):
    rows, d = dh.shape
    tm = _row_tile(rows)
    mix = REC_WIDTH + ATTN_WIDTH

    def body(dh_ref, z_ref, yr_ref, ya_ref, w_hbm, gr_ref, ga_ref, lg_ref, dz_ref, dzb_ref, dyr_ref, dya_ref, yn_ref, dgam_ref, dbet_ref, dgr_ref, dga_ref, w_ref, sems):
        _load_blocks(w_hbm, [(*loc, w_ref)], sems)

        @pl.when(pl.program_id(0) == 0)
        def _():
            for ref in (dgam_ref, dbet_ref, dgr_ref, dga_ref):
                ref[...] = jnp.zeros_like(ref)

        dz, dgam, dbet = _ln_bwd(dh_ref[...], z_ref[...], lg_ref[...])
        dgam_ref[...] += dgam
        dbet_ref[...] += dbet
        dz_ref[...] = dz
        dzb = dz.astype(BF16)
        dzb_ref[...] = dzb
        dyn = lax.dot_general(dzb, w_ref[...], NT, preferred_element_type=F32)
        dyr, dgr, nr = _rms_bwd(dyn[:, 0:REC_WIDTH], yr_ref[...], gr_ref[...])
        dya, dga, na = _rms_bwd(dyn[:, REC_WIDTH:], ya_ref[...], ga_ref[...])
        dyr_ref[...] = dyr
        dya_ref[...] = dya
        dgr_ref[...] += dgr
        dga_ref[...] += dga
        yn_ref[:, 0:REC_WIDTH] = nr.astype(BF16)
        yn_ref[:, REC_WIDTH:] = na.astype(BF16)

    row_d = pl.BlockSpec((tm, d), lambda i: (i, 0))
    row_h = pl.BlockSpec((tm, REC_WIDTH), lambda i: (i, 0))
    row_m = pl.BlockSpec((tm, mix), lambda i: (i, 0))
    return pl.pallas_call(
        body,
        name=name,
        grid=(rows // tm,),
        in_specs=[row_d, row_d, row_h, row_h, ANY, _full((1, REC_WIDTH)), _full((1, ATTN_WIDTH)), _full((1, d))],
        out_specs=[row_d, row_d, row_h, row_h, row_m, _full((1, d)), _full((1, d)), _full((1, REC_WIDTH)), _full((1, ATTN_WIDTH))],
        out_shape=[
            jax.ShapeDtypeStruct((rows, d), F32),
            jax.ShapeDtypeStruct((rows, d), BF16),
            jax.ShapeDtypeStruct((rows, REC_WIDTH), F32),
            jax.ShapeDtypeStruct((rows, ATTN_WIDTH), F32),
            jax.ShapeDtypeStruct((rows, mix), BF16),
            jax.ShapeDtypeStruct((1, d), F32),
            jax.ShapeDtypeStruct((1, d), F32),
            jax.ShapeDtypeStruct((1, REC_WIDTH), F32),
            jax.ShapeDtypeStruct((1, ATTN_WIDTH), F32),
        ],
        scratch_shapes=_weight_scratch([loc], d),
    )(dh, z, y_rec, y_attn, wbuf, g_rec, g_attn, ln_g)


def _loss_head(y, target, t_len, first_token, name):
    rows, d = y.shape
    tm = _row_tile(rows)

    def body(y_ref, t_ref, loss_ref, dy_ref):
        i = pl.program_id(0)

        @pl.when(i == 0)
        def _():
            loss_ref[...] = jnp.zeros_like(loss_ref)

        row = i * tm + lax.broadcasted_iota(jnp.int32, (tm, 1), 0)
        is_token = lax.rem(row, t_len) >= first_token
        err = jnp.where(is_token, y_ref[...] - t_ref[...], 0.0)
        dy_ref[...] = err / d
        per_row = jnp.sum(err * err, axis=-1, keepdims=True) / d
        loss_ref[...] += jnp.broadcast_to(0.5 * jnp.sum(per_row, axis=0, keepdims=True), (1, BLOCK))

    row_d = pl.BlockSpec((tm, d), lambda i: (i, 0))
    return pl.pallas_call(
        body,
        name=name,
        grid=(rows // tm,),
        in_specs=[row_d, row_d],
        out_specs=[_full((1, BLOCK)), row_d],
        out_shape=[jax.ShapeDtypeStruct((1, BLOCK), F32), jax.ShapeDtypeStruct((rows, d), F32)],
    )(y, target)


def _meta_grad(dh0, t_len, pad, name):
    rows, d = dh0.shape
    nb = t_len // BLOCK
    n_seq = rows // t_len

    def body(dh_ref, o_ref):
        @pl.when(pl.program_id(0) == 0)
        def _():
            o_ref[...] = jnp.zeros_like(o_ref)

        o_ref[...] += dh_ref[pad : pad + N_META, :]

    return pl.pallas_call(
        body,
        name=name,
        grid=(n_seq,),
        in_specs=[pl.BlockSpec((BLOCK, d), lambda s: (s * nb, 0))],
        out_specs=_full((N_META, d)),
        out_shape=jax.ShapeDtypeStruct((N_META, d), F32),
    )(dh0)


def _adamw_math(w, g, m, v):
    nm = ADAM_B1 * m + (1.0 - ADAM_B1) * g
    nv = ADAM_B2 * v + (1.0 - ADAM_B2) * (g * g)
    m_hat = nm / (1.0 - ADAM_B1**ADAM_STEP)
    v_hat = nv / (1.0 - ADAM_B2**ADAM_STEP)
    return -ADAM_LR * (m_hat / (jnp.sqrt(v_hat) + ADAM_EPS) + ADAM_WD * w), nm, nv


def _reduce_update(parts, own, w, m, v, outs, layer, transposed, name):
    _, nr, d = parts.shape
    _, a_dim, b_dim = w.shape

    def body(p_ref, own_hbm, w_ref, m_ref, v_ref, g_in, dl_in, nm_in, nv_in, g_ref, dl_ref, nm_ref, nv_ref, own_v, acc, sem):
        me = 4 * lax.axis_index("x") + 2 * lax.axis_index("y") + lax.axis_index("c")
        cp = pltpu.make_async_copy(own_hbm.at[me], own_v, sem)
        cp.start()
        cp.wait()
        acc[...] = own_v[...].astype(F32)
        for dev in range(N_DEV):

            @pl.when(me != dev)
            def _():
                acc[...] += p_ref[dev].astype(F32)

        g = acc[...]
        if transposed:
            eye = (lax.broadcasted_iota(jnp.int32, (nr, nr), 0) == lax.broadcasted_iota(jnp.int32, (nr, nr), 1)).astype(BF16)
            hi = g.astype(BF16)
            r1 = g - hi.astype(F32)
            mid = r1.astype(BF16)
            lo = (r1 - mid.astype(F32)).astype(BF16)
            g = lax.dot_general(hi, eye, TN, preferred_element_type=F32)
            g = g + lax.dot_general(mid, eye, TN, preferred_element_type=F32)
            g = g + lax.dot_general(lo, eye, TN, preferred_element_type=F32)
        g_ref[0] = g
        dl_ref[0], nm_ref[0], nv_ref[0] = _adamw_math(w_ref[0], g, m_ref[0], v_ref[0])

    at_layer = pl.BlockSpec((1, a_dim, b_dim), lambda i: (layer, 0, 0))
    return pl.pallas_call(
        body,
        name=name,
        grid=(1,),
        in_specs=[_full((N_DEV, nr, d)), ANY, at_layer, at_layer, at_layer, ANY, ANY, ANY, ANY],
        out_specs=[at_layer] * 4,
        out_shape=[jax.ShapeDtypeStruct(w.shape, F32)] * 4,
        input_output_aliases={5: 0, 6: 1, 7: 2, 8: 3},
        scratch_shapes=[pltpu.VMEM((nr, d), parts.dtype), pltpu.VMEM((nr, d), F32), pltpu.SemaphoreType.DMA(())],
        compiler_params=_params(VMEM_BIG),
    )(parts, own, w, m, v, *outs)


def _adamw(w, g, m, v, name):
    shape = w.shape
    cols = shape[-1]
    rows = w.size // cols
    tr = rows
    for cand in (512, 256, 128, 64):
        if rows % cand == 0 and rows > cand:
            tr = cand
            break

    def body(w_ref, g_ref, m_ref, v_ref, d_ref, nm_ref, nv_ref):
        d_ref[...], nm_ref[...], nv_ref[...] = _adamw_math(w_ref[...], g_ref[...], m_ref[...], v_ref[...])

    blk = pl.BlockSpec((tr, cols), lambda i: (i, 0))
    flat = [a.reshape(rows, cols) for a in (w, g, m, v)]
    outs = pl.pallas_call(
        body,
        name=name,
        grid=(rows // tr,),
        in_specs=[blk] * 4,
        out_specs=[blk] * 3,
        out_shape=[jax.ShapeDtypeStruct((rows, cols), F32)] * 3,
    )(*flat)
    return [o.reshape(shape) for o in outs]


WEIGHTS = ["meta_tokens", "ffn1_w_gate", "ffn1_w_up", "ffn1_w_down", "ln1_g", "ln1_b", "w_in", "conv_w", "conv_b", "gate_a_w", "gate_a_b",
           "gate_x_w", "gate_x_b", "lru_lambda", "attn_sinks", "norm_rec_g", "norm_attn_g", "w_out", "ln2_g", "ln2_b", "ffn2_w_gate",
           "ffn2_w_up", "ffn2_w_down", "ln3_g", "ln3_b"]
BIG = [("ffn1_w_gate", True), ("ffn1_w_up", True), ("ffn1_w_down", False), ("w_in", True), ("w_out", False),
       ("ffn2_w_gate", True), ("ffn2_w_up", True), ("ffn2_w_down", False)]


SMALL = ["ln1_g", "ln1_b", "ln2_g", "ln2_b", "ln3_g", "ln3_b", "conv_b", "gate_a_b", "gate_x_b", "lru_lambda", "norm_rec_g", "norm_attn_g",
         "conv_w", "gate_a_w", "gate_x_w", "attn_sinks"]
GATHER_GROUPS = [["ffn1_w_gate", "ffn1_w_up", "ffn1_w_down"], ["w_in", "w_out"], ["ffn2_w_gate", "ffn2_w_up", "ffn2_w_down"]]
EXCHANGE_GROUPS = [["ffn2_w_gate", "ffn2_w_up", "ffn2_w_down", "w_out"], ["w_in", "ffn1_w_gate", "ffn1_w_up", "ffn1_w_down"]]


def _rope_tables(t_len, pad):
    pos = (jnp.arange(t_len) - pad).astype(F32)
    inv_freq = ROPE_THETA ** (-jnp.arange(0, ROPE_DIM, 2, dtype=F32) / ROPE_DIM)
    ang = pos[:, None] * inv_freq[None, :]
    cos, sin = jnp.cos(ang), jnp.sin(ang)
    half = ROPE_DIM // 2
    rest = jnp.zeros((t_len, HEAD_DIM - ROPE_DIM), F32)
    zero = jnp.zeros((t_len, half), F32)
    cos_h = jnp.concatenate([cos, cos, rest + 1.0], axis=1)
    up_h = jnp.concatenate([-sin, zero, rest], axis=1)
    down_h = jnp.concatenate([zero, sin, rest], axis=1)
    return [jnp.tile(t, (1, ATTN_WIDTH // HEAD_DIM)) for t in (cos_h, up_h, down_h)]


def _gate_tiles(w):
    z = jnp.zeros((HEAD_DIM, HEAD_DIM), w.dtype)
    tiles = [jnp.block([[w[2 * j], z], [z, w[2 * j + 1]]]) for j in range(w.shape[0] // 2)]
    return jnp.stack(tiles).astype(BF16)


def _gate_blocks(tiles):
    out = []
    for j in range(tiles.shape[0]):
        out += [tiles[j, :HEAD_DIM, :HEAD_DIM], tiles[j, HEAD_DIM:, HEAD_DIM:]]
    return jnp.stack(out)


def kernel(x, meta_tokens, ffn1_w_gate, ffn1_w_up, ffn1_w_down, ln1_g, ln1_b, w_in, conv_w, conv_b, gate_a_w, gate_a_b, gate_x_w, gate_x_b, lru_lambda, attn_sinks, norm_rec_g, norm_attn_g, w_out, ln2_g, ln2_b, ffn2_w_gate, ffn2_w_up, ffn2_w_down, ln3_g, ln3_b, loss_target, m_meta_tokens, m_ffn1_w_gate, m_ffn1_w_up, m_ffn1_w_down, m_ln1_g, m_ln1_b, m_w_in, m_conv_w, m_conv_b, m_gate_a_w, m_gate_a_b, m_gate_x_w, m_gate_x_b, m_lru_lambda, m_attn_sinks, m_norm_rec_g, m_norm_attn_g, m_w_out, m_ln2_g, m_ln2_b, m_ffn2_w_gate, m_ffn2_w_up, m_ffn2_w_down, m_ln3_g, m_ln3_b, v_meta_tokens, v_ffn1_w_gate, v_ffn1_w_up, v_ffn1_w_down, v_ln1_g, v_ln1_b, v_w_in, v_conv_w, v_conv_b, v_gate_a_w, v_gate_a_b, v_gate_x_w, v_gate_x_b, v_lru_lambda, v_attn_sinks, v_norm_rec_g, v_norm_attn_g, v_w_out, v_ln2_g, v_ln2_b, v_ffn2_w_gate, v_ffn2_w_up, v_ffn2_w_down, v_ln3_g, v_ln3_b):
    given = dict(locals())
    w = {k: given[k] for k in WEIGHTS}
    n_seq, seq, d = x.shape
    depth = ln1_g.shape[0]
    alpha = (2.0 * depth) ** 0.25
    pad = (-(N_META + seq)) % BLOCK
    t_len = pad + N_META + seq
    rows = n_seq * t_len
    me = 4 * lax.axis_index("x") + 2 * lax.axis_index("y") + lax.axis_index("c")

    transposed = dict(BIG)
    small = jnp.concatenate([meta_tokens, conv_w.reshape(-1, BLOCK)], axis=0)
    (small_all,) = _all_gather([small], small, "gather_small")
    sent, loc, gathers, chain = {}, {}, {}, small_all

    def start_gather(l, k):
        shard = jnp.concatenate([sent[name][l] for name in GATHER_GROUPS[k]], axis=0)
        zone = lax.dynamic_update_slice(lax.empty((N_DEV,) + shard.shape, BF16), shard[None], (me, 0, 0))
        gathers[(l, k)] = _push_start([shard], [zone], chain, False, f"gather_start_{l}_{k}")
        return gathers[(l, k)][2][0]

    for k, group in enumerate(GATHER_GROUPS):
        off = 0
        for name in group:
            sent[name] = _transpose_to_bf16(w[name], chain, "t_" + name) if transposed[name] else w[name].astype(BF16)
            loc[name] = (off, sent[name].shape[1])
            off += sent[name].shape[1]
        chain = start_gather(0, k)
    for l in range(1, depth):
        for k in range(len(GATHER_GROUPS)):
            chain = start_gather(l, k)
    ffn1_loc = [loc["ffn1_w_gate"], loc["ffn1_w_up"], loc["ffn1_w_down"]]
    ffn2_loc = [loc["ffn2_w_gate"], loc["ffn2_w_up"], loc["ffn2_w_down"]]
    meta_full = small_all[:, :N_META, :].transpose(1, 0, 2).reshape(N_META, d)
    conv_shard = conv_w.shape[-1]
    conv_full = small_all[:, N_META:, :].reshape(N_DEV, depth, CONV_WIDTH, conv_shard).transpose(1, 2, 0, 3).reshape(depth, CONV_WIDTH, REC_WIDTH)

    cos, sin_up, sin_down = _rope_tables(t_len, pad)
    row1 = lambda a: a.reshape(1, -1)

    h = jnp.concatenate([jnp.zeros((n_seq, pad, d), F32), jnp.broadcast_to(meta_full[None], (n_seq, N_META, d)), x], axis=1).reshape(rows, d)
    target = jnp.pad(loss_target, ((0, 0), (pad + N_META, 0), (0, 0))).reshape(rows, d)
    hb = h.astype(BF16)
    saved, wbufs = [], []

    def gathered(l, k, after):
        _, (buf,) = _push_wait(*gathers[(l, k)], after, False, f"gather_wait_{l}_{k}")
        return buf

    for l in range(depth):
        w_ffn1 = gathered(l, 0, chain if l == 0 else h)
        s = {"h0": hb}
        h, s["g1"], s["u1"], s["z1"], hb = _ffn_fwd(h, w_ffn1, ffn1_loc, row1(ln1_g[l]), row1(ln1_b[l]), alpha, "ffn1_fwd")
        s["h1"] = hb
        s["wa"], s["wx"] = _gate_tiles(gate_a_w[l]), _gate_tiles(gate_x_w[l])
        s["sinks"] = jnp.broadcast_to(attn_sinks[l][:, None], (N_Q_HEADS, BLOCK))
        wbuf = gathered(l, 1, h)
        proj = _proj_fwd(h, wbuf, loc["w_in"], "proj_fwd")
        s["proj"] = proj
        s["xc"], s["hrec"], s["y_rec"] = _rglru_fwd(proj, conv_full[l], row1(conv_b[l]), s["wa"], row1(gate_a_b[l]), s["wx"], row1(gate_x_b[l]),
                                                     row1(lru_lambda[l]), t_len, pad, "rglru_fwd")
        s["qkv"] = _rope_fwd(proj, cos, sin_up, sin_down, t_len, "rope_fwd")
        s["y_attn"] = _attn_fwd(s["qkv"], s["sinks"], t_len, pad, "attn_fwd")
        h, s["z2"], hb = _mix_out_fwd(s["y_rec"], s["y_attn"], h, wbuf, loc["w_out"], row1(norm_rec_g[l]), row1(norm_attn_g[l]), row1(ln2_g[l]), row1(ln2_b[l]), alpha, "mix_out_fwd")
        s["h2"] = hb
        w_ffn2 = gathered(l, 2, h)
        h, s["g2"], s["u2"], s["z3"], hb = _ffn_fwd(h, w_ffn2, ffn2_loc, row1(ln3_g[l]), row1(ln3_b[l]), alpha, "ffn2_fwd")
        saved.append(s)
        wbufs.append((w_ffn1, wbuf, w_ffn2))
    loss_part, dh = _loss_head(h, target, t_len, pad + N_META, "loss_head")

    exchanges = {}
    small_grads = [None] * depth
    behind = dh

    def exchange(l, k, bg, chain):
        gs = [bg[name].reshape(N_DEV, -1, d) for name in EXCHANGE_GROUPS[k]]
        zones = [lax.empty(g.shape, g.dtype) for g in gs]
        exchanges[(l, k)] = _push_start(gs, zones, chain, True, f"exchange_start_{l}_{k}")
        return exchanges[(l, k)][2][0]

    for l in reversed(range(depth)):
        (w_ffn1, wbuf, w_ffn2), s = wbufs[l], saved[l]
        bg, sg = {}, {}
        dh, dg, du, act, dyb, sg["ln3_g"], sg["ln3_b"] = _ffn_bwd(dh, s["z3"], s["g2"], s["u2"], w_ffn2, ffn2_loc, row1(ln3_g[l]), behind, alpha, "ffn2_bwd")
        bg["ffn2_w_gate"] = _mm_tn(dg, s["h2"], BF16, "ffn_wgrad_in")
        bg["ffn2_w_up"] = _mm_tn(du, s["h2"], BF16, "ffn_wgrad_in")
        bg["ffn2_w_down"] = _mm_tn(act, dyb, BF16, "ffn_wgrad_down")
        dz, dzb, dy_rec, dy_attn, yn, sg["ln2_g"], sg["ln2_b"], sg["norm_rec_g"], sg["norm_attn_g"] = _mix_out_bwd(
            dh, s["z2"], s["y_rec"], s["y_attn"], wbuf, loc["w_out"], row1(norm_rec_g[l]), row1(norm_attn_g[l]), row1(ln2_g[l]), "mix_out_bwd")
        bg["w_out"] = _mm_tn(yn, dzb, BF16, "w_out_wgrad")
        behind = exchange(l, 0, bg, behind)
        dxr, dgt, sg["conv_w"], sg["conv_b"], dwa, sg["gate_a_b"], dwx, sg["gate_x_b"], sg["lru_lambda"] = _rglru_bwd(
            dy_rec, s["hrec"], s["proj"], s["xc"], conv_full[l], s["wa"], row1(gate_a_b[l]), s["wx"], row1(gate_x_b[l]), row1(lru_lambda[l]), t_len, pad, "rglru_bwd")
        sg["gate_a_w"], sg["gate_x_w"] = _gate_blocks(dwa), _gate_blocks(dwx)
        dq, dk, dv, dsink = _attn_bwd(s["qkv"], s["sinks"], s["y_attn"], dy_attn, t_len, pad, "attn_bwd")
        sg["attn_sinks"] = dsink[:, 0]
        dqkv = _rope_bwd(dq, dk, dv, cos, sin_up, sin_down, t_len, "rope_bwd")
        dproj = jnp.concatenate([dxr, dgt, dqkv], axis=1)
        bg["w_in"] = _mm_tn(dproj, s["h1"], BF16, "w_in_wgrad")
        dh = _proj_bwd(dproj, wbuf, loc["w_in"], dz, alpha, "proj_bwd")
        dh, dg, du, act, dyb, sg["ln1_g"], sg["ln1_b"] = _ffn_bwd(dh, s["z1"], s["g1"], s["u1"], w_ffn1, ffn1_loc, row1(ln1_g[l]), behind, alpha, "ffn1_bwd")
        small_grads[l] = sg
        if l == 0:
            dmeta = _meta_grad(dh, t_len, pad, "meta_grad")
            pieces = [small_grads[j][name].reshape(-1) for j in range(depth) for name in SMALL] + [dmeta.reshape(-1), loss_part[0, :1]]
            sizes = [p.shape[0] for p in pieces]
            flat = jnp.concatenate(pieces)
            width = 1024
            n_rows = -(-flat.shape[0] // (8 * width)) * 8
            flat = jnp.pad(flat, (0, n_rows * width - flat.shape[0])).reshape(n_rows, width)
            zone = lax.dynamic_update_slice(lax.empty((N_DEV, n_rows, width), F32), flat[None], (me, 0, 0))
            small_gather = _push_start([flat], [zone], behind, False, "small_grads_start")
            behind = small_gather[2][0]
        bg["ffn1_w_gate"] = _mm_tn(dg, s["h0"], BF16, "ffn_wgrad_in")
        bg["ffn1_w_up"] = _mm_tn(du, s["h0"], BF16, "ffn_wgrad_in")
        bg["ffn1_w_down"] = _mm_tn(act, dyb, BF16, "ffn_wgrad_down")
        behind = exchange(l, 1, bg, behind)
    grad_x = dh.reshape(n_seq, t_len, d)[:, pad + N_META :, :]

    transposed = dict(BIG)
    outs = {name: [lax.empty(w[name].shape, F32) for _ in range(4)] for name, _ in BIG}

    def reduce_group(l, k, after):
        send_sems, recv_sems, srcs, zones = exchanges[(l, k)]
        owns, parts = _push_wait(send_sems, recv_sems, srcs, zones, after, True, f"exchange_wait_{l}_{k}")
        for name, own, p in zip(EXCHANGE_GROUPS[k], owns, parts):
            outs[name] = _reduce_update(p, own, w[name], given["m_" + name], given["v_" + name], outs[name], l, transposed[name], "reduce_update")
        return outs[EXCHANGE_GROUPS[k][-1]][0]

    grads = {}
    order = [(l, k) for l in reversed(range(depth)) for k in range(len(EXCHANGE_GROUPS))]
    after = behind
    for l, k in order[:-1]:
        after = reduce_group(l, k, after)
    _, (flat_all,) = _push_wait(*small_gather, after, False, "small_grads_wait")
    total = _sum8(flat_all, "sum_small_grads").reshape(-1)
    offs = [0]
    for sz in sizes:
        offs.append(offs[-1] + sz)
    taken = [total[offs[k] : offs[k + 1]] for k in range(len(sizes))]
    for j, name in enumerate(SMALL):
        full_shape = (depth,) + ((CONV_WIDTH, REC_WIDTH) if name == "conv_w" else w[name].shape[1:])
        grads[name] = jnp.stack([taken[l * len(SMALL) + j] for l in range(depth)]).reshape(full_shape)
    grads["conv_w"] = lax.dynamic_slice_in_dim(grads["conv_w"], me * conv_shard, conv_shard, axis=2)
    meta_shard = meta_tokens.shape[1]
    grads["meta_tokens"] = lax.dynamic_slice_in_dim(taken[-2].reshape(N_META, d), me * meta_shard, meta_shard, axis=1)
    loss = taken[-1][0]
    deltas, new_m, new_v = {}, {}, {}
    for name in WEIGHTS:
        if name not in dict(BIG):
            deltas[name], new_m[name], new_v[name] = _adamw(w[name], grads[name], given["m_" + name], given["v_" + name], "adamw")

    reduce_group(*order[-1], new_v[WEIGHTS[-1]])
    for name, _ in BIG:
        grads[name], deltas[name], new_m[name], new_v[name] = outs[name]
    return (loss, grad_x, *[grads[k] for k in WEIGHTS], *[deltas[k] for k in WEIGHTS], *[new_m[k] for k in WEIGHTS], *[new_v[k] for k in WEIGHTS])
```

```python
import functools

import jax
import jax.numpy as jnp
from jax import lax
from jax.experimental import pallas as pl
from jax.experimental.pallas import tpu as pltpu

F32 = jnp.float32
BF16 = jnp.bfloat16
MESH = pl.DeviceIdType.MESH

N_DEV = 8
N_META = 16
BLOCK = 128
WINDOW = 128
REC_WIDTH = 512
ATTN_WIDTH = 512
KV_WIDTH = 128
HEAD_DIM = 64
N_Q_HEADS = 8
ROPE_DIM = 16
ROPE_THETA = 500000.0
CONV_WIDTH = 4
LRU_C = 8.0
LN_EPS = 1e-5
RMS_EPS = 1e-6
NEG_INF = -1e30
ADAM_LR, ADAM_B1, ADAM_B2, ADAM_EPS, ADAM_WD, ADAM_STEP = 0.001, 0.9, 0.999, 1e-08, 0.01, 10

NT = (((1,), (1,)), ((), ()))
TN = (((0,), (0,)), ((), ()))
VMEM_BIG = 56 * 1024 * 1024


def _params(vmem=None):
    return pltpu.CompilerParams(vmem_limit_bytes=vmem)


def _row_tile(rows):
    return 256 if rows % 256 == 0 else 128


def _sigmoid(x):
    return 1.0 / (1.0 + jnp.exp(-x))


def _expm1_nonpos(x):
    series = x * (1.0 + 0.5 * x * (1.0 + x / 3.0 * (1.0 + 0.25 * x * (1.0 + 0.2 * x))))
    return jnp.where(x > -0.05, series, jnp.exp(x) - 1.0)


def _softplus(x):
    e = jnp.exp(-jnp.abs(x))
    log1p = jnp.where(e < 1e-3, e * (1.0 - e * (0.5 - e / 3.0)), jnp.log(1.0 + e))
    return jnp.maximum(x, 0.0) + log1p


def _gelu_and_grad(x):
    c0 = 0.7978845608028654
    x2 = x * x
    t = jnp.tanh(c0 * (x + 0.044715 * x * x2))
    gelu = 0.5 * x * (1.0 + t)
    grad = 0.5 * (1.0 + t) + 0.5 * x * (1.0 - t * t) * c0 * (1.0 + 3.0 * 0.044715 * x2)
    return gelu, grad


def _ln_stats(z):
    mu = jnp.mean(z, axis=-1, keepdims=True)
    zc = z - mu
    var = jnp.mean(zc * zc, axis=-1, keepdims=True)
    rstd = lax.rsqrt(var + LN_EPS)
    return zc * rstd, rstd


def _ln_bwd(dy, z, gamma):
    xhat, rstd = _ln_stats(z)
    dxh = dy * gamma
    m1 = jnp.mean(dxh, axis=-1, keepdims=True)
    m2 = jnp.mean(dxh * xhat, axis=-1, keepdims=True)
    dz = rstd * (dxh - m1 - xhat * m2)
    return dz, jnp.sum(dy * xhat, axis=0, keepdims=True), jnp.sum(dy, axis=0, keepdims=True)


def _rms(y, gamma):
    r = lax.rsqrt(jnp.mean(y * y, axis=-1, keepdims=True) + RMS_EPS)
    return y * r * gamma


def _rms_bwd(dout, y, gamma):
    r = lax.rsqrt(jnp.mean(y * y, axis=-1, keepdims=True) + RMS_EPS)
    n = y * r
    dn = dout * gamma
    dy = r * (dn - n * jnp.mean(dn * n, axis=-1, keepdims=True))
    return dy, jnp.sum(dout * n, axis=0, keepdims=True), n * gamma


def _load_blocks(w_hbm, items, sems):
    @pl.when(pl.program_id(0) == 0)
    def _():
        copies = []
        for k, (off, nr, dst) in enumerate(items):
            for dev in range(N_DEV):
                copies.append(pltpu.make_async_copy(w_hbm.at[dev, pl.ds(off, nr), :], dst.at[pl.ds(dev * nr, nr), :], sems.at[k, dev]))
        for cp in copies:
            cp.start()
        for cp in copies:
            cp.wait()


def _full(shape):
    return pl.BlockSpec(shape, lambda *_: (0,) * len(shape))


ANY = pl.BlockSpec(memory_space=pl.ANY)


def _all_gather(xs, after, name):
    n = len(xs)

    def body(*refs):
        ins, outs = refs[:n], refs[n + 1 : 2 * n + 1]
        send_sems, recv_sems, local_sems = refs[2 * n + 1 :]
        x, y, c = lax.axis_index("x"), lax.axis_index("y"), lax.axis_index("c")
        me, sibling = (x, y, c), (x, y, 1 - c)
        chips = [(1 - x, y), (x, 1 - y), (1 - x, 1 - y)]

        def slot(i, dev):
            return outs[i].at[4 * dev[0] + 2 * dev[1] + dev[2]]

        def copy(i, k, block, to, src=None):
            return pltpu.make_async_remote_copy(
                src_ref=slot(i, block) if src is None else src,
                dst_ref=slot(i, block),
                send_sem=send_sems.at[i, k],
                recv_sem=recv_sems.at[i, k],
                device_id=to,
                device_id_type=MESH,
            )

        mine = [pltpu.make_async_copy(ins[i], slot(i, me), local_sems.at[i]) for i in range(n)]
        for cp in mine:
            cp.start()
        first = []
        for i in range(n):
            first.append(copy(i, 0, me, sibling, src=ins[i]))
            first += [copy(i, 1 + j, me, (*chip, c), src=ins[i]) for j, chip in enumerate(chips)]
        for cp in first:
            cp.start()
        passed = []
        for j, chip in enumerate(chips):
            for i in range(n):
                copy(i, 1 + j, (*chip, c), me).wait_recv()
                fwd = copy(i, 4 + j, (*chip, c), sibling)
                fwd.start()
                passed.append(fwd)
        for i in range(n):
            copy(i, 0, sibling, me).wait_recv()
            for j, chip in enumerate(chips):
                copy(i, 4 + j, (*chip, 1 - c), me).wait_recv()
        for cp in first + passed:
            cp.wait_send()
        for cp in mine:
            cp.wait()

    return pl.pallas_call(
        body,
        name=name,
        out_shape=[jax.ShapeDtypeStruct((N_DEV,) + a.shape, a.dtype) for a in xs],
        in_specs=[ANY] * (n + 1),
        out_specs=[ANY] * n,
        scratch_shapes=[pltpu.SemaphoreType.DMA((n, 7)), pltpu.SemaphoreType.DMA((n, 7)), pltpu.SemaphoreType.DMA((n,))],
    )(*xs, after)


HBM = pl.BlockSpec(memory_space=pltpu.HBM)
SEM = pl.BlockSpec(memory_space=pltpu.SEMAPHORE)
EFFECT = pltpu.SideEffectType.DATAFLOW_SIDE_EFFECTING


def _push_copies(ins, lands, send_sems, recv_sems, scatter):
    x, y, c = lax.axis_index("x"), lax.axis_index("y"), lax.axis_index("c")
    me = 4 * x + 2 * y + c
    copies = []
    for k in range(1, N_DEV):
        px = 1 - x if (k >> 2) & 1 else x
        py = 1 - y if (k >> 1) & 1 else y
        pc = 1 - c if k & 1 else c
        for i in range(len(ins)):
            copies.append(
                pltpu.make_async_remote_copy(
                    src_ref=ins[i].at[4 * px + 2 * py + pc] if scatter else ins[i],
                    dst_ref=lands[i].at[me],
                    send_sem=send_sems.at[i * (N_DEV - 1) + k - 1],
                    recv_sem=recv_sems.at[i * (N_DEV - 1) + k - 1],
                    device_id=(px, py, pc),
                    device_id_type=MESH,
                )
            )
    return copies


def _push_start(xs, lands, chain, scatter, name):
    n = len(xs)

    def body(*refs):
        ins, zones = refs[:n], refs[n : 2 * n]
        for cp in _push_copies(ins, zones, refs[2 * n + 1], refs[2 * n + 2], scatter):
            cp.start()

    outs = pl.pallas_call(
        body,
        name=name,
        out_shape=(
            pltpu.SemaphoreType.DMA((n * (N_DEV - 1),)),
            pltpu.SemaphoreType.DMA((n * (N_DEV - 1),)),
            *[pltpu.HBM(a.shape, a.dtype) for a in list(xs) + list(lands)],
        ),
        in_specs=[HBM] * (2 * n) + [ANY],
        out_specs=(SEM, SEM, *[HBM] * (2 * n)),
        input_output_aliases={i: 2 + i for i in range(2 * n)},
        compiler_params=pltpu.CompilerParams(has_side_effects=EFFECT),
    )(*[pltpu.with_memory_space_constraint(a, pltpu.HBM) for a in list(xs) + list(lands)], chain)
    return outs[0], outs[1], list(outs[2 : 2 + n]), list(outs[2 + n : 2 + 2 * n])


def _push_wait(send_sems, recv_sems, srcs, lands, after, scatter, name):
    n = len(srcs)

    def body(*refs):
        ins, zones = refs[:n], refs[n : 2 * n]
        for cp in _push_copies(ins, zones, refs[2 * n], refs[2 * n + 1], scatter):
            cp.wait_send()
            cp.wait_recv()

    outs = pl.pallas_call(
        body,
        name=name,
        out_shape=[pltpu.HBM(a.shape, a.dtype) for a in srcs + lands],
        in_specs=[HBM] * (2 * n) + [SEM, SEM, ANY],
        out_specs=[HBM] * (2 * n),
        input_output_aliases={i: i for i in range(2 * n)},
        compiler_params=pltpu.CompilerParams(has_side_effects=EFFECT),
    )(*srcs, *lands, send_sems, recv_sems, after)
    return list(outs[:n]), list(outs[n:])


def _sum8(parts, name):
    _, rows, cols = parts.shape
    tr = rows
    for cand in (512, 256, 128, 64, 32, 16):
        if rows % cand == 0 and rows > cand:
            tr = cand
            break

    def body(p_ref, o_ref):
        acc = p_ref[0].astype(F32)
        for d in range(1, N_DEV):
            acc = acc + p_ref[d].astype(F32)
        o_ref[...] = acc

    return pl.pallas_call(
        body,
        name=name,
        grid=(rows // tr,),
        in_specs=[pl.BlockSpec((N_DEV, tr, cols), lambda i: (0, i, 0))],
        out_specs=pl.BlockSpec((tr, cols), lambda i: (i, 0)),
        out_shape=jax.ShapeDtypeStruct((rows, cols), F32),
    )(parts)


def _mm_tn(a, b, out_dtype, name):
    rows, m = a.shape
    n = b.shape[1]
    tm = m // 2 if (m // 2) % 128 == 0 and m > 512 else m
    tr = next(t for t in (2176, 1024, 512, 256, 128) if rows % t == 0)
    nk = rows // tr

    def body(a_ref, b_ref, o_ref, acc):
        k = pl.program_id(1)
        part = lax.dot_general(a_ref[...].astype(BF16), b_ref[...].astype(BF16), TN, preferred_element_type=F32)

        @pl.when(k == 0)
        def _():
            acc[...] = part

        @pl.when(k > 0)
        def _():
            acc[...] += part

        @pl.when(k == nk - 1)
        def _():
            o_ref[...] = acc[...].astype(o_ref.dtype)

    return pl.pallas_call(
        body,
        name=name,
        grid=(m // tm, nk),
        in_specs=[pl.BlockSpec((tr, tm), lambda i, k: (k, i)), pl.BlockSpec((tr, n), lambda i, k: (k, 0))],
        out_specs=pl.BlockSpec((tm, n), lambda i, k: (i, 0)),
        out_shape=jax.ShapeDtypeStruct((m, n), out_dtype),
        scratch_shapes=[pltpu.VMEM((tm, n), F32)],
        compiler_params=_params(VMEM_BIG),
    )(a, b)


def _weight_scratch(locs, d):
    return [pltpu.VMEM((N_DEV * nr, d), BF16) for _, nr in locs] + [pltpu.SemaphoreType.DMA((len(locs), N_DEV))]


def _proj_fwd(h, wbuf, loc, name):
    rows, d = h.shape
    n = N_DEV * loc[1]
    tm = _row_tile(rows)

    def body(h_ref, w_hbm, o_ref, w, sems):
        _load_blocks(w_hbm, [(*loc, w)], sems)
        o_ref[...] = lax.dot_general(h_ref[...].astype(BF16), w[...], NT, preferred_element_type=F32)

    return pl.pallas_call(
        body,
        name=name,
        grid=(rows // tm,),
        in_specs=[pl.BlockSpec((tm, d), lambda i: (i, 0)), ANY],
        out_specs=pl.BlockSpec((tm, n), lambda i: (i, 0)),
        out_shape=jax.ShapeDtypeStruct((rows, n), F32),
        scratch_shapes=_weight_scratch([loc], d),
        compiler_params=_params(VMEM_BIG),
    )(h, wbuf)


def _proj_bwd(dproj, wbuf, loc, dz, alpha, name):
    rows, n = dproj.shape
    d = dz.shape[1]
    tm = _row_tile(rows)

    def body(dp_ref, w_hbm, dz_ref, o_ref, w, sems):
        _load_blocks(w_hbm, [(*loc, w)], sems)
        o_ref[...] = alpha * dz_ref[...] + jnp.dot(dp_ref[...], w[...], preferred_element_type=F32)

    return pl.pallas_call(
        body,
        name=name,
        grid=(rows // tm,),
        in_specs=[pl.BlockSpec((tm, n), lambda i: (i, 0)), ANY, pl.BlockSpec((tm, d), lambda i: (i, 0))],
        out_specs=pl.BlockSpec((tm, d), lambda i: (i, 0)),
        out_shape=jax.ShapeDtypeStruct((rows, d), F32),
        scratch_shapes=_weight_scratch([loc], d),
        compiler_params=_params(VMEM_BIG),
    )(dproj, wbuf, dz)


FFN_CHUNKS = 1


def _ffn_fwd(h, wbuf, locs, ln_g, ln_b, alpha, name):
    rows, d = h.shape
    f = N_DEV * locs[0][1]
    tm = _row_tile(rows)
    tf = f // FFN_CHUNKS

    def body(h_ref, w_hbm, lg_ref, lb_ref, ho_ref, g_ref, u_ref, z_ref, hb_ref, wg, wu, wdv, sems):
        _load_blocks(w_hbm, [(*locs[0], wg), (*locs[1], wu), (*locs[2], wdv)], sems)
        x = h_ref[...]
        xb = x.astype(BF16)
        y = None
        for c in range(FFN_CHUNKS):
            sl = slice(c * tf, (c + 1) * tf)
            g = lax.dot_general(xb, wg[sl, :], NT, preferred_element_type=F32)
            u = lax.dot_general(xb, wu[sl, :], NT, preferred_element_type=F32)
            g_ref[:, sl] = g
            u_ref[:, sl] = u
            a = (g * _sigmoid(g) * u).astype(BF16)
            part = jnp.dot(a, wdv[sl, :], preferred_element_type=F32)
            y = part if y is None else y + part
        z = alpha * x + 0.5 * y
        z_ref[...] = z
        xhat, _ = _ln_stats(z)
        ho = xhat * lg_ref[...] + lb_ref[...]
        ho_ref[...] = ho
        hb_ref[...] = ho.astype(BF16)

    row_d = pl.BlockSpec((tm, d), lambda i: (i, 0))
    row_f = pl.BlockSpec((tm, f), lambda i: (i, 0))
    return pl.pallas_call(
        body,
        name=name,
        grid=(rows // tm,),
        in_specs=[row_d, ANY, _full((1, d)), _full((1, d))],
        out_specs=[row_d, row_f, row_f, row_d, row_d],
        out_shape=[
            jax.ShapeDtypeStruct((rows, d), F32),
            jax.ShapeDtypeStruct((rows, f), F32),
            jax.ShapeDtypeStruct((rows, f), F32),
            jax.ShapeDtypeStruct((rows, d), F32),
            jax.ShapeDtypeStruct((rows, d), BF16),
        ],
        scratch_shapes=_weight_scratch(locs, d),
        compiler_params=_params(VMEM_BIG),
    )(h, wbuf, ln_g, ln_b)


def _ffn_bwd(dh, z, g, u, wbuf, locs, ln_g, after, alpha, name):
    rows, d = dh.shape
    f = N_DEV * locs[0][1]
    tm = _row_tile(rows)
    tf = f // FFN_CHUNKS

    def body(dh_ref, z_ref, g_ref, u_ref, w_hbm, lg_ref, after_ref, dx_ref, dg_ref, du_ref, a_ref, dyb_ref, dgam_ref, dbet_ref, wg, wu, wdv, sems):
        _load_blocks(w_hbm, [(*locs[0], wg), (*locs[1], wu), (*locs[2], wdv)], sems)

        @pl.when(pl.program_id(0) == 0)
        def _():
            dgam_ref[...] = jnp.zeros_like(dgam_ref)
            dbet_ref[...] = jnp.zeros_like(dbet_ref)

        dz, dgam, dbet = _ln_bwd(dh_ref[...], z_ref[...], lg_ref[...])
        dgam_ref[...] += dgam
        dbet_ref[...] += dbet
        dyb = (0.5 * dz).astype(BF16)
        dyb_ref[...] = dyb
        dx = alpha * dz
        for c in range(FFN_CHUNKS):
            sl = slice(c * tf, (c + 1) * tf)
            da = lax.dot_general(dyb, wdv[sl, :], NT, preferred_element_type=F32)
            gc = g_ref[:, sl]
            uc = u_ref[:, sl]
            sg = _sigmoid(gc)
            silu = gc * sg
            dgc = (da * uc * (sg * (1.0 + gc * (1.0 - sg)))).astype(BF16)
            duc = (da * silu).astype(BF16)
            a_ref[:, sl] = (silu * uc).astype(BF16)
            dg_ref[:, sl] = dgc
            du_ref[:, sl] = duc
            dx = dx + jnp.dot(dgc, wg[sl, :], preferred_element_type=F32) + jnp.dot(duc, wu[sl, :], preferred_element_type=F32)
        dx_ref[...] = dx

    row_d = pl.BlockSpec((tm, d), lambda i: (i, 0))
    row_f = pl.BlockSpec((tm, f), lambda i: (i, 0))
    return pl.pallas_call(
        body,
        name=name,
        grid=(rows // tm,),
        in_specs=[row_d, row_d, row_f, row_f, ANY, _full((1, d)), ANY],
        out_specs=[row_d, row_f, row_f, row_f, row_d, _full((1, d)), _full((1, d))],
        out_shape=[
            jax.ShapeDtypeStruct((rows, d), F32),
            jax.ShapeDtypeStruct((rows, f), BF16),
            jax.ShapeDtypeStruct((rows, f), BF16),
            jax.ShapeDtypeStruct((rows, f), BF16),
            jax.ShapeDtypeStruct((rows, d), BF16),
            jax.ShapeDtypeStruct((1, d), F32),
            jax.ShapeDtypeStruct((1, d), F32),
        ],
        scratch_shapes=_weight_scratch(locs, d),
        compiler_params=_params(VMEM_BIG),
    )(dh, z, g, u, wbuf, ln_g, after)


LANES = 128


def _gates(xc, wa_ref, ba, wx_ref, bx, lam):
    xcb = xc.astype(BF16)
    r = _sigmoid(jnp.dot(xcb, wa_ref[0], preferred_element_type=F32) + ba)
    i = _sigmoid(jnp.dot(xcb, wx_ref[0], preferred_element_type=F32) + bx)
    sp = _softplus(-lam)
    log_a = -LRU_C * sp * r
    a = jnp.exp(log_a)
    s = jnp.sqrt(-_expm1_nonpos(2.0 * log_a))
    return xcb, r, i, sp, a, s


def _conv(xr, cw, cb):
    return cb + cw[3:4] * xr + cw[2:3] * pltpu.roll(xr, 1, 0) + cw[1:2] * pltpu.roll(xr, 2, 0) + cw[0:1] * pltpu.roll(xr, 3, 0)


def _rglru_fwd(proj, cw, cb, wa, ba, wx, bx, lam, t_len, pad, name):
    rows = proj.shape[0]
    n_seq = rows // t_len
    n_tile = REC_WIDTH // LANES
    groups = t_len // 8

    def body(xr_ref, gt_ref, cw_ref, cb_ref, wa_ref, ba_ref, wx_ref, bx_ref, lam_ref, xc_ref, h_ref, y_ref, a_s, b_s):
        valid = lax.broadcasted_iota(jnp.int32, (t_len, 1), 0) >= pad
        xr = jnp.where(valid, xr_ref[...], 0.0)
        xc = _conv(xr, cw_ref[...], cb_ref[...])
        xc_ref[...] = xc
        _, _, i, _, a, s = _gates(xc, wa_ref, ba_ref[...], wx_ref, bx_ref[...], lam_ref[...])
        a_s[...] = a
        b_s[...] = jnp.where(valid, s * (i * xc), 0.0)
        sub = lax.broadcasted_iota(jnp.int32, (8, LANES), 0)

        def step(gi, carry):
            r0 = pl.multiple_of(gi * 8, 8)
            av = a_s[pl.ds(r0, 8), :]
            bv = b_s[pl.ds(r0, 8), :]
            for d in (1, 2, 4):
                m = sub >= d
                a_sh = pltpu.roll(av, d, 0)
                b_sh = pltpu.roll(bv, d, 0)
                bv = jnp.where(m, av * b_sh + bv, bv)
                av = jnp.where(m, av * a_sh, av)
            hv = av * carry + bv
            h_ref[pl.ds(r0, 8), :] = hv
            return jnp.broadcast_to(hv[7:8, :], (8, LANES))

        lax.fori_loop(0, groups, step, jnp.zeros((8, LANES), F32), unroll=4)
        gelu, _ = _gelu_and_grad(gt_ref[...])
        y_ref[...] = h_ref[...] * gelu

    seq_tile = lambda col0: pl.BlockSpec((t_len, LANES), lambda s, j: (s, col0 + j))
    vec = pl.BlockSpec((1, LANES), lambda s, j: (0, j))
    out = jax.ShapeDtypeStruct((rows, REC_WIDTH), F32)
    return pl.pallas_call(
        body,
        name=name,
        grid=(n_seq, n_tile),
        in_specs=[
            seq_tile(0),
            seq_tile(n_tile),
            pl.BlockSpec((CONV_WIDTH, LANES), lambda s, j: (0, j)),
            vec,
            pl.BlockSpec((1, LANES, LANES), lambda s, j: (j, 0, 0)),
            vec,
            pl.BlockSpec((1, LANES, LANES), lambda s, j: (j, 0, 0)),
            vec,
            vec,
        ],
        out_specs=[seq_tile(0)] * 3,
        out_shape=[out, out, out],
        scratch_shapes=[pltpu.VMEM((t_len, LANES), F32), pltpu.VMEM((t_len, LANES), F32)],
        compiler_params=_params(VMEM_BIG),
    )(proj, proj, cw, cb, wa, ba, wx, bx, lam)


def _rglru_bwd(dy, h, proj, xc, cw, wa, ba, wx, bx, lam, t_len, pad, name):
    rows = proj.shape[0]
    n_seq = rows // t_len
    n_tile = REC_WIDTH // LANES
    groups = t_len // 8

    def body(dy_ref, h_ref, xr_ref, gt_ref, xc_ref, cw_ref, wa_ref, ba_ref, wx_ref, bx_ref, lam_ref,
             dxr_ref, dgt_ref, dcw_ref, dcb_ref, dwa_ref, dba_ref, dwx_ref, dbx_ref, dlam_ref, a_s, c_s, db_s, da_s):
        first_seq = pl.program_id(1) == 0

        @pl.when(first_seq)
        def _():
            for ref in (dcw_ref, dcb_ref, dwa_ref, dba_ref, dwx_ref, dbx_ref, dlam_ref):
                ref[...] = jnp.zeros_like(ref)

        valid = lax.broadcasted_iota(jnp.int32, (t_len, 1), 0) >= pad
        gelu, dgelu = _gelu_and_grad(gt_ref[...])
        dyv = dy_ref[...]
        dho = dyv * gelu
        dgt_ref[...] = (dyv * h_ref[...] * dgelu).astype(BF16)
        xc = xc_ref[...]
        lam = lam_ref[...]
        xcb, r, i, sp, a, s = _gates(xc, wa_ref, ba_ref[...], wx_ref, bx_ref[...], lam)
        a_s[...] = a
        c_s[...] = a * dho
        db_s[...] = dho
        sub = lax.broadcasted_iota(jnp.int32, (8, LANES), 0)

        def step(k, carry):
            gi = groups - 1 - k
            r0 = pl.multiple_of(gi * 8, 8)
            av = a_s[pl.ds(r0, 8), :]
            cv = c_s[pl.ds(r0, 8), :]
            for d in (1, 2, 4):
                m = sub < 8 - d
                a_sh = pltpu.roll(av, 8 - d, 0)
                c_sh = pltpu.roll(cv, 8 - d, 0)
                cv = jnp.where(m, av * c_sh + cv, cv)
                av = jnp.where(m, av * a_sh, av)
            ev = av * carry + cv
            e_next = jnp.where(sub < 7, pltpu.roll(ev, 7, 0), carry)
            dht = db_s[pl.ds(r0, 8), :] + e_next
            hv = h_ref[pl.ds(r0, 8), :]
            rp = pl.multiple_of(jnp.maximum(gi - 1, 0) * 8, 8)
            h_before = jnp.where(gi > 0, jnp.broadcast_to(h_ref[pl.ds(rp, 8), :][7:8, :], (8, LANES)), 0.0)
            h_prev = jnp.where(sub >= 1, pltpu.roll(hv, 1, 0), h_before)
            db_s[pl.ds(r0, 8), :] = dht
            da_s[pl.ds(r0, 8), :] = dht * h_prev
            return jnp.broadcast_to(ev[0:1, :], (8, LANES))

        lax.fori_loop(0, groups, step, jnp.zeros((8, LANES), F32), unroll=4)

        db = jnp.where(valid, db_s[...], 0.0)
        da = da_s[...]
        ds = db * (i * xc)
        di = db * s * xc
        dxc = db * s * i
        dlog_a = da * a - ds * (a * a) / jnp.maximum(s, 1e-30)
        dr = dlog_a * (-LRU_C * sp)
        dsp = jnp.sum(dlog_a * (-LRU_C) * r, axis=0, keepdims=True)
        dlam_ref[...] += dsp * (-_sigmoid(-lam))
        dpr = dr * r * (1.0 - r)
        dpi = di * i * (1.0 - i)
        dprb = dpr.astype(BF16)
        dpib = dpi.astype(BF16)
        dxc = dxc + lax.dot_general(dprb, wa_ref[0], NT, preferred_element_type=F32) + lax.dot_general(dpib, wx_ref[0], NT, preferred_element_type=F32)
        dwa_ref[0] += lax.dot_general(xcb, dprb, TN, preferred_element_type=F32)
        dwx_ref[0] += lax.dot_general(xcb, dpib, TN, preferred_element_type=F32)
        dba_ref[...] += jnp.sum(dpr, axis=0, keepdims=True)
        dbx_ref[...] += jnp.sum(dpi, axis=0, keepdims=True)
        dxc = jnp.where(valid, dxc, 0.0)
        xr = jnp.where(valid, xr_ref[...], 0.0)
        dcb_ref[...] += jnp.sum(dxc, axis=0, keepdims=True)
        for k in range(CONV_WIDTH):
            shifted = xr if k == CONV_WIDTH - 1 else pltpu.roll(xr, CONV_WIDTH - 1 - k, 0)
            dcw_ref[k : k + 1, :] += jnp.sum(dxc * shifted, axis=0, keepdims=True)
        cw = cw_ref[...]
        dxr = cw[3:4] * dxc + cw[2:3] * pltpu.roll(dxc, t_len - 1, 0) + cw[1:2] * pltpu.roll(dxc, t_len - 2, 0) + cw[0:1] * pltpu.roll(dxc, t_len - 3, 0)
        dxr_ref[...] = jnp.where(valid, dxr, 0.0).astype(BF16)

    seq_tile = lambda col0: pl.BlockSpec((t_len, LANES), lambda j, s: (s, col0 + j))
    vec = pl.BlockSpec((1, LANES), lambda j, s: (0, j))
    mat = pl.BlockSpec((1, LANES, LANES), lambda j, s: (j, 0, 0))
    cwb = pl.BlockSpec((CONV_WIDTH, LANES), lambda j, s: (0, j))
    big = jax.ShapeDtypeStruct((rows, REC_WIDTH), BF16)
    vec_shape = jax.ShapeDtypeStruct((1, REC_WIDTH), F32)
    mat_shape = jax.ShapeDtypeStruct((n_tile, LANES, LANES), F32)
    return pl.pallas_call(
        body,
        name=name,
        grid=(n_tile, n_seq),
        in_specs=[seq_tile(0), seq_tile(0), seq_tile(0), seq_tile(n_tile), seq_tile(0), cwb, mat, vec, mat, vec, vec],
        out_specs=[seq_tile(0), seq_tile(0), cwb, vec, mat, vec, mat, vec, vec],
        out_shape=[big, big, jax.ShapeDtypeStruct((CONV_WIDTH, REC_WIDTH), F32), vec_shape, mat_shape, vec_shape, mat_shape, vec_shape, vec_shape],
        scratch_shapes=[pltpu.VMEM((t_len, LANES), F32)] * 4,
        compiler_params=_params(VMEM_BIG),
    )(dy, h, proj, proj, xc, cw, wa, ba, wx, bx, lam)


QKV_WIDTH = ATTN_WIDTH + 2 * KV_WIDTH


def _rotate(v, cos, sin_up, sin_down):
    width = v.shape[1]
    return v * cos + pltpu.roll(v, width - ROPE_DIM // 2, 1) * sin_up + pltpu.roll(v, ROPE_DIM // 2, 1) * sin_down


def _rope_rows(t_len):
    return t_len // 4 if t_len % 64 == 0 else BLOCK


def _rope_fwd(proj, cos, sin_up, sin_down, t_len, name):
    rows = proj.shape[0]
    rb = _rope_rows(t_len)
    nb = t_len // rb
    n_seq = rows // t_len
    q0 = 2 * REC_WIDTH

    def body(q_ref, k_ref, v_ref, cos_ref, up_ref, down_ref, o_ref):
        cos_t, up_t, down_t = cos_ref[...], up_ref[...], down_ref[...]
        o_ref[:, 0:ATTN_WIDTH] = _rotate(q_ref[...], cos_t, up_t, down_t).astype(BF16)
        o_ref[:, ATTN_WIDTH : ATTN_WIDTH + KV_WIDTH] = _rotate(k_ref[...], cos_t[:, :KV_WIDTH], up_t[:, :KV_WIDTH], down_t[:, :KV_WIDTH]).astype(BF16)
        o_ref[:, ATTN_WIDTH + KV_WIDTH :] = v_ref[...].astype(BF16)

    tab = pl.BlockSpec((rb, ATTN_WIDTH), lambda s, n: (n, 0))
    return pl.pallas_call(
        body,
        name=name,
        grid=(n_seq, nb),
        in_specs=[
            pl.BlockSpec((rb, ATTN_WIDTH), lambda s, n: (s * nb + n, q0 // ATTN_WIDTH)),
            pl.BlockSpec((rb, KV_WIDTH), lambda s, n: (s * nb + n, (q0 + ATTN_WIDTH) // KV_WIDTH)),
            pl.BlockSpec((rb, KV_WIDTH), lambda s, n: (s * nb + n, (q0 + ATTN_WIDTH) // KV_WIDTH + 1)),
            tab,
            tab,
            tab,
        ],
        out_specs=pl.BlockSpec((rb, QKV_WIDTH), lambda s, n: (s * nb + n, 0)),
        out_shape=jax.ShapeDtypeStruct((rows, QKV_WIDTH), BF16),
    )(proj, proj, proj, cos, sin_up, sin_down)


def _rope_bwd(dq, dk, dv, cos, sin_up, sin_down, t_len, name):
    rows = dq.shape[0]
    rb = _rope_rows(t_len)
    nb = t_len // rb
    n_seq = rows // t_len

    def body(dq_ref, dk_ref, dv_ref, cos_ref, up_ref, down_ref, o_ref):
        cos_t, up_t, down_t = cos_ref[...], -up_ref[...], -down_ref[...]
        o_ref[:, 0:ATTN_WIDTH] = _rotate(dq_ref[...], cos_t, up_t, down_t).astype(BF16)
        o_ref[:, ATTN_WIDTH : ATTN_WIDTH + KV_WIDTH] = _rotate(dk_ref[...], cos_t[:, :KV_WIDTH], up_t[:, :KV_WIDTH], down_t[:, :KV_WIDTH]).astype(BF16)
        o_ref[:, ATTN_WIDTH + KV_WIDTH :] = dv_ref[...].astype(BF16)

    tab = pl.BlockSpec((rb, ATTN_WIDTH), lambda s, n: (n, 0))
    blk = lambda w: pl.BlockSpec((rb, w), lambda s, n: (s * nb + n, 0))
    return pl.pallas_call(
        body,
        name=name,
        grid=(n_seq, nb),
        in_specs=[blk(ATTN_WIDTH), blk(KV_WIDTH), blk(KV_WIDTH), tab, tab, tab],
        out_specs=blk(QKV_WIDTH),
        out_shape=jax.ShapeDtypeStruct((rows, QKV_WIDTH), BF16),
    )(dq, dk, dv, cos, sin_up, sin_down)


GROUP = 4


def _attn_mask(n, pad):
    q_pos = n * BLOCK + (lax.broadcasted_iota(jnp.int32, (GROUP * BLOCK, 1), 0) & (BLOCK - 1)) - pad
    col = lax.broadcasted_iota(jnp.int32, (1, 3 * BLOCK), 1)
    first = col < BLOCK
    k_pos = col - pad + jnp.where(first, 0, (n - 2) * BLOCK)
    dist = q_pos - k_pos
    band = (dist >= 0) & (dist < WINDOW) & (k_pos >= N_META) & jnp.logical_not(first)
    meta = (k_pos >= 0) & (k_pos < N_META) & (k_pos <= q_pos) & first
    return band | meta


def _in_half(e):
    lane = lax.broadcasted_iota(jnp.int32, (1, BLOCK), 1)
    return lane >= HEAD_DIM if e else lane < HEAD_DIM


def _head_views(q2, e, g):
    v = jnp.where(_in_half(e), q2, 0.0)
    return pltpu.roll(v, HEAD_DIM, 1) if e != g else v


def _tile(g, t):
    j = (GROUP // 2) * g + t
    return slice(j * BLOCK, (j + 1) * BLOCK)


def _stack_rows(ref, g):
    parts = []
    for t in range(GROUP // 2):
        tile = ref[:, _tile(g, t)].astype(F32)
        parts += [_head_views(tile, e, g) for e in range(2)]
    return jnp.concatenate(parts, axis=0)


def _stack_group(q_ref, sink_ref, g):
    head = lax.broadcasted_iota(jnp.int32, (GROUP * BLOCK, 1), 0) >> (BLOCK.bit_length() - 1)
    sink = jnp.zeros((GROUP * BLOCK, 1), F32)
    for hh in range(GROUP):
        sink = jnp.where(head == hh, sink_ref[GROUP * g + hh : GROUP * g + hh + 1, 0:1], sink)
    return _stack_rows(q_ref, g), sink


def _unstack_pair(v, g, t):
    out = None
    for e in range(2):
        blk = v[(2 * t + e) * BLOCK : (2 * t + e + 1) * BLOCK, :]
        blk = pltpu.roll(blk, HEAD_DIM, 1) if e != g else blk
        out = blk if out is None else out + blk
    return out


def _softmax_with_sink(scores, mask, sink):
    s = jnp.where(mask, scores * (HEAD_DIM**-0.5), NEG_INF)
    m = jnp.maximum(jnp.max(s, axis=-1, keepdims=True), sink)
    p = jnp.exp(s - m)
    p_sink = jnp.exp(sink - m)
    inv = 1.0 / (jnp.sum(p, axis=-1, keepdims=True) + p_sink)
    return p * inv, p_sink * inv


def _kv_specs(nb):
    k_col = ATTN_WIDTH // KV_WIDTH
    specs = []
    for col in (k_col, k_col + 1):
        specs += [
            pl.BlockSpec((BLOCK, KV_WIDTH), lambda s, n, col=col: (s * nb, col)),
            pl.BlockSpec((BLOCK, KV_WIDTH), lambda s, n, col=col: (s * nb + jnp.maximum(n - 1, 0), col)),
            pl.BlockSpec((BLOCK, KV_WIDTH), lambda s, n, col=col: (s * nb + n, col)),
        ]
    return specs


def _attn_fwd(qkv, sinks, t_len, pad, name):
    rows = qkv.shape[0]
    nb = t_len // BLOCK
    n_seq = rows // t_len

    def body(q_ref, km_ref, kp_ref, kc_ref, vm_ref, vp_ref, vc_ref, sink_ref, o_ref):
        n = pl.program_id(1)
        mask = _attn_mask(n, pad)
        keys = jnp.concatenate([km_ref[...], kp_ref[...], kc_ref[...]], axis=0)
        vals = jnp.concatenate([vm_ref[...], vp_ref[...], vc_ref[...]], axis=0)
        for g in range(N_Q_HEADS // GROUP):
            qs, sink = _stack_group(q_ref, sink_ref, g)
            scores = lax.dot_general(qs.astype(BF16), keys, NT, preferred_element_type=F32)
            p, _ = _softmax_with_sink(scores, mask, sink)
            o = jnp.where(_in_half(g), jnp.dot(p.astype(BF16), vals, preferred_element_type=F32), 0.0)
            for t in range(GROUP // 2):
                o_ref[:, _tile(g, t)] = _unstack_pair(o, g, t)

    return pl.pallas_call(
        body,
        name=name,
        grid=(n_seq, nb),
        in_specs=[pl.BlockSpec((BLOCK, ATTN_WIDTH), lambda s, n: (s * nb + n, 0))] + _kv_specs(nb) + [_full((N_Q_HEADS, BLOCK))],
        out_specs=pl.BlockSpec((BLOCK, ATTN_WIDTH), lambda s, n: (s * nb + n, 0)),
        out_shape=jax.ShapeDtypeStruct((rows, ATTN_WIDTH), F32),
    )(qkv, qkv, qkv, qkv, qkv, qkv, qkv, sinks)


def _attn_bwd(qkv, sinks, o, do, t_len, pad, name):
    rows = qkv.shape[0]
    nb = t_len // BLOCK
    n_seq = rows // t_len

    def body(q_ref, km_ref, kp_ref, kc_ref, vm_ref, vp_ref, vc_ref, sink_ref, o_ref, do_ref, dq_ref, dk_ref, dv_ref, dsink_ref):
        s_id, n = pl.program_id(0), pl.program_id(1)

        @pl.when(n == 0)
        def _():
            dk_ref[...] = jnp.zeros_like(dk_ref)
            dv_ref[...] = jnp.zeros_like(dv_ref)

        @pl.when((n == 0) & (s_id == 0))
        def _():
            dsink_ref[...] = jnp.zeros_like(dsink_ref)

        mask = _attn_mask(n, pad)
        keys = jnp.concatenate([km_ref[...], kp_ref[...], kc_ref[...]], axis=0)
        vals = jnp.concatenate([vm_ref[...], vp_ref[...], vc_ref[...]], axis=0)
        dkeys = jnp.zeros((3 * BLOCK, KV_WIDTH), F32)
        dvals = jnp.zeros((3 * BLOCK, KV_WIDTH), F32)
        for g in range(N_Q_HEADS // GROUP):
            qs, sink = _stack_group(q_ref, sink_ref, g)
            qsb = qs.astype(BF16)
            scores = lax.dot_general(qsb, keys, NT, preferred_element_type=F32)
            p, p_sink = _softmax_with_sink(scores, mask, sink)
            dos = _stack_rows(do_ref, g)
            delta = jnp.sum(dos * _stack_rows(o_ref, g), axis=-1, keepdims=True)
            dosb = dos.astype(BF16)
            dp = lax.dot_general(dosb, vals, NT, preferred_element_type=F32)
            ds = (p * (dp - delta) * (HEAD_DIM**-0.5)).astype(BF16)
            dqs = jnp.where(_in_half(g), jnp.dot(ds, keys, preferred_element_type=F32), 0.0)
            for t in range(GROUP // 2):
                dq_ref[:, _tile(g, t)] = _unstack_pair(dqs, g, t)
            dkeys = dkeys + lax.dot_general(ds, qsb, TN, preferred_element_type=F32)
            dvals = dvals + lax.dot_general(p.astype(BF16), dosb, TN, preferred_element_type=F32)
            sink_term = p_sink * delta
            for hh in range(GROUP):
                head = GROUP * g + hh
                part = -jnp.sum(sink_term[hh * BLOCK : (hh + 1) * BLOCK, :], axis=0, keepdims=True)
                dsink_ref[head : head + 1, :] += jnp.broadcast_to(part, (1, BLOCK))
        r_prev = pl.multiple_of(jnp.maximum(n - 1, 0) * BLOCK, BLOCK)
        r_cur = pl.multiple_of(n * BLOCK, BLOCK)
        for acc, d in ((dk_ref, dkeys), (dv_ref, dvals)):
            acc[0:BLOCK, :] += d[0:BLOCK]
            acc[pl.ds(r_prev, BLOCK), :] += d[BLOCK : 2 * BLOCK]
            acc[pl.ds(r_cur, BLOCK), :] += d[2 * BLOCK :]

    q_blk = pl.BlockSpec((BLOCK, ATTN_WIDTH), lambda s, n: (s * nb + n, 0))
    seq_kv = pl.BlockSpec((t_len, KV_WIDTH), lambda s, n: (s, 0))
    return pl.pallas_call(
        body,
        name=name,
        grid=(n_seq, nb),
        in_specs=[q_blk] + _kv_specs(nb) + [_full((N_Q_HEADS, BLOCK)), q_blk, q_blk],
        out_specs=[q_blk, seq_kv, seq_kv, _full((N_Q_HEADS, BLOCK))],
        out_shape=[
            jax.ShapeDtypeStruct((rows, ATTN_WIDTH), F32),
            jax.ShapeDtypeStruct((rows, KV_WIDTH), F32),
            jax.ShapeDtypeStruct((rows, KV_WIDTH), F32),
            jax.ShapeDtypeStruct((N_Q_HEADS, BLOCK), F32),
        ],
    )(qkv, qkv, qkv, qkv, qkv, qkv, qkv, sinks, o, do)


def _mix_out_fwd(y_rec, y_attn, h, wbuf, loc, g_rec, g_attn, ln_g, ln_b, alpha, name):
    rows, d = h.shape
    tm = _row_tile(rows)

    def body(yr_ref, ya_ref, h_ref, w_hbm, gr_ref, ga_ref, lg_ref, lb_ref, ho_ref, z_ref, hb_ref, w_ref, sems):
        _load_blocks(w_hbm, [(*loc, w_ref)], sems)
        nr = _rms(yr_ref[...], gr_ref[...]).astype(BF16)
        na = _rms(ya_ref[...], ga_ref[...]).astype(BF16)
        m = jnp.dot(nr, w_ref[0:REC_WIDTH, :], preferred_element_type=F32) + jnp.dot(na, w_ref[REC_WIDTH:, :], preferred_element_type=F32)
        z = alpha * h_ref[...] + m
        z_ref[...] = z
        xhat, _ = _ln_stats(z)
        ho = xhat * lg_ref[...] + lb_ref[...]
        ho_ref[...] = ho
        hb_ref[...] = ho.astype(BF16)

    row_d = pl.BlockSpec((tm, d), lambda i: (i, 0))
    row_h = pl.BlockSpec((tm, REC_WIDTH), lambda i: (i, 0))
    return pl.pallas_call(
        body,
        name=name,
        grid=(rows // tm,),
        in_specs=[row_h, row_h, row_d, ANY, _full((1, REC_WIDTH)), _full((1, ATTN_WIDTH)), _full((1, d)), _full((1, d))],
        out_specs=[row_d, row_d, row_d],
        out_shape=[jax.ShapeDtypeStruct((rows, d), F32)] * 2 + [jax.ShapeDtypeStruct((rows, d), BF16)],
        scratch_shapes=_weight_scratch([loc], d),
    )(y_rec, y_attn, h, wbuf, g_rec, g_attn, ln_g, ln_b)


def _mix_out_bwd(dh, z, y_rec, y_attn, wbuf, loc, g_rec, g_attn, ln_g, name):
    rows, d = dh.shape
    tm = _row_tile(rows)
    mix = REC_WIDTH + ATTN_WIDTH

    def body(dh_ref, z_ref, yr_ref, ya_ref, w_hbm, gr_ref, ga_ref, lg_ref, dz_ref, dzb_ref, dyr_ref, dya_ref, yn_ref, dgam_ref, dbet_ref, dgr_ref, dga_ref, w_ref, sems):
        _load_blocks(w_hbm, [(*loc, w_ref)], sems)

        @pl.when(pl.program_id(0) == 0)
        def _():
            for ref in (dgam_ref, dbet_ref, dgr_ref, dga_ref):
                ref[...] = jnp.zeros_like(ref)

        dz, dgam, dbet = _ln_bwd(dh_ref[...], z_ref[...], lg_ref[...])
        dgam_ref[...] += dgam
        dbet_ref[...] += dbet
        dz_ref[...] = dz
        dzb = dz.astype(BF16)
        dzb_ref[...] = dzb
        dyn = lax.dot_general(dzb, w_ref[...], NT, preferred_element_type=F32)
        dyr, dgr, nr = _rms_bwd(dyn[:, 0:REC_WIDTH], yr_ref[...], gr_ref[...])
        dya, dga, na = _rms_bwd(dyn[:, REC_WIDTH:], ya_ref[...], ga_ref[...])
        dyr_ref[...] = dyr
        dya_ref[...] = dya
        dgr_ref[...] += dgr
        dga_ref[...] += dga
        yn_ref[:, 0:REC_WIDTH] = nr.astype(BF16)
        yn_ref[:, REC_WIDTH:] = na.astype(BF16)

    row_d = pl.BlockSpec((tm, d), lambda i: (i, 0))
    row_h = pl.BlockSpec((tm, REC_WIDTH), lambda i: (i, 0))
    row_m = pl.BlockSpec((tm, mix), lambda i: (i, 0))
    return pl.pallas_call(
        body,
        name=name,
        grid=(rows // tm,),
        in_specs=[row_d, row_d, row_h, row_h, ANY, _full((1, REC_WIDTH)), _full((1, ATTN_WIDTH)), _full((1, d))],
        out_specs=[row_d, row_d, row_h, row_h, row_m, _full((1, d)), _full((1, d)), _full((1, REC_WIDTH)), _full((1, ATTN_WIDTH))],
        out_shape=[
            jax.ShapeDtypeStruct((rows, d), F32),
            jax.ShapeDtypeStruct((rows, d), BF16),
            jax.ShapeDtypeStruct((rows, REC_WIDTH), F32),
            jax.ShapeDtypeStruct((rows, ATTN_WIDTH), F32),
            jax.ShapeDtypeStruct((rows, mix), BF16),
            jax.ShapeDtypeStruct((1, d), F32),
            jax.ShapeDtypeStruct((1, d), F32),
            jax.ShapeDtypeStruct((1, REC_WIDTH), F32),
            jax.ShapeDtypeStruct((1, ATTN_WIDTH), F32),
        ],
        scratch_shapes=_weight_scratch([loc], d),
    )(dh, z, y_rec, y_attn, wbuf, g_rec, g_attn, ln_g)


def _loss_head(y, target, t_len, first_token, name):
    rows, d = y.shape
    tm = _row_tile(rows)

    def body(y_ref, t_ref, loss_ref, dy_ref):
        i = pl.program_id(0)

        @pl.when(i == 0)
        def _():
            loss_ref[...] = jnp.zeros_like(loss_ref)

        row = i * tm + lax.broadcasted_iota(jnp.int32, (tm, 1), 0)
        is_token = lax.rem(row, t_len) >= first_token
        err = jnp.where(is_token, y_ref[...] - t_ref[...], 0.0)
        dy_ref[...] = err / d
        per_row = jnp.sum(err * err, axis=-1, keepdims=True) / d
        loss_ref[...] += jnp.broadcast_to(0.5 * jnp.sum(per_row, axis=0, keepdims=True), (1, BLOCK))

    row_d = pl.BlockSpec((tm, d), lambda i: (i, 0))
    return pl.pallas_call(
        body,
        name=name,
        grid=(rows // tm,),
        in_specs=[row_d, row_d],
        out_specs=[_full((1, BLOCK)), row_d],
        out_shape=[jax.ShapeDtypeStruct((1, BLOCK), F32), jax.ShapeDtypeStruct((rows, d), F32)],
    )(y, target)


def _meta_grad(dh0, t_len, pad, name):
    rows, d = dh0.shape
    nb = t_len // BLOCK
    n_seq = rows // t_len

    def body(dh_ref, o_ref):
        @pl.when(pl.program_id(0) == 0)
        def _():
            o_ref[...] = jnp.zeros_like(o_ref)

        o_ref[...] += dh_ref[pad : pad + N_META, :]

    return pl.pallas_call(
        body,
        name=name,
        grid=(n_seq,),
        in_specs=[pl.BlockSpec((BLOCK, d), lambda s: (s * nb, 0))],
        out_specs=_full((N_META, d)),
        out_shape=jax.ShapeDtypeStruct((N_META, d), F32),
    )(dh0)


def _adamw_math(w, g, m, v):
    nm = ADAM_B1 * m + (1.0 - ADAM_B1) * g
    nv = ADAM_B2 * v + (1.0 - ADAM_B2) * (g * g)
    m_hat = nm / (1.0 - ADAM_B1**ADAM_STEP)
    v_hat = nv / (1.0 - ADAM_B2**ADAM_STEP)
    return -ADAM_LR * (m_hat / (jnp.sqrt(v_hat) + ADAM_EPS) + ADAM_WD * w), nm, nv


def _reduce_update(parts, own, w, m, v, outs, layer, name):
    _, nr, d = parts.shape
    _, a_dim, b_dim = w.shape
    assert (a_dim, b_dim) == (nr, d), (w.shape, parts.shape)

    def body(p_ref, own_hbm, w_ref, m_ref, v_ref, g_in, dl_in, nm_in, nv_in, g_ref, dl_ref, nm_ref, nv_ref, own_v, acc, sem):
        me = 4 * lax.axis_index("x") + 2 * lax.axis_index("y") + lax.axis_index("c")
        cp = pltpu.make_async_copy(own_hbm.at[me], own_v, sem)
        cp.start()
        cp.wait()
        acc[...] = own_v[...].astype(F32)
        for dev in range(N_DEV):

            @pl.when(me != dev)
            def _():
                acc[...] += p_ref[dev].astype(F32)

        g = acc[...]
        g_ref[0] = g
        dl_ref[0], nm_ref[0], nv_ref[0] = _adamw_math(w_ref[0], g, m_ref[0], v_ref[0])

    at_layer = pl.BlockSpec((1, a_dim, b_dim), lambda i: (layer, 0, 0))
    return pl.pallas_call(
        body,
        name=name,
        grid=(1,),
        in_specs=[_full((N_DEV, nr, d)), ANY, at_layer, at_layer, at_layer, ANY, ANY, ANY, ANY],
        out_specs=[at_layer] * 4,
        out_shape=[jax.ShapeDtypeStruct(w.shape, F32)] * 4,
        input_output_aliases={5: 0, 6: 1, 7: 2, 8: 3},
        scratch_shapes=[pltpu.VMEM((nr, d), parts.dtype), pltpu.VMEM((nr, d), F32), pltpu.SemaphoreType.DMA(())],
        compiler_params=_params(VMEM_BIG),
    )(parts, own, w, m, v, *outs)


def _adamw(w, g, m, v, name):
    shape = w.shape
    cols = shape[-1]
    rows = w.size // cols
    tr = rows
    for cand in (512, 256, 128, 64):
        if rows % cand == 0 and rows > cand:
            tr = cand
            break

    def body(w_ref, g_ref, m_ref, v_ref, d_ref, nm_ref, nv_ref):
        d_ref[...], nm_ref[...], nv_ref[...] = _adamw_math(w_ref[...], g_ref[...], m_ref[...], v_ref[...])

    blk = pl.BlockSpec((tr, cols), lambda i: (i, 0))
    flat = [a.reshape(rows, cols) for a in (w, g, m, v)]
    outs = pl.pallas_call(
        body,
        name=name,
        grid=(rows // tr,),
        in_specs=[blk] * 4,
        out_specs=[blk] * 3,
        out_shape=[jax.ShapeDtypeStruct((rows, cols), F32)] * 3,
    )(*flat)
    return [o.reshape(shape) for o in outs]


WEIGHTS = ["meta_tokens", "ffn1_w_gate", "ffn1_w_up", "ffn1_w_down", "ln1_g", "ln1_b", "w_in", "conv_w", "conv_b", "gate_a_w", "gate_a_b",
           "gate_x_w", "gate_x_b", "lru_lambda", "attn_sinks", "norm_rec_g", "norm_attn_g", "w_out", "ln2_g", "ln2_b", "ffn2_w_gate",
           "ffn2_w_up", "ffn2_w_down", "ln3_g", "ln3_b"]
BIG = [("ffn1_w_gate", True), ("ffn1_w_up", True), ("ffn1_w_down", False), ("w_in", True), ("w_out", False),
       ("ffn2_w_gate", True), ("ffn2_w_up", True), ("ffn2_w_down", False)]


SMALL = ["ln1_g", "ln1_b", "ln2_g", "ln2_b", "ln3_g", "ln3_b", "conv_b", "gate_a_b", "gate_x_b", "lru_lambda", "norm_rec_g", "norm_attn_g",
         "conv_w", "gate_a_w", "gate_x_w", "attn_sinks"]
GATHER_GROUPS = [["ffn1_w_gate", "ffn1_w_up", "ffn1_w_down"], ["w_in", "w_out"], ["ffn2_w_gate", "ffn2_w_up", "ffn2_w_down"]]
EXCHANGE_GROUPS = [["ffn2_w_gate", "ffn2_w_up", "ffn2_w_down", "w_out"], ["w_in", "ffn1_w_gate", "ffn1_w_up", "ffn1_w_down"]]


def _rope_tables(t_len, pad):
    pos = (jnp.arange(t_len) - pad).astype(F32)
    inv_freq = ROPE_THETA ** (-jnp.arange(0, ROPE_DIM, 2, dtype=F32) / ROPE_DIM)
    ang = pos[:, None] * inv_freq[None, :]
    cos, sin = jnp.cos(ang), jnp.sin(ang)
    half = ROPE_DIM // 2
    rest = jnp.zeros((t_len, HEAD_DIM - ROPE_DIM), F32)
    zero = jnp.zeros((t_len, half), F32)
    cos_h = jnp.concatenate([cos, cos, rest + 1.0], axis=1)
    up_h = jnp.concatenate([-sin, zero, rest], axis=1)
    down_h = jnp.concatenate([zero, sin, rest], axis=1)
    return [jnp.tile(t, (1, ATTN_WIDTH // HEAD_DIM)) for t in (cos_h, up_h, down_h)]


def _gate_tiles(w):
    z = jnp.zeros((HEAD_DIM, HEAD_DIM), w.dtype)
    tiles = [jnp.block([[w[2 * j], z], [z, w[2 * j + 1]]]) for j in range(w.shape[0] // 2)]
    return jnp.stack(tiles).astype(BF16)


def _gate_blocks(tiles):
    out = []
    for j in range(tiles.shape[0]):
        out += [tiles[j, :HEAD_DIM, :HEAD_DIM], tiles[j, HEAD_DIM:, HEAD_DIM:]]
    return jnp.stack(out)


def kernel(x, meta_tokens, ffn1_w_gate, ffn1_w_up, ffn1_w_down, ln1_g, ln1_b, w_in, conv_w, conv_b, gate_a_w, gate_a_b, gate_x_w, gate_x_b, lru_lambda, attn_sinks, norm_rec_g, norm_attn_g, w_out, ln2_g, ln2_b, ffn2_w_gate, ffn2_w_up, ffn2_w_down, ln3_g, ln3_b, loss_target, m_meta_tokens, m_ffn1_w_gate, m_ffn1_w_up, m_ffn1_w_down, m_ln1_g, m_ln1_b, m_w_in, m_conv_w, m_conv_b, m_gate_a_w, m_gate_a_b, m_gate_x_w, m_gate_x_b, m_lru_lambda, m_attn_sinks, m_norm_rec_g, m_norm_attn_g, m_w_out, m_ln2_g, m_ln2_b, m_ffn2_w_gate, m_ffn2_w_up, m_ffn2_w_down, m_ln3_g, m_ln3_b, v_meta_tokens, v_ffn1_w_gate, v_ffn1_w_up, v_ffn1_w_down, v_ln1_g, v_ln1_b, v_w_in, v_conv_w, v_conv_b, v_gate_a_w, v_gate_a_b, v_gate_x_w, v_gate_x_b, v_lru_lambda, v_attn_sinks, v_norm_rec_g, v_norm_attn_g, v_w_out, v_ln2_g, v_ln2_b, v_ffn2_w_gate, v_ffn2_w_up, v_ffn2_w_down, v_ln3_g, v_ln3_b):
    given = dict(locals())
    w = {k: given[k] for k in WEIGHTS}
    n_seq, seq, d = x.shape
    depth = ln1_g.shape[0]
    alpha = (2.0 * depth) ** 0.25
    pad = (-(N_META + seq)) % BLOCK
    t_len = pad + N_META + seq
    rows = n_seq * t_len
    me = 4 * lax.axis_index("x") + 2 * lax.axis_index("y") + lax.axis_index("c")

    transposed = dict(BIG)

    def turned(a, name):
        return jnp.swapaxes(a, 1, 2) if transposed[name] else a

    small = jnp.concatenate([meta_tokens, conv_w.reshape(-1, BLOCK)], axis=0)
    (small_all,) = _all_gather([small], small, "gather_small")
    sent, loc, gathers, chain = {}, {}, {}, small_all

    def start_gather(l, k):
        shard = jnp.concatenate([sent[name][l] for name in GATHER_GROUPS[k]], axis=0)
        zone = lax.dynamic_update_slice(lax.empty((N_DEV,) + shard.shape, BF16), shard[None], (me, 0, 0))
        gathers[(l, k)] = _push_start([shard], [zone], chain, False, f"gather_start_{l}_{k}")
        return gathers[(l, k)][2][0]

    for k, group in enumerate(GATHER_GROUPS):
        off = 0
        for name in group:
            sent[name] = turned(w[name], name).astype(BF16)
            loc[name] = (off, sent[name].shape[1])
            off += sent[name].shape[1]
        chain = start_gather(0, k)
    for l in range(1, depth):
        for k in range(len(GATHER_GROUPS)):
            chain = start_gather(l, k)
    ffn1_loc = [loc["ffn1_w_gate"], loc["ffn1_w_up"], loc["ffn1_w_down"]]
    ffn2_loc = [loc["ffn2_w_gate"], loc["ffn2_w_up"], loc["ffn2_w_down"]]
    meta_full = small_all[:, :N_META, :].transpose(1, 0, 2).reshape(N_META, d)
    conv_shard = conv_w.shape[-1]
    conv_full = small_all[:, N_META:, :].reshape(N_DEV, depth, CONV_WIDTH, conv_shard).transpose(1, 2, 0, 3).reshape(depth, CONV_WIDTH, REC_WIDTH)

    cos, sin_up, sin_down = _rope_tables(t_len, pad)
    row1 = lambda a: a.reshape(1, -1)

    h = jnp.concatenate([jnp.zeros((n_seq, pad, d), F32), jnp.broadcast_to(meta_full[None], (n_seq, N_META, d)), x], axis=1).reshape(rows, d)
    target = jnp.pad(loss_target, ((0, 0), (pad + N_META, 0), (0, 0))).reshape(rows, d)
    hb = h.astype(BF16)
    saved, wbufs = [], []

    def gathered(l, k, after):
        _, (buf,) = _push_wait(*gathers[(l, k)], after, False, f"gather_wait_{l}_{k}")
        return buf

    for l in range(depth):
        w_ffn1 = gathered(l, 0, chain if l == 0 else h)
        s = {"h0": hb}
        h, s["g1"], s["u1"], s["z1"], hb = _ffn_fwd(h, w_ffn1, ffn1_loc, row1(ln1_g[l]), row1(ln1_b[l]), alpha, "ffn1_fwd")
        s["h1"] = hb
        s["wa"], s["wx"] = _gate_tiles(gate_a_w[l]), _gate_tiles(gate_x_w[l])
        s["sinks"] = jnp.broadcast_to(attn_sinks[l][:, None], (N_Q_HEADS, BLOCK))
        wbuf = gathered(l, 1, h)
        proj = _proj_fwd(h, wbuf, loc["w_in"], "proj_fwd")
        s["proj"] = proj
        s["xc"], s["hrec"], s["y_rec"] = _rglru_fwd(proj, conv_full[l], row1(conv_b[l]), s["wa"], row1(gate_a_b[l]), s["wx"], row1(gate_x_b[l]),
                                                     row1(lru_lambda[l]), t_len, pad, "rglru_fwd")
        s["qkv"] = _rope_fwd(proj, cos, sin_up, sin_down, t_len, "rope_fwd")
        s["y_attn"] = _attn_fwd(s["qkv"], s["sinks"], t_len, pad, "attn_fwd")
        h, s["z2"], hb = _mix_out_fwd(s["y_rec"], s["y_attn"], h, wbuf, loc["w_out"], row1(norm_rec_g[l]), row1(norm_attn_g[l]), row1(ln2_g[l]), row1(ln2_b[l]), alpha, "mix_out_fwd")
        s["h2"] = hb
        w_ffn2 = gathered(l, 2, h)
        h, s["g2"], s["u2"], s["z3"], hb = _ffn_fwd(h, w_ffn2, ffn2_loc, row1(ln3_g[l]), row1(ln3_b[l]), alpha, "ffn2_fwd")
        saved.append(s)
        wbufs.append((w_ffn1, wbuf, w_ffn2))
    loss_part, dh = _loss_head(h, target, t_len, pad + N_META, "loss_head")

    exchanges = {}
    small_grads = [None] * depth
    behind = dh

    def exchange(l, k, bg, chain):
        gs = [bg[name].reshape(N_DEV, -1, d) for name in EXCHANGE_GROUPS[k]]
        zones = [lax.empty(g.shape, g.dtype) for g in gs]
        exchanges[(l, k)] = _push_start(gs, zones, chain, True, f"exchange_start_{l}_{k}")
        return exchanges[(l, k)][2][0]

    for l in reversed(range(depth)):
        (w_ffn1, wbuf, w_ffn2), s = wbufs[l], saved[l]
        bg, sg = {}, {}
        dh, dg, du, act, dyb, sg["ln3_g"], sg["ln3_b"] = _ffn_bwd(dh, s["z3"], s["g2"], s["u2"], w_ffn2, ffn2_loc, row1(ln3_g[l]), behind, alpha, "ffn2_bwd")
        bg["ffn2_w_gate"] = _mm_tn(dg, s["h2"], BF16, "ffn_wgrad_in")
        bg["ffn2_w_up"] = _mm_tn(du, s["h2"], BF16, "ffn_wgrad_in")
        bg["ffn2_w_down"] = _mm_tn(act, dyb, BF16, "ffn_wgrad_down")
        dz, dzb, dy_rec, dy_attn, yn, sg["ln2_g"], sg["ln2_b"], sg["norm_rec_g"], sg["norm_attn_g"] = _mix_out_bwd(
            dh, s["z2"], s["y_rec"], s["y_attn"], wbuf, loc["w_out"], row1(norm_rec_g[l]), row1(norm_attn_g[l]), row1(ln2_g[l]), "mix_out_bwd")
        bg["w_out"] = _mm_tn(yn, dzb, BF16, "w_out_wgrad")
        behind = exchange(l, 0, bg, behind)
        dxr, dgt, sg["conv_w"], sg["conv_b"], dwa, sg["gate_a_b"], dwx, sg["gate_x_b"], sg["lru_lambda"] = _rglru_bwd(
            dy_rec, s["hrec"], s["proj"], s["xc"], conv_full[l], s["wa"], row1(gate_a_b[l]), s["wx"], row1(gate_x_b[l]), row1(lru_lambda[l]), t_len, pad, "rglru_bwd")
        sg["gate_a_w"], sg["gate_x_w"] = _gate_blocks(dwa), _gate_blocks(dwx)
        dq, dk, dv, dsink = _attn_bwd(s["qkv"], s["sinks"], s["y_attn"], dy_attn, t_len, pad, "attn_bwd")
        sg["attn_sinks"] = dsink[:, 0]
        dqkv = _rope_bwd(dq, dk, dv, cos, sin_up, sin_down, t_len, "rope_bwd")
        dproj = jnp.concatenate([dxr, dgt, dqkv], axis=1)
        bg["w_in"] = _mm_tn(dproj, s["h1"], BF16, "w_in_wgrad")
        dh = _proj_bwd(dproj, wbuf, loc["w_in"], dz, alpha, "proj_bwd")
        dh, dg, du, act, dyb, sg["ln1_g"], sg["ln1_b"] = _ffn_bwd(dh, s["z1"], s["g1"], s["u1"], w_ffn1, ffn1_loc, row1(ln1_g[l]), behind, alpha, "ffn1_bwd")
        small_grads[l] = sg
        if l == 0:
            dmeta = _meta_grad(dh, t_len, pad, "meta_grad")
            pieces = [small_grads[j][name].reshape(-1) for j in range(depth) for name in SMALL] + [dmeta.reshape(-1), loss_part[0, :1]]
            sizes = [p.shape[0] for p in pieces]
            flat = jnp.concatenate(pieces)
            width = 1024
            n_rows = -(-flat.shape[0] // (8 * width)) * 8
            flat = jnp.pad(flat, (0, n_rows * width - flat.shape[0])).reshape(n_rows, width)
            zone = lax.dynamic_update_slice(lax.empty((N_DEV, n_rows, width), F32), flat[None], (me, 0, 0))
            small_gather = _push_start([flat], [zone], behind, False, "small_grads_start")
            behind = small_gather[2][0]
        bg["ffn1_w_gate"] = _mm_tn(dg, s["h0"], BF16, "ffn_wgrad_in")
        bg["ffn1_w_up"] = _mm_tn(du, s["h0"], BF16, "ffn_wgrad_in")
        bg["ffn1_w_down"] = _mm_tn(act, dyb, BF16, "ffn_wgrad_down")
        behind = exchange(l, 1, bg, behind)
    grad_x = dh.reshape(n_seq, t_len, d)[:, pad + N_META :, :]

    state = {name: [turned(a, name) for a in (w[name], given["m_" + name], given["v_" + name])] for name, _ in BIG}
    outs = {name: [lax.empty(state[name][0].shape, F32) for _ in range(4)] for name, _ in BIG}

    def reduce_group(l, k, after):
        send_sems, recv_sems, srcs, zones = exchanges[(l, k)]
        owns, parts = _push_wait(send_sems, recv_sems, srcs, zones, after, True, f"exchange_wait_{l}_{k}")
        for name, own, p in zip(EXCHANGE_GROUPS[k], owns, parts):
            outs[name] = _reduce_update(p, own, *state[name], outs[name], l, "reduce_update")
        return outs[EXCHANGE_GROUPS[k][-1]][0]

    grads = {}
    order = [(l, k) for l in reversed(range(depth)) for k in range(len(EXCHANGE_GROUPS))]
    after = behind
    for l, k in order[:-1]:
        after = reduce_group(l, k, after)
    _, (flat_all,) = _push_wait(*small_gather, after, False, "small_grads_wait")
    total = _sum8(flat_all, "sum_small_grads").reshape(-1)
    offs = [0]
    for sz in sizes:
        offs.append(offs[-1] + sz)
    taken = [total[offs[k] : offs[k + 1]] for k in range(len(sizes))]
    for j, name in enumerate(SMALL):
        full_shape = (depth,) + ((CONV_WIDTH, REC_WIDTH) if name == "conv_w" else w[name].shape[1:])
        grads[name] = jnp.stack([taken[l * len(SMALL) + j] for l in range(depth)]).reshape(full_shape)
    grads["conv_w"] = lax.dynamic_slice_in_dim(grads["conv_w"], me * conv_shard, conv_shard, axis=2)
    meta_shard = meta_tokens.shape[1]
    grads["meta_tokens"] = lax.dynamic_slice_in_dim(taken[-2].reshape(N_META, d), me * meta_shard, meta_shard, axis=1)
    loss = taken[-1][0]
    deltas, new_m, new_v = {}, {}, {}
    for name in WEIGHTS:
        if name not in dict(BIG):
            deltas[name], new_m[name], new_v[name] = _adamw(w[name], grads[name], given["m_" + name], given["v_" + name], "adamw")

    reduce_group(*order[-1], new_v[WEIGHTS[-1]])
    for name, _ in BIG:
        grads[name], deltas[name], new_m[name], new_v[name] = [turned(a, name) for a in outs[name]]
    return (loss, grad_x, *[grads[k] for k in WEIGHTS], *[deltas[k] for k in WEIGHTS], *[new_m[k] for k in WEIGHTS], *[new_v[k] for k in WEIGHTS])
```

```python
import functools

import jax
import jax.numpy as jnp
from jax import lax
from jax.experimental import pallas as pl
from jax.experimental.pallas import tpu as pltpu

F32 = jnp.float32
BF16 = jnp.bfloat16
MESH = pl.DeviceIdType.MESH

N_DEV = 8
N_META = 16
BLOCK = 128
WINDOW = 128
REC_WIDTH = 512
ATTN_WIDTH = 512
KV_WIDTH = 128
HEAD_DIM = 64
N_Q_HEADS = 8
ROPE_DIM = 16
ROPE_THETA = 500000.0
CONV_WIDTH = 4
LRU_C = 8.0
LN_EPS = 1e-5
RMS_EPS = 1e-6
NEG_INF = -1e30
ADAM_LR, ADAM_B1, ADAM_B2, ADAM_EPS, ADAM_WD, ADAM_STEP = 0.001, 0.9, 0.999, 1e-08, 0.01, 10

NT = (((1,), (1,)), ((), ()))
TN = (((0,), (0,)), ((), ()))
VMEM_BIG = 56 * 1024 * 1024


def _params(vmem=None):
    return pltpu.CompilerParams(vmem_limit_bytes=vmem)


def _row_tile(rows):
    return 256 if rows % 256 == 0 else 128


def _sigmoid(x):
    return 1.0 / (1.0 + jnp.exp(-x))


def _expm1_nonpos(x):
    series = x * (1.0 + 0.5 * x * (1.0 + x / 3.0 * (1.0 + 0.25 * x * (1.0 + 0.2 * x))))
    return jnp.where(x > -0.05, series, jnp.exp(x) - 1.0)


def _softplus(x):
    e = jnp.exp(-jnp.abs(x))
    log1p = jnp.where(e < 1e-3, e * (1.0 - e * (0.5 - e / 3.0)), jnp.log(1.0 + e))
    return jnp.maximum(x, 0.0) + log1p


def _gelu_and_grad(x):
    c0 = 0.7978845608028654
    x2 = x * x
    t = jnp.tanh(c0 * (x + 0.044715 * x * x2))
    gelu = 0.5 * x * (1.0 + t)
    grad = 0.5 * (1.0 + t) + 0.5 * x * (1.0 - t * t) * c0 * (1.0 + 3.0 * 0.044715 * x2)
    return gelu, grad


def _ln_stats(z):
    mu = jnp.mean(z, axis=-1, keepdims=True)
    zc = z - mu
    var = jnp.mean(zc * zc, axis=-1, keepdims=True)
    rstd = lax.rsqrt(var + LN_EPS)
    return zc * rstd, rstd


def _ln_bwd(dy, z, gamma):
    xhat, rstd = _ln_stats(z)
    dxh = dy * gamma
    m1 = jnp.mean(dxh, axis=-1, keepdims=True)
    m2 = jnp.mean(dxh * xhat, axis=-1, keepdims=True)
    dz = rstd * (dxh - m1 - xhat * m2)
    return dz, jnp.sum(dy * xhat, axis=0, keepdims=True), jnp.sum(dy, axis=0, keepdims=True)


def _rms(y, gamma):
    r = lax.rsqrt(jnp.mean(y * y, axis=-1, keepdims=True) + RMS_EPS)
    return y * r * gamma


def _rms_bwd(dout, y, gamma):
    r = lax.rsqrt(jnp.mean(y * y, axis=-1, keepdims=True) + RMS_EPS)
    n = y * r
    dn = dout * gamma
    dy = r * (dn - n * jnp.mean(dn * n, axis=-1, keepdims=True))
    return dy, jnp.sum(dout * n, axis=0, keepdims=True), n * gamma


def _load_blocks(w_hbm, items, sems):
    @pl.when(pl.program_id(0) == 0)
    def _():
        copies = []
        for k, (off, nr, dst) in enumerate(items):
            for dev in range(N_DEV):
                copies.append(pltpu.make_async_copy(w_hbm.at[dev, pl.ds(off, nr), :], dst.at[pl.ds(dev * nr, nr), :], sems.at[k, dev]))
        for cp in copies:
            cp.start()
        for cp in copies:
            cp.wait()


def _full(shape):
    return pl.BlockSpec(shape, lambda *_: (0,) * len(shape))


ANY = pl.BlockSpec(memory_space=pl.ANY)


def _all_gather(xs, after, name):
    n = len(xs)

    def body(*refs):
        ins, outs = refs[:n], refs[n + 1 : 2 * n + 1]
        send_sems, recv_sems, local_sems = refs[2 * n + 1 :]
        x, y, c = lax.axis_index("x"), lax.axis_index("y"), lax.axis_index("c")
        me, sibling = (x, y, c), (x, y, 1 - c)
        chips = [(1 - x, y), (x, 1 - y), (1 - x, 1 - y)]

        def slot(i, dev):
            return outs[i].at[4 * dev[0] + 2 * dev[1] + dev[2]]

        def copy(i, k, block, to, src=None):
            return pltpu.make_async_remote_copy(
                src_ref=slot(i, block) if src is None else src,
                dst_ref=slot(i, block),
                send_sem=send_sems.at[i, k],
                recv_sem=recv_sems.at[i, k],
                device_id=to,
                device_id_type=MESH,
            )

        mine = [pltpu.make_async_copy(ins[i], slot(i, me), local_sems.at[i]) for i in range(n)]
        for cp in mine:
            cp.start()
        first = []
        for i in range(n):
            first.append(copy(i, 0, me, sibling, src=ins[i]))
            first += [copy(i, 1 + j, me, (*chip, c), src=ins[i]) for j, chip in enumerate(chips)]
        for cp in first:
            cp.start()
        passed = []
        for j, chip in enumerate(chips):
            for i in range(n):
                copy(i, 1 + j, (*chip, c), me).wait_recv()
                fwd = copy(i, 4 + j, (*chip, c), sibling)
                fwd.start()
                passed.append(fwd)
        for i in range(n):
            copy(i, 0, sibling, me).wait_recv()
            for j, chip in enumerate(chips):
                copy(i, 4 + j, (*chip, 1 - c), me).wait_recv()
        for cp in first + passed:
            cp.wait_send()
        for cp in mine:
            cp.wait()

    return pl.pallas_call(
        body,
        name=name,
        out_shape=[jax.ShapeDtypeStruct((N_DEV,) + a.shape, a.dtype) for a in xs],
        in_specs=[ANY] * (n + 1),
        out_specs=[ANY] * n,
        scratch_shapes=[pltpu.SemaphoreType.DMA((n, 7)), pltpu.SemaphoreType.DMA((n, 7)), pltpu.SemaphoreType.DMA((n,))],
    )(*xs, after)


HBM = pl.BlockSpec(memory_space=pltpu.HBM)
SEM = pl.BlockSpec(memory_space=pltpu.SEMAPHORE)
EFFECT = pltpu.SideEffectType.DATAFLOW_SIDE_EFFECTING


def _push_copies(ins, lands, send_sems, recv_sems, scatter):
    x, y, c = lax.axis_index("x"), lax.axis_index("y"), lax.axis_index("c")
    me = 4 * x + 2 * y + c
    copies = []
    for k in range(1, N_DEV):
        px = 1 - x if (k >> 2) & 1 else x
        py = 1 - y if (k >> 1) & 1 else y
        pc = 1 - c if k & 1 else c
        for i in range(len(ins)):
            copies.append(
                pltpu.make_async_remote_copy(
                    src_ref=ins[i].at[4 * px + 2 * py + pc] if scatter else ins[i],
                    dst_ref=lands[i].at[me],
                    send_sem=send_sems.at[i * (N_DEV - 1) + k - 1],
                    recv_sem=recv_sems.at[i * (N_DEV - 1) + k - 1],
                    device_id=(px, py, pc),
                    device_id_type=MESH,
                )
            )
    return copies


def _push_start(xs, lands, chain, scatter, name):
    n = len(xs)

    def body(*refs):
        ins, zones = refs[:n], refs[n : 2 * n]
        for cp in _push_copies(ins, zones, refs[2 * n + 1], refs[2 * n + 2], scatter):
            cp.start()

    outs = pl.pallas_call(
        body,
        name=name,
        out_shape=(
            pltpu.SemaphoreType.DMA((n * (N_DEV - 1),)),
            pltpu.SemaphoreType.DMA((n * (N_DEV - 1),)),
            *[pltpu.HBM(a.shape, a.dtype) for a in list(xs) + list(lands)],
        ),
        in_specs=[HBM] * (2 * n) + [ANY],
        out_specs=(SEM, SEM, *[HBM] * (2 * n)),
        input_output_aliases={i: 2 + i for i in range(2 * n)},
        compiler_params=pltpu.CompilerParams(has_side_effects=EFFECT),
    )(*[pltpu.with_memory_space_constraint(a, pltpu.HBM) for a in list(xs) + list(lands)], chain)
    return outs[0], outs[1], list(outs[2 : 2 + n]), list(outs[2 + n : 2 + 2 * n])


def _push_wait(send_sems, recv_sems, srcs, lands, after, scatter, name):
    n = len(srcs)

    def body(*refs):
        ins, zones = refs[:n], refs[n : 2 * n]
        for cp in _push_copies(ins, zones, refs[2 * n], refs[2 * n + 1], scatter):
            cp.wait_send()
            cp.wait_recv()

    outs = pl.pallas_call(
        body,
        name=name,
        out_shape=[pltpu.HBM(a.shape, a.dtype) for a in srcs + lands],
        in_specs=[HBM] * (2 * n) + [SEM, SEM, ANY],
        out_specs=[HBM] * (2 * n),
        input_output_aliases={i: i for i in range(2 * n)},
        compiler_params=pltpu.CompilerParams(has_side_effects=EFFECT),
    )(*srcs, *lands, send_sems, recv_sems, after)
    return list(outs[:n]), list(outs[n:])


def _sum8(parts, name):
    _, rows, cols = parts.shape
    tr = rows
    for cand in (512, 256, 128, 64, 32, 16):
        if rows % cand == 0 and rows > cand:
            tr = cand
            break

    def body(p_ref, o_ref):
        acc = p_ref[0].astype(F32)
        for d in range(1, N_DEV):
            acc = acc + p_ref[d].astype(F32)
        o_ref[...] = acc

    return pl.pallas_call(
        body,
        name=name,
        grid=(rows // tr,),
        in_specs=[pl.BlockSpec((N_DEV, tr, cols), lambda i: (0, i, 0))],
        out_specs=pl.BlockSpec((tr, cols), lambda i: (i, 0)),
        out_shape=jax.ShapeDtypeStruct((rows, cols), F32),
    )(parts)


def _mm_tn(a, b, out_dtype, name):
    rows, m = a.shape
    n = b.shape[1]
    tm = m // 2 if (m // 2) % 128 == 0 and m > 512 else m
    tr = next(t for t in (2176, 1024, 512, 256, 128) if rows % t == 0)
    nk = rows // tr

    def body(a_ref, b_ref, o_ref, acc):
        k = pl.program_id(1)
        part = lax.dot_general(a_ref[...].astype(BF16), b_ref[...].astype(BF16), TN, preferred_element_type=F32)

        @pl.when(k == 0)
        def _():
            acc[...] = part

        @pl.when(k > 0)
        def _():
            acc[...] += part

        @pl.when(k == nk - 1)
        def _():
            o_ref[...] = acc[...].astype(o_ref.dtype)

    return pl.pallas_call(
        body,
        name=name,
        grid=(m // tm, nk),
        in_specs=[pl.BlockSpec((tr, tm), lambda i, k: (k, i)), pl.BlockSpec((tr, n), lambda i, k: (k, 0))],
        out_specs=pl.BlockSpec((tm, n), lambda i, k: (i, 0)),
        out_shape=jax.ShapeDtypeStruct((m, n), out_dtype),
        scratch_shapes=[pltpu.VMEM((tm, n), F32)],
        compiler_params=_params(VMEM_BIG),
    )(a, b)


def _weight_scratch(locs, d):
    return [pltpu.VMEM((N_DEV * nr, d), BF16) for _, nr in locs] + [pltpu.SemaphoreType.DMA((len(locs), N_DEV))]


def _proj_fwd(h, wbuf, loc, name):
    rows, d = h.shape
    n = N_DEV * loc[1]
    tm = _row_tile(rows)

    def body(h_ref, w_hbm, o_ref, w, sems):
        _load_blocks(w_hbm, [(*loc, w)], sems)
        o_ref[...] = lax.dot_general(h_ref[...].astype(BF16), w[...], NT, preferred_element_type=F32)

    return pl.pallas_call(
        body,
        name=name,
        grid=(rows // tm,),
        in_specs=[pl.BlockSpec((tm, d), lambda i: (i, 0)), ANY],
        out_specs=pl.BlockSpec((tm, n), lambda i: (i, 0)),
        out_shape=jax.ShapeDtypeStruct((rows, n), F32),
        scratch_shapes=_weight_scratch([loc], d),
        compiler_params=_params(VMEM_BIG),
    )(h, wbuf)


def _proj_bwd(dproj, wbuf, loc, dz, alpha, name):
    rows, n = dproj.shape
    d = dz.shape[1]
    tm = _row_tile(rows)

    def body(dp_ref, w_hbm, dz_ref, o_ref, w, sems):
        _load_blocks(w_hbm, [(*loc, w)], sems)
        o_ref[...] = alpha * dz_ref[...] + jnp.dot(dp_ref[...], w[...], preferred_element_type=F32)

    return pl.pallas_call(
        body,
        name=name,
        grid=(rows // tm,),
        in_specs=[pl.BlockSpec((tm, n), lambda i: (i, 0)), ANY, pl.BlockSpec((tm, d), lambda i: (i, 0))],
        out_specs=pl.BlockSpec((tm, d), lambda i: (i, 0)),
        out_shape=jax.ShapeDtypeStruct((rows, d), F32),
        scratch_shapes=_weight_scratch([loc], d),
        compiler_params=_params(VMEM_BIG),
    )(dproj, wbuf, dz)


FFN_CHUNKS = 1


def _ffn_fwd(h, wbuf, locs, ln_g, ln_b, alpha, name):
    rows, d = h.shape
    f = N_DEV * locs[0][1]
    tm = _row_tile(rows)
    tf = f // FFN_CHUNKS

    def body(h_ref, w_hbm, lg_ref, lb_ref, ho_ref, g_ref, u_ref, z_ref, hb_ref, wg, wu, wdv, sems):
        _load_blocks(w_hbm, [(*locs[0], wg), (*locs[1], wu), (*locs[2], wdv)], sems)
        x = h_ref[...]
        xb = x.astype(BF16)
        y = None
        for c in range(FFN_CHUNKS):
            sl = slice(c * tf, (c + 1) * tf)
            g = lax.dot_general(xb, wg[sl, :], NT, preferred_element_type=F32)
            u = lax.dot_general(xb, wu[sl, :], NT, preferred_element_type=F32)
            g_ref[:, sl] = g
            u_ref[:, sl] = u
            a = (g * _sigmoid(g) * u).astype(BF16)
            part = jnp.dot(a, wdv[sl, :], preferred_element_type=F32)
            y = part if y is None else y + part
        z = alpha * x + 0.5 * y
        z_ref[...] = z
        xhat, _ = _ln_stats(z)
        ho = xhat * lg_ref[...] + lb_ref[...]
        ho_ref[...] = ho
        hb_ref[...] = ho.astype(BF16)

    row_d = pl.BlockSpec((tm, d), lambda i: (i, 0))
    row_f = pl.BlockSpec((tm, f), lambda i: (i, 0))
    return pl.pallas_call(
        body,
        name=name,
        grid=(rows // tm,),
        in_specs=[row_d, ANY, _full((1, d)), _full((1, d))],
        out_specs=[row_d, row_f, row_f, row_d, row_d],
        out_shape=[
            jax.ShapeDtypeStruct((rows, d), F32),
            jax.ShapeDtypeStruct((rows, f), F32),
            jax.ShapeDtypeStruct((rows, f), F32),
            jax.ShapeDtypeStruct((rows, d), F32),
            jax.ShapeDtypeStruct((rows, d), BF16),
        ],
        scratch_shapes=_weight_scratch(locs, d),
        compiler_params=_params(VMEM_BIG),
    )(h, wbuf, ln_g, ln_b)


def _ffn_bwd(dh, z, g, u, wbuf, locs, ln_g, after, alpha, name):
    rows, d = dh.shape
    f = N_DEV * locs[0][1]
    tm = _row_tile(rows)
    tf = f // FFN_CHUNKS

    def body(dh_ref, z_ref, g_ref, u_ref, w_hbm, lg_ref, after_ref, dx_ref, dg_ref, du_ref, a_ref, dyb_ref, dgam_ref, dbet_ref, wg, wu, wdv, sems):
        _load_blocks(w_hbm, [(*locs[0], wg), (*locs[1], wu), (*locs[2], wdv)], sems)

        @pl.when(pl.program_id(0) == 0)
        def _():
            dgam_ref[...] = jnp.zeros_like(dgam_ref)
            dbet_ref[...] = jnp.zeros_like(dbet_ref)

        dz, dgam, dbet = _ln_bwd(dh_ref[...], z_ref[...], lg_ref[...])
        dgam_ref[...] += dgam
        dbet_ref[...] += dbet
        dyb = (0.5 * dz).astype(BF16)
        dyb_ref[...] = dyb
        dx = alpha * dz
        for c in range(FFN_CHUNKS):
            sl = slice(c * tf, (c + 1) * tf)
            da = lax.dot_general(dyb, wdv[sl, :], NT, preferred_element_type=F32)
            gc = g_ref[:, sl]
            uc = u_ref[:, sl]
            sg = _sigmoid(gc)
            silu = gc * sg
            dgc = (da * uc * (sg * (1.0 + gc * (1.0 - sg)))).astype(BF16)
            duc = (da * silu).astype(BF16)
            a_ref[:, sl] = (silu * uc).astype(BF16)
            dg_ref[:, sl] = dgc
            du_ref[:, sl] = duc
            dx = dx + jnp.dot(dgc, wg[sl, :], preferred_element_type=F32) + jnp.dot(duc, wu[sl, :], preferred_element_type=F32)
        dx_ref[...] = dx

    row_d = pl.BlockSpec((tm, d), lambda i: (i, 0))
    row_f = pl.BlockSpec((tm, f), lambda i: (i, 0))
    return pl.pallas_call(
        body,
        name=name,
        grid=(rows // tm,),
        in_specs=[row_d, row_d, row_f, row_f, ANY, _full((1, d)), ANY],
        out_specs=[row_d, row_f, row_f, row_f, row_d, _full((1, d)), _full((1, d))],
        out_shape=[
            jax.ShapeDtypeStruct((rows, d), F32),
            jax.ShapeDtypeStruct((rows, f), BF16),
            jax.ShapeDtypeStruct((rows, f), BF16),
            jax.ShapeDtypeStruct((rows, f), BF16),
            jax.ShapeDtypeStruct((rows, d), BF16),
            jax.ShapeDtypeStruct((1, d), F32),
            jax.ShapeDtypeStruct((1, d), F32),
        ],
        scratch_shapes=_weight_scratch(locs, d),
        compiler_params=_params(VMEM_BIG),
    )(dh, z, g, u, wbuf, ln_g, after)


LANES = 128


def _gates(xc, wa_ref, ba, wx_ref, bx, lam):
    xcb = xc.astype(BF16)
    r = _sigmoid(jnp.dot(xcb, wa_ref[0], preferred_element_type=F32) + ba)
    i = _sigmoid(jnp.dot(xcb, wx_ref[0], preferred_element_type=F32) + bx)
    sp = _softplus(-lam)
    log_a = -LRU_C * sp * r
    a = jnp.exp(log_a)
    s = jnp.sqrt(-_expm1_nonpos(2.0 * log_a))
    return xcb, r, i, sp, a, s


def _conv(xr, cw, cb):
    return cb + cw[3:4] * xr + cw[2:3] * pltpu.roll(xr, 1, 0) + cw[1:2] * pltpu.roll(xr, 2, 0) + cw[0:1] * pltpu.roll(xr, 3, 0)


def _rglru_fwd(proj, cw, cb, wa, ba, wx, bx, lam, t_len, pad, name):
    rows = proj.shape[0]
    n_seq = rows // t_len
    n_tile = REC_WIDTH // LANES
    groups = t_len // 8

    def body(xr_ref, gt_ref, cw_ref, cb_ref, wa_ref, ba_ref, wx_ref, bx_ref, lam_ref, xc_ref, h_ref, y_ref, a_s, b_s):
        valid = lax.broadcasted_iota(jnp.int32, (t_len, 1), 0) >= pad
        xr = jnp.where(valid, xr_ref[...], 0.0)
        xc = _conv(xr, cw_ref[...], cb_ref[...])
        xc_ref[...] = xc
        _, _, i, _, a, s = _gates(xc, wa_ref, ba_ref[...], wx_ref, bx_ref[...], lam_ref[...])
        a_s[...] = a
        b_s[...] = jnp.where(valid, s * (i * xc), 0.0)
        sub = lax.broadcasted_iota(jnp.int32, (8, LANES), 0)

        def step(gi, carry):
            r0 = pl.multiple_of(gi * 8, 8)
            av = a_s[pl.ds(r0, 8), :]
            bv = b_s[pl.ds(r0, 8), :]
            for d in (1, 2, 4):
                m = sub >= d
                a_sh = pltpu.roll(av, d, 0)
                b_sh = pltpu.roll(bv, d, 0)
                bv = jnp.where(m, av * b_sh + bv, bv)
                av = jnp.where(m, av * a_sh, av)
            hv = av * carry + bv
            h_ref[pl.ds(r0, 8), :] = hv
            return jnp.broadcast_to(hv[7:8, :], (8, LANES))

        lax.fori_loop(0, groups, step, jnp.zeros((8, LANES), F32), unroll=4)
        gelu, _ = _gelu_and_grad(gt_ref[...])
        y_ref[...] = h_ref[...] * gelu

    seq_tile = lambda col0: pl.BlockSpec((t_len, LANES), lambda s, j: (s, col0 + j))
    vec = pl.BlockSpec((1, LANES), lambda s, j: (0, j))
    out = jax.ShapeDtypeStruct((rows, REC_WIDTH), F32)
    return pl.pallas_call(
        body,
        name=name,
        grid=(n_seq, n_tile),
        in_specs=[
            seq_tile(0),
            seq_tile(n_tile),
            pl.BlockSpec((CONV_WIDTH, LANES), lambda s, j: (0, j)),
            vec,
            pl.BlockSpec((1, LANES, LANES), lambda s, j: (j, 0, 0)),
            vec,
            pl.BlockSpec((1, LANES, LANES), lambda s, j: (j, 0, 0)),
            vec,
            vec,
        ],
        out_specs=[seq_tile(0)] * 3,
        out_shape=[out, out, out],
        scratch_shapes=[pltpu.VMEM((t_len, LANES), F32), pltpu.VMEM((t_len, LANES), F32)],
        compiler_params=_params(VMEM_BIG),
    )(proj, proj, cw, cb, wa, ba, wx, bx, lam)


def _rglru_bwd(dy, h, proj, xc, cw, wa, ba, wx, bx, lam, t_len, pad, name):
    rows = proj.shape[0]
    n_seq = rows // t_len
    n_tile = REC_WIDTH // LANES
    groups = t_len // 8

    def body(dy_ref, h_ref, xr_ref, gt_ref, xc_ref, cw_ref, wa_ref, ba_ref, wx_ref, bx_ref, lam_ref,
             dxr_ref, dgt_ref, dcw_ref, dcb_ref, dwa_ref, dba_ref, dwx_ref, dbx_ref, dlam_ref, a_s, c_s, db_s, da_s):
        first_seq = pl.program_id(1) == 0

        @pl.when(first_seq)
        def _():
            for ref in (dcw_ref, dcb_ref, dwa_ref, dba_ref, dwx_ref, dbx_ref, dlam_ref):
                ref[...] = jnp.zeros_like(ref)

        valid = lax.broadcasted_iota(jnp.int32, (t_len, 1), 0) >= pad
        gelu, dgelu = _gelu_and_grad(gt_ref[...])
        dyv = dy_ref[...]
        dho = dyv * gelu
        dgt_ref[...] = (dyv * h_ref[...] * dgelu).astype(BF16)
        xc = xc_ref[...]
        lam = lam_ref[...]
        xcb, r, i, sp, a, s = _gates(xc, wa_ref, ba_ref[...], wx_ref, bx_ref[...], lam)
        a_s[...] = a
        c_s[...] = a * dho
        db_s[...] = dho
        sub = lax.broadcasted_iota(jnp.int32, (8, LANES), 0)

        def step(k, carry):
            gi = groups - 1 - k
            r0 = pl.multiple_of(gi * 8, 8)
            av = a_s[pl.ds(r0, 8), :]
            cv = c_s[pl.ds(r0, 8), :]
            for d in (1, 2, 4):
                m = sub < 8 - d
                a_sh = pltpu.roll(av, 8 - d, 0)
                c_sh = pltpu.roll(cv, 8 - d, 0)
                cv = jnp.where(m, av * c_sh + cv, cv)
                av = jnp.where(m, av * a_sh, av)
            ev = av * carry + cv
            e_next = jnp.where(sub < 7, pltpu.roll(ev, 7, 0), carry)
            dht = db_s[pl.ds(r0, 8), :] + e_next
            hv = h_ref[pl.ds(r0, 8), :]
            rp = pl.multiple_of(jnp.maximum(gi - 1, 0) * 8, 8)
            h_before = jnp.where(gi > 0, jnp.broadcast_to(h_ref[pl.ds(rp, 8), :][7:8, :], (8, LANES)), 0.0)
            h_prev = jnp.where(sub >= 1, pltpu.roll(hv, 1, 0), h_before)
            db_s[pl.ds(r0, 8), :] = dht
            da_s[pl.ds(r0, 8), :] = dht * h_prev
            return jnp.broadcast_to(ev[0:1, :], (8, LANES))

        lax.fori_loop(0, groups, step, jnp.zeros((8, LANES), F32), unroll=4)

        db = jnp.where(valid, db_s[...], 0.0)
        da = da_s[...]
        ds = db * (i * xc)
        di = db * s * xc
        dxc = db * s * i
        dlog_a = da * a - ds * (a * a) / jnp.maximum(s, 1e-30)
        dr = dlog_a * (-LRU_C * sp)
        dsp = jnp.sum(dlog_a * (-LRU_C) * r, axis=0, keepdims=True)
        dlam_ref[...] += dsp * (-_sigmoid(-lam))
        dpr = dr * r * (1.0 - r)
        dpi = di * i * (1.0 - i)
        dprb = dpr.astype(BF16)
        dpib = dpi.astype(BF16)
        dxc = dxc + lax.dot_general(dprb, wa_ref[0], NT, preferred_element_type=F32) + lax.dot_general(dpib, wx_ref[0], NT, preferred_element_type=F32)
        dwa_ref[0] += lax.dot_general(xcb, dprb, TN, preferred_element_type=F32)
        dwx_ref[0] += lax.dot_general(xcb, dpib, TN, preferred_element_type=F32)
        dba_ref[...] += jnp.sum(dpr, axis=0, keepdims=True)
        dbx_ref[...] += jnp.sum(dpi, axis=0, keepdims=True)
        dxc = jnp.where(valid, dxc, 0.0)
        xr = jnp.where(valid, xr_ref[...], 0.0)
        dcb_ref[...] += jnp.sum(dxc, axis=0, keepdims=True)
        for k in range(CONV_WIDTH):
            shifted = xr if k == CONV_WIDTH - 1 else pltpu.roll(xr, CONV_WIDTH - 1 - k, 0)
            dcw_ref[k : k + 1, :] += jnp.sum(dxc * shifted, axis=0, keepdims=True)
        cw = cw_ref[...]
        dxr = cw[3:4] * dxc + cw[2:3] * pltpu.roll(dxc, t_len - 1, 0) + cw[1:2] * pltpu.roll(dxc, t_len - 2, 0) + cw[0:1] * pltpu.roll(dxc, t_len - 3, 0)
        dxr_ref[...] = jnp.where(valid, dxr, 0.0).astype(BF16)

    seq_tile = lambda col0: pl.BlockSpec((t_len, LANES), lambda j, s: (s, col0 + j))
    vec = pl.BlockSpec((1, LANES), lambda j, s: (0, j))
    mat = pl.BlockSpec((1, LANES, LANES), lambda j, s: (j, 0, 0))
    cwb = pl.BlockSpec((CONV_WIDTH, LANES), lambda j, s: (0, j))
    big = jax.ShapeDtypeStruct((rows, REC_WIDTH), BF16)
    vec_shape = jax.ShapeDtypeStruct((1, REC_WIDTH), F32)
    mat_shape = jax.ShapeDtypeStruct((n_tile, LANES, LANES), F32)
    return pl.pallas_call(
        body,
        name=name,
        grid=(n_tile, n_seq),
        in_specs=[seq_tile(0), seq_tile(0), seq_tile(0), seq_tile(n_tile), seq_tile(0), cwb, mat, vec, mat, vec, vec],
        out_specs=[seq_tile(0), seq_tile(0), cwb, vec, mat, vec, mat, vec, vec],
        out_shape=[big, big, jax.ShapeDtypeStruct((CONV_WIDTH, REC_WIDTH), F32), vec_shape, mat_shape, vec_shape, mat_shape, vec_shape, vec_shape],
        scratch_shapes=[pltpu.VMEM((t_len, LANES), F32)] * 4,
        compiler_params=_params(VMEM_BIG),
    )(dy, h, proj, proj, xc, cw, wa, ba, wx, bx, lam)


QKV_WIDTH = ATTN_WIDTH + 2 * KV_WIDTH


def _rotate(v, cos, sin_up, sin_down):
    width = v.shape[1]
    return v * cos + pltpu.roll(v, width - ROPE_DIM // 2, 1) * sin_up + pltpu.roll(v, ROPE_DIM // 2, 1) * sin_down


def _rope_rows(t_len):
    return t_len // 4 if t_len % 64 == 0 else BLOCK


def _rope_fwd(proj, cos, sin_up, sin_down, t_len, name):
    rows = proj.shape[0]
    rb = _rope_rows(t_len)
    nb = t_len // rb
    n_seq = rows // t_len
    q0 = 2 * REC_WIDTH

    def body(q_ref, k_ref, v_ref, cos_ref, up_ref, down_ref, o_ref):
        cos_t, up_t, down_t = cos_ref[...], up_ref[...], down_ref[...]
        o_ref[:, 0:ATTN_WIDTH] = _rotate(q_ref[...], cos_t, up_t, down_t).astype(BF16)
        o_ref[:, ATTN_WIDTH : ATTN_WIDTH + KV_WIDTH] = _rotate(k_ref[...], cos_t[:, :KV_WIDTH], up_t[:, :KV_WIDTH], down_t[:, :KV_WIDTH]).astype(BF16)
        o_ref[:, ATTN_WIDTH + KV_WIDTH :] = v_ref[...].astype(BF16)

    tab = pl.BlockSpec((rb, ATTN_WIDTH), lambda s, n: (n, 0))
    return pl.pallas_call(
        body,
        name=name,
        grid=(n_seq, nb),
        in_specs=[
            pl.BlockSpec((rb, ATTN_WIDTH), lambda s, n: (s * nb + n, q0 // ATTN_WIDTH)),
            pl.BlockSpec((rb, KV_WIDTH), lambda s, n: (s * nb + n, (q0 + ATTN_WIDTH) // KV_WIDTH)),
            pl.BlockSpec((rb, KV_WIDTH), lambda s, n: (s * nb + n, (q0 + ATTN_WIDTH) // KV_WIDTH + 1)),
            tab,
            tab,
            tab,
        ],
        out_specs=pl.BlockSpec((rb, QKV_WIDTH), lambda s, n: (s * nb + n, 0)),
        out_shape=jax.ShapeDtypeStruct((rows, QKV_WIDTH), BF16),
    )(proj, proj, proj, cos, sin_up, sin_down)


def _rope_bwd(dq, dk, dv, cos, sin_up, sin_down, t_len, name):
    rows = dq.shape[0]
    rb = _rope_rows(t_len)
    nb = t_len // rb
    n_seq = rows // t_len

    def body(dq_ref, dk_ref, dv_ref, cos_ref, up_ref, down_ref, o_ref):
        cos_t, up_t, down_t = cos_ref[...], -up_ref[...], -down_ref[...]
        o_ref[:, 0:ATTN_WIDTH] = _rotate(dq_ref[...], cos_t, up_t, down_t).astype(BF16)
        o_ref[:, ATTN_WIDTH : ATTN_WIDTH + KV_WIDTH] = _rotate(dk_ref[...], cos_t[:, :KV_WIDTH], up_t[:, :KV_WIDTH], down_t[:, :KV_WIDTH]).astype(BF16)
        o_ref[:, ATTN_WIDTH + KV_WIDTH :] = dv_ref[...].astype(BF16)

    tab = pl.BlockSpec((rb, ATTN_WIDTH), lambda s, n: (n, 0))
    blk = lambda w: pl.BlockSpec((rb, w), lambda s, n: (s * nb + n, 0))
    return pl.pallas_call(
        body,
        name=name,
        grid=(n_seq, nb),
        in_specs=[blk(ATTN_WIDTH), blk(KV_WIDTH), blk(KV_WIDTH), tab, tab, tab],
        out_specs=blk(QKV_WIDTH),
        out_shape=jax.ShapeDtypeStruct((rows, QKV_WIDTH), BF16),
    )(dq, dk, dv, cos, sin_up, sin_down)


GROUP = 4


def _attn_bias(pad):
    out = []
    for n in range(3):
        q_pos = n * BLOCK + jnp.arange(BLOCK)[:, None] - pad
        col = jnp.arange(3 * BLOCK)[None, :]
        first = col < BLOCK
        k_pos = col - pad + jnp.where(first, 0, (n - 2) * BLOCK)
        dist = q_pos - k_pos
        band = (dist >= 0) & (dist < WINDOW) & (k_pos >= N_META) & jnp.logical_not(first)
        meta = (k_pos >= 0) & (k_pos < N_META) & (k_pos <= q_pos) & first
        out.append(jnp.tile(jnp.where(band | meta, 0.0, NEG_INF).astype(F32), (GROUP, 1)))
    return jnp.stack(out)


def _bias_spec():
    return pl.BlockSpec((1, GROUP * BLOCK, 3 * BLOCK), lambda s, n: (jnp.minimum(n, 2), 0, 0))


def _in_half(e):
    lane = lax.broadcasted_iota(jnp.int32, (1, BLOCK), 1)
    return lane >= HEAD_DIM if e else lane < HEAD_DIM


def _head_views(q2, e, g):
    v = jnp.where(_in_half(e), q2, 0.0)
    return pltpu.roll(v, HEAD_DIM, 1) if e != g else v


def _tile(g, t):
    j = (GROUP // 2) * g + t
    return slice(j * BLOCK, (j + 1) * BLOCK)


def _stack_rows(ref, g):
    parts = []
    for t in range(GROUP // 2):
        tile = ref[:, _tile(g, t)].astype(F32)
        parts += [_head_views(tile, e, g) for e in range(2)]
    return jnp.concatenate(parts, axis=0)


def _stack_group(q_ref, sink_ref, g):
    return _stack_rows(q_ref, g), sink_ref[g][:, 0:1]


def _unstack_pair(v, g, t):
    out = None
    for e in range(2):
        blk = v[(2 * t + e) * BLOCK : (2 * t + e + 1) * BLOCK, :]
        blk = pltpu.roll(blk, HEAD_DIM, 1) if e != g else blk
        out = blk if out is None else out + blk
    return out


def _exp_scores(scores, bias, sink):
    s = scores * (HEAD_DIM**-0.5) + bias
    m = jnp.maximum(jnp.max(s, axis=-1, keepdims=True), sink)
    return jnp.exp(s - m), jnp.exp(sink - m)


def _softmax_with_sink(scores, bias, sink):
    p, p_sink = _exp_scores(scores, bias, sink)
    inv = 1.0 / (jnp.sum(p, axis=-1, keepdims=True) + p_sink)
    return p * inv, p_sink * inv


def _kv_specs(nb):
    k_col = ATTN_WIDTH // KV_WIDTH
    specs = []
    for col in (k_col, k_col + 1):
        specs += [
            pl.BlockSpec((BLOCK, KV_WIDTH), lambda s, n, col=col: (s * nb, col)),
            pl.BlockSpec((BLOCK, KV_WIDTH), lambda s, n, col=col: (s * nb + jnp.maximum(n - 1, 0), col)),
            pl.BlockSpec((BLOCK, KV_WIDTH), lambda s, n, col=col: (s * nb + n, col)),
        ]
    return specs


def _attn_fwd(qkv, sink_rows, bias, t_len, name):
    rows = qkv.shape[0]
    nb = t_len // BLOCK
    n_seq = rows // t_len

    def body(q_ref, km_ref, kp_ref, kc_ref, vm_ref, vp_ref, vc_ref, sink_ref, bias_ref, o_ref):
        keys = jnp.concatenate([km_ref[...], kp_ref[...], kc_ref[...]], axis=0)
        vals = jnp.concatenate([vm_ref[...], vp_ref[...], vc_ref[...]], axis=0).astype(F32)
        for g in range(N_Q_HEADS // GROUP):
            qs, sink = _stack_group(q_ref, sink_ref, g)
            scores = lax.dot_general(qs.astype(BF16), keys, NT, preferred_element_type=F32)
            p, p_sink = _exp_scores(scores, bias_ref[0], sink)
            ext = jnp.where(_in_half(g), vals, 1.0).astype(BF16)
            oe = jnp.dot(p.astype(BF16), ext, preferred_element_type=F32)
            denom = pltpu.roll(oe, HEAD_DIM, 1) + p_sink
            o = jnp.where(_in_half(g), oe / denom, 0.0)
            for t in range(GROUP // 2):
                o_ref[:, _tile(g, t)] = _unstack_pair(o, g, t)

    return pl.pallas_call(
        body,
        name=name,
        grid=(n_seq, nb),
        in_specs=[pl.BlockSpec((BLOCK, ATTN_WIDTH), lambda s, n: (s * nb + n, 0))] + _kv_specs(nb) + [_full(sink_rows.shape), _bias_spec()],
        out_specs=pl.BlockSpec((BLOCK, ATTN_WIDTH), lambda s, n: (s * nb + n, 0)),
        out_shape=jax.ShapeDtypeStruct((rows, ATTN_WIDTH), F32),
    )(qkv, qkv, qkv, qkv, qkv, qkv, qkv, sink_rows, bias)


def _attn_bwd(qkv, sink_rows, bias, o, do, t_len, name):
    rows = qkv.shape[0]
    nb = t_len // BLOCK
    n_seq = rows // t_len

    def body(q_ref, km_ref, kp_ref, kc_ref, vm_ref, vp_ref, vc_ref, sink_ref, bias_ref, o_ref, do_ref, dq_ref, dk_ref, dv_ref, dsink_ref):
        s_id, n = pl.program_id(0), pl.program_id(1)

        @pl.when(n == 0)
        def _():
            dk_ref[...] = jnp.zeros_like(dk_ref)
            dv_ref[...] = jnp.zeros_like(dv_ref)

        @pl.when((n == 0) & (s_id == 0))
        def _():
            dsink_ref[...] = jnp.zeros_like(dsink_ref)

        keys = jnp.concatenate([km_ref[...], kp_ref[...], kc_ref[...]], axis=0)
        vals = jnp.concatenate([vm_ref[...], vp_ref[...], vc_ref[...]], axis=0)
        dkeys = jnp.zeros((3 * BLOCK, KV_WIDTH), F32)
        dvals = jnp.zeros((3 * BLOCK, KV_WIDTH), F32)
        for g in range(N_Q_HEADS // GROUP):
            qs, sink = _stack_group(q_ref, sink_ref, g)
            qsb = qs.astype(BF16)
            scores = lax.dot_general(qsb, keys, NT, preferred_element_type=F32)
            p, p_sink = _softmax_with_sink(scores, bias_ref[0], sink)
            dos = _stack_rows(do_ref, g)
            delta = jnp.sum(dos * _stack_rows(o_ref, g), axis=-1, keepdims=True)
            dosb = dos.astype(BF16)
            dp = lax.dot_general(dosb, vals, NT, preferred_element_type=F32)
            ds = (p * (dp - delta) * (HEAD_DIM**-0.5)).astype(BF16)
            dqs = jnp.where(_in_half(g), jnp.dot(ds, keys, preferred_element_type=F32), 0.0)
            for t in range(GROUP // 2):
                dq_ref[:, _tile(g, t)] = _unstack_pair(dqs, g, t)
            dkeys = dkeys + lax.dot_general(ds, qsb, TN, preferred_element_type=F32)
            dvals = dvals + lax.dot_general(p.astype(BF16), dosb, TN, preferred_element_type=F32)
            sink_term = p_sink * delta
            for hh in range(GROUP):
                head = GROUP * g + hh
                part = -jnp.sum(sink_term[hh * BLOCK : (hh + 1) * BLOCK, :], axis=0, keepdims=True)
                dsink_ref[head : head + 1, :] += jnp.broadcast_to(part, (1, BLOCK))
        r_prev = pl.multiple_of(jnp.maximum(n - 1, 0) * BLOCK, BLOCK)
        r_cur = pl.multiple_of(n * BLOCK, BLOCK)
        for acc, d in ((dk_ref, dkeys), (dv_ref, dvals)):
            acc[0:BLOCK, :] += d[0:BLOCK]
            acc[pl.ds(r_prev, BLOCK), :] += d[BLOCK : 2 * BLOCK]
            acc[pl.ds(r_cur, BLOCK), :] += d[2 * BLOCK :]

    q_blk = pl.BlockSpec((BLOCK, ATTN_WIDTH), lambda s, n: (s * nb + n, 0))
    seq_kv = pl.BlockSpec((t_len, KV_WIDTH), lambda s, n: (s, 0))
    return pl.pallas_call(
        body,
        name=name,
        grid=(n_seq, nb),
        in_specs=[q_blk] + _kv_specs(nb) + [_full(sink_rows.shape), _bias_spec(), q_blk, q_blk],
        out_specs=[q_blk, seq_kv, seq_kv, _full((N_Q_HEADS, BLOCK))],
        out_shape=[
            jax.ShapeDtypeStruct((rows, ATTN_WIDTH), F32),
            jax.ShapeDtypeStruct((rows, KV_WIDTH), F32),
            jax.ShapeDtypeStruct((rows, KV_WIDTH), F32),
            jax.ShapeDtypeStruct((N_Q_HEADS, BLOCK), F32),
        ],
    )(qkv, qkv, qkv, qkv, qkv, qkv, qkv, sink_rows, bias, o, do)


def _mix_out_fwd(y_rec, y_attn, h, wbuf, loc, g_rec, g_attn, ln_g, ln_b, alpha, name):
    rows, d = h.shape
    tm = _row_tile(rows)

    def body(yr_ref, ya_ref, h_ref, w_hbm, gr_ref, ga_ref, lg_ref, lb_ref, ho_ref, z_ref, hb_ref, w_ref, sems):
        _load_blocks(w_hbm, [(*loc, w_ref)], sems)
        nr = _rms(yr_ref[...], gr_ref[...]).astype(BF16)
        na = _rms(ya_ref[...], ga_ref[...]).astype(BF16)
        m = jnp.dot(nr, w_ref[0:REC_WIDTH, :], preferred_element_type=F32) + jnp.dot(na, w_ref[REC_WIDTH:, :], preferred_element_type=F32)
        z = alpha * h_ref[...] + m
        z_ref[...] = z
        xhat, _ = _ln_stats(z)
        ho = xhat * lg_ref[...] + lb_ref[...]
        ho_ref[...] = ho
        hb_ref[...] = ho.astype(BF16)

    row_d = pl.BlockSpec((tm, d), lambda i: (i, 0))
    row_h = pl.BlockSpec((tm, REC_WIDTH), lambda i: (i, 0))
    return pl.pallas_call(
        body,
        name=name,
        grid=(rows // tm,),
        in_specs=[row_h, row_h, row_d, ANY, _full((1, REC_WIDTH)), _full((1, ATTN_WIDTH)), _full((1, d)), _full((1, d))],
        out_specs=[row_d, row_d, row_d],
        out_shape=[jax.ShapeDtypeStruct((rows, d), F32)] * 2 + [jax.ShapeDtypeStruct((rows, d), BF16)],
        scratch_shapes=_weight_scratch([loc], d),
    )(y_rec, y_attn, h, wbuf, g_rec, g_attn, ln_g, ln_b)


def _mix_out_bwd(dh, z, y_rec, y_attn, wbuf, loc, g_rec, g_attn, ln_g, name):
    rows, d = dh.shape
    tm = _row_tile(rows)
    mix = REC_WIDTH + ATTN_WIDTH

    def body(dh_ref, z_ref, yr_ref, ya_ref, w_hbm, gr_ref, ga_ref, lg_ref, dz_ref, dzb_ref, dyr_ref, dya_ref, yn_ref, dgam_ref, dbet_ref, dgr_ref, dga_ref, w_ref, sems):
        _load_blocks(w_hbm, [(*loc, w_ref)], sems)

        @pl.when(pl.program_id(0) == 0)
        def _():
            for ref in (dgam_ref, dbet_ref, dgr_ref, dga_ref):
                ref[...] = jnp.zeros_like(ref)

        dz, dgam, dbet = _ln_bwd(dh_ref[...], z_ref[...], lg_ref[...])
        dgam_ref[...] += dgam
        dbet_ref[...] += dbet
        dz_ref[...] = dz
        dzb = dz.astype(BF16)
        dzb_ref[...] = dzb
        dyn = lax.dot_general(dzb, w_ref[...], NT, preferred_element_type=F32)
        dyr, dgr, nr = _rms_bwd(dyn[:, 0:REC_WIDTH], yr_ref[...], gr_ref[...])
        dya, dga, na = _rms_bwd(dyn[:, REC_WIDTH:], ya_ref[...], ga_ref[...])
        dyr_ref[...] = dyr
        dya_ref[...] = dya
        dgr_ref[...] += dgr
        dga_ref[...] += dga
        yn_ref[:, 0:REC_WIDTH] = nr.astype(BF16)
        yn_ref[:, REC_WIDTH:] = na.astype(BF16)

    row_d = pl.BlockSpec((tm, d), lambda i: (i, 0))
    row_h = pl.BlockSpec((tm, REC_WIDTH), lambda i: (i, 0))
    row_m = pl.BlockSpec((tm, mix), lambda i: (i, 0))
    return pl.pallas_call(
        body,
        name=name,
        grid=(rows // tm,),
        in_specs=[row_d, row_d, row_h, row_h, ANY, _full((1, REC_WIDTH)), _full((1, ATTN_WIDTH)), _full((1, d))],
        out_specs=[row_d, row_d, row_h, row_h, row_m, _full((1, d)), _full((1, d)), _full((1, REC_WIDTH)), _full((1, ATTN_WIDTH))],
        out_shape=[
            jax.ShapeDtypeStruct((rows, d), F32),
            jax.ShapeDtypeStruct((rows, d), BF16),
            jax.ShapeDtypeStruct((rows, REC_WIDTH), F32),
            jax.ShapeDtypeStruct((rows, ATTN_WIDTH), F32),
            jax.ShapeDtypeStruct((rows, mix), BF16),
            jax.ShapeDtypeStruct((1, d), F32),
            jax.ShapeDtypeStruct((1, d), F32),
            jax.ShapeDtypeStruct((1, REC_WIDTH), F32),
            jax.ShapeDtypeStruct((1, ATTN_WIDTH), F32),
        ],
        scratch_shapes=_weight_scratch([loc], d),
    )(dh, z, y_rec, y_attn, wbuf, g_rec, g_attn, ln_g)


def _loss_head(y, target, t_len, first_token, name):
    rows, d = y.shape
    tm = _row_tile(rows)

    def body(y_ref, t_ref, loss_ref, dy_ref):
        i = pl.program_id(0)

        @pl.when(i == 0)
        def _():
            loss_ref[...] = jnp.zeros_like(loss_ref)

        row = i * tm + lax.broadcasted_iota(jnp.int32, (tm, 1), 0)
        is_token = lax.rem(row, t_len) >= first_token
        err = jnp.where(is_token, y_ref[...] - t_ref[...], 0.0)
        dy_ref[...] = err / d
        per_row = jnp.sum(err * err, axis=-1, keepdims=True) / d
        loss_ref[...] += jnp.broadcast_to(0.5 * jnp.sum(per_row, axis=0, keepdims=True), (1, BLOCK))

    row_d = pl.BlockSpec((tm, d), lambda i: (i, 0))
    return pl.pallas_call(
        body,
        name=name,
        grid=(rows // tm,),
        in_specs=[row_d, row_d],
        out_specs=[_full((1, BLOCK)), row_d],
        out_shape=[jax.ShapeDtypeStruct((1, BLOCK), F32), jax.ShapeDtypeStruct((rows, d), F32)],
    )(y, target)


def _meta_grad(dh0, t_len, pad, name):
    rows, d = dh0.shape
    nb = t_len // BLOCK
    n_seq = rows // t_len

    def body(dh_ref, o_ref):
        @pl.when(pl.program_id(0) == 0)
        def _():
            o_ref[...] = jnp.zeros_like(o_ref)

        o_ref[...] += dh_ref[pad : pad + N_META, :]

    return pl.pallas_call(
        body,
        name=name,
        grid=(n_seq,),
        in_specs=[pl.BlockSpec((BLOCK, d), lambda s: (s * nb, 0))],
        out_specs=_full((N_META, d)),
        out_shape=jax.ShapeDtypeStruct((N_META, d), F32),
    )(dh0)


def _adamw_math(w, g, m, v):
    nm = ADAM_B1 * m + (1.0 - ADAM_B1) * g
    nv = ADAM_B2 * v + (1.0 - ADAM_B2) * (g * g)
    m_hat = nm / (1.0 - ADAM_B1**ADAM_STEP)
    v_hat = nv / (1.0 - ADAM_B2**ADAM_STEP)
    return -ADAM_LR * (m_hat / (jnp.sqrt(v_hat) + ADAM_EPS) + ADAM_WD * w), nm, nv


def _reduce_update(parts, own, w, m, v, outs, layer, name):
    _, nr, d = parts.shape
    _, a_dim, b_dim = w.shape
    assert (a_dim, b_dim) == (nr, d), (w.shape, parts.shape)

    def body(p_ref, own_hbm, w_ref, m_ref, v_ref, g_in, dl_in, nm_in, nv_in, g_ref, dl_ref, nm_ref, nv_ref, own_v, acc, sem):
        me = 4 * lax.axis_index("x") + 2 * lax.axis_index("y") + lax.axis_index("c")
        cp = pltpu.make_async_copy(own_hbm.at[me], own_v, sem)
        cp.start()
        cp.wait()
        acc[...] = own_v[...].astype(F32)
        for dev in range(N_DEV):

            @pl.when(me != dev)
            def _():
                acc[...] += p_ref[dev].astype(F32)

        g = acc[...]
        g_ref[0] = g
        dl_ref[0], nm_ref[0], nv_ref[0] = _adamw_math(w_ref[0], g, m_ref[0], v_ref[0])

    at_layer = pl.BlockSpec((1, a_dim, b_dim), lambda i: (layer, 0, 0))
    return pl.pallas_call(
        body,
        name=name,
        grid=(1,),
        in_specs=[_full((N_DEV, nr, d)), ANY, at_layer, at_layer, at_layer, ANY, ANY, ANY, ANY],
        out_specs=[at_layer] * 4,
        out_shape=[jax.ShapeDtypeStruct(w.shape, F32)] * 4,
        input_output_aliases={5: 0, 6: 1, 7: 2, 8: 3},
        scratch_shapes=[pltpu.VMEM((nr, d), parts.dtype), pltpu.VMEM((nr, d), F32), pltpu.SemaphoreType.DMA(())],
        compiler_params=_params(VMEM_BIG),
    )(parts, own, w, m, v, *outs)


def _adamw(w, g, m, v, name):
    shape = w.shape
    cols = shape[-1]
    rows = w.size // cols
    tr = rows
    for cand in (512, 256, 128, 64):
        if rows % cand == 0 and rows > cand:
            tr = cand
            break

    def body(w_ref, g_ref, m_ref, v_ref, d_ref, nm_ref, nv_ref):
        d_ref[...], nm_ref[...], nv_ref[...] = _adamw_math(w_ref[...], g_ref[...], m_ref[...], v_ref[...])

    blk = pl.BlockSpec((tr, cols), lambda i: (i, 0))
    flat = [a.reshape(rows, cols) for a in (w, g, m, v)]
    outs = pl.pallas_call(
        body,
        name=name,
        grid=(rows // tr,),
        in_specs=[blk] * 4,
        out_specs=[blk] * 3,
        out_shape=[jax.ShapeDtypeStruct((rows, cols), F32)] * 3,
    )(*flat)
    return [o.reshape(shape) for o in outs]


WEIGHTS = ["meta_tokens", "ffn1_w_gate", "ffn1_w_up", "ffn1_w_down", "ln1_g", "ln1_b", "w_in", "conv_w", "conv_b", "gate_a_w", "gate_a_b",
           "gate_x_w", "gate_x_b", "lru_lambda", "attn_sinks", "norm_rec_g", "norm_attn_g", "w_out", "ln2_g", "ln2_b", "ffn2_w_gate",
           "ffn2_w_up", "ffn2_w_down", "ln3_g", "ln3_b"]
BIG = [("ffn1_w_gate", True), ("ffn1_w_up", True), ("ffn1_w_down", False), ("w_in", True), ("w_out", False),
       ("ffn2_w_gate", True), ("ffn2_w_up", True), ("ffn2_w_down", False)]


SMALL = ["ln1_g", "ln1_b", "ln2_g", "ln2_b", "ln3_g", "ln3_b", "conv_b", "gate_a_b", "gate_x_b", "lru_lambda", "norm_rec_g", "norm_attn_g",
         "conv_w", "gate_a_w", "gate_x_w", "attn_sinks"]
GATHER_GROUPS = [["ffn1_w_gate", "ffn1_w_up", "ffn1_w_down"], ["w_in", "w_out"], ["ffn2_w_gate", "ffn2_w_up", "ffn2_w_down"]]
EXCHANGE_GROUPS = [["ffn2_w_gate", "ffn2_w_up", "ffn2_w_down", "w_out"], ["w_in", "ffn1_w_gate", "ffn1_w_up", "ffn1_w_down"]]


def _rope_tables(t_len, pad):
    pos = (jnp.arange(t_len) - pad).astype(F32)
    inv_freq = ROPE_THETA ** (-jnp.arange(0, ROPE_DIM, 2, dtype=F32) / ROPE_DIM)
    ang = pos[:, None] * inv_freq[None, :]
    cos, sin = jnp.cos(ang), jnp.sin(ang)
    half = ROPE_DIM // 2
    rest = jnp.zeros((t_len, HEAD_DIM - ROPE_DIM), F32)
    zero = jnp.zeros((t_len, half), F32)
    cos_h = jnp.concatenate([cos, cos, rest + 1.0], axis=1)
    up_h = jnp.concatenate([-sin, zero, rest], axis=1)
    down_h = jnp.concatenate([zero, sin, rest], axis=1)
    return [jnp.tile(t, (1, ATTN_WIDTH // HEAD_DIM)) for t in (cos_h, up_h, down_h)]


def _gate_tiles(w):
    z = jnp.zeros((HEAD_DIM, HEAD_DIM), w.dtype)
    tiles = [jnp.block([[w[2 * j], z], [z, w[2 * j + 1]]]) for j in range(w.shape[0] // 2)]
    return jnp.stack(tiles).astype(BF16)


def _gate_blocks(tiles):
    out = []
    for j in range(tiles.shape[0]):
        out += [tiles[j, :HEAD_DIM, :HEAD_DIM], tiles[j, HEAD_DIM:, HEAD_DIM:]]
    return jnp.stack(out)


def kernel(x, meta_tokens, ffn1_w_gate, ffn1_w_up, ffn1_w_down, ln1_g, ln1_b, w_in, conv_w, conv_b, gate_a_w, gate_a_b, gate_x_w, gate_x_b, lru_lambda, attn_sinks, norm_rec_g, norm_attn_g, w_out, ln2_g, ln2_b, ffn2_w_gate, ffn2_w_up, ffn2_w_down, ln3_g, ln3_b, loss_target, m_meta_tokens, m_ffn1_w_gate, m_ffn1_w_up, m_ffn1_w_down, m_ln1_g, m_ln1_b, m_w_in, m_conv_w, m_conv_b, m_gate_a_w, m_gate_a_b, m_gate_x_w, m_gate_x_b, m_lru_lambda, m_attn_sinks, m_norm_rec_g, m_norm_attn_g, m_w_out, m_ln2_g, m_ln2_b, m_ffn2_w_gate, m_ffn2_w_up, m_ffn2_w_down, m_ln3_g, m_ln3_b, v_meta_tokens, v_ffn1_w_gate, v_ffn1_w_up, v_ffn1_w_down, v_ln1_g, v_ln1_b, v_w_in, v_conv_w, v_conv_b, v_gate_a_w, v_gate_a_b, v_gate_x_w, v_gate_x_b, v_lru_lambda, v_attn_sinks, v_norm_rec_g, v_norm_attn_g, v_w_out, v_ln2_g, v_ln2_b, v_ffn2_w_gate, v_ffn2_w_up, v_ffn2_w_down, v_ln3_g, v_ln3_b):
    given = dict(locals())
    w = {k: given[k] for k in WEIGHTS}
    n_seq, seq, d = x.shape
    depth = ln1_g.shape[0]
    alpha = (2.0 * depth) ** 0.25
    pad = (-(N_META + seq)) % BLOCK
    t_len = pad + N_META + seq
    rows = n_seq * t_len
    me = 4 * lax.axis_index("x") + 2 * lax.axis_index("y") + lax.axis_index("c")

    transposed = dict(BIG)

    def turned(a, name):
        return jnp.swapaxes(a, 1, 2) if transposed[name] else a

    small = jnp.concatenate([meta_tokens, conv_w.reshape(-1, BLOCK)], axis=0)
    (small_all,) = _all_gather([small], small, "gather_small")
    sent, loc, gathers, chain = {}, {}, {}, small_all

    def start_gather(l, k):
        shard = jnp.concatenate([sent[name][l] for name in GATHER_GROUPS[k]], axis=0)
        zone = lax.dynamic_update_slice(lax.empty((N_DEV,) + shard.shape, BF16), shard[None], (me, 0, 0))
        gathers[(l, k)] = _push_start([shard], [zone], chain, False, f"gather_start_{l}_{k}")
        return gathers[(l, k)][2][0]

    for k, group in enumerate(GATHER_GROUPS):
        off = 0
        for name in group:
            sent[name] = turned(w[name], name).astype(BF16)
            loc[name] = (off, sent[name].shape[1])
            off += sent[name].shape[1]
        chain = start_gather(0, k)
    for l in range(1, depth):
        for k in range(len(GATHER_GROUPS)):
            chain = start_gather(l, k)
    ffn1_loc = [loc["ffn1_w_gate"], loc["ffn1_w_up"], loc["ffn1_w_down"]]
    ffn2_loc = [loc["ffn2_w_gate"], loc["ffn2_w_up"], loc["ffn2_w_down"]]
    meta_full = small_all[:, :N_META, :].transpose(1, 0, 2).reshape(N_META, d)
    conv_shard = conv_w.shape[-1]
    conv_full = small_all[:, N_META:, :].reshape(N_DEV, depth, CONV_WIDTH, conv_shard).transpose(1, 2, 0, 3).reshape(depth, CONV_WIDTH, REC_WIDTH)

    cos, sin_up, sin_down = _rope_tables(t_len, pad)
    bias = _attn_bias(pad)
    row1 = lambda a: a.reshape(1, -1)

    h = jnp.concatenate([jnp.zeros((n_seq, pad, d), F32), jnp.broadcast_to(meta_full[None], (n_seq, N_META, d)), x], axis=1).reshape(rows, d)
    target = jnp.pad(loss_target, ((0, 0), (pad + N_META, 0), (0, 0))).reshape(rows, d)
    hb = h.astype(BF16)
    saved, wbufs = [], []

    def gathered(l, k, after):
        _, (buf,) = _push_wait(*gathers[(l, k)], after, False, f"gather_wait_{l}_{k}")
        return buf

    for l in range(depth):
        w_ffn1 = gathered(l, 0, chain if l == 0 else h)
        s = {"h0": hb}
        h, s["g1"], s["u1"], s["z1"], hb = _ffn_fwd(h, w_ffn1, ffn1_loc, row1(ln1_g[l]), row1(ln1_b[l]), alpha, "ffn1_fwd")
        s["h1"] = hb
        s["wa"], s["wx"] = _gate_tiles(gate_a_w[l]), _gate_tiles(gate_x_w[l])
        s["sinks"] = jnp.broadcast_to(jnp.repeat(attn_sinks[l].reshape(N_Q_HEADS // GROUP, GROUP), BLOCK, axis=1)[:, :, None], (N_Q_HEADS // GROUP, GROUP * BLOCK, BLOCK))
        wbuf = gathered(l, 1, h)
        proj = _proj_fwd(h, wbuf, loc["w_in"], "proj_fwd")
        s["proj"] = proj
        s["xc"], s["hrec"], s["y_rec"] = _rglru_fwd(proj, conv_full[l], row1(conv_b[l]), s["wa"], row1(gate_a_b[l]), s["wx"], row1(gate_x_b[l]),
                                                     row1(lru_lambda[l]), t_len, pad, "rglru_fwd")
        s["qkv"] = _rope_fwd(proj, cos, sin_up, sin_down, t_len, "rope_fwd")
        s["y_attn"] = _attn_fwd(s["qkv"], s["sinks"], bias, t_len, "attn_fwd")
        h, s["z2"], hb = _mix_out_fwd(s["y_rec"], s["y_attn"], h, wbuf, loc["w_out"], row1(norm_rec_g[l]), row1(norm_attn_g[l]), row1(ln2_g[l]), row1(ln2_b[l]), alpha, "mix_out_fwd")
        s["h2"] = hb
        w_ffn2 = gathered(l, 2, h)
        h, s["g2"], s["u2"], s["z3"], hb = _ffn_fwd(h, w_ffn2, ffn2_loc, row1(ln3_g[l]), row1(ln3_b[l]), alpha, "ffn2_fwd")
        saved.append(s)
        wbufs.append((w_ffn1, wbuf, w_ffn2))
    loss_part, dh = _loss_head(h, target, t_len, pad + N_META, "loss_head")

    exchanges = {}
    small_grads = [None] * depth
    behind = dh

    def exchange(l, k, bg, chain):
        gs = [bg[name].reshape(N_DEV, -1, d) for name in EXCHANGE_GROUPS[k]]
        zones = [lax.empty(g.shape, g.dtype) for g in gs]
        exchanges[(l, k)] = _push_start(gs, zones, chain, True, f"exchange_start_{l}_{k}")
        return exchanges[(l, k)][2][0]

    for l in reversed(range(depth)):
        (w_ffn1, wbuf, w_ffn2), s = wbufs[l], saved[l]
        bg, sg = {}, {}
        dh, dg, du, act, dyb, sg["ln3_g"], sg["ln3_b"] = _ffn_bwd(dh, s["z3"], s["g2"], s["u2"], w_ffn2, ffn2_loc, row1(ln3_g[l]), behind, alpha, "ffn2_bwd")
        bg["ffn2_w_gate"] = _mm_tn(dg, s["h2"], BF16, "ffn_wgrad_in")
        bg["ffn2_w_up"] = _mm_tn(du, s["h2"], BF16, "ffn_wgrad_in")
        bg["ffn2_w_down"] = _mm_tn(act, dyb, BF16, "ffn_wgrad_down")
        dz, dzb, dy_rec, dy_attn, yn, sg["ln2_g"], sg["ln2_b"], sg["norm_rec_g"], sg["norm_attn_g"] = _mix_out_bwd(
            dh, s["z2"], s["y_rec"], s["y_attn"], wbuf, loc["w_out"], row1(norm_rec_g[l]), row1(norm_attn_g[l]), row1(ln2_g[l]), "mix_out_bwd")
        bg["w_out"] = _mm_tn(yn, dzb, BF16, "w_out_wgrad")
        behind = exchange(l, 0, bg, behind)
        dxr, dgt, sg["conv_w"], sg["conv_b"], dwa, sg["gate_a_b"], dwx, sg["gate_x_b"], sg["lru_lambda"] = _rglru_bwd(
            dy_rec, s["hrec"], s["proj"], s["xc"], conv_full[l], s["wa"], row1(gate_a_b[l]), s["wx"], row1(gate_x_b[l]), row1(lru_lambda[l]), t_len, pad, "rglru_bwd")
        sg["gate_a_w"], sg["gate_x_w"] = _gate_blocks(dwa), _gate_blocks(dwx)
        dq, dk, dv, dsink = _attn_bwd(s["qkv"], s["sinks"], bias, s["y_attn"], dy_attn, t_len, "attn_bwd")
        sg["attn_sinks"] = dsink[:, 0]
        dqkv = _rope_bwd(dq, dk, dv, cos, sin_up, sin_down, t_len, "rope_bwd")
        dproj = jnp.concatenate([dxr, dgt, dqkv], axis=1)
        bg["w_in"] = _mm_tn(dproj, s["h1"], BF16, "w_in_wgrad")
        dh = _proj_bwd(dproj, wbuf, loc["w_in"], dz, alpha, "proj_bwd")
        dh, dg, du, act, dyb, sg["ln1_g"], sg["ln1_b"] = _ffn_bwd(dh, s["z1"], s["g1"], s["u1"], w_ffn1, ffn1_loc, row1(ln1_g[l]), behind, alpha, "ffn1_bwd")
        small_grads[l] = sg
        if l == 0:
            dmeta = _meta_grad(dh, t_len, pad, "meta_grad")
            pieces = [small_grads[j][name].reshape(-1) for j in range(depth) for name in SMALL] + [dmeta.reshape(-1), loss_part[0, :1]]
            sizes = [p.shape[0] for p in pieces]
            flat = jnp.concatenate(pieces)
            width = 1024
            n_rows = -(-flat.shape[0] // (8 * width)) * 8
            flat = jnp.pad(flat, (0, n_rows * width - flat.shape[0])).reshape(n_rows, width)
            zone = lax.dynamic_update_slice(lax.empty((N_DEV, n_rows, width), F32), flat[None], (me, 0, 0))
            small_gather = _push_start([flat], [zone], behind, False, "small_grads_start")
            behind = small_gather[2][0]
        bg["ffn1_w_gate"] = _mm_tn(dg, s["h0"], BF16, "ffn_wgrad_in")
        bg["ffn1_w_up"] = _mm_tn(du, s["h0"], BF16, "ffn_wgrad_in")
        bg["ffn1_w_down"] = _mm_tn(act, dyb, BF16, "ffn_wgrad_down")
        behind = exchange(l, 1, bg, behind)
    grad_x = dh.reshape(n_seq, t_len, d)[:, pad + N_META :, :]

    state = {name: [turned(a, name) for a in (w[name], given["m_" + name], given["v_" + name])] for name, _ in BIG}
    outs = {name: [lax.empty(state[name][0].shape, F32) for _ in range(4)] for name, _ in BIG}

    def reduce_group(l, k, after):
        send_sems, recv_sems, srcs, zones = exchanges[(l, k)]
        owns, parts = _push_wait(send_sems, recv_sems, srcs, zones, after, True, f"exchange_wait_{l}_{k}")
        for name, own, p in zip(EXCHANGE_GROUPS[k], owns, parts):
            outs[name] = _reduce_update(p, own, *state[name], outs[name], l, "reduce_update")
        return outs[EXCHANGE_GROUPS[k][-1]][0]

    grads = {}
    order = [(l, k) for l in reversed(range(depth)) for k in range(len(EXCHANGE_GROUPS))]
    after = behind
    for l, k in order[:-1]:
        after = reduce_group(l, k, after)
    _, (flat_all,) = _push_wait(*small_gather, after, False, "small_grads_wait")
    total = _sum8(flat_all, "sum_small_grads").reshape(-1)
    offs = [0]
    for sz in sizes:
        offs.append(offs[-1] + sz)
    taken = [total[offs[k] : offs[k + 1]] for k in range(len(sizes))]
    for j, name in enumerate(SMALL):
        full_shape = (depth,) + ((CONV_WIDTH, REC_WIDTH) if name == "conv_w" else w[name].shape[1:])
        grads[name] = jnp.stack([taken[l * len(SMALL) + j] for l in range(depth)]).reshape(full_shape)
    grads["conv_w"] = lax.dynamic_slice_in_dim(grads["conv_w"], me * conv_shard, conv_shard, axis=2)
    meta_shard = meta_tokens.shape[1]
    grads["meta_tokens"] = lax.dynamic_slice_in_dim(taken[-2].reshape(N_META, d), me * meta_shard, meta_shard, axis=1)
    loss = taken[-1][0]
    deltas, new_m, new_v = {}, {}, {}
    for name in WEIGHTS:
        if name not in dict(BIG):
            deltas[name], new_m[name], new_v[name] = _adamw(w[name], grads[name], given["m_" + name], given["v_" + name], "adamw")

    reduce_group(*order[-1], new_v[WEIGHTS[-1]])
    for name, _ in BIG:
        grads[name], deltas[name], new_m[name], new_v[name] = [turned(a, name) for a in outs[name]]
    return (loss, grad_x, *[grads[k] for k in WEIGHTS], *[deltas[k] for k in WEIGHTS], *[new_m[k] for k in WEIGHTS], *[new_v[k] for k in WEIGHTS])
```

```python
import functools

import jax
import jax.numpy as jnp
from jax import lax
from jax.experimental import pallas as pl
from jax.experimental.pallas import tpu as pltpu

F32 = jnp.float32
BF16 = jnp.bfloat16
MESH = pl.DeviceIdType.MESH

N_DEV = 8
N_META = 16
BLOCK = 128
WINDOW = 128
REC_WIDTH = 512
ATTN_WIDTH = 512
KV_WIDTH = 128
HEAD_DIM = 64
N_Q_HEADS = 8
ROPE_DIM = 16
ROPE_THETA = 500000.0
CONV_WIDTH = 4
LRU_C = 8.0
LN_EPS = 1e-5
RMS_EPS = 1e-6
NEG_INF = -1e30
ADAM_LR, ADAM_B1, ADAM_B2, ADAM_EPS, ADAM_WD, ADAM_STEP = 0.001, 0.9, 0.999, 1e-08, 0.01, 10

NT = (((1,), (1,)), ((), ()))
TN = (((0,), (0,)), ((), ()))
VMEM_BIG = 56 * 1024 * 1024


def _params(vmem=None):
    return pltpu.CompilerParams(vmem_limit_bytes=vmem)


def _row_tile(rows):
    return 256 if rows % 256 == 0 else 128


def _sigmoid(x):
    return 1.0 / (1.0 + jnp.exp(-x))


def _expm1_nonpos(x):
    series = x * (1.0 + 0.5 * x * (1.0 + x / 3.0 * (1.0 + 0.25 * x * (1.0 + 0.2 * x))))
    return jnp.where(x > -0.05, series, jnp.exp(x) - 1.0)


def _softplus(x):
    e = jnp.exp(-jnp.abs(x))
    log1p = jnp.where(e < 1e-3, e * (1.0 - e * (0.5 - e / 3.0)), jnp.log(1.0 + e))
    return jnp.maximum(x, 0.0) + log1p


def _gelu_and_grad(x):
    c0 = 0.7978845608028654
    x2 = x * x
    t = jnp.tanh(c0 * (x + 0.044715 * x * x2))
    gelu = 0.5 * x * (1.0 + t)
    grad = 0.5 * (1.0 + t) + 0.5 * x * (1.0 - t * t) * c0 * (1.0 + 3.0 * 0.044715 * x2)
    return gelu, grad


def _ln_stats(z):
    mu = jnp.mean(z, axis=-1, keepdims=True)
    zc = z - mu
    var = jnp.mean(zc * zc, axis=-1, keepdims=True)
    rstd = lax.rsqrt(var + LN_EPS)
    return zc * rstd, rstd


def _ln_bwd(dy, z, gamma):
    xhat, rstd = _ln_stats(z)
    dxh = dy * gamma
    m1 = jnp.mean(dxh, axis=-1, keepdims=True)
    m2 = jnp.mean(dxh * xhat, axis=-1, keepdims=True)
    dz = rstd * (dxh - m1 - xhat * m2)
    return dz, jnp.sum(dy * xhat, axis=0, keepdims=True), jnp.sum(dy, axis=0, keepdims=True)


def _rms(y, gamma):
    r = lax.rsqrt(jnp.mean(y * y, axis=-1, keepdims=True) + RMS_EPS)
    return y * r * gamma


def _rms_bwd(dout, y, gamma):
    r = lax.rsqrt(jnp.mean(y * y, axis=-1, keepdims=True) + RMS_EPS)
    n = y * r
    dn = dout * gamma
    dy = r * (dn - n * jnp.mean(dn * n, axis=-1, keepdims=True))
    return dy, jnp.sum(dout * n, axis=0, keepdims=True), n * gamma


def _load_blocks(w_hbm, items, sems):
    @pl.when(pl.program_id(0) == 0)
    def _():
        copies = []
        for k, (off, nr, dst) in enumerate(items):
            for dev in range(N_DEV):
                copies.append(pltpu.make_async_copy(w_hbm.at[dev, pl.ds(off, nr), :], dst.at[pl.ds(dev * nr, nr), :], sems.at[k, dev]))
        for cp in copies:
            cp.start()
        for cp in copies:
            cp.wait()


def _full(shape):
    return pl.BlockSpec(shape, lambda *_: (0,) * len(shape))


ANY = pl.BlockSpec(memory_space=pl.ANY)


def _all_gather(xs, after, name):
    n = len(xs)

    def body(*refs):
        ins, outs = refs[:n], refs[n + 1 : 2 * n + 1]
        send_sems, recv_sems, local_sems = refs[2 * n + 1 :]
        x, y, c = lax.axis_index("x"), lax.axis_index("y"), lax.axis_index("c")
        me, sibling = (x, y, c), (x, y, 1 - c)
        chips = [(1 - x, y), (x, 1 - y), (1 - x, 1 - y)]

        def slot(i, dev):
            return outs[i].at[4 * dev[0] + 2 * dev[1] + dev[2]]

        def copy(i, k, block, to, src=None):
            return pltpu.make_async_remote_copy(
                src_ref=slot(i, block) if src is None else src,
                dst_ref=slot(i, block),
                send_sem=send_sems.at[i, k],
                recv_sem=recv_sems.at[i, k],
                device_id=to,
                device_id_type=MESH,
            )

        mine = [pltpu.make_async_copy(ins[i], slot(i, me), local_sems.at[i]) for i in range(n)]
        for cp in mine:
            cp.start()
        first = []
        for i in range(n):
            first.append(copy(i, 0, me, sibling, src=ins[i]))
            first += [copy(i, 1 + j, me, (*chip, c), src=ins[i]) for j, chip in enumerate(chips)]
        for cp in first:
            cp.start()
        passed = []
        for j, chip in enumerate(chips):
            for i in range(n):
                copy(i, 1 + j, (*chip, c), me).wait_recv()
                fwd = copy(i, 4 + j, (*chip, c), sibling)
                fwd.start()
                passed.append(fwd)
        for i in range(n):
            copy(i, 0, sibling, me).wait_recv()
            for j, chip in enumerate(chips):
                copy(i, 4 + j, (*chip, 1 - c), me).wait_recv()
        for cp in first + passed:
            cp.wait_send()
        for cp in mine:
            cp.wait()

    return pl.pallas_call(
        body,
        name=name,
        out_shape=[jax.ShapeDtypeStruct((N_DEV,) + a.shape, a.dtype) for a in xs],
        in_specs=[ANY] * (n + 1),
        out_specs=[ANY] * n,
        scratch_shapes=[pltpu.SemaphoreType.DMA((n, 7)), pltpu.SemaphoreType.DMA((n, 7)), pltpu.SemaphoreType.DMA((n,))],
    )(*xs, after)


HBM = pl.BlockSpec(memory_space=pltpu.HBM)
SEM = pl.BlockSpec(memory_space=pltpu.SEMAPHORE)
EFFECT = pltpu.SideEffectType.DATAFLOW_SIDE_EFFECTING


def _push_copies(ins, lands, send_sems, recv_sems, scatter):
    x, y, c = lax.axis_index("x"), lax.axis_index("y"), lax.axis_index("c")
    me = 4 * x + 2 * y + c
    copies = []
    for k in range(1, N_DEV):
        px = 1 - x if (k >> 2) & 1 else x
        py = 1 - y if (k >> 1) & 1 else y
        pc = 1 - c if k & 1 else c
        for i in range(len(ins)):
            copies.append(
                pltpu.make_async_remote_copy(
                    src_ref=ins[i].at[4 * px + 2 * py + pc] if scatter else ins[i],
                    dst_ref=lands[i].at[me],
                    send_sem=send_sems.at[i * (N_DEV - 1) + k - 1],
                    recv_sem=recv_sems.at[i * (N_DEV - 1) + k - 1],
                    device_id=(px, py, pc),
                    device_id_type=MESH,
                )
            )
    return copies


def _push_start(xs, lands, chain, scatter, name):
    n = len(xs)

    def body(*refs):
        ins, zones = refs[:n], refs[n : 2 * n]
        for cp in _push_copies(ins, zones, refs[2 * n + 1], refs[2 * n + 2], scatter):
            cp.start()

    outs = pl.pallas_call(
        body,
        name=name,
        out_shape=(
            pltpu.SemaphoreType.DMA((n * (N_DEV - 1),)),
            pltpu.SemaphoreType.DMA((n * (N_DEV - 1),)),
            *[pltpu.HBM(a.shape, a.dtype) for a in list(xs) + list(lands)],
        ),
        in_specs=[HBM] * (2 * n) + [ANY],
        out_specs=(SEM, SEM, *[HBM] * (2 * n)),
        input_output_aliases={i: 2 + i for i in range(2 * n)},
        compiler_params=pltpu.CompilerParams(has_side_effects=EFFECT),
    )(*[pltpu.with_memory_space_constraint(a, pltpu.HBM) for a in list(xs) + list(lands)], chain)
    return outs[0], outs[1], list(outs[2 : 2 + n]), list(outs[2 + n : 2 + 2 * n])


def _push_wait(send_sems, recv_sems, srcs, lands, after, scatter, name):
    n = len(srcs)

    def body(*refs):
        ins, zones = refs[:n], refs[n : 2 * n]
        for cp in _push_copies(ins, zones, refs[2 * n], refs[2 * n + 1], scatter):
            cp.wait_send()
            cp.wait_recv()

    outs = pl.pallas_call(
        body,
        name=name,
        out_shape=[pltpu.HBM(a.shape, a.dtype) for a in srcs + lands],
        in_specs=[HBM] * (2 * n) + [SEM, SEM, ANY],
        out_specs=[HBM] * (2 * n),
        input_output_aliases={i: i for i in range(2 * n)},
        compiler_params=pltpu.CompilerParams(has_side_effects=EFFECT),
    )(*srcs, *lands, send_sems, recv_sems, after)
    return list(outs[:n]), list(outs[n:])


def _sum8(parts, name):
    _, rows, cols = parts.shape
    tr = rows
    for cand in (512, 256, 128, 64, 32, 16):
        if rows % cand == 0 and rows > cand:
            tr = cand
            break

    def body(p_ref, o_ref):
        acc = p_ref[0].astype(F32)
        for d in range(1, N_DEV):
            acc = acc + p_ref[d].astype(F32)
        o_ref[...] = acc

    return pl.pallas_call(
        body,
        name=name,
        grid=(rows // tr,),
        in_specs=[pl.BlockSpec((N_DEV, tr, cols), lambda i: (0, i, 0))],
        out_specs=pl.BlockSpec((tr, cols), lambda i: (i, 0)),
        out_shape=jax.ShapeDtypeStruct((rows, cols), F32),
    )(parts)


def _mm_tn(a, b, out_dtype, name):
    rows, m = a.shape
    n = b.shape[1]
    tm = m // 2 if (m // 2) % 128 == 0 and m > 512 else m
    tr = next(t for t in (2176, 1024, 512, 256, 128) if rows % t == 0)
    nk = rows // tr

    def body(a_ref, b_ref, o_ref, acc):
        k = pl.program_id(1)
        part = lax.dot_general(a_ref[...].astype(BF16), b_ref[...].astype(BF16), TN, preferred_element_type=F32)

        @pl.when(k == 0)
        def _():
            acc[...] = part

        @pl.when(k > 0)
        def _():
            acc[...] += part

        @pl.when(k == nk - 1)
        def _():
            o_ref[...] = acc[...].astype(o_ref.dtype)

    return pl.pallas_call(
        body,
        name=name,
        grid=(m // tm, nk),
        in_specs=[pl.BlockSpec((tr, tm), lambda i, k: (k, i)), pl.BlockSpec((tr, n), lambda i, k: (k, 0))],
        out_specs=pl.BlockSpec((tm, n), lambda i, k: (i, 0)),
        out_shape=jax.ShapeDtypeStruct((m, n), out_dtype),
        scratch_shapes=[pltpu.VMEM((tm, n), F32)],
        compiler_params=_params(VMEM_BIG),
    )(a, b)


def _weight_scratch(locs, d):
    return [pltpu.VMEM((N_DEV * nr, d), BF16) for _, nr in locs] + [pltpu.SemaphoreType.DMA((len(locs), N_DEV))]


def _proj_fwd(h, wbuf, loc, name):
    rows, d = h.shape
    n = N_DEV * loc[1]
    tm = _row_tile(rows)

    def body(h_ref, w_hbm, o_ref, w, sems):
        _load_blocks(w_hbm, [(*loc, w)], sems)
        o_ref[...] = lax.dot_general(h_ref[...].astype(BF16), w[...], NT, preferred_element_type=F32)

    return pl.pallas_call(
        body,
        name=name,
        grid=(rows // tm,),
        in_specs=[pl.BlockSpec((tm, d), lambda i: (i, 0)), ANY],
        out_specs=pl.BlockSpec((tm, n), lambda i: (i, 0)),
        out_shape=jax.ShapeDtypeStruct((rows, n), F32),
        scratch_shapes=_weight_scratch([loc], d),
        compiler_params=_params(VMEM_BIG),
    )(h, wbuf)


def _proj_bwd(dproj, wbuf, loc, dz, alpha, name):
    rows, n = dproj.shape
    d = dz.shape[1]
    tm = _row_tile(rows)

    def body(dp_ref, w_hbm, dz_ref, o_ref, w, sems):
        _load_blocks(w_hbm, [(*loc, w)], sems)
        o_ref[...] = alpha * dz_ref[...] + jnp.dot(dp_ref[...], w[...], preferred_element_type=F32)

    return pl.pallas_call(
        body,
        name=name,
        grid=(rows // tm,),
        in_specs=[pl.BlockSpec((tm, n), lambda i: (i, 0)), ANY, pl.BlockSpec((tm, d), lambda i: (i, 0))],
        out_specs=pl.BlockSpec((tm, d), lambda i: (i, 0)),
        out_shape=jax.ShapeDtypeStruct((rows, d), F32),
        scratch_shapes=_weight_scratch([loc], d),
        compiler_params=_params(VMEM_BIG),
    )(dproj, wbuf, dz)


FFN_CHUNKS = 1


def _ffn_fwd(h, wbuf, locs, ln_g, ln_b, alpha, name):
    rows, d = h.shape
    f = N_DEV * locs[0][1]
    tm = _row_tile(rows)
    tf = f // FFN_CHUNKS

    def body(h_ref, w_hbm, lg_ref, lb_ref, ho_ref, g_ref, u_ref, z_ref, hb_ref, wg, wu, wdv, sems):
        _load_blocks(w_hbm, [(*locs[0], wg), (*locs[1], wu), (*locs[2], wdv)], sems)
        x = h_ref[...]
        xb = x.astype(BF16)
        y = None
        for c in range(FFN_CHUNKS):
            sl = slice(c * tf, (c + 1) * tf)
            g = lax.dot_general(xb, wg[sl, :], NT, preferred_element_type=F32)
            u = lax.dot_general(xb, wu[sl, :], NT, preferred_element_type=F32)
            g_ref[:, sl] = g.astype(BF16)
            u_ref[:, sl] = u.astype(BF16)
            a = (g * _sigmoid(g) * u).astype(BF16)
            part = jnp.dot(a, wdv[sl, :], preferred_element_type=F32)
            y = part if y is None else y + part
        z = alpha * x + 0.5 * y
        z_ref[...] = z
        xhat, _ = _ln_stats(z)
        ho = xhat * lg_ref[...] + lb_ref[...]
        ho_ref[...] = ho
        hb_ref[...] = ho.astype(BF16)

    row_d = pl.BlockSpec((tm, d), lambda i: (i, 0))
    row_f = pl.BlockSpec((tm, f), lambda i: (i, 0))
    return pl.pallas_call(
        body,
        name=name,
        grid=(rows // tm,),
        in_specs=[row_d, ANY, _full((1, d)), _full((1, d))],
        out_specs=[row_d, row_f, row_f, row_d, row_d],
        out_shape=[
            jax.ShapeDtypeStruct((rows, d), F32),
            jax.ShapeDtypeStruct((rows, f), BF16),
            jax.ShapeDtypeStruct((rows, f), BF16),
            jax.ShapeDtypeStruct((rows, d), F32),
            jax.ShapeDtypeStruct((rows, d), BF16),
        ],
        scratch_shapes=_weight_scratch(locs, d),
        compiler_params=_params(VMEM_BIG),
    )(h, wbuf, ln_g, ln_b)


def _ffn_bwd(dh, z, g, u, wbuf, locs, ln_g, after, alpha, name):
    rows, d = dh.shape
    f = N_DEV * locs[0][1]
    tm = _row_tile(rows)
    tf = f // FFN_CHUNKS

    def body(dh_ref, z_ref, g_ref, u_ref, w_hbm, lg_ref, after_ref, dx_ref, dg_ref, du_ref, a_ref, dyb_ref, dgam_ref, dbet_ref, wg, wu, wdv, sems):
        _load_blocks(w_hbm, [(*locs[0], wg), (*locs[1], wu), (*locs[2], wdv)], sems)

        @pl.when(pl.program_id(0) == 0)
        def _():
            dgam_ref[...] = jnp.zeros_like(dgam_ref)
            dbet_ref[...] = jnp.zeros_like(dbet_ref)

        dz, dgam, dbet = _ln_bwd(dh_ref[...], z_ref[...], lg_ref[...])
        dgam_ref[...] += dgam
        dbet_ref[...] += dbet
        dyb = (0.5 * dz).astype(BF16)
        dyb_ref[...] = dyb
        dx = alpha * dz
        for c in range(FFN_CHUNKS):
            sl = slice(c * tf, (c + 1) * tf)
            da = lax.dot_general(dyb, wdv[sl, :], NT, preferred_element_type=F32)
            gc = g_ref[:, sl].astype(F32)
            uc = u_ref[:, sl].astype(F32)
            sg = _sigmoid(gc)
            silu = gc * sg
            dgc = (da * uc * (sg * (1.0 + gc * (1.0 - sg)))).astype(BF16)
            duc = (da * silu).astype(BF16)
            a_ref[:, sl] = (silu * uc).astype(BF16)
            dg_ref[:, sl] = dgc
            du_ref[:, sl] = duc
            dx = dx + jnp.dot(dgc, wg[sl, :], preferred_element_type=F32) + jnp.dot(duc, wu[sl, :], preferred_element_type=F32)
        dx_ref[...] = dx

    row_d = pl.BlockSpec((tm, d), lambda i: (i, 0))
    row_f = pl.BlockSpec((tm, f), lambda i: (i, 0))
    return pl.pallas_call(
        body,
        name=name,
        grid=(rows // tm,),
        in_specs=[row_d, row_d, row_f, row_f, ANY, _full((1, d)), ANY],
        out_specs=[row_d, row_f, row_f, row_f, row_d, _full((1, d)), _full((1, d))],
        out_shape=[
            jax.ShapeDtypeStruct((rows, d), F32),
            jax.ShapeDtypeStruct((rows, f), BF16),
            jax.ShapeDtypeStruct((rows, f), BF16),
            jax.ShapeDtypeStruct((rows, f), BF16),
            jax.ShapeDtypeStruct((rows, d), BF16),
            jax.ShapeDtypeStruct((1, d), F32),
            jax.ShapeDtypeStruct((1, d), F32),
        ],
        scratch_shapes=_weight_scratch(locs, d),
        compiler_params=_params(VMEM_BIG),
    )(dh, z, g, u, wbuf, ln_g, after)


LANES = 128


def _gates(xc, wa_ref, ba, wx_ref, bx, lam):
    xcb = xc.astype(BF16)
    r = _sigmoid(jnp.dot(xcb, wa_ref[0], preferred_element_type=F32) + ba)
    i = _sigmoid(jnp.dot(xcb, wx_ref[0], preferred_element_type=F32) + bx)
    sp = _softplus(-lam)
    log_a = -LRU_C * sp * r
    a = jnp.exp(log_a)
    s = jnp.sqrt(-_expm1_nonpos(2.0 * log_a))
    return xcb, r, i, sp, a, s


def _conv(xr, cw, cb):
    return cb + cw[3:4] * xr + cw[2:3] * pltpu.roll(xr, 1, 0) + cw[1:2] * pltpu.roll(xr, 2, 0) + cw[0:1] * pltpu.roll(xr, 3, 0)


def _rglru_fwd(proj, cw, cb, wa, ba, wx, bx, lam, t_len, pad, name):
    rows = proj.shape[0]
    n_seq = rows // t_len
    n_tile = REC_WIDTH // LANES
    groups = t_len // 8

    def body(xr_ref, gt_ref, cw_ref, cb_ref, wa_ref, ba_ref, wx_ref, bx_ref, lam_ref, xc_ref, h_ref, y_ref, a_s, b_s):
        valid = lax.broadcasted_iota(jnp.int32, (t_len, 1), 0) >= pad
        xr = jnp.where(valid, xr_ref[...], 0.0)
        xc = _conv(xr, cw_ref[...], cb_ref[...])
        xc_ref[...] = xc
        _, _, i, _, a, s = _gates(xc, wa_ref, ba_ref[...], wx_ref, bx_ref[...], lam_ref[...])
        a_s[...] = a
        b_s[...] = jnp.where(valid, s * (i * xc), 0.0)
        sub = lax.broadcasted_iota(jnp.int32, (8, LANES), 0)

        def step(gi, carry):
            r0 = pl.multiple_of(gi * 8, 8)
            av = a_s[pl.ds(r0, 8), :]
            bv = b_s[pl.ds(r0, 8), :]
            for d in (1, 2, 4):
                m = sub >= d
                a_sh = pltpu.roll(av, d, 0)
                b_sh = pltpu.roll(bv, d, 0)
                bv = jnp.where(m, av * b_sh + bv, bv)
                av = jnp.where(m, av * a_sh, av)
            hv = av * carry + bv
            h_ref[pl.ds(r0, 8), :] = hv
            return jnp.broadcast_to(hv[7:8, :], (8, LANES))

        lax.fori_loop(0, groups, step, jnp.zeros((8, LANES), F32), unroll=4)
        gelu, _ = _gelu_and_grad(gt_ref[...])
        y_ref[...] = h_ref[...] * gelu

    seq_tile = lambda col0: pl.BlockSpec((t_len, LANES), lambda s, j: (s, col0 + j))
    vec = pl.BlockSpec((1, LANES), lambda s, j: (0, j))
    out = jax.ShapeDtypeStruct((rows, REC_WIDTH), F32)
    return pl.pallas_call(
        body,
        name=name,
        grid=(n_seq, n_tile),
        in_specs=[
            seq_tile(0),
            seq_tile(n_tile),
            pl.BlockSpec((CONV_WIDTH, LANES), lambda s, j: (0, j)),
            vec,
            pl.BlockSpec((1, LANES, LANES), lambda s, j: (j, 0, 0)),
            vec,
            pl.BlockSpec((1, LANES, LANES), lambda s, j: (j, 0, 0)),
            vec,
            vec,
        ],
        out_specs=[seq_tile(0)] * 3,
        out_shape=[out, out, out],
        scratch_shapes=[pltpu.VMEM((t_len, LANES), F32), pltpu.VMEM((t_len, LANES), F32)],
        compiler_params=_params(VMEM_BIG),
    )(proj, proj, cw, cb, wa, ba, wx, bx, lam)


def _rglru_bwd(dy, h, proj, xc, cw, wa, ba, wx, bx, lam, t_len, pad, name):
    rows = proj.shape[0]
    n_seq = rows // t_len
    n_tile = REC_WIDTH // LANES
    groups = t_len // 8

    def body(dy_ref, h_ref, xr_ref, gt_ref, xc_ref, cw_ref, wa_ref, ba_ref, wx_ref, bx_ref, lam_ref,
             dxr_ref, dgt_ref, dcw_ref, dcb_ref, dwa_ref, dba_ref, dwx_ref, dbx_ref, dlam_ref, a_s, c_s, db_s, da_s):
        first_seq = pl.program_id(1) == 0

        @pl.when(first_seq)
        def _():
            for ref in (dcw_ref, dcb_ref, dwa_ref, dba_ref, dwx_ref, dbx_ref, dlam_ref):
                ref[...] = jnp.zeros_like(ref)

        valid = lax.broadcasted_iota(jnp.int32, (t_len, 1), 0) >= pad
        gelu, dgelu = _gelu_and_grad(gt_ref[...])
        dyv = dy_ref[...]
        dho = dyv * gelu
        dgt_ref[...] = (dyv * h_ref[...] * dgelu).astype(BF16)
        xc = xc_ref[...]
        lam = lam_ref[...]
        xcb, r, i, sp, a, s = _gates(xc, wa_ref, ba_ref[...], wx_ref, bx_ref[...], lam)
        a_s[...] = a
        c_s[...] = a * dho
        db_s[...] = dho
        sub = lax.broadcasted_iota(jnp.int32, (8, LANES), 0)

        def step(k, carry):
            gi = groups - 1 - k
            r0 = pl.multiple_of(gi * 8, 8)
            av = a_s[pl.ds(r0, 8), :]
            cv = c_s[pl.ds(r0, 8), :]
            for d in (1, 2, 4):
                m = sub < 8 - d
                a_sh = pltpu.roll(av, 8 - d, 0)
                c_sh = pltpu.roll(cv, 8 - d, 0)
                cv = jnp.where(m, av * c_sh + cv, cv)
                av = jnp.where(m, av * a_sh, av)
            ev = av * carry + cv
            e_next = jnp.where(sub < 7, pltpu.roll(ev, 7, 0), carry)
            dht = db_s[pl.ds(r0, 8), :] + e_next
            hv = h_ref[pl.ds(r0, 8), :]
            rp = pl.multiple_of(jnp.maximum(gi - 1, 0) * 8, 8)
            h_before = jnp.where(gi > 0, jnp.broadcast_to(h_ref[pl.ds(rp, 8), :][7:8, :], (8, LANES)), 0.0)
            h_prev = jnp.where(sub >= 1, pltpu.roll(hv, 1, 0), h_before)
            db_s[pl.ds(r0, 8), :] = dht
            da_s[pl.ds(r0, 8), :] = dht * h_prev
            return jnp.broadcast_to(ev[0:1, :], (8, LANES))

        lax.fori_loop(0, groups, step, jnp.zeros((8, LANES), F32), unroll=4)

        db = jnp.where(valid, db_s[...], 0.0)
        da = da_s[...]
        ds = db * (i * xc)
        di = db * s * xc
        dxc = db * s * i
        dlog_a = da * a - ds * (a * a) / jnp.maximum(s, 1e-30)
        dr = dlog_a * (-LRU_C * sp)
        dsp = jnp.sum(dlog_a * (-LRU_C) * r, axis=0, keepdims=True)
        dlam_ref[...] += dsp * (-_sigmoid(-lam))
        dpr = dr * r * (1.0 - r)
        dpi = di * i * (1.0 - i)
        dprb = dpr.astype(BF16)
        dpib = dpi.astype(BF16)
        dxc = dxc + lax.dot_general(dprb, wa_ref[0], NT, preferred_element_type=F32) + lax.dot_general(dpib, wx_ref[0], NT, preferred_element_type=F32)
        dwa_ref[0] += lax.dot_general(xcb, dprb, TN, preferred_element_type=F32)
        dwx_ref[0] += lax.dot_general(xcb, dpib, TN, preferred_element_type=F32)
        dba_ref[...] += jnp.sum(dpr, axis=0, keepdims=True)
        dbx_ref[...] += jnp.sum(dpi, axis=0, keepdims=True)
        dxc = jnp.where(valid, dxc, 0.0)
        xr = jnp.where(valid, xr_ref[...], 0.0)
        dcb_ref[...] += jnp.sum(dxc, axis=0, keepdims=True)
        for k in range(CONV_WIDTH):
            shifted = xr if k == CONV_WIDTH - 1 else pltpu.roll(xr, CONV_WIDTH - 1 - k, 0)
            dcw_ref[k : k + 1, :] += jnp.sum(dxc * shifted, axis=0, keepdims=True)
        cw = cw_ref[...]
        dxr = cw[3:4] * dxc + cw[2:3] * pltpu.roll(dxc, t_len - 1, 0) + cw[1:2] * pltpu.roll(dxc, t_len - 2, 0) + cw[0:1] * pltpu.roll(dxc, t_len - 3, 0)
        dxr_ref[...] = jnp.where(valid, dxr, 0.0).astype(BF16)

    seq_tile = lambda col0: pl.BlockSpec((t_len, LANES), lambda j, s: (s, col0 + j))
    vec = pl.BlockSpec((1, LANES), lambda j, s: (0, j))
    mat = pl.BlockSpec((1, LANES, LANES), lambda j, s: (j, 0, 0))
    cwb = pl.BlockSpec((CONV_WIDTH, LANES), lambda j, s: (0, j))
    big = jax.ShapeDtypeStruct((rows, REC_WIDTH), BF16)
    vec_shape = jax.ShapeDtypeStruct((1, REC_WIDTH), F32)
    mat_shape = jax.ShapeDtypeStruct((n_tile, LANES, LANES), F32)
    return pl.pallas_call(
        body,
        name=name,
        grid=(n_tile, n_seq),
        in_specs=[seq_tile(0), seq_tile(0), seq_tile(0), seq_tile(n_tile), seq_tile(0), cwb, mat, vec, mat, vec, vec],
        out_specs=[seq_tile(0), seq_tile(0), cwb, vec, mat, vec, mat, vec, vec],
        out_shape=[big, big, jax.ShapeDtypeStruct((CONV_WIDTH, REC_WIDTH), F32), vec_shape, mat_shape, vec_shape, mat_shape, vec_shape, vec_shape],
        scratch_shapes=[pltpu.VMEM((t_len, LANES), F32)] * 4,
        compiler_params=_params(VMEM_BIG),
    )(dy, h, proj, proj, xc, cw, wa, ba, wx, bx, lam)


QKV_WIDTH = ATTN_WIDTH + 2 * KV_WIDTH


def _rotate(v, cos, sin_up, sin_down):
    width = v.shape[1]
    return v * cos + pltpu.roll(v, width - ROPE_DIM // 2, 1) * sin_up + pltpu.roll(v, ROPE_DIM // 2, 1) * sin_down


def _rope_rows(t_len):
    return t_len // 4 if t_len % 64 == 0 else BLOCK


def _rope_fwd(proj, cos, sin_up, sin_down, t_len, name):
    rows = proj.shape[0]
    rb = _rope_rows(t_len)
    nb = t_len // rb
    n_seq = rows // t_len
    q0 = 2 * REC_WIDTH

    def body(q_ref, k_ref, v_ref, cos_ref, up_ref, down_ref, o_ref):
        cos_t, up_t, down_t = cos_ref[...], up_ref[...], down_ref[...]
        o_ref[:, 0:ATTN_WIDTH] = _rotate(q_ref[...], cos_t, up_t, down_t).astype(BF16)
        o_ref[:, ATTN_WIDTH : ATTN_WIDTH + KV_WIDTH] = _rotate(k_ref[...], cos_t[:, :KV_WIDTH], up_t[:, :KV_WIDTH], down_t[:, :KV_WIDTH]).astype(BF16)
        o_ref[:, ATTN_WIDTH + KV_WIDTH :] = v_ref[...].astype(BF16)

    tab = pl.BlockSpec((rb, ATTN_WIDTH), lambda s, n: (n, 0))
    return pl.pallas_call(
        body,
        name=name,
        grid=(n_seq, nb),
        in_specs=[
            pl.BlockSpec((rb, ATTN_WIDTH), lambda s, n: (s * nb + n, q0 // ATTN_WIDTH)),
            pl.BlockSpec((rb, KV_WIDTH), lambda s, n: (s * nb + n, (q0 + ATTN_WIDTH) // KV_WIDTH)),
            pl.BlockSpec((rb, KV_WIDTH), lambda s, n: (s * nb + n, (q0 + ATTN_WIDTH) // KV_WIDTH + 1)),
            tab,
            tab,
            tab,
        ],
        out_specs=pl.BlockSpec((rb, QKV_WIDTH), lambda s, n: (s * nb + n, 0)),
        out_shape=jax.ShapeDtypeStruct((rows, QKV_WIDTH), BF16),
    )(proj, proj, proj, cos, sin_up, sin_down)


def _rope_bwd(dq, dk, dv, cos, sin_up, sin_down, t_len, name):
    rows = dq.shape[0]
    rb = _rope_rows(t_len)
    nb = t_len // rb
    n_seq = rows // t_len

    def body(dq_ref, dk_ref, dv_ref, cos_ref, up_ref, down_ref, o_ref):
        cos_t, up_t, down_t = cos_ref[...], -up_ref[...], -down_ref[...]
        o_ref[:, 0:ATTN_WIDTH] = _rotate(dq_ref[...], cos_t, up_t, down_t).astype(BF16)
        o_ref[:, ATTN_WIDTH : ATTN_WIDTH + KV_WIDTH] = _rotate(dk_ref[...], cos_t[:, :KV_WIDTH], up_t[:, :KV_WIDTH], down_t[:, :KV_WIDTH]).astype(BF16)
        o_ref[:, ATTN_WIDTH + KV_WIDTH :] = dv_ref[...].astype(BF16)

    tab = pl.BlockSpec((rb, ATTN_WIDTH), lambda s, n: (n, 0))
    blk = lambda w: pl.BlockSpec((rb, w), lambda s, n: (s * nb + n, 0))
    return pl.pallas_call(
        body,
        name=name,
        grid=(n_seq, nb),
        in_specs=[blk(ATTN_WIDTH), blk(KV_WIDTH), blk(KV_WIDTH), tab, tab, tab],
        out_specs=blk(QKV_WIDTH),
        out_shape=jax.ShapeDtypeStruct((rows, QKV_WIDTH), BF16),
    )(dq, dk, dv, cos, sin_up, sin_down)


GROUP = 4


def _attn_bias(pad):
    out = []
    for n in range(3):
        q_pos = n * BLOCK + jnp.arange(BLOCK)[:, None] - pad
        col = jnp.arange(3 * BLOCK)[None, :]
        first = col < BLOCK
        k_pos = col - pad + jnp.where(first, 0, (n - 2) * BLOCK)
        dist = q_pos - k_pos
        band = (dist >= 0) & (dist < WINDOW) & (k_pos >= N_META) & jnp.logical_not(first)
        meta = (k_pos >= 0) & (k_pos < N_META) & (k_pos <= q_pos) & first
        out.append(jnp.tile(jnp.where(band | meta, 0.0, NEG_INF).astype(F32), (GROUP, 1)))
    return jnp.stack(out)


def _bias_spec():
    return pl.BlockSpec((1, GROUP * BLOCK, 3 * BLOCK), lambda s, n: (jnp.minimum(n, 2), 0, 0))


def _in_half(e):
    lane = lax.broadcasted_iota(jnp.int32, (1, BLOCK), 1)
    return lane >= HEAD_DIM if e else lane < HEAD_DIM


def _head_views(q2, e, g):
    v = jnp.where(_in_half(e), q2, 0.0)
    return pltpu.roll(v, HEAD_DIM, 1) if e != g else v


def _tile(g, t):
    j = (GROUP // 2) * g + t
    return slice(j * BLOCK, (j + 1) * BLOCK)


def _stack_rows(ref, g):
    parts = []
    for t in range(GROUP // 2):
        tile = ref[:, _tile(g, t)].astype(F32)
        parts += [_head_views(tile, e, g) for e in range(2)]
    return jnp.concatenate(parts, axis=0)


def _stack_group(q_ref, sink_ref, g):
    return _stack_rows(q_ref, g), sink_ref[g][:, 0:1]


def _unstack_pair(v, g, t):
    out = None
    for e in range(2):
        blk = v[(2 * t + e) * BLOCK : (2 * t + e + 1) * BLOCK, :]
        blk = pltpu.roll(blk, HEAD_DIM, 1) if e != g else blk
        out = blk if out is None else out + blk
    return out


def _exp_scores(scores, bias, sink):
    s = scores * (HEAD_DIM**-0.5) + bias
    m = jnp.maximum(jnp.max(s, axis=-1, keepdims=True), sink)
    return jnp.exp(s - m), jnp.exp(sink - m)


def _softmax_with_sink(scores, bias, sink):
    p, p_sink = _exp_scores(scores, bias, sink)
    inv = 1.0 / (jnp.sum(p, axis=-1, keepdims=True) + p_sink)
    return p * inv, p_sink * inv


def _kv_specs(nb):
    k_col = ATTN_WIDTH // KV_WIDTH
    specs = []
    for col in (k_col, k_col + 1):
        specs += [
            pl.BlockSpec((BLOCK, KV_WIDTH), lambda s, n, col=col: (s * nb, col)),
            pl.BlockSpec((BLOCK, KV_WIDTH), lambda s, n, col=col: (s * nb + jnp.maximum(n - 1, 0), col)),
            pl.BlockSpec((BLOCK, KV_WIDTH), lambda s, n, col=col: (s * nb + n, col)),
        ]
    return specs


def _attn_fwd(qkv, sink_rows, bias, t_len, name):
    rows = qkv.shape[0]
    nb = t_len // BLOCK
    n_seq = rows // t_len

    def body(q_ref, km_ref, kp_ref, kc_ref, vm_ref, vp_ref, vc_ref, sink_ref, bias_ref, o_ref):
        keys = jnp.concatenate([km_ref[...], kp_ref[...], kc_ref[...]], axis=0)
        vals = jnp.concatenate([vm_ref[...], vp_ref[...], vc_ref[...]], axis=0).astype(F32)
        for g in range(N_Q_HEADS // GROUP):
            qs, sink = _stack_group(q_ref, sink_ref, g)
            scores = lax.dot_general(qs.astype(BF16), keys, NT, preferred_element_type=F32)
            p, p_sink = _exp_scores(scores, bias_ref[0], sink)
            ext = jnp.where(_in_half(g), vals, 1.0).astype(BF16)
            oe = jnp.dot(p.astype(BF16), ext, preferred_element_type=F32)
            denom = pltpu.roll(oe, HEAD_DIM, 1) + p_sink
            o = jnp.where(_in_half(g), oe / denom, 0.0)
            for t in range(GROUP // 2):
                o_ref[:, _tile(g, t)] = _unstack_pair(o, g, t)

    return pl.pallas_call(
        body,
        name=name,
        grid=(n_seq, nb),
        in_specs=[pl.BlockSpec((BLOCK, ATTN_WIDTH), lambda s, n: (s * nb + n, 0))] + _kv_specs(nb) + [_full(sink_rows.shape), _bias_spec()],
        out_specs=pl.BlockSpec((BLOCK, ATTN_WIDTH), lambda s, n: (s * nb + n, 0)),
        out_shape=jax.ShapeDtypeStruct((rows, ATTN_WIDTH), F32),
    )(qkv, qkv, qkv, qkv, qkv, qkv, qkv, sink_rows, bias)


def _attn_bwd(qkv, sink_rows, bias, o, do, t_len, name):
    rows = qkv.shape[0]
    nb = t_len // BLOCK
    n_seq = rows // t_len

    def body(q_ref, km_ref, kp_ref, kc_ref, vm_ref, vp_ref, vc_ref, sink_ref, bias_ref, o_ref, do_ref, dq_ref, dk_ref, dv_ref, dsink_ref):
        s_id, n = pl.program_id(0), pl.program_id(1)

        @pl.when(n == 0)
        def _():
            dk_ref[...] = jnp.zeros_like(dk_ref)
            dv_ref[...] = jnp.zeros_like(dv_ref)

        @pl.when((n == 0) & (s_id == 0))
        def _():
            dsink_ref[...] = jnp.zeros_like(dsink_ref)

        keys = jnp.concatenate([km_ref[...], kp_ref[...], kc_ref[...]], axis=0)
        vals = jnp.concatenate([vm_ref[...], vp_ref[...], vc_ref[...]], axis=0)
        dkeys = jnp.zeros((3 * BLOCK, KV_WIDTH), F32)
        dvals = jnp.zeros((3 * BLOCK, KV_WIDTH), F32)
        for g in range(N_Q_HEADS // GROUP):
            qs, sink = _stack_group(q_ref, sink_ref, g)
            qsb = qs.astype(BF16)
            scores = lax.dot_general(qsb, keys, NT, preferred_element_type=F32)
            p, p_sink = _softmax_with_sink(scores, bias_ref[0], sink)
            dos = _stack_rows(do_ref, g)
            delta = jnp.sum(dos * _stack_rows(o_ref, g), axis=-1, keepdims=True)
            dosb = dos.astype(BF16)
            dp = lax.dot_general(dosb, vals, NT, preferred_element_type=F32)
            ds = (p * (dp - delta) * (HEAD_DIM**-0.5)).astype(BF16)
            dqs = jnp.where(_in_half(g), jnp.dot(ds, keys, preferred_element_type=F32), 0.0)
            for t in range(GROUP // 2):
                dq_ref[:, _tile(g, t)] = _unstack_pair(dqs, g, t)
            dkeys = dkeys + lax.dot_general(ds, qsb, TN, preferred_element_type=F32)
            dvals = dvals + lax.dot_general(p.astype(BF16), dosb, TN, preferred_element_type=F32)
            sink_term = p_sink * delta
            for hh in range(GROUP):
                head = GROUP * g + hh
                part = -jnp.sum(sink_term[hh * BLOCK : (hh + 1) * BLOCK, :], axis=0, keepdims=True)
                dsink_ref[head : head + 1, :] += jnp.broadcast_to(part, (1, BLOCK))
        r_prev = pl.multiple_of(jnp.maximum(n - 1, 0) * BLOCK, BLOCK)
        r_cur = pl.multiple_of(n * BLOCK, BLOCK)
        for acc, d in ((dk_ref, dkeys), (dv_ref, dvals)):
            acc[0:BLOCK, :] += d[0:BLOCK]
            acc[pl.ds(r_prev, BLOCK), :] += d[BLOCK : 2 * BLOCK]
            acc[pl.ds(r_cur, BLOCK), :] += d[2 * BLOCK :]

    q_blk = pl.BlockSpec((BLOCK, ATTN_WIDTH), lambda s, n: (s * nb + n, 0))
    seq_kv = pl.BlockSpec((t_len, KV_WIDTH), lambda s, n: (s, 0))
    return pl.pallas_call(
        body,
        name=name,
        grid=(n_seq, nb),
        in_specs=[q_blk] + _kv_specs(nb) + [_full(sink_rows.shape), _bias_spec(), q_blk, q_blk],
        out_specs=[q_blk, seq_kv, seq_kv, _full((N_Q_HEADS, BLOCK))],
        out_shape=[
            jax.ShapeDtypeStruct((rows, ATTN_WIDTH), F32),
            jax.ShapeDtypeStruct((rows, KV_WIDTH), F32),
            jax.ShapeDtypeStruct((rows, KV_WIDTH), F32),
            jax.ShapeDtypeStruct((N_Q_HEADS, BLOCK), F32),
        ],
    )(qkv, qkv, qkv, qkv, qkv, qkv, qkv, sink_rows, bias, o, do)


def _mix_out_fwd(y_rec, y_attn, h, wbuf, loc, g_rec, g_attn, ln_g, ln_b, alpha, name):
    rows, d = h.shape
    tm = _row_tile(rows)

    def body(yr_ref, ya_ref, h_ref, w_hbm, gr_ref, ga_ref, lg_ref, lb_ref, ho_ref, z_ref, hb_ref, w_ref, sems):
        _load_blocks(w_hbm, [(*loc, w_ref)], sems)
        nr = _rms(yr_ref[...], gr_ref[...]).astype(BF16)
        na = _rms(ya_ref[...], ga_ref[...]).astype(BF16)
        m = jnp.dot(nr, w_ref[0:REC_WIDTH, :], preferred_element_type=F32) + jnp.dot(na, w_ref[REC_WIDTH:, :], preferred_element_type=F32)
        z = alpha * h_ref[...] + m
        z_ref[...] = z
        xhat, _ = _ln_stats(z)
        ho = xhat * lg_ref[...] + lb_ref[...]
        ho_ref[...] = ho
        hb_ref[...] = ho.astype(BF16)

    row_d = pl.BlockSpec((tm, d), lambda i: (i, 0))
    row_h = pl.BlockSpec((tm, REC_WIDTH), lambda i: (i, 0))
    return pl.pallas_call(
        body,
        name=name,
        grid=(rows // tm,),
        in_specs=[row_h, row_h, row_d, ANY, _full((1, REC_WIDTH)), _full((1, ATTN_WIDTH)), _full((1, d)), _full((1, d))],
        out_specs=[row_d, row_d, row_d],
        out_shape=[jax.ShapeDtypeStruct((rows, d), F32)] * 2 + [jax.ShapeDtypeStruct((rows, d), BF16)],
        scratch_shapes=_weight_scratch([loc], d),
    )(y_rec, y_attn, h, wbuf, g_rec, g_attn, ln_g, ln_b)


def _mix_out_bwd(dh, z, y_rec, y_attn, wbuf, loc, g_rec, g_attn, ln_g, name):
    rows, d = dh.shape
    tm = _row_tile(rows)
    mix = REC_WIDTH + ATTN_WIDTH

    def body(dh_ref, z_ref, yr_ref, ya_ref, w_hbm, gr_ref, ga_ref, lg_ref, dz_ref, dzb_ref, dyr_ref, dya_ref, yn_ref, dgam_ref, dbet_ref, dgr_ref, dga_ref, w_ref, sems):
        _load_blocks(w_hbm, [(*loc, w_ref)], sems)

        @pl.when(pl.program_id(0) == 0)
        def _():
            for ref in (dgam_ref, dbet_ref, dgr_ref, dga_ref):
                ref[...] = jnp.zeros_like(ref)

        dz, dgam, dbet = _ln_bwd(dh_ref[...], z_ref[...], lg_ref[...])
        dgam_ref[...] += dgam
        dbet_ref[...] += dbet
        dz_ref[...] = dz
        dzb = dz.astype(BF16)
        dzb_ref[...] = dzb
        dyn = lax.dot_general(dzb, w_ref[...], NT, preferred_element_type=F32)
        dyr, dgr, nr = _rms_bwd(dyn[:, 0:REC_WIDTH], yr_ref[...], gr_ref[...])
        dya, dga, na = _rms_bwd(dyn[:, REC_WIDTH:], ya_ref[...], ga_ref[...])
        dyr_ref[...] = dyr
        dya_ref[...] = dya
        dgr_ref[...] += dgr
        dga_ref[...] += dga
        yn_ref[:, 0:REC_WIDTH] = nr.astype(BF16)
        yn_ref[:, REC_WIDTH:] = na.astype(BF16)

    row_d = pl.BlockSpec((tm, d), lambda i: (i, 0))
    row_h = pl.BlockSpec((tm, REC_WIDTH), lambda i: (i, 0))
    row_m = pl.BlockSpec((tm, mix), lambda i: (i, 0))
    return pl.pallas_call(
        body,
        name=name,
        grid=(rows // tm,),
        in_specs=[row_d, row_d, row_h, row_h, ANY, _full((1, REC_WIDTH)), _full((1, ATTN_WIDTH)), _full((1, d))],
        out_specs=[row_d, row_d, row_h, row_h, row_m, _full((1, d)), _full((1, d)), _full((1, REC_WIDTH)), _full((1, ATTN_WIDTH))],
        out_shape=[
            jax.ShapeDtypeStruct((rows, d), F32),
            jax.ShapeDtypeStruct((rows, d), BF16),
            jax.ShapeDtypeStruct((rows, REC_WIDTH), F32),
            jax.ShapeDtypeStruct((rows, ATTN_WIDTH), F32),
            jax.ShapeDtypeStruct((rows, mix), BF16),
            jax.ShapeDtypeStruct((1, d), F32),
            jax.ShapeDtypeStruct((1, d), F32),
            jax.ShapeDtypeStruct((1, REC_WIDTH), F32),
            jax.ShapeDtypeStruct((1, ATTN_WIDTH), F32),
        ],
        scratch_shapes=_weight_scratch([loc], d),
    )(dh, z, y_rec, y_attn, wbuf, g_rec, g_attn, ln_g)


def _loss_head(y, target, t_len, first_token, name):
    rows, d = y.shape
    tm = _row_tile(rows)

    def body(y_ref, t_ref, loss_ref, dy_ref):
        i = pl.program_id(0)

        @pl.when(i == 0)
        def _():
            loss_ref[...] = jnp.zeros_like(loss_ref)

        row = i * tm + lax.broadcasted_iota(jnp.int32, (tm, 1), 0)
        is_token = lax.rem(row, t_len) >= first_token
        err = jnp.where(is_token, y_ref[...] - t_ref[...], 0.0)
        dy_ref[...] = err / d
        per_row = jnp.sum(err * err, axis=-1, keepdims=True) / d
        loss_ref[...] += jnp.broadcast_to(0.5 * jnp.sum(per_row, axis=0, keepdims=True), (1, BLOCK))

    row_d = pl.BlockSpec((tm, d), lambda i: (i, 0))
    return pl.pallas_call(
        body,
        name=name,
        grid=(rows // tm,),
        in_specs=[row_d, row_d],
        out_specs=[_full((1, BLOCK)), row_d],
        out_shape=[jax.ShapeDtypeStruct((1, BLOCK), F32), jax.ShapeDtypeStruct((rows, d), F32)],
    )(y, target)


def _meta_grad(dh0, t_len, pad, name):
    rows, d = dh0.shape
    nb = t_len // BLOCK
    n_seq = rows // t_len

    def body(dh_ref, o_ref):
        @pl.when(pl.program_id(0) == 0)
        def _():
            o_ref[...] = jnp.zeros_like(o_ref)

        o_ref[...] += dh_ref[pad : pad + N_META, :]

    return pl.pallas_call(
        body,
        name=name,
        grid=(n_seq,),
        in_specs=[pl.BlockSpec((BLOCK, d), lambda s: (s * nb, 0))],
        out_specs=_full((N_META, d)),
        out_shape=jax.ShapeDtypeStruct((N_META, d), F32),
    )(dh0)


def _adamw_math(w, g, m, v):
    nm = ADAM_B1 * m + (1.0 - ADAM_B1) * g
    nv = ADAM_B2 * v + (1.0 - ADAM_B2) * (g * g)
    m_hat = nm / (1.0 - ADAM_B1**ADAM_STEP)
    v_hat = nv / (1.0 - ADAM_B2**ADAM_STEP)
    return -ADAM_LR * (m_hat / (jnp.sqrt(v_hat) + ADAM_EPS) + ADAM_WD * w), nm, nv


def _reduce_update(parts, own, w, m, v, outs, layer, name):
    _, nr, d = parts.shape
    _, a_dim, b_dim = w.shape
    assert (a_dim, b_dim) == (nr, d), (w.shape, parts.shape)

    def body(p_ref, own_hbm, w_ref, m_ref, v_ref, g_in, dl_in, nm_in, nv_in, g_ref, dl_ref, nm_ref, nv_ref, own_v, acc, sem):
        me = 4 * lax.axis_index("x") + 2 * lax.axis_index("y") + lax.axis_index("c")
        cp = pltpu.make_async_copy(own_hbm.at[me], own_v, sem)
        cp.start()
        cp.wait()
        acc[...] = own_v[...].astype(F32)
        for dev in range(N_DEV):

            @pl.when(me != dev)
            def _():
                acc[...] += p_ref[dev].astype(F32)

        g = acc[...]
        g_ref[0] = g
        dl_ref[0], nm_ref[0], nv_ref[0] = _adamw_math(w_ref[0], g, m_ref[0], v_ref[0])

    at_layer = pl.BlockSpec((1, a_dim, b_dim), lambda i: (layer, 0, 0))
    return pl.pallas_call(
        body,
        name=name,
        grid=(1,),
        in_specs=[_full((N_DEV, nr, d)), ANY, at_layer, at_layer, at_layer, ANY, ANY, ANY, ANY],
        out_specs=[at_layer] * 4,
        out_shape=[jax.ShapeDtypeStruct(w.shape, F32)] * 4,
        input_output_aliases={5: 0, 6: 1, 7: 2, 8: 3},
        scratch_shapes=[pltpu.VMEM((nr, d), parts.dtype), pltpu.VMEM((nr, d), F32), pltpu.SemaphoreType.DMA(())],
        compiler_params=_params(VMEM_BIG),
    )(parts, own, w, m, v, *outs)


def _adamw(w, g, m, v, name):
    shape = w.shape
    cols = shape[-1]
    rows = w.size // cols
    tr = rows
    for cand in (512, 256, 128, 64):
        if rows % cand == 0 and rows > cand:
            tr = cand
            break

    def body(w_ref, g_ref, m_ref, v_ref, d_ref, nm_ref, nv_ref):
        d_ref[...], nm_ref[...], nv_ref[...] = _adamw_math(w_ref[...], g_ref[...], m_ref[...], v_ref[...])

    blk = pl.BlockSpec((tr, cols), lambda i: (i, 0))
    flat = [a.reshape(rows, cols) for a in (w, g, m, v)]
    outs = pl.pallas_call(
        body,
        name=name,
        grid=(rows // tr,),
        in_specs=[blk] * 4,
        out_specs=[blk] * 3,
        out_shape=[jax.ShapeDtypeStruct((rows, cols), F32)] * 3,
    )(*flat)
    return [o.reshape(shape) for o in outs]


WEIGHTS = ["meta_tokens", "ffn1_w_gate", "ffn1_w_up", "ffn1_w_down", "ln1_g", "ln1_b", "w_in", "conv_w", "conv_b", "gate_a_w", "gate_a_b",
           "gate_x_w", "gate_x_b", "lru_lambda", "attn_sinks", "norm_rec_g", "norm_attn_g", "w_out", "ln2_g", "ln2_b", "ffn2_w_gate",
           "ffn2_w_up", "ffn2_w_down", "ln3_g", "ln3_b"]
BIG = [("ffn1_w_gate", True), ("ffn1_w_up", True), ("ffn1_w_down", False), ("w_in", True), ("w_out", False),
       ("ffn2_w_gate", True), ("ffn2_w_up", True), ("ffn2_w_down", False)]


SMALL = ["ln1_g", "ln1_b", "ln2_g", "ln2_b", "ln3_g", "ln3_b", "conv_b", "gate_a_b", "gate_x_b", "lru_lambda", "norm_rec_g", "norm_attn_g",
         "conv_w", "gate_a_w", "gate_x_w", "attn_sinks"]
GATHER_GROUPS = [["ffn1_w_gate", "ffn1_w_up", "ffn1_w_down"], ["w_in", "w_out"], ["ffn2_w_gate", "ffn2_w_up", "ffn2_w_down"]]
EXCHANGE_GROUPS = [["ffn2_w_gate", "ffn2_w_up", "ffn2_w_down", "w_out"], ["w_in", "ffn1_w_gate", "ffn1_w_up", "ffn1_w_down"]]


def _rope_tables(t_len, pad):
    pos = (jnp.arange(t_len) - pad).astype(F32)
    inv_freq = ROPE_THETA ** (-jnp.arange(0, ROPE_DIM, 2, dtype=F32) / ROPE_DIM)
    ang = pos[:, None] * inv_freq[None, :]
    cos, sin = jnp.cos(ang), jnp.sin(ang)
    half = ROPE_DIM // 2
    rest = jnp.zeros((t_len, HEAD_DIM - ROPE_DIM), F32)
    zero = jnp.zeros((t_len, half), F32)
    cos_h = jnp.concatenate([cos, cos, rest + 1.0], axis=1)
    up_h = jnp.concatenate([-sin, zero, rest], axis=1)
    down_h = jnp.concatenate([zero, sin, rest], axis=1)
    return [jnp.tile(t, (1, ATTN_WIDTH // HEAD_DIM)) for t in (cos_h, up_h, down_h)]


def _gate_tiles(w):
    tiles = jnp.zeros((w.shape[0], w.shape[1] // 2, 2 * HEAD_DIM, 2 * HEAD_DIM), w.dtype)
    tiles = tiles.at[:, :, :HEAD_DIM, :HEAD_DIM].set(w[:, 0::2]).at[:, :, HEAD_DIM:, HEAD_DIM:].set(w[:, 1::2])
    return tiles.astype(BF16)


def _gate_blocks(tiles):
    pairs = jnp.stack([tiles[:, :HEAD_DIM, :HEAD_DIM], tiles[:, HEAD_DIM:, HEAD_DIM:]], axis=1)
    return pairs.reshape(2 * tiles.shape[0], HEAD_DIM, HEAD_DIM)


def kernel(x, meta_tokens, ffn1_w_gate, ffn1_w_up, ffn1_w_down, ln1_g, ln1_b, w_in, conv_w, conv_b, gate_a_w, gate_a_b, gate_x_w, gate_x_b, lru_lambda, attn_sinks, norm_rec_g, norm_attn_g, w_out, ln2_g, ln2_b, ffn2_w_gate, ffn2_w_up, ffn2_w_down, ln3_g, ln3_b, loss_target, m_meta_tokens, m_ffn1_w_gate, m_ffn1_w_up, m_ffn1_w_down, m_ln1_g, m_ln1_b, m_w_in, m_conv_w, m_conv_b, m_gate_a_w, m_gate_a_b, m_gate_x_w, m_gate_x_b, m_lru_lambda, m_attn_sinks, m_norm_rec_g, m_norm_attn_g, m_w_out, m_ln2_g, m_ln2_b, m_ffn2_w_gate, m_ffn2_w_up, m_ffn2_w_down, m_ln3_g, m_ln3_b, v_meta_tokens, v_ffn1_w_gate, v_ffn1_w_up, v_ffn1_w_down, v_ln1_g, v_ln1_b, v_w_in, v_conv_w, v_conv_b, v_gate_a_w, v_gate_a_b, v_gate_x_w, v_gate_x_b, v_lru_lambda, v_attn_sinks, v_norm_rec_g, v_norm_attn_g, v_w_out, v_ln2_g, v_ln2_b, v_ffn2_w_gate, v_ffn2_w_up, v_ffn2_w_down, v_ln3_g, v_ln3_b):
    given = dict(locals())
    w = {k: given[k] for k in WEIGHTS}
    n_seq, seq, d = x.shape
    depth = ln1_g.shape[0]
    alpha = (2.0 * depth) ** 0.25
    pad = (-(N_META + seq)) % BLOCK
    t_len = pad + N_META + seq
    rows = n_seq * t_len
    me = 4 * lax.axis_index("x") + 2 * lax.axis_index("y") + lax.axis_index("c")

    transposed = dict(BIG)

    def turned(a, name):
        return jnp.swapaxes(a, 1, 2) if transposed[name] else a

    small = jnp.concatenate([meta_tokens, conv_w.reshape(-1, BLOCK)], axis=0)
    (small_all,) = _all_gather([small], small, "gather_small")
    sent, loc, gathers, chain = {}, {}, {}, small_all

    def start_gather(l, k):
        shard = jnp.concatenate([sent[name][l] for name in GATHER_GROUPS[k]], axis=0)
        zone = lax.dynamic_update_slice(lax.empty((N_DEV,) + shard.shape, BF16), shard[None], (me, 0, 0))
        gathers[(l, k)] = _push_start([shard], [zone], chain, False, f"gather_start_{l}_{k}")
        return gathers[(l, k)][2][0]

    for k, group in enumerate(GATHER_GROUPS):
        off = 0
        for name in group:
            sent[name] = turned(w[name], name).astype(BF16)
            loc[name] = (off, sent[name].shape[1])
            off += sent[name].shape[1]
        chain = start_gather(0, k)
    for l in range(1, depth):
        for k in range(len(GATHER_GROUPS)):
            chain = start_gather(l, k)
    ffn1_loc = [loc["ffn1_w_gate"], loc["ffn1_w_up"], loc["ffn1_w_down"]]
    ffn2_loc = [loc["ffn2_w_gate"], loc["ffn2_w_up"], loc["ffn2_w_down"]]
    meta_full = small_all[:, :N_META, :].transpose(1, 0, 2).reshape(N_META, d)
    conv_shard = conv_w.shape[-1]
    conv_full = small_all[:, N_META:, :].reshape(N_DEV, depth, CONV_WIDTH, conv_shard).transpose(1, 2, 0, 3).reshape(depth, CONV_WIDTH, REC_WIDTH)

    cos, sin_up, sin_down = _rope_tables(t_len, pad)
    bias = _attn_bias(pad)
    wa_tiles, wx_tiles = _gate_tiles(gate_a_w), _gate_tiles(gate_x_w)
    row1 = lambda a: a.reshape(1, -1)

    h = jnp.concatenate([jnp.zeros((n_seq, pad, d), F32), jnp.broadcast_to(meta_full[None], (n_seq, N_META, d)), x], axis=1).reshape(rows, d)
    target = jnp.pad(loss_target, ((0, 0), (pad + N_META, 0), (0, 0))).reshape(rows, d)
    hb = h.astype(BF16)
    saved, wbufs = [], []

    def gathered(l, k, after):
        _, (buf,) = _push_wait(*gathers[(l, k)], after, False, f"gather_wait_{l}_{k}")
        return buf

    for l in range(depth):
        w_ffn1 = gathered(l, 0, chain if l == 0 else h)
        s = {"h0": hb}
        h, s["g1"], s["u1"], s["z1"], hb = _ffn_fwd(h, w_ffn1, ffn1_loc, row1(ln1_g[l]), row1(ln1_b[l]), alpha, "ffn1_fwd")
        s["h1"] = hb
        s["wa"], s["wx"] = wa_tiles[l], wx_tiles[l]
        s["sinks"] = jnp.broadcast_to(jnp.repeat(attn_sinks[l].reshape(N_Q_HEADS // GROUP, GROUP), BLOCK, axis=1)[:, :, None], (N_Q_HEADS // GROUP, GROUP * BLOCK, BLOCK))
        wbuf = gathered(l, 1, h)
        proj = _proj_fwd(h, wbuf, loc["w_in"], "proj_fwd")
        s["proj"] = proj
        s["xc"], s["hrec"], s["y_rec"] = _rglru_fwd(proj, conv_full[l], row1(conv_b[l]), s["wa"], row1(gate_a_b[l]), s["wx"], row1(gate_x_b[l]),
                                                     row1(lru_lambda[l]), t_len, pad, "rglru_fwd")
        s["qkv"] = _rope_fwd(proj, cos, sin_up, sin_down, t_len, "rope_fwd")
        s["y_attn"] = _attn_fwd(s["qkv"], s["sinks"], bias, t_len, "attn_fwd")
        h, s["z2"], hb = _mix_out_fwd(s["y_rec"], s["y_attn"], h, wbuf, loc["w_out"], row1(norm_rec_g[l]), row1(norm_attn_g[l]), row1(ln2_g[l]), row1(ln2_b[l]), alpha, "mix_out_fwd")
        s["h2"] = hb
        w_ffn2 = gathered(l, 2, h)
        h, s["g2"], s["u2"], s["z3"], hb = _ffn_fwd(h, w_ffn2, ffn2_loc, row1(ln3_g[l]), row1(ln3_b[l]), alpha, "ffn2_fwd")
        saved.append(s)
        wbufs.append((w_ffn1, wbuf, w_ffn2))
    loss_part, dh = _loss_head(h, target, t_len, pad + N_META, "loss_head")

    exchanges = {}
    small_grads = [None] * depth
    behind = dh

    def exchange(l, k, bg, chain):
        gs = [bg[name].reshape(N_DEV, -1, d) for name in EXCHANGE_GROUPS[k]]
        zones = [lax.empty(g.shape, g.dtype) for g in gs]
        exchanges[(l, k)] = _push_start(gs, zones, chain, True, f"exchange_start_{l}_{k}")
        return exchanges[(l, k)][2][0]

    for l in reversed(range(depth)):
        (w_ffn1, wbuf, w_ffn2), s = wbufs[l], saved[l]
        bg, sg = {}, {}
        dh, dg, du, act, dyb, sg["ln3_g"], sg["ln3_b"] = _ffn_bwd(dh, s["z3"], s["g2"], s["u2"], w_ffn2, ffn2_loc, row1(ln3_g[l]), behind, alpha, "ffn2_bwd")
        bg["ffn2_w_gate"] = _mm_tn(dg, s["h2"], BF16, "ffn_wgrad_in")
        bg["ffn2_w_up"] = _mm_tn(du, s["h2"], BF16, "ffn_wgrad_in")
        bg["ffn2_w_down"] = _mm_tn(act, dyb, BF16, "ffn_wgrad_down")
        dz, dzb, dy_rec, dy_attn, yn, sg["ln2_g"], sg["ln2_b"], sg["norm_rec_g"], sg["norm_attn_g"] = _mix_out_bwd(
            dh, s["z2"], s["y_rec"], s["y_attn"], wbuf, loc["w_out"], row1(norm_rec_g[l]), row1(norm_attn_g[l]), row1(ln2_g[l]), "mix_out_bwd")
        bg["w_out"] = _mm_tn(yn, dzb, BF16, "w_out_wgrad")
        behind = exchange(l, 0, bg, behind)
        dxr, dgt, sg["conv_w"], sg["conv_b"], dwa, sg["gate_a_b"], dwx, sg["gate_x_b"], sg["lru_lambda"] = _rglru_bwd(
            dy_rec, s["hrec"], s["proj"], s["xc"], conv_full[l], s["wa"], row1(gate_a_b[l]), s["wx"], row1(gate_x_b[l]), row1(lru_lambda[l]), t_len, pad, "rglru_bwd")
        sg["gate_a_w"], sg["gate_x_w"] = _gate_blocks(dwa), _gate_blocks(dwx)
        dq, dk, dv, dsink = _attn_bwd(s["qkv"], s["sinks"], bias, s["y_attn"], dy_attn, t_len, "attn_bwd")
        sg["attn_sinks"] = dsink[:, 0]
        dqkv = _rope_bwd(dq, dk, dv, cos, sin_up, sin_down, t_len, "rope_bwd")
        dproj = jnp.concatenate([dxr, dgt, dqkv], axis=1)
        bg["w_in"] = _mm_tn(dproj, s["h1"], BF16, "w_in_wgrad")
        dh = _proj_bwd(dproj, wbuf, loc["w_in"], dz, alpha, "proj_bwd")
        dh, dg, du, act, dyb, sg["ln1_g"], sg["ln1_b"] = _ffn_bwd(dh, s["z1"], s["g1"], s["u1"], w_ffn1, ffn1_loc, row1(ln1_g[l]), behind, alpha, "ffn1_bwd")
        small_grads[l] = sg
        if l == 0:
            dmeta = _meta_grad(dh, t_len, pad, "meta_grad")
            pieces = [small_grads[j][name].reshape(-1) for name in SMALL for j in range(depth)] + [dmeta.reshape(-1), loss_part[0, :1]]
            sizes = [p.shape[0] for p in pieces]
            flat = jnp.concatenate(pieces)
            width = 1024
            n_rows = -(-flat.shape[0] // (8 * width)) * 8
            flat = jnp.pad(flat, (0, n_rows * width - flat.shape[0])).reshape(n_rows, width)
            zone = lax.dynamic_update_slice(lax.empty((N_DEV, n_rows, width), F32), flat[None], (me, 0, 0))
            small_gather = _push_start([flat], [zone], behind, False, "small_grads_start")
            behind = small_gather[2][0]
        bg["ffn1_w_gate"] = _mm_tn(dg, s["h0"], BF16, "ffn_wgrad_in")
        bg["ffn1_w_up"] = _mm_tn(du, s["h0"], BF16, "ffn_wgrad_in")
        bg["ffn1_w_down"] = _mm_tn(act, dyb, BF16, "ffn_wgrad_down")
        behind = exchange(l, 1, bg, behind)
    grad_x = dh.reshape(n_seq, t_len, d)[:, pad + N_META :, :]

    state = {name: [turned(a, name) for a in (w[name], given["m_" + name], given["v_" + name])] for name, _ in BIG}
    outs = {name: [lax.empty(state[name][0].shape, F32) for _ in range(4)] for name, _ in BIG}

    def reduce_group(l, k, after):
        send_sems, recv_sems, srcs, zones = exchanges[(l, k)]
        owns, parts = _push_wait(send_sems, recv_sems, srcs, zones, after, True, f"exchange_wait_{l}_{k}")
        for name, own, p in zip(EXCHANGE_GROUPS[k], owns, parts):
            outs[name] = _reduce_update(p, own, *state[name], outs[name], l, "reduce_update")
        return outs[EXCHANGE_GROUPS[k][-1]][0]

    grads = {}
    order = [(l, k) for l in reversed(range(depth)) for k in range(len(EXCHANGE_GROUPS))]
    after = behind
    for l, k in order[:-1]:
        after = reduce_group(l, k, after)
    _, (flat_all,) = _push_wait(*small_gather, after, False, "small_grads_wait")
    total = _sum8(flat_all, "sum_small_grads").reshape(-1)
    offs = [0]
    for sz in sizes:
        offs.append(offs[-1] + sz)
    taken = [total[offs[k] : offs[k + 1]] for k in range(len(sizes))]
    for j, name in enumerate(SMALL):
        full_shape = (depth,) + ((CONV_WIDTH, REC_WIDTH) if name == "conv_w" else w[name].shape[1:])
        grads[name] = total[offs[j * depth] : offs[(j + 1) * depth]].reshape(full_shape)
    grads["conv_w"] = lax.dynamic_slice_in_dim(grads["conv_w"], me * conv_shard, conv_shard, axis=2)
    meta_shard = meta_tokens.shape[1]
    grads["meta_tokens"] = lax.dynamic_slice_in_dim(taken[-2].reshape(N_META, d), me * meta_shard, meta_shard, axis=1)
    loss = taken[-1][0]
    deltas, new_m, new_v = {}, {}, {}
    for name in WEIGHTS:
        if name not in dict(BIG):
            deltas[name], new_m[name], new_v[name] = _adamw(w[name], grads[name], given["m_" + name], given["v_" + name], "adamw")

    reduce_group(*order[-1], new_v[WEIGHTS[-1]])
    for name, _ in BIG:
        grads[name], deltas[name], new_m[name], new_v[name] = [turned(a, name) for a in outs[name]]
    return (loss, grad_x, *[grads[k] for k in WEIGHTS], *[deltas[k] for k in WEIGHTS], *[new_m[k] for k in WEIGHTS], *[new_v[k] for k in WEIGHTS])
```

```python
import functools

import jax
import jax.numpy as jnp
from jax import lax
from jax.experimental import pallas as pl
from jax.experimental.pallas import tpu as pltpu

F32 = jnp.float32
BF16 = jnp.bfloat16
MESH = pl.DeviceIdType.MESH

N_DEV = 8
N_META = 16
BLOCK = 128
WINDOW = 128
REC_WIDTH = 512
ATTN_WIDTH = 512
KV_WIDTH = 128
HEAD_DIM = 64
N_Q_HEADS = 8
ROPE_DIM = 16
ROPE_THETA = 500000.0
CONV_WIDTH = 4
LRU_C = 8.0
LN_EPS = 1e-5
RMS_EPS = 1e-6
NEG_INF = -1e30
ADAM_LR, ADAM_B1, ADAM_B2, ADAM_EPS, ADAM_WD, ADAM_STEP = 0.001, 0.9, 0.999, 1e-08, 0.01, 10

NT = (((1,), (1,)), ((), ()))
TN = (((0,), (0,)), ((), ()))
VMEM_BIG = 56 * 1024 * 1024


def _params(vmem=None):
    return pltpu.CompilerParams(vmem_limit_bytes=vmem)


def _row_tile(rows):
    return 256 if rows % 256 == 0 else 128


def _sigmoid(x):
    return 1.0 / (1.0 + jnp.exp(-x))


def _expm1_nonpos(x):
    series = x * (1.0 + 0.5 * x * (1.0 + x / 3.0 * (1.0 + 0.25 * x * (1.0 + 0.2 * x))))
    return jnp.where(x > -0.05, series, jnp.exp(x) - 1.0)


def _softplus(x):
    e = jnp.exp(-jnp.abs(x))
    log1p = jnp.where(e < 1e-3, e * (1.0 - e * (0.5 - e / 3.0)), jnp.log(1.0 + e))
    return jnp.maximum(x, 0.0) + log1p


def _gelu_and_grad(x):
    c0 = 0.7978845608028654
    x2 = x * x
    t = jnp.tanh(c0 * (x + 0.044715 * x * x2))
    gelu = 0.5 * x * (1.0 + t)
    grad = 0.5 * (1.0 + t) + 0.5 * x * (1.0 - t * t) * c0 * (1.0 + 3.0 * 0.044715 * x2)
    return gelu, grad


def _ln_stats(z):
    mu = jnp.mean(z, axis=-1, keepdims=True)
    zc = z - mu
    var = jnp.mean(zc * zc, axis=-1, keepdims=True)
    rstd = lax.rsqrt(var + LN_EPS)
    return zc * rstd, rstd


def _ln_bwd(dy, z, gamma):
    xhat, rstd = _ln_stats(z)
    dxh = dy * gamma
    m1 = jnp.mean(dxh, axis=-1, keepdims=True)
    m2 = jnp.mean(dxh * xhat, axis=-1, keepdims=True)
    dz = rstd * (dxh - m1 - xhat * m2)
    return dz, jnp.sum(dy * xhat, axis=0, keepdims=True), jnp.sum(dy, axis=0, keepdims=True)


def _rms(y, gamma):
    r = lax.rsqrt(jnp.mean(y * y, axis=-1, keepdims=True) + RMS_EPS)
    return y * r * gamma


def _rms_bwd(dout, y, gamma):
    r = lax.rsqrt(jnp.mean(y * y, axis=-1, keepdims=True) + RMS_EPS)
    n = y * r
    dn = dout * gamma
    dy = r * (dn - n * jnp.mean(dn * n, axis=-1, keepdims=True))
    return dy, jnp.sum(dout * n, axis=0, keepdims=True), n * gamma


def _load_blocks(w_hbm, items, sems):
    @pl.when(pl.program_id(0) == 0)
    def _():
        copies = []
        for k, (off, nr, dst) in enumerate(items):
            for dev in range(N_DEV):
                copies.append(pltpu.make_async_copy(w_hbm.at[dev, pl.ds(off, nr), :], dst.at[pl.ds(dev * nr, nr), :], sems.at[k, dev]))
        for cp in copies:
            cp.start()
        for cp in copies:
            cp.wait()


def _full(shape):
    return pl.BlockSpec(shape, lambda *_: (0,) * len(shape))


ANY = pl.BlockSpec(memory_space=pl.ANY)


def _all_gather(xs, after, name):
    n = len(xs)

    def body(*refs):
        ins, outs = refs[:n], refs[n + 1 : 2 * n + 1]
        send_sems, recv_sems, local_sems = refs[2 * n + 1 :]
        x, y, c = lax.axis_index("x"), lax.axis_index("y"), lax.axis_index("c")
        me, sibling = (x, y, c), (x, y, 1 - c)
        chips = [(1 - x, y), (x, 1 - y), (1 - x, 1 - y)]

        def slot(i, dev):
            return outs[i].at[4 * dev[0] + 2 * dev[1] + dev[2]]

        def copy(i, k, block, to, src=None):
            return pltpu.make_async_remote_copy(
                src_ref=slot(i, block) if src is None else src,
                dst_ref=slot(i, block),
                send_sem=send_sems.at[i, k],
                recv_sem=recv_sems.at[i, k],
                device_id=to,
                device_id_type=MESH,
            )

        mine = [pltpu.make_async_copy(ins[i], slot(i, me), local_sems.at[i]) for i in range(n)]
        for cp in mine:
            cp.start()
        first = []
        for i in range(n):
            first.append(copy(i, 0, me, sibling, src=ins[i]))
            first += [copy(i, 1 + j, me, (*chip, c), src=ins[i]) for j, chip in enumerate(chips)]
        for cp in first:
            cp.start()
        passed = []
        for j, chip in enumerate(chips):
            for i in range(n):
                copy(i, 1 + j, (*chip, c), me).wait_recv()
                fwd = copy(i, 4 + j, (*chip, c), sibling)
                fwd.start()
                passed.append(fwd)
        for i in range(n):
            copy(i, 0, sibling, me).wait_recv()
            for j, chip in enumerate(chips):
                copy(i, 4 + j, (*chip, 1 - c), me).wait_recv()
        for cp in first + passed:
            cp.wait_send()
        for cp in mine:
            cp.wait()

    return pl.pallas_call(
        body,
        name=name,
        out_shape=[jax.ShapeDtypeStruct((N_DEV,) + a.shape, a.dtype) for a in xs],
        in_specs=[ANY] * (n + 1),
        out_specs=[ANY] * n,
        scratch_shapes=[pltpu.SemaphoreType.DMA((n, 7)), pltpu.SemaphoreType.DMA((n, 7)), pltpu.SemaphoreType.DMA((n,))],
    )(*xs, after)


HBM = pl.BlockSpec(memory_space=pltpu.HBM)
SEM = pl.BlockSpec(memory_space=pltpu.SEMAPHORE)
EFFECT = pltpu.SideEffectType.DATAFLOW_SIDE_EFFECTING


def _push_copies(ins, lands, send_sems, recv_sems, scatter):
    x, y, c = lax.axis_index("x"), lax.axis_index("y"), lax.axis_index("c")
    me = 4 * x + 2 * y + c
    copies = []
    for k in range(1, N_DEV):
        px = 1 - x if (k >> 2) & 1 else x
        py = 1 - y if (k >> 1) & 1 else y
        pc = 1 - c if k & 1 else c
        for i in range(len(ins)):
            copies.append(
                pltpu.make_async_remote_copy(
                    src_ref=ins[i].at[4 * px + 2 * py + pc] if scatter else ins[i],
                    dst_ref=lands[i].at[me],
                    send_sem=send_sems.at[i * (N_DEV - 1) + k - 1],
                    recv_sem=recv_sems.at[i * (N_DEV - 1) + k - 1],
                    device_id=(px, py, pc),
                    device_id_type=MESH,
                )
            )
    return copies


def _push_start(xs, lands, chain, scatter, name):
    n = len(xs)

    def body(*refs):
        ins, zones = refs[:n], refs[n : 2 * n]
        for cp in _push_copies(ins, zones, refs[2 * n + 1], refs[2 * n + 2], scatter):
            cp.start()

    outs = pl.pallas_call(
        body,
        name=name,
        out_shape=(
            pltpu.SemaphoreType.DMA((n * (N_DEV - 1),)),
            pltpu.SemaphoreType.DMA((n * (N_DEV - 1),)),
            *[pltpu.HBM(a.shape, a.dtype) for a in list(xs) + list(lands)],
        ),
        in_specs=[HBM] * (2 * n) + [ANY],
        out_specs=(SEM, SEM, *[HBM] * (2 * n)),
        input_output_aliases={i: 2 + i for i in range(2 * n)},
        compiler_params=pltpu.CompilerParams(has_side_effects=EFFECT),
    )(*[pltpu.with_memory_space_constraint(a, pltpu.HBM) for a in list(xs) + list(lands)], chain)
    return outs[0], outs[1], list(outs[2 : 2 + n]), list(outs[2 + n : 2 + 2 * n])


def _push_wait(send_sems, recv_sems, srcs, lands, after, scatter, name):
    n = len(srcs)

    def body(*refs):
        ins, zones = refs[:n], refs[n : 2 * n]
        for cp in _push_copies(ins, zones, refs[2 * n], refs[2 * n + 1], scatter):
            cp.wait_send()
            cp.wait_recv()

    outs = pl.pallas_call(
        body,
        name=name,
        out_shape=[pltpu.HBM(a.shape, a.dtype) for a in srcs + lands],
        in_specs=[HBM] * (2 * n) + [SEM, SEM, ANY],
        out_specs=[HBM] * (2 * n),
        input_output_aliases={i: i for i in range(2 * n)},
        compiler_params=pltpu.CompilerParams(has_side_effects=EFFECT),
    )(*srcs, *lands, send_sems, recv_sems, after)
    return list(outs[:n]), list(outs[n:])


def _sum8(parts, name):
    _, rows, cols = parts.shape
    tr = rows
    for cand in (512, 256, 128, 64, 32, 16):
        if rows % cand == 0 and rows > cand:
            tr = cand
            break

    def body(p_ref, o_ref):
        acc = p_ref[0].astype(F32)
        for d in range(1, N_DEV):
            acc = acc + p_ref[d].astype(F32)
        o_ref[...] = acc

    return pl.pallas_call(
        body,
        name=name,
        grid=(rows // tr,),
        in_specs=[pl.BlockSpec((N_DEV, tr, cols), lambda i: (0, i, 0))],
        out_specs=pl.BlockSpec((tr, cols), lambda i: (i, 0)),
        out_shape=jax.ShapeDtypeStruct((rows, cols), F32),
    )(parts)


def _mm_tn(a, b, out_dtype, name):
    rows, m = a.shape
    n = b.shape[1]
    tm = m // 2 if (m // 2) % 128 == 0 and m > 512 else m
    tr = next(t for t in (2176, 1024, 512, 256, 128) if rows % t == 0)
    nk = rows // tr

    def body(a_ref, b_ref, o_ref, acc):
        k = pl.program_id(1)
        part = lax.dot_general(a_ref[...].astype(BF16), b_ref[...].astype(BF16), TN, preferred_element_type=F32)

        @pl.when(k == 0)
        def _():
            acc[...] = part

        @pl.when(k > 0)
        def _():
            acc[...] += part

        @pl.when(k == nk - 1)
        def _():
            o_ref[...] = acc[...].astype(o_ref.dtype)

    return pl.pallas_call(
        body,
        name=name,
        grid=(m // tm, nk),
        in_specs=[pl.BlockSpec((tr, tm), lambda i, k: (k, i)), pl.BlockSpec((tr, n), lambda i, k: (k, 0))],
        out_specs=pl.BlockSpec((tm, n), lambda i, k: (i, 0)),
        out_shape=jax.ShapeDtypeStruct((m, n), out_dtype),
        scratch_shapes=[pltpu.VMEM((tm, n), F32)],
        compiler_params=_params(VMEM_BIG),
    )(a, b)


def _weight_scratch(locs, d):
    return [pltpu.VMEM((N_DEV * nr, d), BF16) for _, nr in locs] + [pltpu.SemaphoreType.DMA((len(locs), N_DEV))]


def _proj_fwd(h, wbuf, loc, name):
    rows, d = h.shape
    n = N_DEV * loc[1]
    tm = _row_tile(rows)

    def body(h_ref, w_hbm, o_ref, w, sems):
        _load_blocks(w_hbm, [(*loc, w)], sems)
        o_ref[...] = lax.dot_general(h_ref[...].astype(BF16), w[...], NT, preferred_element_type=F32)

    return pl.pallas_call(
        body,
        name=name,
        grid=(rows // tm,),
        in_specs=[pl.BlockSpec((tm, d), lambda i: (i, 0)), ANY],
        out_specs=pl.BlockSpec((tm, n), lambda i: (i, 0)),
        out_shape=jax.ShapeDtypeStruct((rows, n), F32),
        scratch_shapes=_weight_scratch([loc], d),
        compiler_params=_params(VMEM_BIG),
    )(h, wbuf)


def _proj_bwd(dproj, wbuf, loc, dz, alpha, name):
    rows, n = dproj.shape
    d = dz.shape[1]
    tm = _row_tile(rows)

    def body(dp_ref, w_hbm, dz_ref, o_ref, w, sems):
        _load_blocks(w_hbm, [(*loc, w)], sems)
        o_ref[...] = alpha * dz_ref[...] + jnp.dot(dp_ref[...], w[...], preferred_element_type=F32)

    return pl.pallas_call(
        body,
        name=name,
        grid=(rows // tm,),
        in_specs=[pl.BlockSpec((tm, n), lambda i: (i, 0)), ANY, pl.BlockSpec((tm, d), lambda i: (i, 0))],
        out_specs=pl.BlockSpec((tm, d), lambda i: (i, 0)),
        out_shape=jax.ShapeDtypeStruct((rows, d), F32),
        scratch_shapes=_weight_scratch([loc], d),
        compiler_params=_params(VMEM_BIG),
    )(dproj, wbuf, dz)


FFN_CHUNKS = 1


def _ffn_fwd(h, wbuf, locs, ln_g, ln_b, alpha, name):
    rows, d = h.shape
    f = N_DEV * locs[0][1]
    tm = _row_tile(rows)
    tf = f // FFN_CHUNKS

    def body(h_ref, w_hbm, lg_ref, lb_ref, ho_ref, g_ref, u_ref, z_ref, hb_ref, wg, wu, wdv, sems):
        _load_blocks(w_hbm, [(*locs[0], wg), (*locs[1], wu), (*locs[2], wdv)], sems)
        x = h_ref[...]
        xb = x.astype(BF16)
        y = None
        for c in range(FFN_CHUNKS):
            sl = slice(c * tf, (c + 1) * tf)
            g = lax.dot_general(xb, wg[sl, :], NT, preferred_element_type=F32)
            u = lax.dot_general(xb, wu[sl, :], NT, preferred_element_type=F32)
            g_ref[:, sl] = g
            u_ref[:, sl] = u
            a = (g * _sigmoid(g) * u).astype(BF16)
            part = jnp.dot(a, wdv[sl, :], preferred_element_type=F32)
            y = part if y is None else y + part
        z = alpha * x + 0.5 * y
        z_ref[...] = z
        xhat, _ = _ln_stats(z)
        ho = xhat * lg_ref[...] + lb_ref[...]
        ho_ref[...] = ho
        hb_ref[...] = ho.astype(BF16)

    row_d = pl.BlockSpec((tm, d), lambda i: (i, 0))
    row_f = pl.BlockSpec((tm, f), lambda i: (i, 0))
    return pl.pallas_call(
        body,
        name=name,
        grid=(rows // tm,),
        in_specs=[row_d, ANY, _full((1, d)), _full((1, d))],
        out_specs=[row_d, row_f, row_f, row_d, row_d],
        out_shape=[
            jax.ShapeDtypeStruct((rows, d), F32),
            jax.ShapeDtypeStruct((rows, f), F32),
            jax.ShapeDtypeStruct((rows, f), F32),
            jax.ShapeDtypeStruct((rows, d), F32),
            jax.ShapeDtypeStruct((rows, d), BF16),
        ],
        scratch_shapes=_weight_scratch(locs, d),
        compiler_params=_params(VMEM_BIG),
    )(h, wbuf, ln_g, ln_b)


def _ffn_bwd(dh, z, g, u, wbuf, locs, ln_g, after, alpha, name):
    rows, d = dh.shape
    f = N_DEV * locs[0][1]
    tm = _row_tile(rows)
    tf = f // FFN_CHUNKS

    def body(dh_ref, z_ref, g_ref, u_ref, w_hbm, lg_ref, after_ref, dx_ref, dg_ref, du_ref, a_ref, dyb_ref, dgam_ref, dbet_ref, wg, wu, wdv, sems):
        _load_blocks(w_hbm, [(*locs[0], wg), (*locs[1], wu), (*locs[2], wdv)], sems)

        @pl.when(pl.program_id(0) == 0)
        def _():
            dgam_ref[...] = jnp.zeros_like(dgam_ref)
            dbet_ref[...] = jnp.zeros_like(dbet_ref)

        dz, dgam, dbet = _ln_bwd(dh_ref[...], z_ref[...], lg_ref[...])
        dgam_ref[...] += dgam
        dbet_ref[...] += dbet
        dyb = (0.5 * dz).astype(BF16)
        dyb_ref[...] = dyb
        dx = alpha * dz
        for c in range(FFN_CHUNKS):
            sl = slice(c * tf, (c + 1) * tf)
            da = lax.dot_general(dyb, wdv[sl, :], NT, preferred_element_type=F32)
            gc = g_ref[:, sl]
            uc = u_ref[:, sl]
            sg = _sigmoid(gc)
            silu = gc * sg
            dgc = (da * uc * (sg * (1.0 + gc * (1.0 - sg)))).astype(BF16)
            duc = (da * silu).astype(BF16)
            a_ref[:, sl] = (silu * uc).astype(BF16)
            dg_ref[:, sl] = dgc
            du_ref[:, sl] = duc
            dx = dx + jnp.dot(dgc, wg[sl, :], preferred_element_type=F32) + jnp.dot(duc, wu[sl, :], preferred_element_type=F32)
        dx_ref[...] = dx

    row_d = pl.BlockSpec((tm, d), lambda i: (i, 0))
    row_f = pl.BlockSpec((tm, f), lambda i: (i, 0))
    return pl.pallas_call(
        body,
        name=name,
        grid=(rows // tm,),
        in_specs=[row_d, row_d, row_f, row_f, ANY, _full((1, d)), ANY],
        out_specs=[row_d, row_f, row_f, row_f, row_d, _full((1, d)), _full((1, d))],
        out_shape=[
            jax.ShapeDtypeStruct((rows, d), F32),
            jax.ShapeDtypeStruct((rows, f), BF16),
            jax.ShapeDtypeStruct((rows, f), BF16),
            jax.ShapeDtypeStruct((rows, f), BF16),
            jax.ShapeDtypeStruct((rows, d), BF16),
            jax.ShapeDtypeStruct((1, d), F32),
            jax.ShapeDtypeStruct((1, d), F32),
        ],
        scratch_shapes=_weight_scratch(locs, d),
        compiler_params=_params(VMEM_BIG),
    )(dh, z, g, u, wbuf, ln_g, after)


LANES = 128


def _gates(xc, wa_ref, ba, wx_ref, bx, lam):
    xcb = xc.astype(BF16)
    r = _sigmoid(jnp.dot(xcb, wa_ref[0], preferred_element_type=F32) + ba)
    i = _sigmoid(jnp.dot(xcb, wx_ref[0], preferred_element_type=F32) + bx)
    sp = _softplus(-lam)
    log_a = -LRU_C * sp * r
    a = jnp.exp(log_a)
    s = jnp.sqrt(-_expm1_nonpos(2.0 * log_a))
    return xcb, r, i, sp, a, s


def _conv(xr, cw, cb):
    return cb + cw[3:4] * xr + cw[2:3] * pltpu.roll(xr, 1, 0) + cw[1:2] * pltpu.roll(xr, 2, 0) + cw[0:1] * pltpu.roll(xr, 3, 0)


def _rglru_fwd(proj, cw, cb, wa, ba, wx, bx, lam, t_len, pad, name):
    rows = proj.shape[0]
    n_seq = rows // t_len
    n_tile = REC_WIDTH // LANES
    groups = t_len // 8

    def body(xr_ref, gt_ref, cw_ref, cb_ref, wa_ref, ba_ref, wx_ref, bx_ref, lam_ref, xc_ref, h_ref, y_ref, a_s, b_s):
        valid = lax.broadcasted_iota(jnp.int32, (t_len, 1), 0) >= pad
        xr = jnp.where(valid, xr_ref[...], 0.0)
        xc = _conv(xr, cw_ref[...], cb_ref[...])
        xc_ref[...] = xc
        _, _, i, _, a, s = _gates(xc, wa_ref, ba_ref[...], wx_ref, bx_ref[...], lam_ref[...])
        a_s[...] = a
        b_s[...] = jnp.where(valid, s * (i * xc), 0.0)
        sub = lax.broadcasted_iota(jnp.int32, (8, LANES), 0)

        def step(gi, carry):
            r0 = pl.multiple_of(gi * 8, 8)
            av = a_s[pl.ds(r0, 8), :]
            bv = b_s[pl.ds(r0, 8), :]
            for d in (1, 2, 4):
                m = sub >= d
                a_sh = pltpu.roll(av, d, 0)
                b_sh = pltpu.roll(bv, d, 0)
                bv = jnp.where(m, av * b_sh + bv, bv)
                av = jnp.where(m, av * a_sh, av)
            hv = av * carry + bv
            h_ref[pl.ds(r0, 8), :] = hv
            return jnp.broadcast_to(hv[7:8, :], (8, LANES))

        lax.fori_loop(0, groups, step, jnp.zeros((8, LANES), F32), unroll=4)
        gelu, _ = _gelu_and_grad(gt_ref[...])
        y_ref[...] = h_ref[...] * gelu

    seq_tile = lambda col0: pl.BlockSpec((t_len, LANES), lambda s, j: (s, col0 + j))
    vec = pl.BlockSpec((1, LANES), lambda s, j: (0, j))
    out = jax.ShapeDtypeStruct((rows, REC_WIDTH), F32)
    return pl.pallas_call(
        body,
        name=name,
        grid=(n_seq, n_tile),
        in_specs=[
            seq_tile(0),
            seq_tile(n_tile),
            pl.BlockSpec((CONV_WIDTH, LANES), lambda s, j: (0, j)),
            vec,
            pl.BlockSpec((1, LANES, LANES), lambda s, j: (j, 0, 0)),
            vec,
            pl.BlockSpec((1, LANES, LANES), lambda s, j: (j, 0, 0)),
            vec,
            vec,
        ],
        out_specs=[seq_tile(0)] * 3,
        out_shape=[out, out, out],
        scratch_shapes=[pltpu.VMEM((t_len, LANES), F32), pltpu.VMEM((t_len, LANES), F32)],
        compiler_params=_params(VMEM_BIG),
    )(proj, proj, cw, cb, wa, ba, wx, bx, lam)


def _rglru_bwd(dy, h, proj, xc, cw, wa, ba, wx, bx, lam, t_len, pad, name):
    rows = proj.shape[0]
    n_seq = rows // t_len
    n_tile = REC_WIDTH // LANES
    groups = t_len // 8

    def body(dy_ref, h_ref, xr_ref, gt_ref, xc_ref, cw_ref, wa_ref, ba_ref, wx_ref, bx_ref, lam_ref,
             dxr_ref, dgt_ref, dcw_ref, dcb_ref, dwa_ref, dba_ref, dwx_ref, dbx_ref, dlam_ref, a_s, c_s, db_s, da_s):
        first_seq = pl.program_id(1) == 0

        @pl.when(first_seq)
        def _():
            for ref in (dcw_ref, dcb_ref, dwa_ref, dba_ref, dwx_ref, dbx_ref, dlam_ref):
                ref[...] = jnp.zeros_like(ref)

        valid = lax.broadcasted_iota(jnp.int32, (t_len, 1), 0) >= pad
        gelu, dgelu = _gelu_and_grad(gt_ref[...])
        dyv = dy_ref[...]
        dho = dyv * gelu
        dgt_ref[...] = (dyv * h_ref[...] * dgelu).astype(BF16)
        xc = xc_ref[...]
        lam = lam_ref[...]
        xcb, r, i, sp, a, s = _gates(xc, wa_ref, ba_ref[...], wx_ref, bx_ref[...], lam)
        a_s[...] = a
        c_s[...] = a * dho
        db_s[...] = dho
        sub = lax.broadcasted_iota(jnp.int32, (8, LANES), 0)

        def step(k, carry):
            gi = groups - 1 - k
            r0 = pl.multiple_of(gi * 8, 8)
            av = a_s[pl.ds(r0, 8), :]
            cv = c_s[pl.ds(r0, 8), :]
            for d in (1, 2, 4):
                m = sub < 8 - d
                a_sh = pltpu.roll(av, 8 - d, 0)
                c_sh = pltpu.roll(cv, 8 - d, 0)
                cv = jnp.where(m, av * c_sh + cv, cv)
                av = jnp.where(m, av * a_sh, av)
            ev = av * carry + cv
            e_next = jnp.where(sub < 7, pltpu.roll(ev, 7, 0), carry)
            dht = db_s[pl.ds(r0, 8), :] + e_next
            hv = h_ref[pl.ds(r0, 8), :]
            rp = pl.multiple_of(jnp.maximum(gi - 1, 0) * 8, 8)
            h_before = jnp.where(gi > 0, jnp.broadcast_to(h_ref[pl.ds(rp, 8), :][7:8, :], (8, LANES)), 0.0)
            h_prev = jnp.where(sub >= 1, pltpu.roll(hv, 1, 0), h_before)
            db_s[pl.ds(r0, 8), :] = dht
            da_s[pl.ds(r0, 8), :] = dht * h_prev
            return jnp.broadcast_to(ev[0:1, :], (8, LANES))

        lax.fori_loop(0, groups, step, jnp.zeros((8, LANES), F32), unroll=4)

        db = jnp.where(valid, db_s[...], 0.0)
        da = da_s[...]
        ds = db * (i * xc)
        di = db * s * xc
        dxc = db * s * i
        dlog_a = da * a - ds * (a * a) / jnp.maximum(s, 1e-30)
        dr = dlog_a * (-LRU_C * sp)
        dsp = jnp.sum(dlog_a * (-LRU_C) * r, axis=0, keepdims=True)
        dlam_ref[...] += dsp * (-_sigmoid(-lam))
        dpr = dr * r * (1.0 - r)
        dpi = di * i * (1.0 - i)
        dprb = dpr.astype(BF16)
        dpib = dpi.astype(BF16)
        dxc = dxc + lax.dot_general(dprb, wa_ref[0], NT, preferred_element_type=F32) + lax.dot_general(dpib, wx_ref[0], NT, preferred_element_type=F32)
        dwa_ref[0] += lax.dot_general(xcb, dprb, TN, preferred_element_type=F32)
        dwx_ref[0] += lax.dot_general(xcb, dpib, TN, preferred_element_type=F32)
        dba_ref[...] += jnp.sum(dpr, axis=0, keepdims=True)
        dbx_ref[...] += jnp.sum(dpi, axis=0, keepdims=True)
        dxc = jnp.where(valid, dxc, 0.0)
        xr = jnp.where(valid, xr_ref[...], 0.0)
        dcb_ref[...] += jnp.sum(dxc, axis=0, keepdims=True)
        for k in range(CONV_WIDTH):
            shifted = xr if k == CONV_WIDTH - 1 else pltpu.roll(xr, CONV_WIDTH - 1 - k, 0)
            dcw_ref[k : k + 1, :] += jnp.sum(dxc * shifted, axis=0, keepdims=True)
        cw = cw_ref[...]
        dxr = cw[3:4] * dxc + cw[2:3] * pltpu.roll(dxc, t_len - 1, 0) + cw[1:2] * pltpu.roll(dxc, t_len - 2, 0) + cw[0:1] * pltpu.roll(dxc, t_len - 3, 0)
        dxr_ref[...] = jnp.where(valid, dxr, 0.0).astype(BF16)

    seq_tile = lambda col0: pl.BlockSpec((t_len, LANES), lambda j, s: (s, col0 + j))
    vec = pl.BlockSpec((1, LANES), lambda j, s: (0, j))
    mat = pl.BlockSpec((1, LANES, LANES), lambda j, s: (j, 0, 0))
    cwb = pl.BlockSpec((CONV_WIDTH, LANES), lambda j, s: (0, j))
    big = jax.ShapeDtypeStruct((rows, REC_WIDTH), BF16)
    vec_shape = jax.ShapeDtypeStruct((1, REC_WIDTH), F32)
    mat_shape = jax.ShapeDtypeStruct((n_tile, LANES, LANES), F32)
    return pl.pallas_call(
        body,
        name=name,
        grid=(n_tile, n_seq),
        in_specs=[seq_tile(0), seq_tile(0), seq_tile(0), seq_tile(n_tile), seq_tile(0), cwb, mat, vec, mat, vec, vec],
        out_specs=[seq_tile(0), seq_tile(0), cwb, vec, mat, vec, mat, vec, vec],
        out_shape=[big, big, jax.ShapeDtypeStruct((CONV_WIDTH, REC_WIDTH), F32), vec_shape, mat_shape, vec_shape, mat_shape, vec_shape, vec_shape],
        scratch_shapes=[pltpu.VMEM((t_len, LANES), F32)] * 4,
        compiler_params=_params(VMEM_BIG),
    )(dy, h, proj, proj, xc, cw, wa, ba, wx, bx, lam)


QKV_WIDTH = ATTN_WIDTH + 2 * KV_WIDTH


def _rotate(v, cos, sin_up, sin_down):
    width = v.shape[1]
    return v * cos + pltpu.roll(v, width - ROPE_DIM // 2, 1) * sin_up + pltpu.roll(v, ROPE_DIM // 2, 1) * sin_down


def _rope_rows(t_len):
    return t_len // 4 if t_len % 64 == 0 else BLOCK


def _rope_fwd(proj, cos, sin_up, sin_down, t_len, name):
    rows = proj.shape[0]
    rb = _rope_rows(t_len)
    nb = t_len // rb
    n_seq = rows // t_len
    q0 = 2 * REC_WIDTH

    def body(q_ref, k_ref, v_ref, cos_ref, up_ref, down_ref, o_ref):
        cos_t, up_t, down_t = cos_ref[...], up_ref[...], down_ref[...]
        o_ref[:, 0:ATTN_WIDTH] = _rotate(q_ref[...], cos_t, up_t, down_t).astype(BF16)
        o_ref[:, ATTN_WIDTH : ATTN_WIDTH + KV_WIDTH] = _rotate(k_ref[...], cos_t[:, :KV_WIDTH], up_t[:, :KV_WIDTH], down_t[:, :KV_WIDTH]).astype(BF16)
        o_ref[:, ATTN_WIDTH + KV_WIDTH :] = v_ref[...].astype(BF16)

    tab = pl.BlockSpec((rb, ATTN_WIDTH), lambda s, n: (n, 0))
    return pl.pallas_call(
        body,
        name=name,
        grid=(n_seq, nb),
        in_specs=[
            pl.BlockSpec((rb, ATTN_WIDTH), lambda s, n: (s * nb + n, q0 // ATTN_WIDTH)),
            pl.BlockSpec((rb, KV_WIDTH), lambda s, n: (s * nb + n, (q0 + ATTN_WIDTH) // KV_WIDTH)),
            pl.BlockSpec((rb, KV_WIDTH), lambda s, n: (s * nb + n, (q0 + ATTN_WIDTH) // KV_WIDTH + 1)),
            tab,
            tab,
            tab,
        ],
        out_specs=pl.BlockSpec((rb, QKV_WIDTH), lambda s, n: (s * nb + n, 0)),
        out_shape=jax.ShapeDtypeStruct((rows, QKV_WIDTH), BF16),
    )(proj, proj, proj, cos, sin_up, sin_down)


def _rope_bwd(dq, dk, dv, cos, sin_up, sin_down, t_len, name):
    rows = dq.shape[0]
    rb = _rope_rows(t_len)
    nb = t_len // rb
    n_seq = rows // t_len

    def body(dq_ref, dk_ref, dv_ref, cos_ref, up_ref, down_ref, o_ref):
        cos_t, up_t, down_t = cos_ref[...], -up_ref[...], -down_ref[...]
        o_ref[:, 0:ATTN_WIDTH] = _rotate(dq_ref[...], cos_t, up_t, down_t).astype(BF16)
        o_ref[:, ATTN_WIDTH : ATTN_WIDTH + KV_WIDTH] = _rotate(dk_ref[...], cos_t[:, :KV_WIDTH], up_t[:, :KV_WIDTH], down_t[:, :KV_WIDTH]).astype(BF16)
        o_ref[:, ATTN_WIDTH + KV_WIDTH :] = dv_ref[...].astype(BF16)

    tab = pl.BlockSpec((rb, ATTN_WIDTH), lambda s, n: (n, 0))
    blk = lambda w: pl.BlockSpec((rb, w), lambda s, n: (s * nb + n, 0))
    return pl.pallas_call(
        body,
        name=name,
        grid=(n_seq, nb),
        in_specs=[blk(ATTN_WIDTH), blk(KV_WIDTH), blk(KV_WIDTH), tab, tab, tab],
        out_specs=blk(QKV_WIDTH),
        out_shape=jax.ShapeDtypeStruct((rows, QKV_WIDTH), BF16),
    )(dq, dk, dv, cos, sin_up, sin_down)


GROUP = 4


def _attn_bias(pad):
    out = []
    for n in range(3):
        q_pos = n * BLOCK + jnp.arange(BLOCK)[:, None] - pad
        col = jnp.arange(3 * BLOCK)[None, :]
        first = col < BLOCK
        k_pos = col - pad + jnp.where(first, 0, (n - 2) * BLOCK)
        dist = q_pos - k_pos
        band = (dist >= 0) & (dist < WINDOW) & (k_pos >= N_META) & jnp.logical_not(first)
        meta = (k_pos >= 0) & (k_pos < N_META) & (k_pos <= q_pos) & first
        out.append(jnp.tile(jnp.where(band | meta, 0.0, NEG_INF).astype(F32), (GROUP, 1)))
    return jnp.stack(out)


def _bias_spec():
    return pl.BlockSpec((1, GROUP * BLOCK, 3 * BLOCK), lambda s, n: (jnp.minimum(n, 2), 0, 0))


def _in_half(e):
    lane = lax.broadcasted_iota(jnp.int32, (1, BLOCK), 1)
    return lane >= HEAD_DIM if e else lane < HEAD_DIM


def _head_views(q2, e, g):
    v = jnp.where(_in_half(e), q2, 0.0)
    return pltpu.roll(v, HEAD_DIM, 1) if e != g else v


def _tile(g, t):
    j = (GROUP // 2) * g + t
    return slice(j * BLOCK, (j + 1) * BLOCK)


def _stack_rows(ref, g):
    parts = []
    for t in range(GROUP // 2):
        tile = ref[:, _tile(g, t)].astype(F32)
        parts += [_head_views(tile, e, g) for e in range(2)]
    return jnp.concatenate(parts, axis=0)


def _stack_group(q_ref, sink_ref, g):
    return _stack_rows(q_ref, g), sink_ref[g][:, 0:1]


def _unstack_pair(v, g, t):
    out = None
    for e in range(2):
        blk = v[(2 * t + e) * BLOCK : (2 * t + e + 1) * BLOCK, :]
        blk = pltpu.roll(blk, HEAD_DIM, 1) if e != g else blk
        out = blk if out is None else out + blk
    return out


def _exp_scores(scores, bias, sink):
    s = scores * (HEAD_DIM**-0.5) + bias
    m = jnp.maximum(jnp.max(s, axis=-1, keepdims=True), sink)
    return jnp.exp(s - m), jnp.exp(sink - m)


def _softmax_with_sink(scores, bias, sink):
    p, p_sink = _exp_scores(scores, bias, sink)
    inv = 1.0 / (jnp.sum(p, axis=-1, keepdims=True) + p_sink)
    return p * inv, p_sink * inv


def _kv_specs(nb):
    k_col = ATTN_WIDTH // KV_WIDTH
    specs = []
    for col in (k_col, k_col + 1):
        specs += [
            pl.BlockSpec((BLOCK, KV_WIDTH), lambda s, n, col=col: (s * nb, col)),
            pl.BlockSpec((BLOCK, KV_WIDTH), lambda s, n, col=col: (s * nb + jnp.maximum(n - 1, 0), col)),
            pl.BlockSpec((BLOCK, KV_WIDTH), lambda s, n, col=col: (s * nb + n, col)),
        ]
    return specs


def _attn_fwd(qkv, sink_rows, bias, t_len, name):
    rows = qkv.shape[0]
    nb = t_len // BLOCK
    n_seq = rows // t_len

    def body(q_ref, km_ref, kp_ref, kc_ref, vm_ref, vp_ref, vc_ref, sink_ref, bias_ref, o_ref):
        keys = jnp.concatenate([km_ref[...], kp_ref[...], kc_ref[...]], axis=0)
        vals = jnp.concatenate([vm_ref[...], vp_ref[...], vc_ref[...]], axis=0).astype(F32)
        for g in range(N_Q_HEADS // GROUP):
            qs, sink = _stack_group(q_ref, sink_ref, g)
            scores = lax.dot_general(qs.astype(BF16), keys, NT, preferred_element_type=F32)
            p, p_sink = _exp_scores(scores, bias_ref[0], sink)
            ext = jnp.where(_in_half(g), vals, 1.0).astype(BF16)
            oe = jnp.dot(p.astype(BF16), ext, preferred_element_type=F32)
            denom = pltpu.roll(oe, HEAD_DIM, 1) + p_sink
            o = jnp.where(_in_half(g), oe / denom, 0.0)
            for t in range(GROUP // 2):
                o_ref[:, _tile(g, t)] = _unstack_pair(o, g, t)

    return pl.pallas_call(
        body,
        name=name,
        grid=(n_seq, nb),
        in_specs=[pl.BlockSpec((BLOCK, ATTN_WIDTH), lambda s, n: (s * nb + n, 0))] + _kv_specs(nb) + [_full(sink_rows.shape), _bias_spec()],
        out_specs=pl.BlockSpec((BLOCK, ATTN_WIDTH), lambda s, n: (s * nb + n, 0)),
        out_shape=jax.ShapeDtypeStruct((rows, ATTN_WIDTH), F32),
    )(qkv, qkv, qkv, qkv, qkv, qkv, qkv, sink_rows, bias)


def _attn_bwd(qkv, sink_rows, bias, o, do, t_len, name):
    rows = qkv.shape[0]
    nb = t_len // BLOCK
    n_seq = rows // t_len

    def body(q_ref, km_ref, kp_ref, kc_ref, vm_ref, vp_ref, vc_ref, sink_ref, bias_ref, o_ref, do_ref, dq_ref, dk_ref, dv_ref, dsink_ref):
        s_id, n = pl.program_id(0), pl.program_id(1)

        @pl.when(n == 0)
        def _():
            dk_ref[...] = jnp.zeros_like(dk_ref)
            dv_ref[...] = jnp.zeros_like(dv_ref)

        @pl.when((n == 0) & (s_id == 0))
        def _():
            dsink_ref[...] = jnp.zeros_like(dsink_ref)

        keys = jnp.concatenate([km_ref[...], kp_ref[...], kc_ref[...]], axis=0)
        vals = jnp.concatenate([vm_ref[...], vp_ref[...], vc_ref[...]], axis=0)
        dkeys = jnp.zeros((3 * BLOCK, KV_WIDTH), F32)
        dvals = jnp.zeros((3 * BLOCK, KV_WIDTH), F32)
        for g in range(N_Q_HEADS // GROUP):
            qs, sink = _stack_group(q_ref, sink_ref, g)
            qsb = qs.astype(BF16)
            scores = lax.dot_general(qsb, keys, NT, preferred_element_type=F32)
            p, p_sink = _softmax_with_sink(scores, bias_ref[0], sink)
            dos = _stack_rows(do_ref, g)
            delta = jnp.sum(dos * _stack_rows(o_ref, g), axis=-1, keepdims=True)
            dosb = dos.astype(BF16)
            dp = lax.dot_general(dosb, vals, NT, preferred_element_type=F32)
            ds = (p * (dp - delta) * (HEAD_DIM**-0.5)).astype(BF16)
            dqs = jnp.where(_in_half(g), jnp.dot(ds, keys, preferred_element_type=F32), 0.0)
            for t in range(GROUP // 2):
                dq_ref[:, _tile(g, t)] = _unstack_pair(dqs, g, t)
            dkeys = dkeys + lax.dot_general(ds, qsb, TN, preferred_element_type=F32)
            dvals = dvals + lax.dot_general(p.astype(BF16), dosb, TN, preferred_element_type=F32)
            sink_term = p_sink * delta
            for hh in range(GROUP):
                head = GROUP * g + hh
                part = -jnp.sum(sink_term[hh * BLOCK : (hh + 1) * BLOCK, :], axis=0, keepdims=True)
                dsink_ref[head : head + 1, :] += jnp.broadcast_to(part, (1, BLOCK))
        r_prev = pl.multiple_of(jnp.maximum(n - 1, 0) * BLOCK, BLOCK)
        r_cur = pl.multiple_of(n * BLOCK, BLOCK)
        for acc, d in ((dk_ref, dkeys), (dv_ref, dvals)):
            acc[0:BLOCK, :] += d[0:BLOCK]
            acc[pl.ds(r_prev, BLOCK), :] += d[BLOCK : 2 * BLOCK]
            acc[pl.ds(r_cur, BLOCK), :] += d[2 * BLOCK :]

    q_blk = pl.BlockSpec((BLOCK, ATTN_WIDTH), lambda s, n: (s * nb + n, 0))
    seq_kv = pl.BlockSpec((t_len, KV_WIDTH), lambda s, n: (s, 0))
    return pl.pallas_call(
        body,
        name=name,
        grid=(n_seq, nb),
        in_specs=[q_blk] + _kv_specs(nb) + [_full(sink_rows.shape), _bias_spec(), q_blk, q_blk],
        out_specs=[q_blk, seq_kv, seq_kv, _full((N_Q_HEADS, BLOCK))],
        out_shape=[
            jax.ShapeDtypeStruct((rows, ATTN_WIDTH), F32),
            jax.ShapeDtypeStruct((rows, KV_WIDTH), F32),
            jax.ShapeDtypeStruct((rows, KV_WIDTH), F32),
            jax.ShapeDtypeStruct((N_Q_HEADS, BLOCK), F32),
        ],
    )(qkv, qkv, qkv, qkv, qkv, qkv, qkv, sink_rows, bias, o, do)


def _mix_out_fwd(y_rec, y_attn, h, wbuf, loc, g_rec, g_attn, ln_g, ln_b, alpha, name):
    rows, d = h.shape
    tm = _row_tile(rows)

    def body(yr_ref, ya_ref, h_ref, w_hbm, gr_ref, ga_ref, lg_ref, lb_ref, ho_ref, z_ref, hb_ref, w_ref, sems):
        _load_blocks(w_hbm, [(*loc, w_ref)], sems)
        nr = _rms(yr_ref[...], gr_ref[...]).astype(BF16)
        na = _rms(ya_ref[...], ga_ref[...]).astype(BF16)
        m = jnp.dot(nr, w_ref[0:REC_WIDTH, :], preferred_element_type=F32) + jnp.dot(na, w_ref[REC_WIDTH:, :], preferred_element_type=F32)
        z = alpha * h_ref[...] + m
        z_ref[...] = z
        xhat, _ = _ln_stats(z)
        ho = xhat * lg_ref[...] + lb_ref[...]
        ho_ref[...] = ho
        hb_ref[...] = ho.astype(BF16)

    row_d = pl.BlockSpec((tm, d), lambda i: (i, 0))
    row_h = pl.BlockSpec((tm, REC_WIDTH), lambda i: (i, 0))
    return pl.pallas_call(
        body,
        name=name,
        grid=(rows // tm,),
        in_specs=[row_h, row_h, row_d, ANY, _full((1, REC_WIDTH)), _full((1, ATTN_WIDTH)), _full((1, d)), _full((1, d))],
        out_specs=[row_d, row_d, row_d],
        out_shape=[jax.ShapeDtypeStruct((rows, d), F32)] * 2 + [jax.ShapeDtypeStruct((rows, d), BF16)],
        scratch_shapes=_weight_scratch([loc], d),
    )(y_rec, y_attn, h, wbuf, g_rec, g_attn, ln_g, ln_b)


def _mix_out_bwd(dh, z, y_rec, y_attn, wbuf, loc, g_rec, g_attn, ln_g, name):
    rows, d = dh.shape
    tm = _row_tile(rows)
    mix = REC_WIDTH + ATTN_WIDTH

    def body(dh_ref, z_ref, yr_ref, ya_ref, w_hbm, gr_ref, ga_ref, lg_ref, dz_ref, dzb_ref, dyr_ref, dya_ref, yn_ref, dgam_ref, dbet_ref, dgr_ref, dga_ref, w_ref, sems):
        _load_blocks(w_hbm, [(*loc, w_ref)], sems)

        @pl.when(pl.program_id(0) == 0)
        def _():
            for ref in (dgam_ref, dbet_ref, dgr_ref, dga_ref):
                ref[...] = jnp.zeros_like(ref)

        dz, dgam, dbet = _ln_bwd(dh_ref[...], z_ref[...], lg_ref[...])
        dgam_ref[...] += dgam
        dbet_ref[...] += dbet
        dz_ref[...] = dz
        dzb = dz.astype(BF16)
        dzb_ref[...] = dzb
        dyn = lax.dot_general(dzb, w_ref[...], NT, preferred_element_type=F32)
        dyr, dgr, nr = _rms_bwd(dyn[:, 0:REC_WIDTH], yr_ref[...], gr_ref[...])
        dya, dga, na = _rms_bwd(dyn[:, REC_WIDTH:], ya_ref[...], ga_ref[...])
        dyr_ref[...] = dyr
        dya_ref[...] = dya
        dgr_ref[...] += dgr
        dga_ref[...] += dga
        yn_ref[:, 0:REC_WIDTH] = nr.astype(BF16)
        yn_ref[:, REC_WIDTH:] = na.astype(BF16)

    row_d = pl.BlockSpec((tm, d), lambda i: (i, 0))
    row_h = pl.BlockSpec((tm, REC_WIDTH), lambda i: (i, 0))
    row_m = pl.BlockSpec((tm, mix), lambda i: (i, 0))
    return pl.pallas_call(
        body,
        name=name,
        grid=(rows // tm,),
        in_specs=[row_d, row_d, row_h, row_h, ANY, _full((1, REC_WIDTH)), _full((1, ATTN_WIDTH)), _full((1, d))],
        out_specs=[row_d, row_d, row_h, row_h, row_m, _full((1, d)), _full((1, d)), _full((1, REC_WIDTH)), _full((1, ATTN_WIDTH))],
        out_shape=[
            jax.ShapeDtypeStruct((rows, d), F32),
            jax.ShapeDtypeStruct((rows, d), BF16),
            jax.ShapeDtypeStruct((rows, REC_WIDTH), F32),
            jax.ShapeDtypeStruct((rows, ATTN_WIDTH), F32),
            jax.ShapeDtypeStruct((rows, mix), BF16),
            jax.ShapeDtypeStruct((1, d), F32),
            jax.ShapeDtypeStruct((1, d), F32),
            jax.ShapeDtypeStruct((1, REC_WIDTH), F32),
            jax.ShapeDtypeStruct((1, ATTN_WIDTH), F32),
        ],
        scratch_shapes=_weight_scratch([loc], d),
    )(dh, z, y_rec, y_attn, wbuf, g_rec, g_attn, ln_g)


def _loss_head(y, target, t_len, first_token, name):
    rows, d = y.shape
    tm = _row_tile(rows)

    def body(y_ref, t_ref, loss_ref, dy_ref):
        i = pl.program_id(0)

        @pl.when(i == 0)
        def _():
            loss_ref[...] = jnp.zeros_like(loss_ref)

        row = i * tm + lax.broadcasted_iota(jnp.int32, (tm, 1), 0)
        is_token = lax.rem(row, t_len) >= first_token
        err = jnp.where(is_token, y_ref[...] - t_ref[...], 0.0)
        dy_ref[...] = err / d
        per_row = jnp.sum(err * err, axis=-1, keepdims=True) / d
        loss_ref[...] += jnp.broadcast_to(0.5 * jnp.sum(per_row, axis=0, keepdims=True), (1, BLOCK))

    row_d = pl.BlockSpec((tm, d), lambda i: (i, 0))
    return pl.pallas_call(
        body,
        name=name,
        grid=(rows // tm,),
        in_specs=[row_d, row_d],
        out_specs=[_full((1, BLOCK)), row_d],
        out_shape=[jax.ShapeDtypeStruct((1, BLOCK), F32), jax.ShapeDtypeStruct((rows, d), F32)],
    )(y, target)


def _meta_grad(dh0, t_len, pad, name):
    rows, d = dh0.shape
    nb = t_len // BLOCK
    n_seq = rows // t_len

    def body(dh_ref, o_ref):
        @pl.when(pl.program_id(0) == 0)
        def _():
            o_ref[...] = jnp.zeros_like(o_ref)

        o_ref[...] += dh_ref[pad : pad + N_META, :]

    return pl.pallas_call(
        body,
        name=name,
        grid=(n_seq,),
        in_specs=[pl.BlockSpec((BLOCK, d), lambda s: (s * nb, 0))],
        out_specs=_full((N_META, d)),
        out_shape=jax.ShapeDtypeStruct((N_META, d), F32),
    )(dh0)


def _adamw_math(w, g, m, v):
    nm = ADAM_B1 * m + (1.0 - ADAM_B1) * g
    nv = ADAM_B2 * v + (1.0 - ADAM_B2) * (g * g)
    m_hat = nm / (1.0 - ADAM_B1**ADAM_STEP)
    v_hat = nv / (1.0 - ADAM_B2**ADAM_STEP)
    return -ADAM_LR * (m_hat / (jnp.sqrt(v_hat) + ADAM_EPS) + ADAM_WD * w), nm, nv


def _reduce_update(parts, own, w, m, v, outs, layer, name):
    _, nr, d = parts.shape
    _, a_dim, b_dim = w.shape
    assert (a_dim, b_dim) == (nr, d), (w.shape, parts.shape)

    def body(p_ref, own_hbm, w_ref, m_ref, v_ref, g_in, dl_in, nm_in, nv_in, g_ref, dl_ref, nm_ref, nv_ref, own_v, acc, sem):
        me = 4 * lax.axis_index("x") + 2 * lax.axis_index("y") + lax.axis_index("c")
        cp = pltpu.make_async_copy(own_hbm.at[me], own_v, sem)
        cp.start()
        cp.wait()
        acc[...] = own_v[...].astype(F32)
        for dev in range(N_DEV):

            @pl.when(me != dev)
            def _():
                acc[...] += p_ref[dev].astype(F32)

        g = acc[...]
        g_ref[0] = g
        dl_ref[0], nm_ref[0], nv_ref[0] = _adamw_math(w_ref[0], g, m_ref[0], v_ref[0])

    at_layer = pl.BlockSpec((1, a_dim, b_dim), lambda i: (layer, 0, 0))
    return pl.pallas_call(
        body,
        name=name,
        grid=(1,),
        in_specs=[_full((N_DEV, nr, d)), ANY, at_layer, at_layer, at_layer, ANY, ANY, ANY, ANY],
        out_specs=[at_layer] * 4,
        out_shape=[jax.ShapeDtypeStruct(w.shape, F32)] * 4,
        input_output_aliases={5: 0, 6: 1, 7: 2, 8: 3},
        scratch_shapes=[pltpu.VMEM((nr, d), parts.dtype), pltpu.VMEM((nr, d), F32), pltpu.SemaphoreType.DMA(())],
        compiler_params=_params(VMEM_BIG),
    )(parts, own, w, m, v, *outs)


def _adamw(w, g, m, v, name):
    shape = w.shape
    cols = shape[-1]
    rows = w.size // cols
    tr = rows
    for cand in (512, 256, 128, 64):
        if rows % cand == 0 and rows > cand:
            tr = cand
            break

    def body(w_ref, g_ref, m_ref, v_ref, d_ref, nm_ref, nv_ref):
        d_ref[...], nm_ref[...], nv_ref[...] = _adamw_math(w_ref[...], g_ref[...], m_ref[...], v_ref[...])

    blk = pl.BlockSpec((tr, cols), lambda i: (i, 0))
    flat = [a.reshape(rows, cols) for a in (w, g, m, v)]
    outs = pl.pallas_call(
        body,
        name=name,
        grid=(rows // tr,),
        in_specs=[blk] * 4,
        out_specs=[blk] * 3,
        out_shape=[jax.ShapeDtypeStruct((rows, cols), F32)] * 3,
    )(*flat)
    return [o.reshape(shape) for o in outs]


WEIGHTS = ["meta_tokens", "ffn1_w_gate", "ffn1_w_up", "ffn1_w_down", "ln1_g", "ln1_b", "w_in", "conv_w", "conv_b", "gate_a_w", "gate_a_b",
           "gate_x_w", "gate_x_b", "lru_lambda", "attn_sinks", "norm_rec_g", "norm_attn_g", "w_out", "ln2_g", "ln2_b", "ffn2_w_gate",
           "ffn2_w_up", "ffn2_w_down", "ln3_g", "ln3_b"]
BIG = [("ffn1_w_gate", True), ("ffn1_w_up", True), ("ffn1_w_down", False), ("w_in", True), ("w_out", False),
       ("ffn2_w_gate", True), ("ffn2_w_up", True), ("ffn2_w_down", False)]


SMALL = ["ln1_g", "ln1_b", "ln2_g", "ln2_b", "ln3_g", "ln3_b", "conv_b", "gate_a_b", "gate_x_b", "lru_lambda", "norm_rec_g", "norm_attn_g",
         "conv_w", "gate_a_w", "gate_x_w", "attn_sinks"]
GATHER_GROUPS = [["ffn1_w_gate", "ffn1_w_up", "ffn1_w_down"], ["w_in", "w_out"], ["ffn2_w_gate", "ffn2_w_up", "ffn2_w_down"]]
EXCHANGE_GROUPS = [["ffn2_w_gate", "ffn2_w_up", "ffn2_w_down", "w_out"], ["w_in", "ffn1_w_gate", "ffn1_w_up", "ffn1_w_down"]]


def _rope_tables(t_len, pad):
    pos = (jnp.arange(t_len) - pad).astype(F32)
    inv_freq = ROPE_THETA ** (-jnp.arange(0, ROPE_DIM, 2, dtype=F32) / ROPE_DIM)
    ang = pos[:, None] * inv_freq[None, :]
    cos, sin = jnp.cos(ang), jnp.sin(ang)
    half = ROPE_DIM // 2
    rest = jnp.zeros((t_len, HEAD_DIM - ROPE_DIM), F32)
    zero = jnp.zeros((t_len, half), F32)
    cos_h = jnp.concatenate([cos, cos, rest + 1.0], axis=1)
    up_h = jnp.concatenate([-sin, zero, rest], axis=1)
    down_h = jnp.concatenate([zero, sin, rest], axis=1)
    return [jnp.tile(t, (1, ATTN_WIDTH // HEAD_DIM)) for t in (cos_h, up_h, down_h)]


def _gate_tiles(w):
    tiles = jnp.zeros((w.shape[0], w.shape[1] // 2, 2 * HEAD_DIM, 2 * HEAD_DIM), w.dtype)
    tiles = tiles.at[:, :, :HEAD_DIM, :HEAD_DIM].set(w[:, 0::2]).at[:, :, HEAD_DIM:, HEAD_DIM:].set(w[:, 1::2])
    return tiles.astype(BF16)


def _gate_blocks(tiles):
    pairs = jnp.stack([tiles[:, :HEAD_DIM, :HEAD_DIM], tiles[:, HEAD_DIM:, HEAD_DIM:]], axis=1)
    return pairs.reshape(2 * tiles.shape[0], HEAD_DIM, HEAD_DIM)


def kernel(x, meta_tokens, ffn1_w_gate, ffn1_w_up, ffn1_w_down, ln1_g, ln1_b, w_in, conv_w, conv_b, gate_a_w, gate_a_b, gate_x_w, gate_x_b, lru_lambda, attn_sinks, norm_rec_g, norm_attn_g, w_out, ln2_g, ln2_b, ffn2_w_gate, ffn2_w_up, ffn2_w_down, ln3_g, ln3_b, loss_target, m_meta_tokens, m_ffn1_w_gate, m_ffn1_w_up, m_ffn1_w_down, m_ln1_g, m_ln1_b, m_w_in, m_conv_w, m_conv_b, m_gate_a_w, m_gate_a_b, m_gate_x_w, m_gate_x_b, m_lru_lambda, m_attn_sinks, m_norm_rec_g, m_norm_attn_g, m_w_out, m_ln2_g, m_ln2_b, m_ffn2_w_gate, m_ffn2_w_up, m_ffn2_w_down, m_ln3_g, m_ln3_b, v_meta_tokens, v_ffn1_w_gate, v_ffn1_w_up, v_ffn1_w_down, v_ln1_g, v_ln1_b, v_w_in, v_conv_w, v_conv_b, v_gate_a_w, v_gate_a_b, v_gate_x_w, v_gate_x_b, v_lru_lambda, v_attn_sinks, v_norm_rec_g, v_norm_attn_g, v_w_out, v_ln2_g, v_ln2_b, v_ffn2_w_gate, v_ffn2_w_up, v_ffn2_w_down, v_ln3_g, v_ln3_b):
    given = dict(locals())
    w = {k: given[k] for k in WEIGHTS}
    n_seq, seq, d = x.shape
    depth = ln1_g.shape[0]
    alpha = (2.0 * depth) ** 0.25
    pad = (-(N_META + seq)) % BLOCK
    t_len = pad + N_META + seq
    rows = n_seq * t_len
    me = 4 * lax.axis_index("x") + 2 * lax.axis_index("y") + lax.axis_index("c")

    transposed = dict(BIG)

    def turned(a, name):
        return jnp.swapaxes(a, 1, 2) if transposed[name] else a

    small = jnp.concatenate([meta_tokens, conv_w.reshape(-1, BLOCK)], axis=0)
    (small_all,) = _all_gather([small], small, "gather_small")
    sent, loc, gathers, chain = {}, {}, {}, small_all

    def start_gather(l, k):
        shard = jnp.concatenate([sent[name][l] for name in GATHER_GROUPS[k]], axis=0)
        zone = lax.dynamic_update_slice(lax.empty((N_DEV,) + shard.shape, BF16), shard[None], (me, 0, 0))
        gathers[(l, k)] = _push_start([shard], [zone], chain, False, f"gather_start_{l}_{k}")
        return gathers[(l, k)][2][0]

    for k, group in enumerate(GATHER_GROUPS):
        off = 0
        for name in group:
            sent[name] = turned(w[name], name).astype(BF16)
            loc[name] = (off, sent[name].shape[1])
            off += sent[name].shape[1]
        chain = start_gather(0, k)
    for l in range(1, depth):
        for k in range(len(GATHER_GROUPS)):
            chain = start_gather(l, k)
    ffn1_loc = [loc["ffn1_w_gate"], loc["ffn1_w_up"], loc["ffn1_w_down"]]
    ffn2_loc = [loc["ffn2_w_gate"], loc["ffn2_w_up"], loc["ffn2_w_down"]]
    meta_full = small_all[:, :N_META, :].transpose(1, 0, 2).reshape(N_META, d)
    conv_shard = conv_w.shape[-1]
    conv_full = small_all[:, N_META:, :].reshape(N_DEV, depth, CONV_WIDTH, conv_shard).transpose(1, 2, 0, 3).reshape(depth, CONV_WIDTH, REC_WIDTH)

    cos, sin_up, sin_down = _rope_tables(t_len, pad)
    bias = _attn_bias(pad)
    wa_tiles, wx_tiles = _gate_tiles(gate_a_w), _gate_tiles(gate_x_w)
    row1 = lambda a: a.reshape(1, -1)

    h = jnp.concatenate([jnp.zeros((n_seq, pad, d), F32), jnp.broadcast_to(meta_full[None], (n_seq, N_META, d)), x], axis=1).reshape(rows, d)
    target = jnp.pad(loss_target, ((0, 0), (pad + N_META, 0), (0, 0))).reshape(rows, d)
    hb = h.astype(BF16)
    saved, wbufs = [], []

    def gathered(l, k, after):
        _, (buf,) = _push_wait(*gathers[(l, k)], after, False, f"gather_wait_{l}_{k}")
        return buf

    for l in range(depth):
        w_ffn1 = gathered(l, 0, chain if l == 0 else h)
        s = {"h0": hb}
        h, s["g1"], s["u1"], s["z1"], hb = _ffn_fwd(h, w_ffn1, ffn1_loc, row1(ln1_g[l]), row1(ln1_b[l]), alpha, "ffn1_fwd")
        s["h1"] = hb
        s["wa"], s["wx"] = wa_tiles[l], wx_tiles[l]
        s["sinks"] = jnp.broadcast_to(jnp.repeat(attn_sinks[l].reshape(N_Q_HEADS // GROUP, GROUP), BLOCK, axis=1)[:, :, None], (N_Q_HEADS // GROUP, GROUP * BLOCK, BLOCK))
        wbuf = gathered(l, 1, h)
        proj = _proj_fwd(h, wbuf, loc["w_in"], "proj_fwd")
        s["proj"] = proj
        s["xc"], s["hrec"], s["y_rec"] = _rglru_fwd(proj, conv_full[l], row1(conv_b[l]), s["wa"], row1(gate_a_b[l]), s["wx"], row1(gate_x_b[l]),
                                                     row1(lru_lambda[l]), t_len, pad, "rglru_fwd")
        s["qkv"] = _rope_fwd(proj, cos, sin_up, sin_down, t_len, "rope_fwd")
        s["y_attn"] = _attn_fwd(s["qkv"], s["sinks"], bias, t_len, "attn_fwd")
        h, s["z2"], hb = _mix_out_fwd(s["y_rec"], s["y_attn"], h, wbuf, loc["w_out"], row1(norm_rec_g[l]), row1(norm_attn_g[l]), row1(ln2_g[l]), row1(ln2_b[l]), alpha, "mix_out_fwd")
        s["h2"] = hb
        w_ffn2 = gathered(l, 2, h)
        h, s["g2"], s["u2"], s["z3"], hb = _ffn_fwd(h, w_ffn2, ffn2_loc, row1(ln3_g[l]), row1(ln3_b[l]), alpha, "ffn2_fwd")
        saved.append(s)
        wbufs.append((w_ffn1, wbuf, w_ffn2))
    loss_part, dh = _loss_head(h, target, t_len, pad + N_META, "loss_head")

    exchanges = {}
    small_grads = [None] * depth
    behind = dh

    def exchange(l, k, bg, chain):
        gs = [bg[name].reshape(N_DEV, -1, d) for name in EXCHANGE_GROUPS[k]]
        zones = [lax.empty(g.shape, g.dtype) for g in gs]
        exchanges[(l, k)] = _push_start(gs, zones, chain, True, f"exchange_start_{l}_{k}")
        return exchanges[(l, k)][2][0]

    for l in reversed(range(depth)):
        (w_ffn1, wbuf, w_ffn2), s = wbufs[l], saved[l]
        bg, sg = {}, {}
        dh, dg, du, act, dyb, sg["ln3_g"], sg["ln3_b"] = _ffn_bwd(dh, s["z3"], s["g2"], s["u2"], w_ffn2, ffn2_loc, row1(ln3_g[l]), behind, alpha, "ffn2_bwd")
        bg["ffn2_w_gate"] = _mm_tn(dg, s["h2"], BF16, "ffn_wgrad_in")
        bg["ffn2_w_up"] = _mm_tn(du, s["h2"], BF16, "ffn_wgrad_in")
        bg["ffn2_w_down"] = _mm_tn(act, dyb, BF16, "ffn_wgrad_down")
        dz, dzb, dy_rec, dy_attn, yn, sg["ln2_g"], sg["ln2_b"], sg["norm_rec_g"], sg["norm_attn_g"] = _mix_out_bwd(
            dh, s["z2"], s["y_rec"], s["y_attn"], wbuf, loc["w_out"], row1(norm_rec_g[l]), row1(norm_attn_g[l]), row1(ln2_g[l]), "mix_out_bwd")
        bg["w_out"] = _mm_tn(yn, dzb, BF16, "w_out_wgrad")
        behind = exchange(l, 0, bg, behind)
        dxr, dgt, sg["conv_w"], sg["conv_b"], dwa, sg["gate_a_b"], dwx, sg["gate_x_b"], sg["lru_lambda"] = _rglru_bwd(
            dy_rec, s["hrec"], s["proj"], s["xc"], conv_full[l], s["wa"], row1(gate_a_b[l]), s["wx"], row1(gate_x_b[l]), row1(lru_lambda[l]), t_len, pad, "rglru_bwd")
        sg["gate_a_w"], sg["gate_x_w"] = _gate_blocks(dwa), _gate_blocks(dwx)
        dq, dk, dv, dsink = _attn_bwd(s["qkv"], s["sinks"], bias, s["y_attn"], dy_attn, t_len, "attn_bwd")
        sg["attn_sinks"] = dsink[:, 0]
        dqkv = _rope_bwd(dq, dk, dv, cos, sin_up, sin_down, t_len, "rope_bwd")
        dproj = jnp.concatenate([dxr, dgt, dqkv], axis=1)
        bg["w_in"] = _mm_tn(dproj, s["h1"], BF16, "w_in_wgrad")
        dh = _proj_bwd(dproj, wbuf, loc["w_in"], dz, alpha, "proj_bwd")
        dh, dg, du, act, dyb, sg["ln1_g"], sg["ln1_b"] = _ffn_bwd(dh, s["z1"], s["g1"], s["u1"], w_ffn1, ffn1_loc, row1(ln1_g[l]), behind, alpha, "ffn1_bwd")
        small_grads[l] = sg
        if l == 0:
            dmeta = _meta_grad(dh, t_len, pad, "meta_grad")
            pieces = [small_grads[j][name].reshape(-1) for name in SMALL for j in range(depth)] + [dmeta.reshape(-1), loss_part[0, :1]]
            sizes = [p.shape[0] for p in pieces]
            flat = jnp.concatenate(pieces)
            width = 1024
            n_rows = -(-flat.shape[0] // (8 * width)) * 8
            flat = jnp.pad(flat, (0, n_rows * width - flat.shape[0])).reshape(n_rows, width)
            zone = lax.dynamic_update_slice(lax.empty((N_DEV, n_rows, width), F32), flat[None], (me, 0, 0))
            small_gather = _push_start([flat], [zone], behind, False, "small_grads_start")
            behind = small_gather[2][0]
        bg["ffn1_w_gate"] = _mm_tn(dg, s["h0"], BF16, "ffn_wgrad_in")
        bg["ffn1_w_up"] = _mm_tn(du, s["h0"], BF16, "ffn_wgrad_in")
        bg["ffn1_w_down"] = _mm_tn(act, dyb, BF16, "ffn_wgrad_down")
        behind = exchange(l, 1, bg, behind)
    grad_x = dh.reshape(n_seq, t_len, d)[:, pad + N_META :, :]

    state = {name: [turned(a, name) for a in (w[name], given["m_" + name], given["v_" + name])] for name, _ in BIG}
    outs = {name: [lax.empty(state[name][0].shape, F32) for _ in range(4)] for name, _ in BIG}

    def reduce_group(l, k, after):
        send_sems, recv_sems, srcs, zones = exchanges[(l, k)]
        owns, parts = _push_wait(send_sems, recv_sems, srcs, zones, after, True, f"exchange_wait_{l}_{k}")
        for name, own, p in zip(EXCHANGE_GROUPS[k], owns, parts):
            outs[name] = _reduce_update(p, own, *state[name], outs[name], l, "reduce_update")
        return outs[EXCHANGE_GROUPS[k][-1]][0]

    grads = {}
    order = [(l, k) for l in reversed(range(depth)) for k in range(len(EXCHANGE_GROUPS))]
    after = behind
    for l, k in order[:-1]:
        after = reduce_group(l, k, after)
    _, (flat_all,) = _push_wait(*small_gather, after, False, "small_grads_wait")
    total = _sum8(flat_all, "sum_small_grads").reshape(-1)
    offs = [0]
    for sz in sizes:
        offs.append(offs[-1] + sz)
    taken = [total[offs[k] : offs[k + 1]] for k in range(len(sizes))]
    for j, name in enumerate(SMALL):
        full_shape = (depth,) + ((CONV_WIDTH, REC_WIDTH) if name == "conv_w" else w[name].shape[1:])
        grads[name] = total[offs[j * depth] : offs[(j + 1) * depth]].reshape(full_shape)
    grads["conv_w"] = lax.dynamic_slice_in_dim(grads["conv_w"], me * conv_shard, conv_shard, axis=2)
    meta_shard = meta_tokens.shape[1]
    grads["meta_tokens"] = lax.dynamic_slice_in_dim(taken[-2].reshape(N_META, d), me * meta_shard, meta_shard, axis=1)
    loss = taken[-1][0]
    deltas, new_m, new_v = {}, {}, {}
    for name in WEIGHTS:
        if name not in dict(BIG):
            deltas[name], new_m[name], new_v[name] = _adamw(w[name], grads[name], given["m_" + name], given["v_" + name], "adamw")

    reduce_group(*order[-1], new_v[WEIGHTS[-1]])
    for name, _ in BIG:
        grads[name], deltas[name], new_m[name], new_v[name] = [turned(a, name) for a in outs[name]]
    return (loss, grad_x, *[grads[k] for k in WEIGHTS], *[deltas[k] for k in WEIGHTS], *[new_m[k] for k in WEIGHTS], *[new_v[k] for k in WEIGHTS])
```

```python
import functools

import jax
import jax.numpy as jnp
from jax import lax
from jax.experimental import pallas as pl
from jax.experimental.pallas import tpu as pltpu

F32 = jnp.float32
BF16 = jnp.bfloat16
MESH = pl.DeviceIdType.MESH

N_DEV = 8
N_META = 16
BLOCK = 128
WINDOW = 128
REC_WIDTH = 512
ATTN_WIDTH = 512
KV_WIDTH = 128
HEAD_DIM = 64
N_Q_HEADS = 8
ROPE_DIM = 16
ROPE_THETA = 500000.0
CONV_WIDTH = 4
LRU_C = 8.0
LN_EPS = 1e-5
RMS_EPS = 1e-6
NEG_INF = -1e30
ADAM_LR, ADAM_B1, ADAM_B2, ADAM_EPS, ADAM_WD, ADAM_STEP = 0.001, 0.9, 0.999, 1e-08, 0.01, 10

NT = (((1,), (1,)), ((), ()))
TN = (((0,), (0,)), ((), ()))
VMEM_BIG = 56 * 1024 * 1024


def _params(vmem=None):
    return pltpu.CompilerParams(vmem_limit_bytes=vmem)


def _row_tile(rows):
    return 256 if rows % 256 == 0 else 128


def _sigmoid(x):
    return 1.0 / (1.0 + jnp.exp(-x))


def _expm1_nonpos(x):
    series = x * (1.0 + 0.5 * x * (1.0 + x / 3.0 * (1.0 + 0.25 * x * (1.0 + 0.2 * x))))
    return jnp.where(x > -0.05, series, jnp.exp(x) - 1.0)


def _softplus(x):
    e = jnp.exp(-jnp.abs(x))
    log1p = jnp.where(e < 1e-3, e * (1.0 - e * (0.5 - e / 3.0)), jnp.log(1.0 + e))
    return jnp.maximum(x, 0.0) + log1p


def _gelu_and_grad(x):
    c0 = 0.7978845608028654
    x2 = x * x
    t = jnp.tanh(c0 * (x + 0.044715 * x * x2))
    gelu = 0.5 * x * (1.0 + t)
    grad = 0.5 * (1.0 + t) + 0.5 * x * (1.0 - t * t) * c0 * (1.0 + 3.0 * 0.044715 * x2)
    return gelu, grad


def _ln_stats(z):
    mu = jnp.mean(z, axis=-1, keepdims=True)
    zc = z - mu
    var = jnp.mean(zc * zc, axis=-1, keepdims=True)
    rstd = lax.rsqrt(var + LN_EPS)
    return zc * rstd, rstd


def _ln_bwd(dy, z, gamma):
    xhat, rstd = _ln_stats(z)
    dxh = dy * gamma
    m1 = jnp.mean(dxh, axis=-1, keepdims=True)
    m2 = jnp.mean(dxh * xhat, axis=-1, keepdims=True)
    dz = rstd * (dxh - m1 - xhat * m2)
    return dz, jnp.sum(dy * xhat, axis=0, keepdims=True), jnp.sum(dy, axis=0, keepdims=True)


def _rms(y, gamma):
    r = lax.rsqrt(jnp.mean(y * y, axis=-1, keepdims=True) + RMS_EPS)
    return y * r * gamma


def _rms_bwd(dout, y, gamma):
    r = lax.rsqrt(jnp.mean(y * y, axis=-1, keepdims=True) + RMS_EPS)
    n = y * r
    dn = dout * gamma
    dy = r * (dn - n * jnp.mean(dn * n, axis=-1, keepdims=True))
    return dy, jnp.sum(dout * n, axis=0, keepdims=True), n * gamma


def _load_blocks(w_hbm, items, sems):
    @pl.when(pl.program_id(0) == 0)
    def _():
        copies = []
        for k, (off, nr, dst) in enumerate(items):
            for dev in range(N_DEV):
                copies.append(pltpu.make_async_copy(w_hbm.at[dev, pl.ds(off, nr), :], dst.at[pl.ds(dev * nr, nr), :], sems.at[k, dev]))
        for cp in copies:
            cp.start()
        for cp in copies:
            cp.wait()


def _full(shape):
    return pl.BlockSpec(shape, lambda *_: (0,) * len(shape))


ANY = pl.BlockSpec(memory_space=pl.ANY)


def _all_gather(xs, after, name):
    n = len(xs)

    def body(*refs):
        ins, outs = refs[:n], refs[n + 1 : 2 * n + 1]
        send_sems, recv_sems, local_sems = refs[2 * n + 1 :]
        x, y, c = lax.axis_index("x"), lax.axis_index("y"), lax.axis_index("c")
        me, sibling = (x, y, c), (x, y, 1 - c)
        chips = [(1 - x, y), (x, 1 - y), (1 - x, 1 - y)]

        def slot(i, dev):
            return outs[i].at[4 * dev[0] + 2 * dev[1] + dev[2]]

        def copy(i, k, block, to, src=None):
            return pltpu.make_async_remote_copy(
                src_ref=slot(i, block) if src is None else src,
                dst_ref=slot(i, block),
                send_sem=send_sems.at[i, k],
                recv_sem=recv_sems.at[i, k],
                device_id=to,
                device_id_type=MESH,
            )

        mine = [pltpu.make_async_copy(ins[i], slot(i, me), local_sems.at[i]) for i in range(n)]
        for cp in mine:
            cp.start()
        first = []
        for i in range(n):
            first.append(copy(i, 0, me, sibling, src=ins[i]))
            first += [copy(i, 1 + j, me, (*chip, c), src=ins[i]) for j, chip in enumerate(chips)]
        for cp in first:
            cp.start()
        passed = []
        for j, chip in enumerate(chips):
            for i in range(n):
                copy(i, 1 + j, (*chip, c), me).wait_recv()
                fwd = copy(i, 4 + j, (*chip, c), sibling)
                fwd.start()
                passed.append(fwd)
        for i in range(n):
            copy(i, 0, sibling, me).wait_recv()
            for j, chip in enumerate(chips):
                copy(i, 4 + j, (*chip, 1 - c), me).wait_recv()
        for cp in first + passed:
            cp.wait_send()
        for cp in mine:
            cp.wait()

    return pl.pallas_call(
        body,
        name=name,
        out_shape=[jax.ShapeDtypeStruct((N_DEV,) + a.shape, a.dtype) for a in xs],
        in_specs=[ANY] * (n + 1),
        out_specs=[ANY] * n,
        scratch_shapes=[pltpu.SemaphoreType.DMA((n, 7)), pltpu.SemaphoreType.DMA((n, 7)), pltpu.SemaphoreType.DMA((n,))],
    )(*xs, after)


HBM = pl.BlockSpec(memory_space=pltpu.HBM)
SEM = pl.BlockSpec(memory_space=pltpu.SEMAPHORE)
EFFECT = pltpu.SideEffectType.DATAFLOW_SIDE_EFFECTING


def _push_copies(ins, lands, send_sems, recv_sems, scatter):
    x, y, c = lax.axis_index("x"), lax.axis_index("y"), lax.axis_index("c")
    me = 4 * x + 2 * y + c
    copies = []
    for k in range(1, N_DEV):
        px = 1 - x if (k >> 2) & 1 else x
        py = 1 - y if (k >> 1) & 1 else y
        pc = 1 - c if k & 1 else c
        for i in range(len(ins)):
            copies.append(
                pltpu.make_async_remote_copy(
                    src_ref=ins[i].at[4 * px + 2 * py + pc] if scatter else ins[i],
                    dst_ref=lands[i].at[me],
                    send_sem=send_sems.at[i * (N_DEV - 1) + k - 1],
                    recv_sem=recv_sems.at[i * (N_DEV - 1) + k - 1],
                    device_id=(px, py, pc),
                    device_id_type=MESH,
                )
            )
    return copies


def _push_start(xs, lands, chain, scatter, name):
    n = len(xs)

    def body(*refs):
        ins, zones = refs[:n], refs[n : 2 * n]
        for cp in _push_copies(ins, zones, refs[2 * n + 1], refs[2 * n + 2], scatter):
            cp.start()

    outs = pl.pallas_call(
        body,
        name=name,
        out_shape=(
            pltpu.SemaphoreType.DMA((n * (N_DEV - 1),)),
            pltpu.SemaphoreType.DMA((n * (N_DEV - 1),)),
            *[pltpu.HBM(a.shape, a.dtype) for a in list(xs) + list(lands)],
        ),
        in_specs=[HBM] * (2 * n) + [ANY],
        out_specs=(SEM, SEM, *[HBM] * (2 * n)),
        input_output_aliases={i: 2 + i for i in range(2 * n)},
        compiler_params=pltpu.CompilerParams(has_side_effects=EFFECT),
    )(*[pltpu.with_memory_space_constraint(a, pltpu.HBM) for a in list(xs) + list(lands)], chain)
    return outs[0], outs[1], list(outs[2 : 2 + n]), list(outs[2 + n : 2 + 2 * n])


def _push_wait(send_sems, recv_sems, srcs, lands, after, scatter, name):
    n = len(srcs)

    def body(*refs):
        ins, zones = refs[:n], refs[n : 2 * n]
        for cp in _push_copies(ins, zones, refs[2 * n], refs[2 * n + 1], scatter):
            cp.wait_send()
            cp.wait_recv()

    outs = pl.pallas_call(
        body,
        name=name,
        out_shape=[pltpu.HBM(a.shape, a.dtype) for a in srcs + lands],
        in_specs=[HBM] * (2 * n) + [SEM, SEM, ANY],
        out_specs=[HBM] * (2 * n),
        input_output_aliases={i: i for i in range(2 * n)},
        compiler_params=pltpu.CompilerParams(has_side_effects=EFFECT),
    )(*srcs, *lands, send_sems, recv_sems, after)
    return list(outs[:n]), list(outs[n:])


def _sum8(parts, name):
    _, rows, cols = parts.shape
    tr = rows
    for cand in (512, 256, 128, 64, 32, 16):
        if rows % cand == 0 and rows > cand:
            tr = cand
            break

    def body(p_ref, o_ref):
        acc = p_ref[0].astype(F32)
        for d in range(1, N_DEV):
            acc = acc + p_ref[d].astype(F32)
        o_ref[...] = acc

    return pl.pallas_call(
        body,
        name=name,
        grid=(rows // tr,),
        in_specs=[pl.BlockSpec((N_DEV, tr, cols), lambda i: (0, i, 0))],
        out_specs=pl.BlockSpec((tr, cols), lambda i: (i, 0)),
        out_shape=jax.ShapeDtypeStruct((rows, cols), F32),
    )(parts)


def _mm_tn(a, b, out_dtype, name):
    rows, m = a.shape
    n = b.shape[1]
    tm = m // 2 if (m // 2) % 128 == 0 and m > 512 else m
    tr = next(t for t in (2176, 1024, 512, 256, 128) if rows % t == 0)
    nk = rows // tr

    def body(a_ref, b_ref, o_ref, acc):
        k = pl.program_id(1)
        part = lax.dot_general(a_ref[...].astype(BF16), b_ref[...].astype(BF16), TN, preferred_element_type=F32)

        @pl.when(k == 0)
        def _():
            acc[...] = part

        @pl.when(k > 0)
        def _():
            acc[...] += part

        @pl.when(k == nk - 1)
        def _():
            o_ref[...] = acc[...].astype(o_ref.dtype)

    return pl.pallas_call(
        body,
        name=name,
        grid=(m // tm, nk),
        in_specs=[pl.BlockSpec((tr, tm), lambda i, k: (k, i)), pl.BlockSpec((tr, n), lambda i, k: (k, 0))],
        out_specs=pl.BlockSpec((tm, n), lambda i, k: (i, 0)),
        out_shape=jax.ShapeDtypeStruct((m, n), out_dtype),
        scratch_shapes=[pltpu.VMEM((tm, n), F32)],
        compiler_params=_params(VMEM_BIG),
    )(a, b)


def _weight_scratch(locs, d):
    return [pltpu.VMEM((N_DEV * nr, d), BF16) for _, nr in locs] + [pltpu.SemaphoreType.DMA((len(locs), N_DEV))]


def _proj_fwd(h, wbuf, loc, name):
    rows, d = h.shape
    n = N_DEV * loc[1]
    tm = _row_tile(rows)

    def body(h_ref, w_hbm, o_ref, w, sems):
        _load_blocks(w_hbm, [(*loc, w)], sems)
        o_ref[...] = lax.dot_general(h_ref[...].astype(BF16), w[...], NT, preferred_element_type=F32)

    return pl.pallas_call(
        body,
        name=name,
        grid=(rows // tm,),
        in_specs=[pl.BlockSpec((tm, d), lambda i: (i, 0)), ANY],
        out_specs=pl.BlockSpec((tm, n), lambda i: (i, 0)),
        out_shape=jax.ShapeDtypeStruct((rows, n), F32),
        scratch_shapes=_weight_scratch([loc], d),
        compiler_params=_params(VMEM_BIG),
    )(h, wbuf)


def _proj_bwd(dproj, wbuf, loc, dz, alpha, name):
    rows, n = dproj.shape
    d = dz.shape[1]
    tm = _row_tile(rows)

    def body(dp_ref, w_hbm, dz_ref, o_ref, w, sems):
        _load_blocks(w_hbm, [(*loc, w)], sems)
        o_ref[...] = alpha * dz_ref[...] + jnp.dot(dp_ref[...], w[...], preferred_element_type=F32)

    return pl.pallas_call(
        body,
        name=name,
        grid=(rows // tm,),
        in_specs=[pl.BlockSpec((tm, n), lambda i: (i, 0)), ANY, pl.BlockSpec((tm, d), lambda i: (i, 0))],
        out_specs=pl.BlockSpec((tm, d), lambda i: (i, 0)),
        out_shape=jax.ShapeDtypeStruct((rows, d), F32),
        scratch_shapes=_weight_scratch([loc], d),
        compiler_params=_params(VMEM_BIG),
    )(dproj, wbuf, dz)


FFN_CHUNKS = 1


def _ffn_fwd(h, wbuf, locs, ln_g, ln_b, alpha, name):
    rows, d = h.shape
    f = N_DEV * locs[0][1]
    tm = _row_tile(rows)
    tf = f // FFN_CHUNKS

    def body(h_ref, w_hbm, lg_ref, lb_ref, ho_ref, g_ref, u_ref, z_ref, hb_ref, wg, wu, wdv, sems):
        _load_blocks(w_hbm, [(*locs[0], wg), (*locs[1], wu), (*locs[2], wdv)], sems)
        x = h_ref[...]
        xb = x.astype(BF16)
        y = None
        for c in range(FFN_CHUNKS):
            sl = slice(c * tf, (c + 1) * tf)
            g = lax.dot_general(xb, wg[sl, :], NT, preferred_element_type=F32)
            u = lax.dot_general(xb, wu[sl, :], NT, preferred_element_type=F32)
            g_ref[:, sl] = g
            u_ref[:, sl] = u
            a = (g * _sigmoid(g) * u).astype(BF16)
            part = jnp.dot(a, wdv[sl, :], preferred_element_type=F32)
            y = part if y is None else y + part
        z = alpha * x + 0.5 * y
        z_ref[...] = z
        xhat, _ = _ln_stats(z)
        ho = xhat * lg_ref[...] + lb_ref[...]
        ho_ref[...] = ho
        hb_ref[...] = ho.astype(BF16)

    row_d = pl.BlockSpec((tm, d), lambda i: (i, 0))
    row_f = pl.BlockSpec((tm, f), lambda i: (i, 0))
    return pl.pallas_call(
        body,
        name=name,
        grid=(rows // tm,),
        in_specs=[row_d, ANY, _full((1, d)), _full((1, d))],
        out_specs=[row_d, row_f, row_f, row_d, row_d],
        out_shape=[
            jax.ShapeDtypeStruct((rows, d), F32),
            jax.ShapeDtypeStruct((rows, f), F32),
            jax.ShapeDtypeStruct((rows, f), F32),
            jax.ShapeDtypeStruct((rows, d), F32),
            jax.ShapeDtypeStruct((rows, d), BF16),
        ],
        scratch_shapes=_weight_scratch(locs, d),
        compiler_params=_params(VMEM_BIG),
    )(h, wbuf, ln_g, ln_b)


def _ffn_bwd(dh, z, g, u, wbuf, locs, ln_g, after, alpha, name):
    rows, d = dh.shape
    f = N_DEV * locs[0][1]
    tm = _row_tile(rows)
    tf = f // FFN_CHUNKS

    def body(dh_ref, z_ref, g_ref, u_ref, w_hbm, lg_ref, after_ref, dx_ref, dg_ref, du_ref, a_ref, dyb_ref, dgam_ref, dbet_ref, wg, wu, wdv, sems):
        _load_blocks(w_hbm, [(*locs[0], wg), (*locs[1], wu), (*locs[2], wdv)], sems)

        @pl.when(pl.program_id(0) == 0)
        def _():
            dgam_ref[...] = jnp.zeros_like(dgam_ref)
            dbet_ref[...] = jnp.zeros_like(dbet_ref)

        dz, dgam, dbet = _ln_bwd(dh_ref[...], z_ref[...], lg_ref[...])
        dgam_ref[...] += dgam
        dbet_ref[...] += dbet
        dyb = (0.5 * dz).astype(BF16)
        dyb_ref[...] = dyb
        dx = alpha * dz
        for c in range(FFN_CHUNKS):
            sl = slice(c * tf, (c + 1) * tf)
            da = lax.dot_general(dyb, wdv[sl, :], NT, preferred_element_type=F32)
            gc = g_ref[:, sl]
            uc = u_ref[:, sl]
            sg = _sigmoid(gc)
            silu = gc * sg
            dgc = (da * uc * (sg * (1.0 + gc * (1.0 - sg)))).astype(BF16)
            duc = (da * silu).astype(BF16)
            a_ref[:, sl] = (silu * uc).astype(BF16)
            dg_ref[:, sl] = dgc
            du_ref[:, sl] = duc
            dx = dx + jnp.dot(dgc, wg[sl, :], preferred_element_type=F32) + jnp.dot(duc, wu[sl, :], preferred_element_type=F32)
        dx_ref[...] = dx

    row_d = pl.BlockSpec((tm, d), lambda i: (i, 0))
    row_f = pl.BlockSpec((tm, f), lambda i: (i, 0))
    return pl.pallas_call(
        body,
        name=name,
        grid=(rows // tm,),
        in_specs=[row_d, row_d, row_f, row_f, ANY, _full((1, d)), ANY],
        out_specs=[row_d, row_f, row_f, row_f, row_d, _full((1, d)), _full((1, d))],
        out_shape=[
            jax.ShapeDtypeStruct((rows, d), F32),
            jax.ShapeDtypeStruct((rows, f), BF16),
            jax.ShapeDtypeStruct((rows, f), BF16),
            jax.ShapeDtypeStruct((rows, f), BF16),
            jax.ShapeDtypeStruct((rows, d), BF16),
            jax.ShapeDtypeStruct((1, d), F32),
            jax.ShapeDtypeStruct((1, d), F32),
        ],
        scratch_shapes=_weight_scratch(locs, d),
        compiler_params=_params(VMEM_BIG),
    )(dh, z, g, u, wbuf, ln_g, after)


LANES = 128


def _gates(xc, wa_ref, ba, wx_ref, bx, lam):
    xcb = xc.astype(BF16)
    r = _sigmoid(jnp.dot(xcb, wa_ref[0], preferred_element_type=F32) + ba)
    i = _sigmoid(jnp.dot(xcb, wx_ref[0], preferred_element_type=F32) + bx)
    sp = _softplus(-lam)
    log_a = -LRU_C * sp * r
    a = jnp.exp(log_a)
    s = jnp.sqrt(-_expm1_nonpos(2.0 * log_a))
    return xcb, r, i, sp, a, s


def _conv(xr, cw, cb):
    return cb + cw[3:4] * xr + cw[2:3] * pltpu.roll(xr, 1, 0) + cw[1:2] * pltpu.roll(xr, 2, 0) + cw[0:1] * pltpu.roll(xr, 3, 0)


def _rglru_fwd(proj, cw, cb, wa, ba, wx, bx, lam, t_len, pad, name):
    rows = proj.shape[0]
    n_seq = rows // t_len
    n_tile = REC_WIDTH // LANES
    groups = t_len // 8

    def body(xr_ref, gt_ref, cw_ref, cb_ref, wa_ref, ba_ref, wx_ref, bx_ref, lam_ref, xc_ref, h_ref, y_ref, a_s, b_s):
        valid = lax.broadcasted_iota(jnp.int32, (t_len, 1), 0) >= pad
        xr = jnp.where(valid, xr_ref[...], 0.0)
        xc = _conv(xr, cw_ref[...], cb_ref[...])
        xc_ref[...] = xc
        _, _, i, _, a, s = _gates(xc, wa_ref, ba_ref[...], wx_ref, bx_ref[...], lam_ref[...])
        a_s[...] = a
        b_s[...] = jnp.where(valid, s * (i * xc), 0.0)
        sub = lax.broadcasted_iota(jnp.int32, (8, LANES), 0)

        def step(gi, carry):
            r0 = pl.multiple_of(gi * 8, 8)
            av = a_s[pl.ds(r0, 8), :]
            bv = b_s[pl.ds(r0, 8), :]
            for d in (1, 2, 4):
                m = sub >= d
                a_sh = pltpu.roll(av, d, 0)
                b_sh = pltpu.roll(bv, d, 0)
                bv = jnp.where(m, av * b_sh + bv, bv)
                av = jnp.where(m, av * a_sh, av)
            hv = av * carry + bv
            h_ref[pl.ds(r0, 8), :] = hv
            return jnp.broadcast_to(hv[7:8, :], (8, LANES))

        lax.fori_loop(0, groups, step, jnp.zeros((8, LANES), F32), unroll=4)
        gelu, _ = _gelu_and_grad(gt_ref[...])
        y_ref[...] = h_ref[...] * gelu

    seq_tile = lambda col0: pl.BlockSpec((t_len, LANES), lambda s, j: (s, col0 + j))
    vec = pl.BlockSpec((1, LANES), lambda s, j: (0, j))
    out = jax.ShapeDtypeStruct((rows, REC_WIDTH), F32)
    return pl.pallas_call(
        body,
        name=name,
        grid=(n_seq, n_tile),
        in_specs=[
            seq_tile(0),
            seq_tile(n_tile),
            pl.BlockSpec((CONV_WIDTH, LANES), lambda s, j: (0, j)),
            vec,
            pl.BlockSpec((1, LANES, LANES), lambda s, j: (j, 0, 0)),
            vec,
            pl.BlockSpec((1, LANES, LANES), lambda s, j: (j, 0, 0)),
            vec,
            vec,
        ],
        out_specs=[seq_tile(0)] * 3,
        out_shape=[out, out, out],
        scratch_shapes=[pltpu.VMEM((t_len, LANES), F32), pltpu.VMEM((t_len, LANES), F32)],
        compiler_params=_params(VMEM_BIG),
    )(proj, proj, cw, cb, wa, ba, wx, bx, lam)


def _rglru_bwd(dy, h, proj, xc, cw, wa, ba, wx, bx, lam, t_len, pad, name):
    rows = proj.shape[0]
    n_seq = rows // t_len
    n_tile = REC_WIDTH // LANES
    groups = t_len // 8

    def body(dy_ref, h_ref, xr_ref, gt_ref, xc_ref, cw_ref, wa_ref, ba_ref, wx_ref, bx_ref, lam_ref,
             dxr_ref, dgt_ref, dcw_ref, dcb_ref, dwa_ref, dba_ref, dwx_ref, dbx_ref, dlam_ref, a_s, c_s, db_s, da_s):
        first_seq = pl.program_id(1) == 0

        @pl.when(first_seq)
        def _():
            for ref in (dcw_ref, dcb_ref, dwa_ref, dba_ref, dwx_ref, dbx_ref, dlam_ref):
                ref[...] = jnp.zeros_like(ref)

        valid = lax.broadcasted_iota(jnp.int32, (t_len, 1), 0) >= pad
        gelu, dgelu = _gelu_and_grad(gt_ref[...])
        dyv = dy_ref[...]
        dho = dyv * gelu
        dgt_ref[...] = (dyv * h_ref[...] * dgelu).astype(BF16)
        xc = xc_ref[...]
        lam = lam_ref[...]
        xcb, r, i, sp, a, s = _gates(xc, wa_ref, ba_ref[...], wx_ref, bx_ref[...], lam)
        a_s[...] = a
        c_s[...] = a * dho
        db_s[...] = dho
        sub = lax.broadcasted_iota(jnp.int32, (8, LANES), 0)

        def step(k, carry):
            gi = groups - 1 - k
            r0 = pl.multiple_of(gi * 8, 8)
            av = a_s[pl.ds(r0, 8), :]
            cv = c_s[pl.ds(r0, 8), :]
            for d in (1, 2, 4):
                m = sub < 8 - d
                a_sh = pltpu.roll(av, 8 - d, 0)
                c_sh = pltpu.roll(cv, 8 - d, 0)
                cv = jnp.where(m, av * c_sh + cv, cv)
                av = jnp.where(m, av * a_sh, av)
            ev = av * carry + cv
            e_next = jnp.where(sub < 7, pltpu.roll(ev, 7, 0), carry)
            dht = db_s[pl.ds(r0, 8), :] + e_next
            hv = h_ref[pl.ds(r0, 8), :]
            rp = pl.multiple_of(jnp.maximum(gi - 1, 0) * 8, 8)
            h_before = jnp.where(gi > 0, jnp.broadcast_to(h_ref[pl.ds(rp, 8), :][7:8, :], (8, LANES)), 0.0)
            h_prev = jnp.where(sub >= 1, pltpu.roll(hv, 1, 0), h_before)
            db_s[pl.ds(r0, 8), :] = dht
            da_s[pl.ds(r0, 8), :] = dht * h_prev
            return jnp.broadcast_to(ev[0:1, :], (8, LANES))

        lax.fori_loop(0, groups, step, jnp.zeros((8, LANES), F32), unroll=4)

        db = jnp.where(valid, db_s[...], 0.0)
        da = da_s[...]
        ds = db * (i * xc)
        di = db * s * xc
        dxc = db * s * i
        dlog_a = da * a - ds * (a * a) / jnp.maximum(s, 1e-30)
        dr = dlog_a * (-LRU_C * sp)
        dsp = jnp.sum(dlog_a * (-LRU_C) * r, axis=0, keepdims=True)
        dlam_ref[...] += dsp * (-_sigmoid(-lam))
        dpr = dr * r * (1.0 - r)
        dpi = di * i * (1.0 - i)
        dprb = dpr.astype(BF16)
        dpib = dpi.astype(BF16)
        dxc = dxc + lax.dot_general(dprb, wa_ref[0], NT, preferred_element_type=F32) + lax.dot_general(dpib, wx_ref[0], NT, preferred_element_type=F32)
        dwa_ref[0] += lax.dot_general(xcb, dprb, TN, preferred_element_type=F32)
        dwx_ref[0] += lax.dot_general(xcb, dpib, TN, preferred_element_type=F32)
        dba_ref[...] += jnp.sum(dpr, axis=0, keepdims=True)
        dbx_ref[...] += jnp.sum(dpi, axis=0, keepdims=True)
        dxc = jnp.where(valid, dxc, 0.0)
        xr = jnp.where(valid, xr_ref[...], 0.0)
        dcb_ref[...] += jnp.sum(dxc, axis=0, keepdims=True)
        for k in range(CONV_WIDTH):
            shifted = xr if k == CONV_WIDTH - 1 else pltpu.roll(xr, CONV_WIDTH - 1 - k, 0)
            dcw_ref[k : k + 1, :] += jnp.sum(dxc * shifted, axis=0, keepdims=True)
        cw = cw_ref[...]
        dxr = cw[3:4] * dxc + cw[2:3] * pltpu.roll(dxc, t_len - 1, 0) + cw[1:2] * pltpu.roll(dxc, t_len - 2, 0) + cw[0:1] * pltpu.roll(dxc, t_len - 3, 0)
        dxr_ref[...] = jnp.where(valid, dxr, 0.0).astype(BF16)

    seq_tile = lambda col0: pl.BlockSpec((t_len, LANES), lambda j, s: (s, col0 + j))
    vec = pl.BlockSpec((1, LANES), lambda j, s: (0, j))
    mat = pl.BlockSpec((1, LANES, LANES), lambda j, s: (j, 0, 0))
    cwb = pl.BlockSpec((CONV_WIDTH, LANES), lambda j, s: (0, j))
    big = jax.ShapeDtypeStruct((rows, REC_WIDTH), BF16)
    vec_shape = jax.ShapeDtypeStruct((1, REC_WIDTH), F32)
    mat_shape = jax.ShapeDtypeStruct((n_tile, LANES, LANES), F32)
    return pl.pallas_call(
        body,
        name=name,
        grid=(n_tile, n_seq),
        in_specs=[seq_tile(0), seq_tile(0), seq_tile(0), seq_tile(n_tile), seq_tile(0), cwb, mat, vec, mat, vec, vec],
        out_specs=[seq_tile(0), seq_tile(0), cwb, vec, mat, vec, mat, vec, vec],
        out_shape=[big, big, jax.ShapeDtypeStruct((CONV_WIDTH, REC_WIDTH), F32), vec_shape, mat_shape, vec_shape, mat_shape, vec_shape, vec_shape],
        scratch_shapes=[pltpu.VMEM((t_len, LANES), F32)] * 4,
        compiler_params=_params(VMEM_BIG),
    )(dy, h, proj, proj, xc, cw, wa, ba, wx, bx, lam)


QKV_WIDTH = ATTN_WIDTH + 2 * KV_WIDTH


def _rotate(v, cos, sin_up, sin_down):
    width = v.shape[1]
    return v * cos + pltpu.roll(v, width - ROPE_DIM // 2, 1) * sin_up + pltpu.roll(v, ROPE_DIM // 2, 1) * sin_down


def _rope_rows(t_len):
    return t_len // 4 if t_len % 64 == 0 else BLOCK


def _rope_fwd(proj, cos, sin_up, sin_down, t_len, name):
    rows = proj.shape[0]
    rb = _rope_rows(t_len)
    nb = t_len // rb
    n_seq = rows // t_len
    q0 = 2 * REC_WIDTH

    def body(q_ref, k_ref, v_ref, cos_ref, up_ref, down_ref, o_ref):
        cos_t, up_t, down_t = cos_ref[...], up_ref[...], down_ref[...]
        o_ref[:, 0:ATTN_WIDTH] = _rotate(q_ref[...], cos_t, up_t, down_t).astype(BF16)
        o_ref[:, ATTN_WIDTH : ATTN_WIDTH + KV_WIDTH] = _rotate(k_ref[...], cos_t[:, :KV_WIDTH], up_t[:, :KV_WIDTH], down_t[:, :KV_WIDTH]).astype(BF16)
        o_ref[:, ATTN_WIDTH + KV_WIDTH :] = v_ref[...].astype(BF16)

    tab = pl.BlockSpec((rb, ATTN_WIDTH), lambda s, n: (n, 0))
    return pl.pallas_call(
        body,
        name=name,
        grid=(n_seq, nb),
        in_specs=[
            pl.BlockSpec((rb, ATTN_WIDTH), lambda s, n: (s * nb + n, q0 // ATTN_WIDTH)),
            pl.BlockSpec((rb, KV_WIDTH), lambda s, n: (s * nb + n, (q0 + ATTN_WIDTH) // KV_WIDTH)),
            pl.BlockSpec((rb, KV_WIDTH), lambda s, n: (s * nb + n, (q0 + ATTN_WIDTH) // KV_WIDTH + 1)),
            tab,
            tab,
            tab,
        ],
        out_specs=pl.BlockSpec((rb, QKV_WIDTH), lambda s, n: (s * nb + n, 0)),
        out_shape=jax.ShapeDtypeStruct((rows, QKV_WIDTH), BF16),
    )(proj, proj, proj, cos, sin_up, sin_down)


def _rope_bwd(dq, dk, dv, cos, sin_up, sin_down, t_len, name):
    rows = dq.shape[0]
    rb = _rope_rows(t_len)
    nb = t_len // rb
    n_seq = rows // t_len

    def body(dq_ref, dk_ref, dv_ref, cos_ref, up_ref, down_ref, o_ref):
        cos_t, up_t, down_t = cos_ref[...], -up_ref[...], -down_ref[...]
        o_ref[:, 0:ATTN_WIDTH] = _rotate(dq_ref[...], cos_t, up_t, down_t).astype(BF16)
        o_ref[:, ATTN_WIDTH : ATTN_WIDTH + KV_WIDTH] = _rotate(dk_ref[...], cos_t[:, :KV_WIDTH], up_t[:, :KV_WIDTH], down_t[:, :KV_WIDTH]).astype(BF16)
        o_ref[:, ATTN_WIDTH + KV_WIDTH :] = dv_ref[...].astype(BF16)

    tab = pl.BlockSpec((rb, ATTN_WIDTH), lambda s, n: (n, 0))
    blk = lambda w: pl.BlockSpec((rb, w), lambda s, n: (s * nb + n, 0))
    return pl.pallas_call(
        body,
        name=name,
        grid=(n_seq, nb),
        in_specs=[blk(ATTN_WIDTH), blk(KV_WIDTH), blk(KV_WIDTH), tab, tab, tab],
        out_specs=blk(QKV_WIDTH),
        out_shape=jax.ShapeDtypeStruct((rows, QKV_WIDTH), BF16),
    )(dq, dk, dv, cos, sin_up, sin_down)


GROUP = 4


def _attn_bias(pad):
    out = []
    for n in range(3):
        q_pos = n * BLOCK + jnp.arange(BLOCK)[:, None] - pad
        col = jnp.arange(3 * BLOCK)[None, :]
        first = col < BLOCK
        k_pos = col - pad + jnp.where(first, 0, (n - 2) * BLOCK)
        dist = q_pos - k_pos
        band = (dist >= 0) & (dist < WINDOW) & (k_pos >= N_META) & jnp.logical_not(first)
        meta = (k_pos >= 0) & (k_pos < N_META) & (k_pos <= q_pos) & first
        out.append(jnp.tile(jnp.where(band | meta, 0.0, NEG_INF).astype(F32), (GROUP, 1)))
    return jnp.stack(out)


def _bias_spec():
    return pl.BlockSpec((1, GROUP * BLOCK, 3 * BLOCK), lambda s, n: (jnp.minimum(n, 2), 0, 0))


def _in_half(e):
    lane = lax.broadcasted_iota(jnp.int32, (1, BLOCK), 1)
    return lane >= HEAD_DIM if e else lane < HEAD_DIM


def _head_views(q2, e, g):
    v = jnp.where(_in_half(e), q2, 0.0)
    return pltpu.roll(v, HEAD_DIM, 1) if e != g else v


def _tile(g, t):
    j = (GROUP // 2) * g + t
    return slice(j * BLOCK, (j + 1) * BLOCK)


def _stack_rows(ref, g):
    parts = []
    for t in range(GROUP // 2):
        tile = ref[:, _tile(g, t)].astype(F32)
        parts += [_head_views(tile, e, g) for e in range(2)]
    return jnp.concatenate(parts, axis=0)


def _stack_group(q_ref, sink_ref, g):
    return _stack_rows(q_ref, g), sink_ref[g][:, 0:1]


def _unstack_pair(v, g, t):
    out = None
    for e in range(2):
        blk = v[(2 * t + e) * BLOCK : (2 * t + e + 1) * BLOCK, :]
        blk = pltpu.roll(blk, HEAD_DIM, 1) if e != g else blk
        out = blk if out is None else out + blk
    return out


def _exp_scores(scores, bias, sink):
    s = scores * (HEAD_DIM**-0.5) + bias
    m = jnp.maximum(jnp.max(s, axis=-1, keepdims=True), sink)
    return jnp.exp(s - m), jnp.exp(sink - m)


def _softmax_with_sink(scores, bias, sink):
    p, p_sink = _exp_scores(scores, bias, sink)
    inv = 1.0 / (jnp.sum(p, axis=-1, keepdims=True) + p_sink)
    return p * inv, p_sink * inv


def _kv_specs(nb):
    k_col = ATTN_WIDTH // KV_WIDTH
    specs = []
    for col in (k_col, k_col + 1):
        specs += [
            pl.BlockSpec((BLOCK, KV_WIDTH), lambda s, n, col=col: (s * nb, col)),
            pl.BlockSpec((BLOCK, KV_WIDTH), lambda s, n, col=col: (s * nb + jnp.maximum(n - 1, 0), col)),
            pl.BlockSpec((BLOCK, KV_WIDTH), lambda s, n, col=col: (s * nb + n, col)),
        ]
    return specs


def _attn_fwd(qkv, sink_rows, bias, t_len, name):
    rows = qkv.shape[0]
    nb = t_len // BLOCK
    n_seq = rows // t_len

    def body(q_ref, km_ref, kp_ref, kc_ref, vm_ref, vp_ref, vc_ref, sink_ref, bias_ref, o_ref):
        keys = jnp.concatenate([km_ref[...], kp_ref[...], kc_ref[...]], axis=0)
        vals = jnp.concatenate([vm_ref[...], vp_ref[...], vc_ref[...]], axis=0).astype(F32)
        for g in range(N_Q_HEADS // GROUP):
            qs, sink = _stack_group(q_ref, sink_ref, g)
            scores = lax.dot_general(qs.astype(BF16), keys, NT, preferred_element_type=F32)
            p, p_sink = _exp_scores(scores, bias_ref[0], sink)
            ext = jnp.where(_in_half(g), vals, 1.0).astype(BF16)
            oe = jnp.dot(p.astype(BF16), ext, preferred_element_type=F32)
            denom = pltpu.roll(oe, HEAD_DIM, 1) + p_sink
            o = jnp.where(_in_half(g), oe / denom, 0.0)
            for t in range(GROUP // 2):
                o_ref[:, _tile(g, t)] = _unstack_pair(o, g, t)

    return pl.pallas_call(
        body,
        name=name,
        grid=(n_seq, nb),
        in_specs=[pl.BlockSpec((BLOCK, ATTN_WIDTH), lambda s, n: (s * nb + n, 0))] + _kv_specs(nb) + [_full(sink_rows.shape), _bias_spec()],
        out_specs=pl.BlockSpec((BLOCK, ATTN_WIDTH), lambda s, n: (s * nb + n, 0)),
        out_shape=jax.ShapeDtypeStruct((rows, ATTN_WIDTH), F32),
    )(qkv, qkv, qkv, qkv, qkv, qkv, qkv, sink_rows, bias)


def _attn_bwd(qkv, sink_rows, bias, o, do, t_len, name):
    rows = qkv.shape[0]
    nb = t_len // BLOCK
    n_seq = rows // t_len

    def body(q_ref, km_ref, kp_ref, kc_ref, vm_ref, vp_ref, vc_ref, sink_ref, bias_ref, o_ref, do_ref, dq_ref, dk_ref, dv_ref, dsink_ref):
        s_id, n = pl.program_id(0), pl.program_id(1)

        @pl.when(n == 0)
        def _():
            dk_ref[...] = jnp.zeros_like(dk_ref)
            dv_ref[...] = jnp.zeros_like(dv_ref)

        @pl.when((n == 0) & (s_id == 0))
        def _():
            dsink_ref[...] = jnp.zeros_like(dsink_ref)

        keys = jnp.concatenate([km_ref[...], kp_ref[...], kc_ref[...]], axis=0)
        vals = jnp.concatenate([vm_ref[...], vp_ref[...], vc_ref[...]], axis=0)
        dkeys = jnp.zeros((3 * BLOCK, KV_WIDTH), F32)
        dvals = jnp.zeros((3 * BLOCK, KV_WIDTH), F32)
        for g in range(N_Q_HEADS // GROUP):
            qs, sink = _stack_group(q_ref, sink_ref, g)
            qsb = qs.astype(BF16)
            scores = lax.dot_general(qsb, keys, NT, preferred_element_type=F32)
            p, p_sink = _softmax_with_sink(scores, bias_ref[0], sink)
            dos = _stack_rows(do_ref, g)
            delta = jnp.sum(dos * _stack_rows(o_ref, g), axis=-1, keepdims=True)
            dosb = dos.astype(BF16)
            dp = lax.dot_general(dosb, vals, NT, preferred_element_type=F32)
            ds = (p * (dp - delta) * (HEAD_DIM**-0.5)).astype(BF16)
            dqs = jnp.where(_in_half(g), jnp.dot(ds, keys, preferred_element_type=F32), 0.0)
            for t in range(GROUP // 2):
                dq_ref[:, _tile(g, t)] = _unstack_pair(dqs, g, t)
            dkeys = dkeys + lax.dot_general(ds, qsb, TN, preferred_element_type=F32)
            dvals = dvals + lax.dot_general(p.astype(BF16), dosb, TN, preferred_element_type=F32)
            sink_term = p_sink * delta
            for hh in range(GROUP):
                head = GROUP * g + hh
                part = -jnp.sum(sink_term[hh * BLOCK : (hh + 1) * BLOCK, :], axis=0, keepdims=True)
                dsink_ref[head : head + 1, :] += jnp.broadcast_to(part, (1, BLOCK))
        r_prev = pl.multiple_of(jnp.maximum(n - 1, 0) * BLOCK, BLOCK)
        r_cur = pl.multiple_of(n * BLOCK, BLOCK)
        for acc, d in ((dk_ref, dkeys), (dv_ref, dvals)):
            acc[0:BLOCK, :] += d[0:BLOCK]
            acc[pl.ds(r_prev, BLOCK), :] += d[BLOCK : 2 * BLOCK]
            acc[pl.ds(r_cur, BLOCK), :] += d[2 * BLOCK :]

    q_blk = pl.BlockSpec((BLOCK, ATTN_WIDTH), lambda s, n: (s * nb + n, 0))
    seq_kv = pl.BlockSpec((t_len, KV_WIDTH), lambda s, n: (s, 0))
    return pl.pallas_call(
        body,
        name=name,
        grid=(n_seq, nb),
        in_specs=[q_blk] + _kv_specs(nb) + [_full(sink_rows.shape), _bias_spec(), q_blk, q_blk],
        out_specs=[q_blk, seq_kv, seq_kv, _full((N_Q_HEADS, BLOCK))],
        out_shape=[
            jax.ShapeDtypeStruct((rows, ATTN_WIDTH), F32),
            jax.ShapeDtypeStruct((rows, KV_WIDTH), F32),
            jax.ShapeDtypeStruct((rows, KV_WIDTH), F32),
            jax.ShapeDtypeStruct((N_Q_HEADS, BLOCK), F32),
        ],
    )(qkv, qkv, qkv, qkv, qkv, qkv, qkv, sink_rows, bias, o, do)


def _mix_out_fwd(y_rec, y_attn, h, wbuf, loc, g_rec, g_attn, ln_g, ln_b, alpha, name):
    rows, d = h.shape
    tm = _row_tile(rows)

    def body(yr_ref, ya_ref, h_ref, w_hbm, gr_ref, ga_ref, lg_ref, lb_ref, ho_ref, z_ref, hb_ref, w_ref, sems):
        _load_blocks(w_hbm, [(*loc, w_ref)], sems)
        nr = _rms(yr_ref[...], gr_ref[...]).astype(BF16)
        na = _rms(ya_ref[...], ga_ref[...]).astype(BF16)
        m = jnp.dot(nr, w_ref[0:REC_WIDTH, :], preferred_element_type=F32) + jnp.dot(na, w_ref[REC_WIDTH:, :], preferred_element_type=F32)
        z = alpha * h_ref[...] + m
        z_ref[...] = z
        xhat, _ = _ln_stats(z)
        ho = xhat * lg_ref[...] + lb_ref[...]
        ho_ref[...] = ho
        hb_ref[...] = ho.astype(BF16)

    row_d = pl.BlockSpec((tm, d), lambda i: (i, 0))
    row_h = pl.BlockSpec((tm, REC_WIDTH), lambda i: (i, 0))
    return pl.pallas_call(
        body,
        name=name,
        grid=(rows // tm,),
        in_specs=[row_h, row_h, row_d, ANY, _full((1, REC_WIDTH)), _full((1, ATTN_WIDTH)), _full((1, d)), _full((1, d))],
        out_specs=[row_d, row_d, row_d],
        out_shape=[jax.ShapeDtypeStruct((rows, d), F32)] * 2 + [jax.ShapeDtypeStruct((rows, d), BF16)],
        scratch_shapes=_weight_scratch([loc], d),
    )(y_rec, y_attn, h, wbuf, g_rec, g_attn, ln_g, ln_b)


def _mix_out_bwd(dh, z, y_rec, y_attn, wbuf, loc, g_rec, g_attn, ln_g, name):
    rows, d = dh.shape
    tm = _row_tile(rows)
    mix = REC_WIDTH + ATTN_WIDTH

    def body(dh_ref, z_ref, yr_ref, ya_ref, w_hbm, gr_ref, ga_ref, lg_ref, dz_ref, dzb_ref, dyr_ref, dya_ref, yn_ref, dgam_ref, dbet_ref, dgr_ref, dga_ref, w_ref, sems):
        _load_blocks(w_hbm, [(*loc, w_ref)], sems)

        @pl.when(pl.program_id(0) == 0)
        def _():
            for ref in (dgam_ref, dbet_ref, dgr_ref, dga_ref):
                ref[...] = jnp.zeros_like(ref)

        dz, dgam, dbet = _ln_bwd(dh_ref[...], z_ref[...], lg_ref[...])
        dgam_ref[...] += dgam
        dbet_ref[...] += dbet
        dz_ref[...] = dz
        dzb = dz.astype(BF16)
        dzb_ref[...] = dzb
        dyn = lax.dot_general(dzb, w_ref[...], NT, preferred_element_type=F32)
        dyr, dgr, nr = _rms_bwd(dyn[:, 0:REC_WIDTH], yr_ref[...], gr_ref[...])
        dya, dga, na = _rms_bwd(dyn[:, REC_WIDTH:], ya_ref[...], ga_ref[...])
        dyr_ref[...] = dyr
        dya_ref[...] = dya
        dgr_ref[...] += dgr
        dga_ref[...] += dga
        yn_ref[:, 0:REC_WIDTH] = nr.astype(BF16)
        yn_ref[:, REC_WIDTH:] = na.astype(BF16)

    row_d = pl.BlockSpec((tm, d), lambda i: (i, 0))
    row_h = pl.BlockSpec((tm, REC_WIDTH), lambda i: (i, 0))
    row_m = pl.BlockSpec((tm, mix), lambda i: (i, 0))
    return pl.pallas_call(
        body,
        name=name,
        grid=(rows // tm,),
        in_specs=[row_d, row_d, row_h, row_h, ANY, _full((1, REC_WIDTH)), _full((1, ATTN_WIDTH)), _full((1, d))],
        out_specs=[row_d, row_d, row_h, row_h, row_m, _full((1, d)), _full((1, d)), _full((1, REC_WIDTH)), _full((1, ATTN_WIDTH))],
        out_shape=[
            jax.ShapeDtypeStruct((rows, d), F32),
            jax.ShapeDtypeStruct((rows, d), BF16),
            jax.ShapeDtypeStruct((rows, REC_WIDTH), F32),
            jax.ShapeDtypeStruct((rows, ATTN_WIDTH), F32),
            jax.ShapeDtypeStruct((rows, mix), BF16),
            jax.ShapeDtypeStruct((1, d), F32),
            jax.ShapeDtypeStruct((1, d), F32),
            jax.ShapeDtypeStruct((1, REC_WIDTH), F32),
            jax.ShapeDtypeStruct((1, ATTN_WIDTH), F32),
        ],
        scratch_shapes=_weight_scratch([loc], d),
    )(dh, z, y_rec, y_attn, wbuf, g_rec, g_attn, ln_g)


def _loss_head(y, target, t_len, first_token, name):
    rows, d = y.shape
    tm = _row_tile(rows)

    def body(y_ref, t_ref, loss_ref, dy_ref):
        i = pl.program_id(0)

        @pl.when(i == 0)
        def _():
            loss_ref[...] = jnp.zeros_like(loss_ref)

        row = i * tm + lax.broadcasted_iota(jnp.int32, (tm, 1), 0)
        is_token = lax.rem(row, t_len) >= first_token
        err = jnp.where(is_token, y_ref[...] - t_ref[...], 0.0)
        dy_ref[...] = err / d
        per_row = jnp.sum(err * err, axis=-1, keepdims=True) / d
        loss_ref[...] += jnp.broadcast_to(0.5 * jnp.sum(per_row, axis=0, keepdims=True), (1, BLOCK))

    row_d = pl.BlockSpec((tm, d), lambda i: (i, 0))
    return pl.pallas_call(
        body,
        name=name,
        grid=(rows // tm,),
        in_specs=[row_d, row_d],
        out_specs=[_full((1, BLOCK)), row_d],
        out_shape=[jax.ShapeDtypeStruct((1, BLOCK), F32), jax.ShapeDtypeStruct((rows, d), F32)],
    )(y, target)


def _meta_grad(dh0, t_len, pad, name):
    rows, d = dh0.shape
    nb = t_len // BLOCK
    n_seq = rows // t_len

    def body(dh_ref, o_ref):
        @pl.when(pl.program_id(0) == 0)
        def _():
            o_ref[...] = jnp.zeros_like(o_ref)

        o_ref[...] += dh_ref[pad : pad + N_META, :]

    return pl.pallas_call(
        body,
        name=name,
        grid=(n_seq,),
        in_specs=[pl.BlockSpec((BLOCK, d), lambda s: (s * nb, 0))],
        out_specs=_full((N_META, d)),
        out_shape=jax.ShapeDtypeStruct((N_META, d), F32),
    )(dh0)


def _adamw_math(w, g, m, v):
    nm = ADAM_B1 * m + (1.0 - ADAM_B1) * g
    nv = ADAM_B2 * v + (1.0 - ADAM_B2) * (g * g)
    m_hat = nm / (1.0 - ADAM_B1**ADAM_STEP)
    v_hat = nv / (1.0 - ADAM_B2**ADAM_STEP)
    return -ADAM_LR * (m_hat / (jnp.sqrt(v_hat) + ADAM_EPS) + ADAM_WD * w), nm, nv


def _reduce_update(parts, own, w, m, v, outs, layer, name):
    _, nr, d = parts.shape
    _, a_dim, b_dim = w.shape
    assert (a_dim, b_dim) == (nr, d), (w.shape, parts.shape)

    def body(p_ref, own_hbm, w_ref, m_ref, v_ref, g_in, dl_in, nm_in, nv_in, g_ref, dl_ref, nm_ref, nv_ref, own_v, acc, sem):
        me = 4 * lax.axis_index("x") + 2 * lax.axis_index("y") + lax.axis_index("c")
        cp = pltpu.make_async_copy(own_hbm.at[me], own_v, sem)
        cp.start()
        cp.wait()
        acc[...] = own_v[...].astype(F32)
        for dev in range(N_DEV):

            @pl.when(me != dev)
            def _():
                acc[...] += p_ref[dev].astype(F32)

        g = acc[...]
        g_ref[0] = g
        dl_ref[0], nm_ref[0], nv_ref[0] = _adamw_math(w_ref[0], g, m_ref[0], v_ref[0])

    at_layer = pl.BlockSpec((1, a_dim, b_dim), lambda i: (layer, 0, 0))
    return pl.pallas_call(
        body,
        name=name,
        grid=(1,),
        in_specs=[_full((N_DEV, nr, d)), ANY, at_layer, at_layer, at_layer, ANY, ANY, ANY, ANY],
        out_specs=[at_layer] * 4,
        out_shape=[jax.ShapeDtypeStruct(w.shape, F32)] * 4,
        input_output_aliases={5: 0, 6: 1, 7: 2, 8: 3},
        scratch_shapes=[pltpu.VMEM((nr, d), parts.dtype), pltpu.VMEM((nr, d), F32), pltpu.SemaphoreType.DMA(())],
        compiler_params=_params(VMEM_BIG),
    )(parts, own, w, m, v, *outs)


def _adamw(w, g, m, v, name):
    shape = w.shape
    cols = shape[-1]
    rows = w.size // cols
    tr = rows
    for cand in (512, 256, 128, 64):
        if rows % cand == 0 and rows > cand:
            tr = cand
            break

    def body(w_ref, g_ref, m_ref, v_ref, d_ref, nm_ref, nv_ref):
        d_ref[...], nm_ref[...], nv_ref[...] = _adamw_math(w_ref[...], g_ref[...], m_ref[...], v_ref[...])

    blk = pl.BlockSpec((tr, cols), lambda i: (i, 0))
    flat = [a.reshape(rows, cols) for a in (w, g, m, v)]
    outs = pl.pallas_call(
        body,
        name=name,
        grid=(rows // tr,),
        in_specs=[blk] * 4,
        out_specs=[blk] * 3,
        out_shape=[jax.ShapeDtypeStruct((rows, cols), F32)] * 3,
    )(*flat)
    return [o.reshape(shape) for o in outs]


WEIGHTS = ["meta_tokens", "ffn1_w_gate", "ffn1_w_up", "ffn1_w_down", "ln1_g", "ln1_b", "w_in", "conv_w", "conv_b", "gate_a_w", "gate_a_b",
           "gate_x_w", "gate_x_b", "lru_lambda", "attn_sinks", "norm_rec_g", "norm_attn_g", "w_out", "ln2_g", "ln2_b", "ffn2_w_gate",
           "ffn2_w_up", "ffn2_w_down", "ln3_g", "ln3_b"]
BIG = [("ffn1_w_gate", True), ("ffn1_w_up", True), ("ffn1_w_down", False), ("w_in", True), ("w_out", False),
       ("ffn2_w_gate", True), ("ffn2_w_up", True), ("ffn2_w_down", False)]


SMALL = ["ln1_g", "ln1_b", "ln2_g", "ln2_b", "ln3_g", "ln3_b", "conv_b", "gate_a_b", "gate_x_b", "lru_lambda", "norm_rec_g", "norm_attn_g",
         "conv_w", "gate_a_w", "gate_x_w", "attn_sinks"]
GATHER_GROUPS = [["ffn1_w_gate", "ffn1_w_up", "ffn1_w_down"], ["w_in", "w_out"], ["ffn2_w_gate", "ffn2_w_up", "ffn2_w_down"]]
EXCHANGE_GROUPS = [["ffn2_w_gate", "ffn2_w_up", "ffn2_w_down", "w_out"], ["w_in"], ["ffn1_w_gate", "ffn1_w_up", "ffn1_w_down"]]


def _rope_tables(t_len, pad):
    pos = (jnp.arange(t_len) - pad).astype(F32)
    inv_freq = ROPE_THETA ** (-jnp.arange(0, ROPE_DIM, 2, dtype=F32) / ROPE_DIM)
    ang = pos[:, None] * inv_freq[None, :]
    cos, sin = jnp.cos(ang), jnp.sin(ang)
    half = ROPE_DIM // 2
    rest = jnp.zeros((t_len, HEAD_DIM - ROPE_DIM), F32)
    zero = jnp.zeros((t_len, half), F32)
    cos_h = jnp.concatenate([cos, cos, rest + 1.0], axis=1)
    up_h = jnp.concatenate([-sin, zero, rest], axis=1)
    down_h = jnp.concatenate([zero, sin, rest], axis=1)
    return [jnp.tile(t, (1, ATTN_WIDTH // HEAD_DIM)) for t in (cos_h, up_h, down_h)]


def _gate_tiles(w):
    tiles = jnp.zeros((w.shape[0], w.shape[1] // 2, 2 * HEAD_DIM, 2 * HEAD_DIM), w.dtype)
    tiles = tiles.at[:, :, :HEAD_DIM, :HEAD_DIM].set(w[:, 0::2]).at[:, :, HEAD_DIM:, HEAD_DIM:].set(w[:, 1::2])
    return tiles.astype(BF16)


def _gate_blocks(tiles):
    pairs = jnp.stack([tiles[:, :HEAD_DIM, :HEAD_DIM], tiles[:, HEAD_DIM:, HEAD_DIM:]], axis=1)
    return pairs.reshape(2 * tiles.shape[0], HEAD_DIM, HEAD_DIM)


def kernel(x, meta_tokens, ffn1_w_gate, ffn1_w_up, ffn1_w_down, ln1_g, ln1_b, w_in, conv_w, conv_b, gate_a_w, gate_a_b, gate_x_w, gate_x_b, lru_lambda, attn_sinks, norm_rec_g, norm_attn_g, w_out, ln2_g, ln2_b, ffn2_w_gate, ffn2_w_up, ffn2_w_down, ln3_g, ln3_b, loss_target, m_meta_tokens, m_ffn1_w_gate, m_ffn1_w_up, m_ffn1_w_down, m_ln1_g, m_ln1_b, m_w_in, m_conv_w, m_conv_b, m_gate_a_w, m_gate_a_b, m_gate_x_w, m_gate_x_b, m_lru_lambda, m_attn_sinks, m_norm_rec_g, m_norm_attn_g, m_w_out, m_ln2_g, m_ln2_b, m_ffn2_w_gate, m_ffn2_w_up, m_ffn2_w_down, m_ln3_g, m_ln3_b, v_meta_tokens, v_ffn1_w_gate, v_ffn1_w_up, v_ffn1_w_down, v_ln1_g, v_ln1_b, v_w_in, v_conv_w, v_conv_b, v_gate_a_w, v_gate_a_b, v_gate_x_w, v_gate_x_b, v_lru_lambda, v_attn_sinks, v_norm_rec_g, v_norm_attn_g, v_w_out, v_ln2_g, v_ln2_b, v_ffn2_w_gate, v_ffn2_w_up, v_ffn2_w_down, v_ln3_g, v_ln3_b):
    given = dict(locals())
    w = {k: given[k] for k in WEIGHTS}
    n_seq, seq, d = x.shape
    depth = ln1_g.shape[0]
    alpha = (2.0 * depth) ** 0.25
    pad = (-(N_META + seq)) % BLOCK
    t_len = pad + N_META + seq
    rows = n_seq * t_len
    me = 4 * lax.axis_index("x") + 2 * lax.axis_index("y") + lax.axis_index("c")

    transposed = dict(BIG)

    def turned(a, name):
        return jnp.swapaxes(a, 1, 2) if transposed[name] else a

    small = jnp.concatenate([meta_tokens, conv_w.reshape(-1, BLOCK)], axis=0)
    (small_all,) = _all_gather([small], small, "gather_small")
    sent, loc, gathers, chain = {}, {}, {}, small_all

    def start_gather(l, k):
        shard = jnp.concatenate([sent[name][l] for name in GATHER_GROUPS[k]], axis=0)
        zone = lax.dynamic_update_slice(lax.empty((N_DEV,) + shard.shape, BF16), shard[None], (me, 0, 0))
        gathers[(l, k)] = _push_start([shard], [zone], chain, False, f"gather_start_{l}_{k}")
        return gathers[(l, k)][2][0]

    for k, group in enumerate(GATHER_GROUPS):
        off = 0
        for name in group:
            sent[name] = turned(w[name], name).astype(BF16)
            loc[name] = (off, sent[name].shape[1])
            off += sent[name].shape[1]
        chain = start_gather(0, k)
    for l in range(1, depth):
        for k in range(len(GATHER_GROUPS)):
            chain = start_gather(l, k)
    ffn1_loc = [loc["ffn1_w_gate"], loc["ffn1_w_up"], loc["ffn1_w_down"]]
    ffn2_loc = [loc["ffn2_w_gate"], loc["ffn2_w_up"], loc["ffn2_w_down"]]
    meta_full = small_all[:, :N_META, :].transpose(1, 0, 2).reshape(N_META, d)
    conv_shard = conv_w.shape[-1]
    conv_full = small_all[:, N_META:, :].reshape(N_DEV, depth, CONV_WIDTH, conv_shard).transpose(1, 2, 0, 3).reshape(depth, CONV_WIDTH, REC_WIDTH)

    cos, sin_up, sin_down = _rope_tables(t_len, pad)
    bias = _attn_bias(pad)
    wa_tiles, wx_tiles = _gate_tiles(gate_a_w), _gate_tiles(gate_x_w)
    row1 = lambda a: a.reshape(1, -1)

    h = jnp.concatenate([jnp.zeros((n_seq, pad, d), F32), jnp.broadcast_to(meta_full[None], (n_seq, N_META, d)), x], axis=1).reshape(rows, d)
    target = jnp.pad(loss_target, ((0, 0), (pad + N_META, 0), (0, 0))).reshape(rows, d)
    hb = h.astype(BF16)
    saved, wbufs = [], []

    def gathered(l, k, after):
        _, (buf,) = _push_wait(*gathers[(l, k)], after, False, f"gather_wait_{l}_{k}")
        return buf

    for l in range(depth):
        w_ffn1 = gathered(l, 0, chain if l == 0 else h)
        s = {"h0": hb}
        h, s["g1"], s["u1"], s["z1"], hb = _ffn_fwd(h, w_ffn1, ffn1_loc, row1(ln1_g[l]), row1(ln1_b[l]), alpha, "ffn1_fwd")
        s["h1"] = hb
        s["wa"], s["wx"] = wa_tiles[l], wx_tiles[l]
        s["sinks"] = jnp.broadcast_to(jnp.repeat(attn_sinks[l].reshape(N_Q_HEADS // GROUP, GROUP), BLOCK, axis=1)[:, :, None], (N_Q_HEADS // GROUP, GROUP * BLOCK, BLOCK))
        wbuf = gathered(l, 1, h)
        proj = _proj_fwd(h, wbuf, loc["w_in"], "proj_fwd")
        s["proj"] = proj
        s["xc"], s["hrec"], s["y_rec"] = _rglru_fwd(proj, conv_full[l], row1(conv_b[l]), s["wa"], row1(gate_a_b[l]), s["wx"], row1(gate_x_b[l]),
                                                     row1(lru_lambda[l]), t_len, pad, "rglru_fwd")
        s["qkv"] = _rope_fwd(proj, cos, sin_up, sin_down, t_len, "rope_fwd")
        s["y_attn"] = _attn_fwd(s["qkv"], s["sinks"], bias, t_len, "attn_fwd")
        h, s["z2"], hb = _mix_out_fwd(s["y_rec"], s["y_attn"], h, wbuf, loc["w_out"], row1(norm_rec_g[l]), row1(norm_attn_g[l]), row1(ln2_g[l]), row1(ln2_b[l]), alpha, "mix_out_fwd")
        s["h2"] = hb
        w_ffn2 = gathered(l, 2, h)
        h, s["g2"], s["u2"], s["z3"], hb = _ffn_fwd(h, w_ffn2, ffn2_loc, row1(ln3_g[l]), row1(ln3_b[l]), alpha, "ffn2_fwd")
        saved.append(s)
        wbufs.append((w_ffn1, wbuf, w_ffn2))
    loss_part, dh = _loss_head(h, target, t_len, pad + N_META, "loss_head")

    exchanges = {}
    small_grads = [None] * depth
    behind = dh

    def exchange(l, k, bg, chain):
        gs = [bg[name].reshape(N_DEV, -1, d) for name in EXCHANGE_GROUPS[k]]
        zones = [lax.empty(g.shape, g.dtype) for g in gs]
        exchanges[(l, k)] = _push_start(gs, zones, chain, True, f"exchange_start_{l}_{k}")
        return exchanges[(l, k)][2][0]

    for l in reversed(range(depth)):
        (w_ffn1, wbuf, w_ffn2), s = wbufs[l], saved[l]
        bg, sg = {}, {}
        dh, dg, du, act, dyb, sg["ln3_g"], sg["ln3_b"] = _ffn_bwd(dh, s["z3"], s["g2"], s["u2"], w_ffn2, ffn2_loc, row1(ln3_g[l]), behind, alpha, "ffn2_bwd")
        bg["ffn2_w_gate"] = _mm_tn(dg, s["h2"], BF16, "ffn_wgrad_in")
        bg["ffn2_w_up"] = _mm_tn(du, s["h2"], BF16, "ffn_wgrad_in")
        bg["ffn2_w_down"] = _mm_tn(act, dyb, BF16, "ffn_wgrad_down")
        dz, dzb, dy_rec, dy_attn, yn, sg["ln2_g"], sg["ln2_b"], sg["norm_rec_g"], sg["norm_attn_g"] = _mix_out_bwd(
            dh, s["z2"], s["y_rec"], s["y_attn"], wbuf, loc["w_out"], row1(norm_rec_g[l]), row1(norm_attn_g[l]), row1(ln2_g[l]), "mix_out_bwd")
        bg["w_out"] = _mm_tn(yn, dzb, BF16, "w_out_wgrad")
        behind = exchange(l, 0, bg, behind)
        dxr, dgt, sg["conv_w"], sg["conv_b"], dwa, sg["gate_a_b"], dwx, sg["gate_x_b"], sg["lru_lambda"] = _rglru_bwd(
            dy_rec, s["hrec"], s["proj"], s["xc"], conv_full[l], s["wa"], row1(gate_a_b[l]), s["wx"], row1(gate_x_b[l]), row1(lru_lambda[l]), t_len, pad, "rglru_bwd")
        sg["gate_a_w"], sg["gate_x_w"] = _gate_blocks(dwa), _gate_blocks(dwx)
        dq, dk, dv, dsink = _attn_bwd(s["qkv"], s["sinks"], bias, s["y_attn"], dy_attn, t_len, "attn_bwd")
        sg["attn_sinks"] = dsink[:, 0]
        dqkv = _rope_bwd(dq, dk, dv, cos, sin_up, sin_down, t_len, "rope_bwd")
        dproj = jnp.concatenate([dxr, dgt, dqkv], axis=1)
        bg["w_in"] = _mm_tn(dproj, s["h1"], BF16, "w_in_wgrad")
        behind = exchange(l, 1, bg, behind)
        dh = _proj_bwd(dproj, wbuf, loc["w_in"], dz, alpha, "proj_bwd")
        dh, dg, du, act, dyb, sg["ln1_g"], sg["ln1_b"] = _ffn_bwd(dh, s["z1"], s["g1"], s["u1"], w_ffn1, ffn1_loc, row1(ln1_g[l]), behind, alpha, "ffn1_bwd")
        small_grads[l] = sg
        if l == 0:
            dmeta = _meta_grad(dh, t_len, pad, "meta_grad")
            pieces = [small_grads[j][name].reshape(-1) for name in SMALL for j in range(depth)] + [dmeta.reshape(-1), loss_part[0, :1]]
            sizes = [p.shape[0] for p in pieces]
            flat = jnp.concatenate(pieces)
            width = 1024
            n_rows = -(-flat.shape[0] // (8 * width)) * 8
            flat = jnp.pad(flat, (0, n_rows * width - flat.shape[0])).reshape(n_rows, width)
            zone = lax.dynamic_update_slice(lax.empty((N_DEV, n_rows, width), F32), flat[None], (me, 0, 0))
            small_gather = _push_start([flat], [zone], behind, False, "small_grads_start")
            behind = small_gather[2][0]
        bg["ffn1_w_gate"] = _mm_tn(dg, s["h0"], BF16, "ffn_wgrad_in")
        bg["ffn1_w_up"] = _mm_tn(du, s["h0"], BF16, "ffn_wgrad_in")
        bg["ffn1_w_down"] = _mm_tn(act, dyb, BF16, "ffn_wgrad_down")
        behind = exchange(l, 2, bg, behind)
    grad_x = dh.reshape(n_seq, t_len, d)[:, pad + N_META :, :]

    state = {name: [turned(a, name) for a in (w[name], given["m_" + name], given["v_" + name])] for name, _ in BIG}
    outs = {name: [lax.empty(state[name][0].shape, F32) for _ in range(4)] for name, _ in BIG}

    def reduce_group(l, k, after):
        send_sems, recv_sems, srcs, zones = exchanges[(l, k)]
        owns, parts = _push_wait(send_sems, recv_sems, srcs, zones, after, True, f"exchange_wait_{l}_{k}")
        for name, own, p in zip(EXCHANGE_GROUPS[k], owns, parts):
            outs[name] = _reduce_update(p, own, *state[name], outs[name], l, "reduce_update")
        return outs[EXCHANGE_GROUPS[k][-1]][0]

    grads = {}
    order = [(l, k) for l in reversed(range(depth)) for k in range(len(EXCHANGE_GROUPS))]
    after = behind
    for l, k in order[:-1]:
        after = reduce_group(l, k, after)
    _, (flat_all,) = _push_wait(*small_gather, after, False, "small_grads_wait")
    total = _sum8(flat_all, "sum_small_grads").reshape(-1)
    offs = [0]
    for sz in sizes:
        offs.append(offs[-1] + sz)
    taken = [total[offs[k] : offs[k + 1]] for k in range(len(sizes))]
    for j, name in enumerate(SMALL):
        full_shape = (depth,) + ((CONV_WIDTH, REC_WIDTH) if name == "conv_w" else w[name].shape[1:])
        grads[name] = total[offs[j * depth] : offs[(j + 1) * depth]].reshape(full_shape)
    grads["conv_w"] = lax.dynamic_slice_in_dim(grads["conv_w"], me * conv_shard, conv_shard, axis=2)
    meta_shard = meta_tokens.shape[1]
    grads["meta_tokens"] = lax.dynamic_slice_in_dim(taken[-2].reshape(N_META, d), me * meta_shard, meta_shard, axis=1)
    loss = taken[-1][0]
    deltas, new_m, new_v = {}, {}, {}
    for name in WEIGHTS:
        if name not in dict(BIG):
            deltas[name], new_m[name], new_v[name] = _adamw(w[name], grads[name], given["m_" + name], given["v_" + name], "adamw")

    reduce_group(*order[-1], new_v[WEIGHTS[-1]])
    for name, _ in BIG:
        grads[name], deltas[name], new_m[name], new_v[name] = [turned(a, name) for a in outs[name]]
    return (loss, grad_x, *[grads[k] for k in WEIGHTS], *[deltas[k] for k in WEIGHTS], *[new_m[k] for k in WEIGHTS], *[new_v[k] for k in WEIGHTS])
```

```python
import functools

import jax
import jax.numpy as jnp
from jax import lax
from jax.experimental import pallas as pl
from jax.experimental.pallas import tpu as pltpu

F32 = jnp.float32
BF16 = jnp.bfloat16
MESH = pl.DeviceIdType.MESH

N_DEV = 8
N_META = 16
BLOCK = 128
WINDOW = 128
REC_WIDTH = 512
ATTN_WIDTH = 512
KV_WIDTH = 128
HEAD_DIM = 64
N_Q_HEADS = 8
ROPE_DIM = 16
ROPE_THETA = 500000.0
CONV_WIDTH = 4
LRU_C = 8.0
LN_EPS = 1e-5
RMS_EPS = 1e-6
NEG_INF = -1e30
ADAM_LR, ADAM_B1, ADAM_B2, ADAM_EPS, ADAM_WD, ADAM_STEP = 0.001, 0.9, 0.999, 1e-08, 0.01, 10

NT = (((1,), (1,)), ((), ()))
TN = (((0,), (0,)), ((), ()))
VMEM_BIG = 56 * 1024 * 1024


def _params(vmem=None):
    return pltpu.CompilerParams(vmem_limit_bytes=vmem)


def _row_tile(rows):
    return 256 if rows % 256 == 0 else 128


def _wide_row_tile(rows):
    return 544 if rows % 544 == 0 else _row_tile(rows)


def _sigmoid(x):
    return 1.0 / (1.0 + jnp.exp(-x))


def _expm1_nonpos(x):
    series = x * (1.0 + 0.5 * x * (1.0 + x / 3.0 * (1.0 + 0.25 * x * (1.0 + 0.2 * x))))
    return jnp.where(x > -0.05, series, jnp.exp(x) - 1.0)


def _softplus(x):
    e = jnp.exp(-jnp.abs(x))
    log1p = jnp.where(e < 1e-3, e * (1.0 - e * (0.5 - e / 3.0)), jnp.log(1.0 + e))
    return jnp.maximum(x, 0.0) + log1p


def _gelu_and_grad(x):
    c0 = 0.7978845608028654
    x2 = x * x
    t = jnp.tanh(c0 * (x + 0.044715 * x * x2))
    gelu = 0.5 * x * (1.0 + t)
    grad = 0.5 * (1.0 + t) + 0.5 * x * (1.0 - t * t) * c0 * (1.0 + 3.0 * 0.044715 * x2)
    return gelu, grad


def _ln_stats(z):
    mu = jnp.mean(z, axis=-1, keepdims=True)
    zc = z - mu
    var = jnp.mean(zc * zc, axis=-1, keepdims=True)
    rstd = lax.rsqrt(var + LN_EPS)
    return zc * rstd, rstd


def _ln_bwd(dy, z, gamma):
    xhat, rstd = _ln_stats(z)
    dxh = dy * gamma
    m1 = jnp.mean(dxh, axis=-1, keepdims=True)
    m2 = jnp.mean(dxh * xhat, axis=-1, keepdims=True)
    dz = rstd * (dxh - m1 - xhat * m2)
    return dz, jnp.sum(dy * xhat, axis=0, keepdims=True), jnp.sum(dy, axis=0, keepdims=True)


def _rms(y, gamma):
    r = lax.rsqrt(jnp.mean(y * y, axis=-1, keepdims=True) + RMS_EPS)
    return y * r * gamma


def _rms_bwd(dout, y, gamma):
    r = lax.rsqrt(jnp.mean(y * y, axis=-1, keepdims=True) + RMS_EPS)
    n = y * r
    dn = dout * gamma
    dy = r * (dn - n * jnp.mean(dn * n, axis=-1, keepdims=True))
    return dy, jnp.sum(dout * n, axis=0, keepdims=True), n * gamma


def _load_blocks(w_hbm, items, sems):
    @pl.when(pl.program_id(0) == 0)
    def _():
        copies = []
        for k, (off, nr, dst) in enumerate(items):
            for dev in range(N_DEV):
                copies.append(pltpu.make_async_copy(w_hbm.at[dev, pl.ds(off, nr), :], dst.at[pl.ds(dev * nr, nr), :], sems.at[k, dev]))
        for cp in copies:
            cp.start()
        for cp in copies:
            cp.wait()


def _full(shape):
    return pl.BlockSpec(shape, lambda *_: (0,) * len(shape))


ANY = pl.BlockSpec(memory_space=pl.ANY)


def _all_gather(xs, after, name):
    n = len(xs)

    def body(*refs):
        ins, outs = refs[:n], refs[n + 1 : 2 * n + 1]
        send_sems, recv_sems, local_sems = refs[2 * n + 1 :]
        x, y, c = lax.axis_index("x"), lax.axis_index("y"), lax.axis_index("c")
        me, sibling = (x, y, c), (x, y, 1 - c)
        chips = [(1 - x, y), (x, 1 - y), (1 - x, 1 - y)]

        def slot(i, dev):
            return outs[i].at[4 * dev[0] + 2 * dev[1] + dev[2]]

        def copy(i, k, block, to, src=None):
            return pltpu.make_async_remote_copy(
                src_ref=slot(i, block) if src is None else src,
                dst_ref=slot(i, block),
                send_sem=send_sems.at[i, k],
                recv_sem=recv_sems.at[i, k],
                device_id=to,
                device_id_type=MESH,
            )

        mine = [pltpu.make_async_copy(ins[i], slot(i, me), local_sems.at[i]) for i in range(n)]
        for cp in mine:
            cp.start()
        first = []
        for i in range(n):
            first.append(copy(i, 0, me, sibling, src=ins[i]))
            first += [copy(i, 1 + j, me, (*chip, c), src=ins[i]) for j, chip in enumerate(chips)]
        for cp in first:
            cp.start()
        passed = []
        for j, chip in enumerate(chips):
            for i in range(n):
                copy(i, 1 + j, (*chip, c), me).wait_recv()
                fwd = copy(i, 4 + j, (*chip, c), sibling)
                fwd.start()
                passed.append(fwd)
        for i in range(n):
            copy(i, 0, sibling, me).wait_recv()
            for j, chip in enumerate(chips):
                copy(i, 4 + j, (*chip, 1 - c), me).wait_recv()
        for cp in first + passed:
            cp.wait_send()
        for cp in mine:
            cp.wait()

    return pl.pallas_call(
        body,
        name=name,
        out_shape=[jax.ShapeDtypeStruct((N_DEV,) + a.shape, a.dtype) for a in xs],
        in_specs=[ANY] * (n + 1),
        out_specs=[ANY] * n,
        scratch_shapes=[pltpu.SemaphoreType.DMA((n, 7)), pltpu.SemaphoreType.DMA((n, 7)), pltpu.SemaphoreType.DMA((n,))],
    )(*xs, after)


HBM = pl.BlockSpec(memory_space=pltpu.HBM)
SEM = pl.BlockSpec(memory_space=pltpu.SEMAPHORE)
EFFECT = pltpu.SideEffectType.DATAFLOW_SIDE_EFFECTING


def _push_copies(ins, lands, send_sems, recv_sems, scatter):
    x, y, c = lax.axis_index("x"), lax.axis_index("y"), lax.axis_index("c")
    me = 4 * x + 2 * y + c
    copies = []
    for k in range(1, N_DEV):
        px = 1 - x if (k >> 2) & 1 else x
        py = 1 - y if (k >> 1) & 1 else y
        pc = 1 - c if k & 1 else c
        for i in range(len(ins)):
            copies.append(
                pltpu.make_async_remote_copy(
                    src_ref=ins[i].at[4 * px + 2 * py + pc] if scatter else ins[i],
                    dst_ref=lands[i].at[me],
                    send_sem=send_sems.at[i * (N_DEV - 1) + k - 1],
                    recv_sem=recv_sems.at[i * (N_DEV - 1) + k - 1],
                    device_id=(px, py, pc),
                    device_id_type=MESH,
                )
            )
    return copies


def _push_start(xs, lands, chain, scatter, name):
    n = len(xs)

    def body(*refs):
        ins, zones = refs[:n], refs[n : 2 * n]
        for cp in _push_copies(ins, zones, refs[2 * n + 1], refs[2 * n + 2], scatter):
            cp.start()

    outs = pl.pallas_call(
        body,
        name=name,
        out_shape=(
            pltpu.SemaphoreType.DMA((n * (N_DEV - 1),)),
            pltpu.SemaphoreType.DMA((n * (N_DEV - 1),)),
            *[pltpu.HBM(a.shape, a.dtype) for a in list(xs) + list(lands)],
        ),
        in_specs=[HBM] * (2 * n) + [ANY],
        out_specs=(SEM, SEM, *[HBM] * (2 * n)),
        input_output_aliases={i: 2 + i for i in range(2 * n)},
        compiler_params=pltpu.CompilerParams(has_side_effects=EFFECT),
    )(*[pltpu.with_memory_space_constraint(a, pltpu.HBM) for a in list(xs) + list(lands)], chain)
    return outs[0], outs[1], list(outs[2 : 2 + n]), list(outs[2 + n : 2 + 2 * n])


def _push_wait(send_sems, recv_sems, srcs, lands, after, scatter, name):
    n = len(srcs)

    def body(*refs):
        ins, zones = refs[:n], refs[n : 2 * n]
        for cp in _push_copies(ins, zones, refs[2 * n], refs[2 * n + 1], scatter):
            cp.wait_send()
            cp.wait_recv()

    outs = pl.pallas_call(
        body,
        name=name,
        out_shape=[pltpu.HBM(a.shape, a.dtype) for a in srcs + lands],
        in_specs=[HBM] * (2 * n) + [SEM, SEM, ANY],
        out_specs=[HBM] * (2 * n),
        input_output_aliases={i: i for i in range(2 * n)},
        compiler_params=pltpu.CompilerParams(has_side_effects=EFFECT),
    )(*srcs, *lands, send_sems, recv_sems, after)
    return list(outs[:n]), list(outs[n:])


def _sum8(parts, name):
    _, rows, cols = parts.shape
    tr = rows
    for cand in (512, 256, 128, 64, 32, 16):
        if rows % cand == 0 and rows > cand:
            tr = cand
            break

    def body(p_ref, o_ref):
        acc = p_ref[0].astype(F32)
        for d in range(1, N_DEV):
            acc = acc + p_ref[d].astype(F32)
        o_ref[...] = acc

    return pl.pallas_call(
        body,
        name=name,
        grid=(rows // tr,),
        in_specs=[pl.BlockSpec((N_DEV, tr, cols), lambda i: (0, i, 0))],
        out_specs=pl.BlockSpec((tr, cols), lambda i: (i, 0)),
        out_shape=jax.ShapeDtypeStruct((rows, cols), F32),
    )(parts)


def _mm_tn(a, b, out_dtype, name):
    rows, m = a.shape
    n = b.shape[1]
    tm = m // 2 if (m // 2) % 128 == 0 and m > 512 else m
    tr = next(t for t in (2176, 1024, 512, 256, 128) if rows % t == 0)
    nk = rows // tr

    def body(a_ref, b_ref, o_ref, acc):
        k = pl.program_id(1)
        part = lax.dot_general(a_ref[...].astype(BF16), b_ref[...].astype(BF16), TN, preferred_element_type=F32)

        @pl.when(k == 0)
        def _():
            acc[...] = part

        @pl.when(k > 0)
        def _():
            acc[...] += part

        @pl.when(k == nk - 1)
        def _():
            o_ref[...] = acc[...].astype(o_ref.dtype)

    return pl.pallas_call(
        body,
        name=name,
        grid=(m // tm, nk),
        in_specs=[pl.BlockSpec((tr, tm), lambda i, k: (k, i)), pl.BlockSpec((tr, n), lambda i, k: (k, 0))],
        out_specs=pl.BlockSpec((tm, n), lambda i, k: (i, 0)),
        out_shape=jax.ShapeDtypeStruct((m, n), out_dtype),
        scratch_shapes=[pltpu.VMEM((tm, n), F32)],
        compiler_params=_params(VMEM_BIG),
    )(a, b)


def _weight_scratch(locs, d):
    return [pltpu.VMEM((N_DEV * nr, d), BF16) for _, nr in locs] + [pltpu.SemaphoreType.DMA((len(locs), N_DEV))]


def _proj_fwd(h, wbuf, loc, name):
    rows, d = h.shape
    n = N_DEV * loc[1]
    tm = _wide_row_tile(rows)

    def body(h_ref, w_hbm, o_ref, w, sems):
        _load_blocks(w_hbm, [(*loc, w)], sems)
        o_ref[...] = lax.dot_general(h_ref[...].astype(BF16), w[...], NT, preferred_element_type=F32)

    return pl.pallas_call(
        body,
        name=name,
        grid=(rows // tm,),
        in_specs=[pl.BlockSpec((tm, d), lambda i: (i, 0)), ANY],
        out_specs=pl.BlockSpec((tm, n), lambda i: (i, 0)),
        out_shape=jax.ShapeDtypeStruct((rows, n), F32),
        scratch_shapes=_weight_scratch([loc], d),
        compiler_params=_params(VMEM_BIG),
    )(h, wbuf)


def _proj_bwd(dproj, wbuf, loc, dz, alpha, name):
    rows, n = dproj.shape
    d = dz.shape[1]
    tm = _wide_row_tile(rows)

    def body(dp_ref, w_hbm, dz_ref, o_ref, w, sems):
        _load_blocks(w_hbm, [(*loc, w)], sems)
        o_ref[...] = alpha * dz_ref[...] + jnp.dot(dp_ref[...], w[...], preferred_element_type=F32)

    return pl.pallas_call(
        body,
        name=name,
        grid=(rows // tm,),
        in_specs=[pl.BlockSpec((tm, n), lambda i: (i, 0)), ANY, pl.BlockSpec((tm, d), lambda i: (i, 0))],
        out_specs=pl.BlockSpec((tm, d), lambda i: (i, 0)),
        out_shape=jax.ShapeDtypeStruct((rows, d), F32),
        scratch_shapes=_weight_scratch([loc], d),
        compiler_params=_params(VMEM_BIG),
    )(dproj, wbuf, dz)


FFN_CHUNKS = 1


def _ffn_fwd(h, wbuf, locs, ln_g, ln_b, alpha, name):
    rows, d = h.shape
    f = N_DEV * locs[0][1]
    tm = _row_tile(rows)
    tf = f // FFN_CHUNKS

    def body(h_ref, w_hbm, lg_ref, lb_ref, ho_ref, g_ref, u_ref, z_ref, hb_ref, wg, wu, wdv, sems):
        _load_blocks(w_hbm, [(*locs[0], wg), (*locs[1], wu), (*locs[2], wdv)], sems)
        x = h_ref[...]
        xb = x.astype(BF16)
        y = None
        for c in range(FFN_CHUNKS):
            sl = slice(c * tf, (c + 1) * tf)
            g = lax.dot_general(xb, wg[sl, :], NT, preferred_element_type=F32)
            u = lax.dot_general(xb, wu[sl, :], NT, preferred_element_type=F32)
            g_ref[:, sl] = g
            u_ref[:, sl] = u
            a = (g * _sigmoid(g) * u).astype(BF16)
            part = jnp.dot(a, wdv[sl, :], preferred_element_type=F32)
            y = part if y is None else y + part
        z = alpha * x + 0.5 * y
        z_ref[...] = z
        xhat, _ = _ln_stats(z)
        ho = xhat * lg_ref[...] + lb_ref[...]
        ho_ref[...] = ho
        hb_ref[...] = ho.astype(BF16)

    row_d = pl.BlockSpec((tm, d), lambda i: (i, 0))
    row_f = pl.BlockSpec((tm, f), lambda i: (i, 0))
    return pl.pallas_call(
        body,
        name=name,
        grid=(rows // tm,),
        in_specs=[row_d, ANY, _full((1, d)), _full((1, d))],
        out_specs=[row_d, row_f, row_f, row_d, row_d],
        out_shape=[
            jax.ShapeDtypeStruct((rows, d), F32),
            jax.ShapeDtypeStruct((rows, f), F32),
            jax.ShapeDtypeStruct((rows, f), F32),
            jax.ShapeDtypeStruct((rows, d), F32),
            jax.ShapeDtypeStruct((rows, d), BF16),
        ],
        scratch_shapes=_weight_scratch(locs, d),
        compiler_params=_params(VMEM_BIG),
    )(h, wbuf, ln_g, ln_b)


def _ffn_bwd(dh, z, g, u, wbuf, locs, ln_g, after, alpha, name):
    rows, d = dh.shape
    f = N_DEV * locs[0][1]
    tm = _row_tile(rows)
    tf = f // FFN_CHUNKS

    def body(dh_ref, z_ref, g_ref, u_ref, w_hbm, lg_ref, after_ref, dx_ref, dg_ref, du_ref, a_ref, dyb_ref, dgam_ref, dbet_ref, wg, wu, wdv, sems):
        _load_blocks(w_hbm, [(*locs[0], wg), (*locs[1], wu), (*locs[2], wdv)], sems)

        @pl.when(pl.program_id(0) == 0)
        def _():
            dgam_ref[...] = jnp.zeros_like(dgam_ref)
            dbet_ref[...] = jnp.zeros_like(dbet_ref)

        dz, dgam, dbet = _ln_bwd(dh_ref[...], z_ref[...], lg_ref[...])
        dgam_ref[...] += dgam
        dbet_ref[...] += dbet
        dyb = (0.5 * dz).astype(BF16)
        dyb_ref[...] = dyb
        dx = alpha * dz
        for c in range(FFN_CHUNKS):
            sl = slice(c * tf, (c + 1) * tf)
            da = lax.dot_general(dyb, wdv[sl, :], NT, preferred_element_type=F32)
            gc = g_ref[:, sl]
            uc = u_ref[:, sl]
            sg = _sigmoid(gc)
            silu = gc * sg
            dgc = (da * uc * (sg * (1.0 + gc * (1.0 - sg)))).astype(BF16)
            duc = (da * silu).astype(BF16)
            a_ref[:, sl] = (silu * uc).astype(BF16)
            dg_ref[:, sl] = dgc
            du_ref[:, sl] = duc
            dx = dx + jnp.dot(dgc, wg[sl, :], preferred_element_type=F32) + jnp.dot(duc, wu[sl, :], preferred_element_type=F32)
        dx_ref[...] = dx

    row_d = pl.BlockSpec((tm, d), lambda i: (i, 0))
    row_f = pl.BlockSpec((tm, f), lambda i: (i, 0))
    return pl.pallas_call(
        body,
        name=name,
        grid=(rows // tm,),
        in_specs=[row_d, row_d, row_f, row_f, ANY, _full((1, d)), ANY],
        out_specs=[row_d, row_f, row_f, row_f, row_d, _full((1, d)), _full((1, d))],
        out_shape=[
            jax.ShapeDtypeStruct((rows, d), F32),
            jax.ShapeDtypeStruct((rows, f), BF16),
            jax.ShapeDtypeStruct((rows, f), BF16),
            jax.ShapeDtypeStruct((rows, f), BF16),
            jax.ShapeDtypeStruct((rows, d), BF16),
            jax.ShapeDtypeStruct((1, d), F32),
            jax.ShapeDtypeStruct((1, d), F32),
        ],
        scratch_shapes=_weight_scratch(locs, d),
        compiler_params=_params(VMEM_BIG),
    )(dh, z, g, u, wbuf, ln_g, after)


LANES = 128


def _gates(xc, wa_ref, ba, wx_ref, bx, lam):
    xcb = xc.astype(BF16)
    r = _sigmoid(jnp.dot(xcb, wa_ref[0], preferred_element_type=F32) + ba)
    i = _sigmoid(jnp.dot(xcb, wx_ref[0], preferred_element_type=F32) + bx)
    sp = _softplus(-lam)
    log_a = -LRU_C * sp * r
    a = jnp.exp(log_a)
    s = jnp.sqrt(-_expm1_nonpos(2.0 * log_a))
    return xcb, r, i, sp, a, s


def _conv(xr, cw, cb):
    return cb + cw[3:4] * xr + cw[2:3] * pltpu.roll(xr, 1, 0) + cw[1:2] * pltpu.roll(xr, 2, 0) + cw[0:1] * pltpu.roll(xr, 3, 0)


def _rglru_fwd(proj, cw, cb, wa, ba, wx, bx, lam, t_len, pad, name):
    rows = proj.shape[0]
    n_seq = rows // t_len
    n_tile = REC_WIDTH // LANES
    groups = t_len // 8

    def body(xr_ref, gt_ref, cw_ref, cb_ref, wa_ref, ba_ref, wx_ref, bx_ref, lam_ref, xc_ref, h_ref, y_ref, a_s, b_s):
        valid = lax.broadcasted_iota(jnp.int32, (t_len, 1), 0) >= pad
        xr = jnp.where(valid, xr_ref[...], 0.0)
        xc = _conv(xr, cw_ref[...], cb_ref[...])
        xc_ref[...] = xc
        _, _, i, _, a, s = _gates(xc, wa_ref, ba_ref[...], wx_ref, bx_ref[...], lam_ref[...])
        a_s[...] = a
        b_s[...] = jnp.where(valid, s * (i * xc), 0.0)
        sub = lax.broadcasted_iota(jnp.int32, (8, LANES), 0)

        def step(gi, carry):
            r0 = pl.multiple_of(gi * 8, 8)
            av = a_s[pl.ds(r0, 8), :]
            bv = b_s[pl.ds(r0, 8), :]
            for d in (1, 2, 4):
                m = sub >= d
                a_sh = pltpu.roll(av, d, 0)
                b_sh = pltpu.roll(bv, d, 0)
                bv = jnp.where(m, av * b_sh + bv, bv)
                av = jnp.where(m, av * a_sh, av)
            hv = av * carry + bv
            h_ref[pl.ds(r0, 8), :] = hv
            return jnp.broadcast_to(hv[7:8, :], (8, LANES))

        lax.fori_loop(0, groups, step, jnp.zeros((8, LANES), F32), unroll=4)
        gelu, _ = _gelu_and_grad(gt_ref[...])
        y_ref[...] = h_ref[...] * gelu

    seq_tile = lambda col0: pl.BlockSpec((t_len, LANES), lambda s, j: (s, col0 + j))
    vec = pl.BlockSpec((1, LANES), lambda s, j: (0, j))
    out = jax.ShapeDtypeStruct((rows, REC_WIDTH), F32)
    return pl.pallas_call(
        body,
        name=name,
        grid=(n_seq, n_tile),
        in_specs=[
            seq_tile(0),
            seq_tile(n_tile),
            pl.BlockSpec((CONV_WIDTH, LANES), lambda s, j: (0, j)),
            vec,
            pl.BlockSpec((1, LANES, LANES), lambda s, j: (j, 0, 0)),
            vec,
            pl.BlockSpec((1, LANES, LANES), lambda s, j: (j, 0, 0)),
            vec,
            vec,
        ],
        out_specs=[seq_tile(0)] * 3,
        out_shape=[out, out, out],
        scratch_shapes=[pltpu.VMEM((t_len, LANES), F32), pltpu.VMEM((t_len, LANES), F32)],
        compiler_params=_params(VMEM_BIG),
    )(proj, proj, cw, cb, wa, ba, wx, bx, lam)


def _rglru_bwd(dy, h, proj, xc, cw, wa, ba, wx, bx, lam, t_len, pad, name):
    rows = proj.shape[0]
    n_seq = rows // t_len
    n_tile = REC_WIDTH // LANES
    groups = t_len // 8

    def body(dy_ref, h_ref, xr_ref, gt_ref, xc_ref, cw_ref, wa_ref, ba_ref, wx_ref, bx_ref, lam_ref,
             dxr_ref, dgt_ref, dcw_ref, dcb_ref, dwa_ref, dba_ref, dwx_ref, dbx_ref, dlam_ref, a_s, c_s, db_s, da_s):
        first_seq = pl.program_id(1) == 0

        @pl.when(first_seq)
        def _():
            for ref in (dcw_ref, dcb_ref, dwa_ref, dba_ref, dwx_ref, dbx_ref, dlam_ref):
                ref[...] = jnp.zeros_like(ref)

        valid = lax.broadcasted_iota(jnp.int32, (t_len, 1), 0) >= pad
        gelu, dgelu = _gelu_and_grad(gt_ref[...])
        dyv = dy_ref[...]
        dho = dyv * gelu
        dgt_ref[...] = (dyv * h_ref[...] * dgelu).astype(BF16)
        xc = xc_ref[...]
        lam = lam_ref[...]
        xcb, r, i, sp, a, s = _gates(xc, wa_ref, ba_ref[...], wx_ref, bx_ref[...], lam)
        a_s[...] = a
        c_s[...] = a * dho
        db_s[...] = dho
        sub = lax.broadcasted_iota(jnp.int32, (8, LANES), 0)

        def step(k, carry):
            gi = groups - 1 - k
            r0 = pl.multiple_of(gi * 8, 8)
            av = a_s[pl.ds(r0, 8), :]
            cv = c_s[pl.ds(r0, 8), :]
            for d in (1, 2, 4):
                m = sub < 8 - d
                a_sh = pltpu.roll(av, 8 - d, 0)
                c_sh = pltpu.roll(cv, 8 - d, 0)
                cv = jnp.where(m, av * c_sh + cv, cv)
                av = jnp.where(m, av * a_sh, av)
            ev = av * carry + cv
            e_next = jnp.where(sub < 7, pltpu.roll(ev, 7, 0), carry)
            dht = db_s[pl.ds(r0, 8), :] + e_next
            hv = h_ref[pl.ds(r0, 8), :]
            rp = pl.multiple_of(jnp.maximum(gi - 1, 0) * 8, 8)
            h_before = jnp.where(gi > 0, jnp.broadcast_to(h_ref[pl.ds(rp, 8), :][7:8, :], (8, LANES)), 0.0)
            h_prev = jnp.where(sub >= 1, pltpu.roll(hv, 1, 0), h_before)
            db_s[pl.ds(r0, 8), :] = dht
            da_s[pl.ds(r0, 8), :] = dht * h_prev
            return jnp.broadcast_to(ev[0:1, :], (8, LANES))

        lax.fori_loop(0, groups, step, jnp.zeros((8, LANES), F32), unroll=4)

        db = jnp.where(valid, db_s[...], 0.0)
        da = da_s[...]
        ds = db * (i * xc)
        di = db * s * xc
        dxc = db * s * i
        dlog_a = da * a - ds * (a * a) / jnp.maximum(s, 1e-30)
        dr = dlog_a * (-LRU_C * sp)
        dsp = jnp.sum(dlog_a * (-LRU_C) * r, axis=0, keepdims=True)
        dlam_ref[...] += dsp * (-_sigmoid(-lam))
        dpr = dr * r * (1.0 - r)
        dpi = di * i * (1.0 - i)
        dprb = dpr.astype(BF16)
        dpib = dpi.astype(BF16)
        dxc = dxc + lax.dot_general(dprb, wa_ref[0], NT, preferred_element_type=F32) + lax.dot_general(dpib, wx_ref[0], NT, preferred_element_type=F32)
        dwa_ref[0] += lax.dot_general(xcb, dprb, TN, preferred_element_type=F32)
        dwx_ref[0] += lax.dot_general(xcb, dpib, TN, preferred_element_type=F32)
        dba_ref[...] += jnp.sum(dpr, axis=0, keepdims=True)
        dbx_ref[...] += jnp.sum(dpi, axis=0, keepdims=True)
        dxc = jnp.where(valid, dxc, 0.0)
        xr = jnp.where(valid, xr_ref[...], 0.0)
        dcb_ref[...] += jnp.sum(dxc, axis=0, keepdims=True)
        for k in range(CONV_WIDTH):
            shifted = xr if k == CONV_WIDTH - 1 else pltpu.roll(xr, CONV_WIDTH - 1 - k, 0)
            dcw_ref[k : k + 1, :] += jnp.sum(dxc * shifted, axis=0, keepdims=True)
        cw = cw_ref[...]
        dxr = cw[3:4] * dxc + cw[2:3] * pltpu.roll(dxc, t_len - 1, 0) + cw[1:2] * pltpu.roll(dxc, t_len - 2, 0) + cw[0:1] * pltpu.roll(dxc, t_len - 3, 0)
        dxr_ref[...] = jnp.where(valid, dxr, 0.0).astype(BF16)

    seq_tile = lambda col0: pl.BlockSpec((t_len, LANES), lambda j, s: (s, col0 + j))
    vec = pl.BlockSpec((1, LANES), lambda j, s: (0, j))
    mat = pl.BlockSpec((1, LANES, LANES), lambda j, s: (j, 0, 0))
    cwb = pl.BlockSpec((CONV_WIDTH, LANES), lambda j, s: (0, j))
    big = jax.ShapeDtypeStruct((rows, REC_WIDTH), BF16)
    vec_shape = jax.ShapeDtypeStruct((1, REC_WIDTH), F32)
    mat_shape = jax.ShapeDtypeStruct((n_tile, LANES, LANES), F32)
    return pl.pallas_call(
        body,
        name=name,
        grid=(n_tile, n_seq),
        in_specs=[seq_tile(0), seq_tile(0), seq_tile(0), seq_tile(n_tile), seq_tile(0), cwb, mat, vec, mat, vec, vec],
        out_specs=[seq_tile(0), seq_tile(0), cwb, vec, mat, vec, mat, vec, vec],
        out_shape=[big, big, jax.ShapeDtypeStruct((CONV_WIDTH, REC_WIDTH), F32), vec_shape, mat_shape, vec_shape, mat_shape, vec_shape, vec_shape],
        scratch_shapes=[pltpu.VMEM((t_len, LANES), F32)] * 4,
        compiler_params=_params(VMEM_BIG),
    )(dy, h, proj, proj, xc, cw, wa, ba, wx, bx, lam)


QKV_WIDTH = ATTN_WIDTH + 2 * KV_WIDTH


def _rotate(v, cos, sin_up, sin_down):
    width = v.shape[1]
    return v * cos + pltpu.roll(v, width - ROPE_DIM // 2, 1) * sin_up + pltpu.roll(v, ROPE_DIM // 2, 1) * sin_down


def _rope_rows(t_len):
    return t_len // 4 if t_len % 64 == 0 else BLOCK


def _rope_fwd(proj, cos, sin_up, sin_down, t_len, name):
    rows = proj.shape[0]
    rb = _rope_rows(t_len)
    nb = t_len // rb
    n_seq = rows // t_len
    q0 = 2 * REC_WIDTH

    def body(q_ref, k_ref, v_ref, cos_ref, up_ref, down_ref, o_ref):
        cos_t, up_t, down_t = cos_ref[...], up_ref[...], down_ref[...]
        o_ref[:, 0:ATTN_WIDTH] = _rotate(q_ref[...], cos_t, up_t, down_t).astype(BF16)
        o_ref[:, ATTN_WIDTH : ATTN_WIDTH + KV_WIDTH] = _rotate(k_ref[...], cos_t[:, :KV_WIDTH], up_t[:, :KV_WIDTH], down_t[:, :KV_WIDTH]).astype(BF16)
        o_ref[:, ATTN_WIDTH + KV_WIDTH :] = v_ref[...].astype(BF16)

    tab = pl.BlockSpec((rb, ATTN_WIDTH), lambda s, n: (n, 0))
    return pl.pallas_call(
        body,
        name=name,
        grid=(n_seq, nb),
        in_specs=[
            pl.BlockSpec((rb, ATTN_WIDTH), lambda s, n: (s * nb + n, q0 // ATTN_WIDTH)),
            pl.BlockSpec((rb, KV_WIDTH), lambda s, n: (s * nb + n, (q0 + ATTN_WIDTH) // KV_WIDTH)),
            pl.BlockSpec((rb, KV_WIDTH), lambda s, n: (s * nb + n, (q0 + ATTN_WIDTH) // KV_WIDTH + 1)),
            tab,
            tab,
            tab,
        ],
        out_specs=pl.BlockSpec((rb, QKV_WIDTH), lambda s, n: (s * nb + n, 0)),
        out_shape=jax.ShapeDtypeStruct((rows, QKV_WIDTH), BF16),
    )(proj, proj, proj, cos, sin_up, sin_down)


def _rope_bwd(dq, dk, dv, cos, sin_up, sin_down, t_len, name):
    rows = dq.shape[0]
    rb = _rope_rows(t_len)
    nb = t_len // rb
    n_seq = rows // t_len

    def body(dq_ref, dk_ref, dv_ref, cos_ref, up_ref, down_ref, o_ref):
        cos_t, up_t, down_t = cos_ref[...], -up_ref[...], -down_ref[...]
        o_ref[:, 0:ATTN_WIDTH] = _rotate(dq_ref[...], cos_t, up_t, down_t).astype(BF16)
        o_ref[:, ATTN_WIDTH : ATTN_WIDTH + KV_WIDTH] = _rotate(dk_ref[...], cos_t[:, :KV_WIDTH], up_t[:, :KV_WIDTH], down_t[:, :KV_WIDTH]).astype(BF16)
        o_ref[:, ATTN_WIDTH + KV_WIDTH :] = dv_ref[...].astype(BF16)

    tab = pl.BlockSpec((rb, ATTN_WIDTH), lambda s, n: (n, 0))
    blk = lambda w: pl.BlockSpec((rb, w), lambda s, n: (s * nb + n, 0))
    return pl.pallas_call(
        body,
        name=name,
        grid=(n_seq, nb),
        in_specs=[blk(ATTN_WIDTH), blk(KV_WIDTH), blk(KV_WIDTH), tab, tab, tab],
        out_specs=blk(QKV_WIDTH),
        out_shape=jax.ShapeDtypeStruct((rows, QKV_WIDTH), BF16),
    )(dq, dk, dv, cos, sin_up, sin_down)


GROUP = 4


def _attn_bias(pad):
    out = []
    for n in range(3):
        q_pos = n * BLOCK + jnp.arange(BLOCK)[:, None] - pad
        col = jnp.arange(3 * BLOCK)[None, :]
        first = col < BLOCK
        k_pos = col - pad + jnp.where(first, 0, (n - 2) * BLOCK)
        dist = q_pos - k_pos
        band = (dist >= 0) & (dist < WINDOW) & (k_pos >= N_META) & jnp.logical_not(first)
        meta = (k_pos >= 0) & (k_pos < N_META) & (k_pos <= q_pos) & first
        out.append(jnp.tile(jnp.where(band | meta, 0.0, NEG_INF).astype(F32), (GROUP, 1)))
    return jnp.stack(out)


def _bias_spec():
    return pl.BlockSpec((1, GROUP * BLOCK, 3 * BLOCK), lambda s, n: (jnp.minimum(n, 2), 0, 0))


def _in_half(e):
    lane = lax.broadcasted_iota(jnp.int32, (1, BLOCK), 1)
    return lane >= HEAD_DIM if e else lane < HEAD_DIM


def _head_views(q2, e, g):
    v = jnp.where(_in_half(e), q2, 0.0)
    return pltpu.roll(v, HEAD_DIM, 1) if e != g else v


def _tile(g, t):
    j = (GROUP // 2) * g + t
    return slice(j * BLOCK, (j + 1) * BLOCK)


def _stack_rows(ref, g):
    parts = []
    for t in range(GROUP // 2):
        tile = ref[:, _tile(g, t)].astype(F32)
        parts += [_head_views(tile, e, g) for e in range(2)]
    return jnp.concatenate(parts, axis=0)


def _stack_group(q_ref, sink_ref, g):
    return _stack_rows(q_ref, g), sink_ref[g][:, 0:1]


def _unstack_pair(v, g, t):
    out = None
    for e in range(2):
        blk = v[(2 * t + e) * BLOCK : (2 * t + e + 1) * BLOCK, :]
        blk = pltpu.roll(blk, HEAD_DIM, 1) if e != g else blk
        out = blk if out is None else out + blk
    return out


def _exp_scores(scores, bias, sink):
    s = scores * (HEAD_DIM**-0.5) + bias
    m = jnp.maximum(jnp.max(s, axis=-1, keepdims=True), sink)
    return jnp.exp(s - m), jnp.exp(sink - m)


def _softmax_with_sink(scores, bias, sink):
    p, p_sink = _exp_scores(scores, bias, sink)
    inv = 1.0 / (jnp.sum(p, axis=-1, keepdims=True) + p_sink)
    return p * inv, p_sink * inv


def _kv_specs(nb):
    k_col = ATTN_WIDTH // KV_WIDTH
    specs = []
    for col in (k_col, k_col + 1):
        specs += [
            pl.BlockSpec((BLOCK, KV_WIDTH), lambda s, n, col=col: (s * nb, col)),
            pl.BlockSpec((BLOCK, KV_WIDTH), lambda s, n, col=col: (s * nb + jnp.maximum(n - 1, 0), col)),
            pl.BlockSpec((BLOCK, KV_WIDTH), lambda s, n, col=col: (s * nb + n, col)),
        ]
    return specs


def _attn_fwd(qkv, sink_rows, bias, t_len, name):
    rows = qkv.shape[0]
    nb = t_len // BLOCK
    n_seq = rows // t_len

    def body(q_ref, km_ref, kp_ref, kc_ref, vm_ref, vp_ref, vc_ref, sink_ref, bias_ref, o_ref):
        keys = jnp.concatenate([km_ref[...], kp_ref[...], kc_ref[...]], axis=0)
        vals = jnp.concatenate([vm_ref[...], vp_ref[...], vc_ref[...]], axis=0).astype(F32)
        for g in range(N_Q_HEADS // GROUP):
            qs, sink = _stack_group(q_ref, sink_ref, g)
            scores = lax.dot_general(qs.astype(BF16), keys, NT, preferred_element_type=F32)
            p, p_sink = _exp_scores(scores, bias_ref[0], sink)
            ext = jnp.where(_in_half(g), vals, 1.0).astype(BF16)
            oe = jnp.dot(p.astype(BF16), ext, preferred_element_type=F32)
            denom = pltpu.roll(oe, HEAD_DIM, 1) + p_sink
            o = jnp.where(_in_half(g), oe / denom, 0.0)
            for t in range(GROUP // 2):
                o_ref[:, _tile(g, t)] = _unstack_pair(o, g, t)

    return pl.pallas_call(
        body,
        name=name,
        grid=(n_seq, nb),
        in_specs=[pl.BlockSpec((BLOCK, ATTN_WIDTH), lambda s, n: (s * nb + n, 0))] + _kv_specs(nb) + [_full(sink_rows.shape), _bias_spec()],
        out_specs=pl.BlockSpec((BLOCK, ATTN_WIDTH), lambda s, n: (s * nb + n, 0)),
        out_shape=jax.ShapeDtypeStruct((rows, ATTN_WIDTH), F32),
    )(qkv, qkv, qkv, qkv, qkv, qkv, qkv, sink_rows, bias)


def _attn_bwd(qkv, sink_rows, bias, o, do, t_len, name):
    rows = qkv.shape[0]
    nb = t_len // BLOCK
    n_seq = rows // t_len

    def body(q_ref, km_ref, kp_ref, kc_ref, vm_ref, vp_ref, vc_ref, sink_ref, bias_ref, o_ref, do_ref, dq_ref, dk_ref, dv_ref, dsink_ref):
        s_id, n = pl.program_id(0), pl.program_id(1)

        @pl.when(n == 0)
        def _():
            dk_ref[...] = jnp.zeros_like(dk_ref)
            dv_ref[...] = jnp.zeros_like(dv_ref)

        @pl.when((n == 0) & (s_id == 0))
        def _():
            dsink_ref[...] = jnp.zeros_like(dsink_ref)

        keys = jnp.concatenate([km_ref[...], kp_ref[...], kc_ref[...]], axis=0)
        vals = jnp.concatenate([vm_ref[...], vp_ref[...], vc_ref[...]], axis=0)
        dkeys = jnp.zeros((3 * BLOCK, KV_WIDTH), F32)
        dvals = jnp.zeros((3 * BLOCK, KV_WIDTH), F32)
        for g in range(N_Q_HEADS // GROUP):
            qs, sink = _stack_group(q_ref, sink_ref, g)
            qsb = qs.astype(BF16)
            scores = lax.dot_general(qsb, keys, NT, preferred_element_type=F32)
            p, p_sink = _softmax_with_sink(scores, bias_ref[0], sink)
            dos = _stack_rows(do_ref, g)
            delta = jnp.sum(dos * _stack_rows(o_ref, g), axis=-1, keepdims=True)
            dosb = dos.astype(BF16)
            dp = lax.dot_general(dosb, vals, NT, preferred_element_type=F32)
            ds = (p * (dp - delta) * (HEAD_DIM**-0.5)).astype(BF16)
            dqs = jnp.where(_in_half(g), jnp.dot(ds, keys, preferred_element_type=F32), 0.0)
            for t in range(GROUP // 2):
                dq_ref[:, _tile(g, t)] = _unstack_pair(dqs, g, t)
            dkeys = dkeys + lax.dot_general(ds, qsb, TN, preferred_element_type=F32)
            dvals = dvals + lax.dot_general(p.astype(BF16), dosb, TN, preferred_element_type=F32)
            sink_term = p_sink * delta
            for hh in range(GROUP):
                head = GROUP * g + hh
                part = -jnp.sum(sink_term[hh * BLOCK : (hh + 1) * BLOCK, :], axis=0, keepdims=True)
                dsink_ref[head : head + 1, :] += jnp.broadcast_to(part, (1, BLOCK))
        r_prev = pl.multiple_of(jnp.maximum(n - 1, 0) * BLOCK, BLOCK)
        r_cur = pl.multiple_of(n * BLOCK, BLOCK)
        for acc, d in ((dk_ref, dkeys), (dv_ref, dvals)):
            acc[0:BLOCK, :] += d[0:BLOCK]
            acc[pl.ds(r_prev, BLOCK), :] += d[BLOCK : 2 * BLOCK]
            acc[pl.ds(r_cur, BLOCK), :] += d[2 * BLOCK :]

    q_blk = pl.BlockSpec((BLOCK, ATTN_WIDTH), lambda s, n: (s * nb + n, 0))
    seq_kv = pl.BlockSpec((t_len, KV_WIDTH), lambda s, n: (s, 0))
    return pl.pallas_call(
        body,
        name=name,
        grid=(n_seq, nb),
        in_specs=[q_blk] + _kv_specs(nb) + [_full(sink_rows.shape), _bias_spec(), q_blk, q_blk],
        out_specs=[q_blk, seq_kv, seq_kv, _full((N_Q_HEADS, BLOCK))],
        out_shape=[
            jax.ShapeDtypeStruct((rows, ATTN_WIDTH), F32),
            jax.ShapeDtypeStruct((rows, KV_WIDTH), F32),
            jax.ShapeDtypeStruct((rows, KV_WIDTH), F32),
            jax.ShapeDtypeStruct((N_Q_HEADS, BLOCK), F32),
        ],
    )(qkv, qkv, qkv, qkv, qkv, qkv, qkv, sink_rows, bias, o, do)


def _mix_out_fwd(y_rec, y_attn, h, wbuf, loc, g_rec, g_attn, ln_g, ln_b, alpha, name):
    rows, d = h.shape
    tm = _row_tile(rows)

    def body(yr_ref, ya_ref, h_ref, w_hbm, gr_ref, ga_ref, lg_ref, lb_ref, ho_ref, z_ref, hb_ref, w_ref, sems):
        _load_blocks(w_hbm, [(*loc, w_ref)], sems)
        nr = _rms(yr_ref[...], gr_ref[...]).astype(BF16)
        na = _rms(ya_ref[...], ga_ref[...]).astype(BF16)
        m = jnp.dot(nr, w_ref[0:REC_WIDTH, :], preferred_element_type=F32) + jnp.dot(na, w_ref[REC_WIDTH:, :], preferred_element_type=F32)
        z = alpha * h_ref[...] + m
        z_ref[...] = z
        xhat, _ = _ln_stats(z)
        ho = xhat * lg_ref[...] + lb_ref[...]
        ho_ref[...] = ho
        hb_ref[...] = ho.astype(BF16)

    row_d = pl.BlockSpec((tm, d), lambda i: (i, 0))
    row_h = pl.BlockSpec((tm, REC_WIDTH), lambda i: (i, 0))
    return pl.pallas_call(
        body,
        name=name,
        grid=(rows // tm,),
        in_specs=[row_h, row_h, row_d, ANY, _full((1, REC_WIDTH)), _full((1, ATTN_WIDTH)), _full((1, d)), _full((1, d))],
        out_specs=[row_d, row_d, row_d],
        out_shape=[jax.ShapeDtypeStruct((rows, d), F32)] * 2 + [jax.ShapeDtypeStruct((rows, d), BF16)],
        scratch_shapes=_weight_scratch([loc], d),
    )(y_rec, y_attn, h, wbuf, g_rec, g_attn, ln_g, ln_b)


def _mix_out_bwd(dh, z, y_rec, y_attn, wbuf, loc, g_rec, g_attn, ln_g, name):
    rows, d = dh.shape
    tm = _row_tile(rows)
    mix = REC_WIDTH + ATTN_WIDTH

    def body(dh_ref, z_ref, yr_ref, ya_ref, w_hbm, gr_ref, ga_ref, lg_ref, dz_ref, dzb_ref, dyr_ref, dya_ref, yn_ref, dgam_ref, dbet_ref, dgr_ref, dga_ref, w_ref, sems):
        _load_blocks(w_hbm, [(*loc, w_ref)], sems)

        @pl.when(pl.program_id(0) == 0)
        def _():
            for ref in (dgam_ref, dbet_ref, dgr_ref, dga_ref):
                ref[...] = jnp.zeros_like(ref)

        dz, dgam, dbet = _ln_bwd(dh_ref[...], z_ref[...], lg_ref[...])
        dgam_ref[...] += dgam
        dbet_ref[...] += dbet
        dz_ref[...] = dz
        dzb = dz.astype(BF16)
        dzb_ref[...] = dzb
        dyn = lax.dot_general(dzb, w_ref[...], NT, preferred_element_type=F32)
        dyr, dgr, nr = _rms_bwd(dyn[:, 0:REC_WIDTH], yr_ref[...], gr_ref[...])
        dya, dga, na = _rms_bwd(dyn[:, REC_WIDTH:], ya_ref[...], ga_ref[...])
        dyr_ref[...] = dyr
        dya_ref[...] = dya
        dgr_ref[...] += dgr
        dga_ref[...] += dga
        yn_ref[:, 0:REC_WIDTH] = nr.astype(BF16)
        yn_ref[:, REC_WIDTH:] = na.astype(BF16)

    row_d = pl.BlockSpec((tm, d), lambda i: (i, 0))
    row_h = pl.BlockSpec((tm, REC_WIDTH), lambda i: (i, 0))
    row_m = pl.BlockSpec((tm, mix), lambda i: (i, 0))
    return pl.pallas_call(
        body,
        name=name,
        grid=(rows // tm,),
        in_specs=[row_d, row_d, row_h, row_h, ANY, _full((1, REC_WIDTH)), _full((1, ATTN_WIDTH)), _full((1, d))],
        out_specs=[row_d, row_d, row_h, row_h, row_m, _full((1, d)), _full((1, d)), _full((1, REC_WIDTH)), _full((1, ATTN_WIDTH))],
        out_shape=[
            jax.ShapeDtypeStruct((rows, d), F32),
            jax.ShapeDtypeStruct((rows, d), BF16),
            jax.ShapeDtypeStruct((rows, REC_WIDTH), F32),
            jax.ShapeDtypeStruct((rows, ATTN_WIDTH), F32),
            jax.ShapeDtypeStruct((rows, mix), BF16),
            jax.ShapeDtypeStruct((1, d), F32),
            jax.ShapeDtypeStruct((1, d), F32),
            jax.ShapeDtypeStruct((1, REC_WIDTH), F32),
            jax.ShapeDtypeStruct((1, ATTN_WIDTH), F32),
        ],
        scratch_shapes=_weight_scratch([loc], d),
    )(dh, z, y_rec, y_attn, wbuf, g_rec, g_attn, ln_g)


def _loss_head(y, target, t_len, first_token, name):
    rows, d = y.shape
    tm = _row_tile(rows)

    def body(y_ref, t_ref, loss_ref, dy_ref):
        i = pl.program_id(0)

        @pl.when(i == 0)
        def _():
            loss_ref[...] = jnp.zeros_like(loss_ref)

        row = i * tm + lax.broadcasted_iota(jnp.int32, (tm, 1), 0)
        is_token = lax.rem(row, t_len) >= first_token
        err = jnp.where(is_token, y_ref[...] - t_ref[...], 0.0)
        dy_ref[...] = err / d
        per_row = jnp.sum(err * err, axis=-1, keepdims=True) / d
        loss_ref[...] += jnp.broadcast_to(0.5 * jnp.sum(per_row, axis=0, keepdims=True), (1, BLOCK))

    row_d = pl.BlockSpec((tm, d), lambda i: (i, 0))
    return pl.pallas_call(
        body,
        name=name,
        grid=(rows // tm,),
        in_specs=[row_d, row_d],
        out_specs=[_full((1, BLOCK)), row_d],
        out_shape=[jax.ShapeDtypeStruct((1, BLOCK), F32), jax.ShapeDtypeStruct((rows, d), F32)],
    )(y, target)


def _meta_grad(dh0, t_len, pad, name):
    rows, d = dh0.shape
    nb = t_len // BLOCK
    n_seq = rows // t_len

    def body(dh_ref, o_ref):
        @pl.when(pl.program_id(0) == 0)
        def _():
            o_ref[...] = jnp.zeros_like(o_ref)

        o_ref[...] += dh_ref[pad : pad + N_META, :]

    return pl.pallas_call(
        body,
        name=name,
        grid=(n_seq,),
        in_specs=[pl.BlockSpec((BLOCK, d), lambda s: (s * nb, 0))],
        out_specs=_full((N_META, d)),
        out_shape=jax.ShapeDtypeStruct((N_META, d), F32),
    )(dh0)


def _adamw_math(w, g, m, v):
    nm = ADAM_B1 * m + (1.0 - ADAM_B1) * g
    nv = ADAM_B2 * v + (1.0 - ADAM_B2) * (g * g)
    m_hat = nm / (1.0 - ADAM_B1**ADAM_STEP)
    v_hat = nv / (1.0 - ADAM_B2**ADAM_STEP)
    return -ADAM_LR * (m_hat / (jnp.sqrt(v_hat) + ADAM_EPS) + ADAM_WD * w), nm, nv


def _reduce_update(parts, own, w, m, v, outs, layer, name):
    _, nr, d = parts.shape
    _, a_dim, b_dim = w.shape
    assert (a_dim, b_dim) == (nr, d), (w.shape, parts.shape)

    def body(p_ref, own_hbm, w_ref, m_ref, v_ref, g_in, dl_in, nm_in, nv_in, g_ref, dl_ref, nm_ref, nv_ref, own_v, acc, sem):
        me = 4 * lax.axis_index("x") + 2 * lax.axis_index("y") + lax.axis_index("c")
        cp = pltpu.make_async_copy(own_hbm.at[me], own_v, sem)
        cp.start()
        cp.wait()
        acc[...] = own_v[...].astype(F32)
        for dev in range(N_DEV):

            @pl.when(me != dev)
            def _():
                acc[...] += p_ref[dev].astype(F32)

        g = acc[...]
        g_ref[0] = g
        dl_ref[0], nm_ref[0], nv_ref[0] = _adamw_math(w_ref[0], g, m_ref[0], v_ref[0])

    at_layer = pl.BlockSpec((1, a_dim, b_dim), lambda i: (layer, 0, 0))
    return pl.pallas_call(
        body,
        name=name,
        grid=(1,),
        in_specs=[_full((N_DEV, nr, d)), ANY, at_layer, at_layer, at_layer, ANY, ANY, ANY, ANY],
        out_specs=[at_layer] * 4,
        out_shape=[jax.ShapeDtypeStruct(w.shape, F32)] * 4,
        input_output_aliases={5: 0, 6: 1, 7: 2, 8: 3},
        scratch_shapes=[pltpu.VMEM((nr, d), parts.dtype), pltpu.VMEM((nr, d), F32), pltpu.SemaphoreType.DMA(())],
        compiler_params=_params(VMEM_BIG),
    )(parts, own, w, m, v, *outs)


def _adamw(w, g, m, v, name):
    shape = w.shape
    cols = shape[-1]
    rows = w.size // cols
    tr = rows
    for cand in (512, 256, 128, 64):
        if rows % cand == 0 and rows > cand:
            tr = cand
            break

    def body(w_ref, g_ref, m_ref, v_ref, d_ref, nm_ref, nv_ref):
        d_ref[...], nm_ref[...], nv_ref[...] = _adamw_math(w_ref[...], g_ref[...], m_ref[...], v_ref[...])

    blk = pl.BlockSpec((tr, cols), lambda i: (i, 0))
    flat = [a.reshape(rows, cols) for a in (w, g, m, v)]
    outs = pl.pallas_call(
        body,
        name=name,
        grid=(rows // tr,),
        in_specs=[blk] * 4,
        out_specs=[blk] * 3,
        out_shape=[jax.ShapeDtypeStruct((rows, cols), F32)] * 3,
    )(*flat)
    return [o.reshape(shape) for o in outs]


WEIGHTS = ["meta_tokens", "ffn1_w_gate", "ffn1_w_up", "ffn1_w_down", "ln1_g", "ln1_b", "w_in", "conv_w", "conv_b", "gate_a_w", "gate_a_b",
           "gate_x_w", "gate_x_b", "lru_lambda", "attn_sinks", "norm_rec_g", "norm_attn_g", "w_out", "ln2_g", "ln2_b", "ffn2_w_gate",
           "ffn2_w_up", "ffn2_w_down", "ln3_g", "ln3_b"]
BIG = [("ffn1_w_gate", True), ("ffn1_w_up", True), ("ffn1_w_down", False), ("w_in", True), ("w_out", False),
       ("ffn2_w_gate", True), ("ffn2_w_up", True), ("ffn2_w_down", False)]


SMALL = ["ln1_g", "ln1_b", "ln2_g", "ln2_b", "ln3_g", "ln3_b", "conv_b", "gate_a_b", "gate_x_b", "lru_lambda", "norm_rec_g", "norm_attn_g",
         "conv_w", "gate_a_w", "gate_x_w", "attn_sinks"]
GATHER_GROUPS = [["ffn1_w_gate", "ffn1_w_up", "ffn1_w_down"], ["w_in", "w_out"], ["ffn2_w_gate", "ffn2_w_up", "ffn2_w_down"]]
EXCHANGE_GROUPS = [["ffn2_w_gate", "ffn2_w_up", "ffn2_w_down", "w_out"], ["w_in", "ffn1_w_gate", "ffn1_w_up", "ffn1_w_down"]]


def _rope_tables(t_len, pad):
    pos = (jnp.arange(t_len) - pad).astype(F32)
    inv_freq = ROPE_THETA ** (-jnp.arange(0, ROPE_DIM, 2, dtype=F32) / ROPE_DIM)
    ang = pos[:, None] * inv_freq[None, :]
    cos, sin = jnp.cos(ang), jnp.sin(ang)
    half = ROPE_DIM // 2
    rest = jnp.zeros((t_len, HEAD_DIM - ROPE_DIM), F32)
    zero = jnp.zeros((t_len, half), F32)
    cos_h = jnp.concatenate([cos, cos, rest + 1.0], axis=1)
    up_h = jnp.concatenate([-sin, zero, rest], axis=1)
    down_h = jnp.concatenate([zero, sin, rest], axis=1)
    return [jnp.tile(t, (1, ATTN_WIDTH // HEAD_DIM)) for t in (cos_h, up_h, down_h)]


def _gate_tiles(w):
    z = jnp.zeros((HEAD_DIM, HEAD_DIM), w.dtype)
    tiles = [jnp.block([[w[2 * j], z], [z, w[2 * j + 1]]]) for j in range(w.shape[0] // 2)]
    return jnp.stack(tiles).astype(BF16)


def _gate_blocks(tiles):
    out = []
    for j in range(tiles.shape[0]):
        out += [tiles[j, :HEAD_DIM, :HEAD_DIM], tiles[j, HEAD_DIM:, HEAD_DIM:]]
    return jnp.stack(out)


def kernel(x, meta_tokens, ffn1_w_gate, ffn1_w_up, ffn1_w_down, ln1_g, ln1_b, w_in, conv_w, conv_b, gate_a_w, gate_a_b, gate_x_w, gate_x_b, lru_lambda, attn_sinks, norm_rec_g, norm_attn_g, w_out, ln2_g, ln2_b, ffn2_w_gate, ffn2_w_up, ffn2_w_down, ln3_g, ln3_b, loss_target, m_meta_tokens, m_ffn1_w_gate, m_ffn1_w_up, m_ffn1_w_down, m_ln1_g, m_ln1_b, m_w_in, m_conv_w, m_conv_b, m_gate_a_w, m_gate_a_b, m_gate_x_w, m_gate_x_b, m_lru_lambda, m_attn_sinks, m_norm_rec_g, m_norm_attn_g, m_w_out, m_ln2_g, m_ln2_b, m_ffn2_w_gate, m_ffn2_w_up, m_ffn2_w_down, m_ln3_g, m_ln3_b, v_meta_tokens, v_ffn1_w_gate, v_ffn1_w_up, v_ffn1_w_down, v_ln1_g, v_ln1_b, v_w_in, v_conv_w, v_conv_b, v_gate_a_w, v_gate_a_b, v_gate_x_w, v_gate_x_b, v_lru_lambda, v_attn_sinks, v_norm_rec_g, v_norm_attn_g, v_w_out, v_ln2_g, v_ln2_b, v_ffn2_w_gate, v_ffn2_w_up, v_ffn2_w_down, v_ln3_g, v_ln3_b):
    given = dict(locals())
    w = {k: given[k] for k in WEIGHTS}
    n_seq, seq, d = x.shape
    depth = ln1_g.shape[0]
    alpha = (2.0 * depth) ** 0.25
    pad = (-(N_META + seq)) % BLOCK
    t_len = pad + N_META + seq
    rows = n_seq * t_len
    me = 4 * lax.axis_index("x") + 2 * lax.axis_index("y") + lax.axis_index("c")

    transposed = dict(BIG)

    def turned(a, name):
        return jnp.swapaxes(a, 1, 2) if transposed[name] else a

    small = jnp.concatenate([meta_tokens, conv_w.reshape(-1, BLOCK)], axis=0)
    (small_all,) = _all_gather([small], small, "gather_small")
    sent, loc, gathers, chain = {}, {}, {}, small_all

    def start_gather(l, k):
        shard = jnp.concatenate([sent[name][l] for name in GATHER_GROUPS[k]], axis=0)
        zone = lax.dynamic_update_slice(lax.empty((N_DEV,) + shard.shape, BF16), shard[None], (me, 0, 0))
        gathers[(l, k)] = _push_start([shard], [zone], chain, False, f"gather_start_{l}_{k}")
        return gathers[(l, k)][2][0]

    for k, group in enumerate(GATHER_GROUPS):
        off = 0
        for name in group:
            sent[name] = turned(w[name], name).astype(BF16)
            loc[name] = (off, sent[name].shape[1])
            off += sent[name].shape[1]
        chain = start_gather(0, k)
    for l in range(1, depth):
        for k in range(len(GATHER_GROUPS)):
            chain = start_gather(l, k)
    ffn1_loc = [loc["ffn1_w_gate"], loc["ffn1_w_up"], loc["ffn1_w_down"]]
    ffn2_loc = [loc["ffn2_w_gate"], loc["ffn2_w_up"], loc["ffn2_w_down"]]
    meta_full = small_all[:, :N_META, :].transpose(1, 0, 2).reshape(N_META, d)
    conv_shard = conv_w.shape[-1]
    conv_full = small_all[:, N_META:, :].reshape(N_DEV, depth, CONV_WIDTH, conv_shard).transpose(1, 2, 0, 3).reshape(depth, CONV_WIDTH, REC_WIDTH)

    cos, sin_up, sin_down = _rope_tables(t_len, pad)
    bias = _attn_bias(pad)
    row1 = lambda a: a.reshape(1, -1)

    h = jnp.concatenate([jnp.zeros((n_seq, pad, d), F32), jnp.broadcast_to(meta_full[None], (n_seq, N_META, d)), x], axis=1).reshape(rows, d)
    target = jnp.pad(loss_target, ((0, 0), (pad + N_META, 0), (0, 0))).reshape(rows, d)
    hb = h.astype(BF16)
    saved, wbufs = [], []

    def gathered(l, k, after):
        _, (buf,) = _push_wait(*gathers[(l, k)], after, False, f"gather_wait_{l}_{k}")
        return buf

    for l in range(depth):
        w_ffn1 = gathered(l, 0, chain if l == 0 else h)
        s = {"h0": hb}
        h, s["g1"], s["u1"], s["z1"], hb = _ffn_fwd(h, w_ffn1, ffn1_loc, row1(ln1_g[l]), row1(ln1_b[l]), alpha, "ffn1_fwd")
        s["h1"] = hb
        s["wa"], s["wx"] = _gate_tiles(gate_a_w[l]), _gate_tiles(gate_x_w[l])
        s["sinks"] = jnp.broadcast_to(jnp.repeat(attn_sinks[l].reshape(N_Q_HEADS // GROUP, GROUP), BLOCK, axis=1)[:, :, None], (N_Q_HEADS // GROUP, GROUP * BLOCK, BLOCK))
        wbuf = gathered(l, 1, h)
        proj = _proj_fwd(h, wbuf, loc["w_in"], "proj_fwd")
        s["proj"] = proj
        s["xc"], s["hrec"], s["y_rec"] = _rglru_fwd(proj, conv_full[l], row1(conv_b[l]), s["wa"], row1(gate_a_b[l]), s["wx"], row1(gate_x_b[l]),
                                                     row1(lru_lambda[l]), t_len, pad, "rglru_fwd")
        s["qkv"] = _rope_fwd(proj, cos, sin_up, sin_down, t_len, "rope_fwd")
        s["y_attn"] = _attn_fwd(s["qkv"], s["sinks"], bias, t_len, "attn_fwd")
        h, s["z2"], hb = _mix_out_fwd(s["y_rec"], s["y_attn"], h, wbuf, loc["w_out"], row1(norm_rec_g[l]), row1(norm_attn_g[l]), row1(ln2_g[l]), row1(ln2_b[l]), alpha, "mix_out_fwd")
        s["h2"] = hb
        w_ffn2 = gathered(l, 2, h)
        h, s["g2"], s["u2"], s["z3"], hb = _ffn_fwd(h, w_ffn2, ffn2_loc, row1(ln3_g[l]), row1(ln3_b[l]), alpha, "ffn2_fwd")
        saved.append(s)
        wbufs.append((w_ffn1, wbuf, w_ffn2))
    loss_part, dh = _loss_head(h, target, t_len, pad + N_META, "loss_head")

    exchanges = {}
    small_grads = [None] * depth
    behind = dh

    def exchange(l, k, bg, chain):
        gs = [bg[name].reshape(N_DEV, -1, d) for name in EXCHANGE_GROUPS[k]]
        zones = [lax.empty(g.shape, g.dtype) for g in gs]
        exchanges[(l, k)] = _push_start(gs, zones, chain, True, f"exchange_start_{l}_{k}")
        return exchanges[(l, k)][2][0]

    for l in reversed(range(depth)):
        (w_ffn1, wbuf, w_ffn2), s = wbufs[l], saved[l]
        bg, sg = {}, {}
        dh, dg, du, act, dyb, sg["ln3_g"], sg["ln3_b"] = _ffn_bwd(dh, s["z3"], s["g2"], s["u2"], w_ffn2, ffn2_loc, row1(ln3_g[l]), behind, alpha, "ffn2_bwd")
        bg["ffn2_w_gate"] = _mm_tn(dg, s["h2"], BF16, "ffn_wgrad_in")
        bg["ffn2_w_up"] = _mm_tn(du, s["h2"], BF16, "ffn_wgrad_in")
        bg["ffn2_w_down"] = _mm_tn(act, dyb, BF16, "ffn_wgrad_down")
        dz, dzb, dy_rec, dy_attn, yn, sg["ln2_g"], sg["ln2_b"], sg["norm_rec_g"], sg["norm_attn_g"] = _mix_out_bwd(
            dh, s["z2"], s["y_rec"], s["y_attn"], wbuf, loc["w_out"], row1(norm_rec_g[l]), row1(norm_attn_g[l]), row1(ln2_g[l]), "mix_out_bwd")
        bg["w_out"] = _mm_tn(yn, dzb, BF16, "w_out_wgrad")
        behind = exchange(l, 0, bg, behind)
        dxr, dgt, sg["conv_w"], sg["conv_b"], dwa, sg["gate_a_b"], dwx, sg["gate_x_b"], sg["lru_lambda"] = _rglru_bwd(
            dy_rec, s["hrec"], s["proj"], s["xc"], conv_full[l], s["wa"], row1(gate_a_b[l]), s["wx"], row1(gate_x_b[l]), row1(lru_lambda[l]), t_len, pad, "rglru_bwd")
        sg["gate_a_w"], sg["gate_x_w"] = _gate_blocks(dwa), _gate_blocks(dwx)
        dq, dk, dv, dsink = _attn_bwd(s["qkv"], s["sinks"], bias, s["y_attn"], dy_attn, t_len, "attn_bwd")
        sg["attn_sinks"] = dsink[:, 0]
        dqkv = _rope_bwd(dq, dk, dv, cos, sin_up, sin_down, t_len, "rope_bwd")
        dproj = jnp.concatenate([dxr, dgt, dqkv], axis=1)
        bg["w_in"] = _mm_tn(dproj, s["h1"], BF16, "w_in_wgrad")
        dh = _proj_bwd(dproj, wbuf, loc["w_in"], dz, alpha, "proj_bwd")
        dh, dg, du, act, dyb, sg["ln1_g"], sg["ln1_b"] = _ffn_bwd(dh, s["z1"], s["g1"], s["u1"], w_ffn1, ffn1_loc, row1(ln1_g[l]), behind, alpha, "ffn1_bwd")
        small_grads[l] = sg
        if l == 0:
            dmeta = _meta_grad(dh, t_len, pad, "meta_grad")
            pieces = [small_grads[j][name].reshape(-1) for j in range(depth) for name in SMALL] + [dmeta.reshape(-1), loss_part[0, :1]]
            sizes = [p.shape[0] for p in pieces]
            flat = jnp.concatenate(pieces)
            width = 1024
            n_rows = -(-flat.shape[0] // (8 * width)) * 8
            flat = jnp.pad(flat, (0, n_rows * width - flat.shape[0])).reshape(n_rows, width)
            zone = lax.dynamic_update_slice(lax.empty((N_DEV, n_rows, width), F32), flat[None], (me, 0, 0))
            small_gather = _push_start([flat], [zone], behind, False, "small_grads_start")
            behind = small_gather[2][0]
        bg["ffn1_w_gate"] = _mm_tn(dg, s["h0"], BF16, "ffn_wgrad_in")
        bg["ffn1_w_up"] = _mm_tn(du, s["h0"], BF16, "ffn_wgrad_in")
        bg["ffn1_w_down"] = _mm_tn(act, dyb, BF16, "ffn_wgrad_down")
        behind = exchange(l, 1, bg, behind)
    grad_x = dh.reshape(n_seq, t_len, d)[:, pad + N_META :, :]

    state = {name: [turned(a, name) for a in (w[name], given["m_" + name], given["v_" + name])] for name, _ in BIG}
    outs = {name: [lax.empty(state[name][0].shape, F32) for _ in range(4)] for name, _ in BIG}

    def reduce_group(l, k, after):
        send_sems, recv_sems, srcs, zones = exchanges[(l, k)]
        owns, parts = _push_wait(send_sems, recv_sems, srcs, zones, after, True, f"exchange_wait_{l}_{k}")
        for name, own, p in zip(EXCHANGE_GROUPS[k], owns, parts):
            outs[name] = _reduce_update(p, own, *state[name], outs[name], l, "reduce_update")
        return outs[EXCHANGE_GROUPS[k][-1]][0]

    grads = {}
    order = [(l, k) for l in reversed(range(depth)) for k in range(len(EXCHANGE_GROUPS))]
    after = behind
    for l, k in order[:-1]:
        after = reduce_group(l, k, after)
    _, (flat_all,) = _push_wait(*small_gather, after, False, "small_grads_wait")
    total = _sum8(flat_all, "sum_small_grads").reshape(-1)
    offs = [0]
    for sz in sizes:
        offs.append(offs[-1] + sz)
    taken = [total[offs[k] : offs[k + 1]] for k in range(len(sizes))]
    for j, name in enumerate(SMALL):
        full_shape = (depth,) + ((CONV_WIDTH, REC_WIDTH) if name == "conv_w" else w[name].shape[1:])
        grads[name] = jnp.stack([taken[l * len(SMALL) + j] for l in range(depth)]).reshape(full_shape)
    grads["conv_w"] = lax.dynamic_slice_in_dim(grads["conv_w"], me * conv_shard, conv_shard, axis=2)
    meta_shard = meta_tokens.shape[1]
    grads["meta_tokens"] = lax.dynamic_slice_in_dim(taken[-2].reshape(N_META, d), me * meta_shard, meta_shard, axis=1)
    loss = taken[-1][0]
    deltas, new_m, new_v = {}, {}, {}
    for name in WEIGHTS:
        if name not in dict(BIG):
            deltas[name], new_m[name], new_v[name] = _adamw(w[name], grads[name], given["m_" + name], given["v_" + name], "adamw")

    reduce_group(*order[-1], new_v[WEIGHTS[-1]])
    for name, _ in BIG:
        grads[name], deltas[name], new_m[name], new_v[name] = [turned(a, name) for a in outs[name]]
    return (loss, grad_x, *[grads[k] for k in WEIGHTS], *[deltas[k] for k in WEIGHTS], *[new_m[k] for k in WEIGHTS], *[new_v[k] for k in WEIGHTS])
```
